```python
import math
import functools
import jax
import jax.numpy as jnp
from jax import lax
import numpy as np

D_MODEL = 1024
BATCH = 8
SEQ = 2048
DEPTH = 4
DEC_BATCH = 128
DEC_SEQ = 1
PAST_LEN = 2048
PAGE_SIZE = 128

N_AB = (DEPTH + 1) // 2
N_C = DEPTH // 2

A_HEADS = 8
A_KV_HEADS = 2
A_GROUP = A_HEADS // A_KV_HEADS
A_HEAD_DIM = 64
IDX_HEADS = 4
IDX_DIM = 64
TOPK_MAX = 256
B_HEADS = 4
B_KEY_DIM = 128
B_VAL_DIM = 128
CONV_W = 4
CHUNK = 64
C_HEADS = 8
C_HEAD_DIM = 64
NUM_BUCKETS = 32
MAX_DISTANCE = 128
D_FF = 2816
N_MOD = 9
Q_BLOCK = 128
EPS = 1e-6

A_Q = A_HEADS * A_HEAD_DIM
A_KV = A_KV_HEADS * A_HEAD_DIM
B_K = B_HEADS * B_KEY_DIM
B_V = B_HEADS * B_VAL_DIM
B_CONV_DIM = 2 * B_K + B_V
AB_SIZES = (A_Q, A_KV, A_KV, IDX_HEADS * IDX_DIM, IDX_DIM, IDX_HEADS, B_CONV_DIM, B_V, B_HEADS, B_HEADS)
AB_IN = sum(AB_SIZES)
AB_OUT = A_Q + B_V
C_QK = C_HEADS * 2 * C_HEAD_DIM
C_V = C_HEADS * 2 * C_HEAD_DIM
C_IN = 2 * C_QK + C_V

kernel_name = 'hybrid_dsa_gdn_diffattn_decoder_step'


def split_points(sizes):
    return [int(s) for s in np.cumsum(sizes)[:-1]]


def rms_norm(x, w):
    xf = x.astype(jnp.float32)
    y = xf * lax.rsqrt(jnp.mean(xf * xf, axis=-1, keepdims=True) + EPS)
    return (y * w.astype(jnp.float32)).astype(x.dtype)


def l2_norm(x):
    xf = x.astype(jnp.float32)
    return xf * lax.rsqrt(jnp.sum(xf * xf, axis=-1, keepdims=True) + EPS)


def swiglu(h, w1, w2):
    gt, up = jnp.split(h @ w1, 2, axis=-1)
    return (jax.nn.silu(gt) * up) @ w2


def t5_bucket(dist):
    n = jnp.maximum(dist, 0)
    max_exact = NUM_BUCKETS // 2
    nf = jnp.maximum(n, 1).astype(jnp.float32)
    large = max_exact + (jnp.log(nf / max_exact) / math.log(MAX_DISTANCE / max_exact)
                         * (NUM_BUCKETS - max_exact)).astype(jnp.int32)
    large = jnp.minimum(large, NUM_BUCKETS - 1)
    return jnp.where(n < max_exact, n, large)


def gather_pages(pool, page_table):
    g = pool[page_table]
    return g.reshape(g.shape[0], g.shape[1] * g.shape[2], *g.shape[3:])


def blockwise(fn, xs, q_pos):
    nb = q_pos.shape[0] // Q_BLOCK
    split = lambda a: jnp.moveaxis(a.reshape(a.shape[0], nb, Q_BLOCK, *a.shape[2:]), 1, 0)
    out = lax.map(lambda blk: fn(*blk[0], blk[1]),
                  (tuple(split(a) for a in xs), q_pos.reshape(nb, Q_BLOCK)))
    out = jnp.moveaxis(out, 0, 1)
    return out.reshape(out.shape[0], nb * Q_BLOCK, *out.shape[3:])


def dsa_attend(q, qi, wi, q_pos, k, v, ki, k_pos, topk, bias_tab):
    Bsz, Q = q.shape[:2]
    f32 = jnp.float32
    rel = jax.nn.relu(jnp.einsum('bqhd,bkd->bqhk', qi, ki).astype(f32) * IDX_DIM ** -0.5)
    score = jnp.einsum('bqh,bqhk->bqk', wi.astype(f32) * IDX_HEADS ** -0.5, rel)
    causal = k_pos[None, :] <= q_pos[:, None]
    score = jnp.where(causal[None], score, -jnp.inf)
    _, sel = lax.top_k(score, topk)
    sel_pos = k_pos[sel]
    valid = sel_pos <= q_pos[None, :, None]
    take = jax.vmap(lambda arr, ix: arr[ix])
    k_sel = take(k, sel)
    v_sel = take(v, sel)
    qg = q.reshape(Bsz, Q, A_KV_HEADS, A_GROUP, A_HEAD_DIM)
    logits = jnp.einsum('bqngd,bqknd->bqngk', qg, k_sel).astype(f32) * A_HEAD_DIM ** -0.5
    bias = bias_tab[t5_bucket(q_pos[None, :, None] - sel_pos)].astype(f32)
    bias = jnp.moveaxis(bias.reshape(Bsz, Q, topk, A_KV_HEADS, A_GROUP), 2, 4)
    logits = jnp.where(valid[:, :, None, None, :], logits + bias, -jnp.inf)
    p = jax.nn.softmax(logits, axis=-1)
    o = jnp.einsum('bqngk,bqknd->bqngd', p.astype(v.dtype), v_sel)
    return o.reshape(Bsz, Q, A_Q)


def diff_attend(q, q_pos, k, v, k_pos, lam, bias_tab):
    f32 = jnp.float32
    logits = jnp.einsum('bqhcd,bkhcd->bhcqk', q, k).astype(f32) * C_HEAD_DIM ** -0.5
    bias = jnp.moveaxis(bias_tab[t5_bucket(q_pos[:, None] - k_pos[None, :])].astype(f32), 2, 0)
    causal = k_pos[None, :] <= q_pos[:, None]
    logits = jnp.where(causal, logits + bias[None, :, None], -jnp.inf)
    p = jax.nn.softmax(logits, axis=-1)
    attn = p[:, :, 0] - lam * p[:, :, 1]
    return jnp.einsum('bhqk,bkhe->bqhe', attn.astype(v.dtype), v)


def gated_delta_rule(q, k, v, g, beta, s0):
    Bsz, T, H, _ = q.shape
    DV = v.shape[-1]
    f32 = jnp.float32
    pad = (-T) % CHUNK

    def prep(a):
        a = jnp.pad(a.astype(f32), [(0, 0), (0, pad)] + [(0, 0)] * (a.ndim - 2))
        a = a.reshape(Bsz, a.shape[1] // CHUNK, CHUNK, *a.shape[2:])
        return jnp.moveaxis(a, (1, 0, 3, 2), (0, 1, 2, 3))

    qc, kc, vc, gc, bc = (prep(a) for a in (q, k, v, g, beta))
    gcum = jnp.cumsum(gc, axis=-1)
    kb = kc * bc[..., None]
    vb = vc * bc[..., None]
    tri = jnp.tril(jnp.ones((CHUNK, CHUNK), bool))
    stri = jnp.tril(jnp.ones((CHUNK, CHUNK), bool), -1)
    decay = jnp.exp(jnp.where(tri, gcum[..., :, None] - gcum[..., None, :], -jnp.inf))
    lower = jnp.where(stri, jnp.einsum('nbhcd,nbhsd->nbhcs', kb, kc) * decay, 0.0)
    eye = jnp.eye(CHUNK, dtype=f32)
    tmat = lax.linalg.triangular_solve(lower + eye, jnp.broadcast_to(eye, lower.shape),
                                       left_side=True, lower=True, unit_diagonal=True)
    u = tmat @ vb
    w = tmat @ (kb * jnp.exp(gcum)[..., None])
    a_qk = jnp.where(tri, jnp.einsum('nbhcd,nbhsd->nbhcs', qc, kc) * decay, 0.0)

    def step(S, inp):
        q_i, k_i, u_i, w_i, g_i, a_i = inp
        v_new = u_i - w_i @ S
        o = (q_i * jnp.exp(g_i)[..., None]) @ S + a_i @ v_new
        g_last = g_i[..., -1:]
        S = S * jnp.exp(g_last)[..., None] + jnp.einsum(
            'bhcd,bhce->bhde', k_i * jnp.exp(g_last - g_i)[..., None], v_new)
        return S, o

    S, o = lax.scan(step, s0.astype(f32), (qc, kc, u, w, gcum, a_qk))
    o = jnp.moveaxis(o, (0, 1, 2, 3), (1, 0, 3, 2)).reshape(Bsz, -1, H, DV)[:, :T]
    return o, S


def gated_deltanet(qkv, z, a, b, conv_buf, s0, conv_w, a_log, dt_bias, norm_w):
    Bsz, T, _ = qkv.shape
    f32 = jnp.float32
    ext = jnp.concatenate([conv_buf.astype(qkv.dtype), qkv], axis=1)
    conv = sum(ext[:, j:j + T] * conv_w[j] for j in range(CONV_W))
    conv = jax.nn.silu(conv.astype(f32))
    q, k, v = jnp.split(conv, [B_K, 2 * B_K], axis=-1)
    q = l2_norm(q.reshape(Bsz, T, B_HEADS, B_KEY_DIM)) * B_KEY_DIM ** -0.5
    k = l2_norm(k.reshape(Bsz, T, B_HEADS, B_KEY_DIM))
    v = v.reshape(Bsz, T, B_HEADS, B_VAL_DIM)
    g = -jnp.exp(a_log.astype(f32)) * jax.nn.softplus(a.astype(f32) + dt_bias.astype(f32))
    beta = jax.nn.sigmoid(b.astype(f32))
    o, s_new = gated_delta_rule(q, k, v, g, beta, s0)
    o = rms_norm(o, norm_w) * jax.nn.silu(z.reshape(Bsz, T, B_HEADS, B_VAL_DIM).astype(f32))
    return o.reshape(Bsz, T, B_V).astype(qkv.dtype), s_new.astype(qkv.dtype), ext[:, T:]


def even_mixer(h, past, topk, w_in, w_out, conv_w, a_log, dt_bias, gdn_norm_w, bias_tab):
    Bsz, T, _ = h.shape
    q_a, k_a, v_a, qi, ki, wi, qkv_b, z, a_b, b_b = jnp.split(h @ w_in, split_points(AB_SIZES), axis=-1)
    q_a = q_a.reshape(Bsz, T, A_HEADS, A_HEAD_DIM)
    k_a = k_a.reshape(Bsz, T, A_KV_HEADS, A_HEAD_DIM)
    v_a = v_a.reshape(Bsz, T, A_KV_HEADS, A_HEAD_DIM)
    qi = qi.reshape(Bsz, T, IDX_HEADS, IDX_DIM)
    if past is None:
        k_all, v_all, ki_all = k_a, v_a, ki
        conv_buf = jnp.zeros((Bsz, CONV_W - 1, B_CONV_DIM), h.dtype)
        s0 = jnp.zeros((Bsz, B_HEADS, B_KEY_DIM, B_VAL_DIM), jnp.float32)
    else:
        k_past, v_past, ki_past, conv_buf, s0 = past
        k_all = jnp.concatenate([k_past.astype(k_a.dtype), k_a], axis=1)
        v_all = jnp.concatenate([v_past.astype(v_a.dtype), v_a], axis=1)
        ki_all = jnp.concatenate([ki_past.astype(ki.dtype), ki], axis=1)
    L = k_all.shape[1]
    k_pos = jnp.arange(L, dtype=jnp.int32)
    q_pos = k_pos[L - T:]
    attend = functools.partial(dsa_attend, k=k_all, v=v_all, ki=ki_all, k_pos=k_pos,
                               topk=topk, bias_tab=bias_tab[:, :A_HEADS])
    o_a = blockwise(attend, (q_a, qi, wi), q_pos) if past is None else attend(q_a, qi, wi, q_pos)
    o_b, s_new, buf_new = gated_deltanet(qkv_b, z, a_b, b_b, conv_buf, s0, conv_w, a_log, dt_bias, gdn_norm_w)
    out = jnp.concatenate([o_a, o_b], axis=-1) @ w_out
    return out, (k_a, v_a, ki, s_new, buf_new)


def odd_mixer(h, past, layer_idx, w_in, w_out, lq1, lk1, lq2, lk2, subln_w, bias_tab):
    Bsz, T, _ = h.shape
    f32 = jnp.float32
    q, k, v = jnp.split(h @ w_in, [C_QK, 2 * C_QK], axis=-1)
    q = q.reshape(Bsz, T, C_HEADS, 2, C_HEAD_DIM)
    k_rows = k.reshape(Bsz, T, C_HEADS, 2 * C_HEAD_DIM)
    v_rows = v.reshape(Bsz, T, C_HEADS, 2 * C_HEAD_DIM)
    if past is None:
        k_all, v_all = k_rows, v_rows
    else:
        k_all = jnp.concatenate([past[0].astype(k_rows.dtype), k_rows], axis=1)
        v_all = jnp.concatenate([past[1].astype(v_rows.dtype), v_rows], axis=1)
    L = k_all.shape[1]
    k_pos = jnp.arange(L, dtype=jnp.int32)
    q_pos = k_pos[L - T:]
    lam_init = 0.8 - 0.6 * math.exp(-0.3 * layer_idx)
    lam = (jnp.exp(jnp.sum(lq1.astype(f32) * lk1.astype(f32)))
           - jnp.exp(jnp.sum(lq2.astype(f32) * lk2.astype(f32))) + lam_init)
    attend = functools.partial(diff_attend, k=k_all.reshape(Bsz, L, C_HEADS, 2, C_HEAD_DIM), v=v_all,
                               k_pos=k_pos, lam=lam, bias_tab=bias_tab[:, A_HEADS:])
    o = blockwise(attend, (q,), q_pos) if past is None else attend(q, q_pos)
    o = rms_norm(o, subln_w) * (1.0 - lam_init)
    return o.reshape(Bsz, T, C_V) @ w_out, (k_rows, v_rows)


def trunk(x, c, past, page_table, W):
    T = x.shape[1]
    past_len = 0 if past is None else page_table.shape[1] * PAGE_SIZE
    topk = min(TOPK_MAX, (past_len + T) // 4)
    new = [[] for _ in range(7)]
    for i in range(DEPTH):
        mod = (jax.nn.silu(c) @ W['ada_w'][i] + W['ada_b'][i]).reshape(c.shape[0], 1, N_MOD, D_MODEL)
        sh1, sc1, g1, sh2, sc2, g2, sh3, sc3, g3 = (mod[:, :, m] for m in range(N_MOD))
        h = rms_norm(x, W['norm_w'][i, 0]) * (1 + sc1) + sh1
        x = x + 0.5 * g1 * swiglu(h, W['ffn_w1'][i, 0], W['ffn_w2'][i, 0])
        h = rms_norm(x, W['norm_w'][i, 1]) * (1 + sc2) + sh2
        j = i // 2
        if i % 2 == 0:
            lp = None if past is None else (
                gather_pages(past['a_k'][j], page_table), gather_pages(past['a_v'][j], page_table),
                gather_pages(past['a_kidx'][j], page_table), past['b_conv'][j], past['b_ssm'][j])
            mix, rows = even_mixer(h, lp, topk, W['ab_w_in'][j], W['ab_w_out'][j], W['gdn_conv_w'][j],
                                   W['gdn_a_log'][j], W['gdn_dt_bias'][j], W['gdn_norm_w'][j], W['rel_bias'])
            for lst, arr in zip(new[:5], rows):
                lst.append(arr)
        else:
            lp = None if past is None else (
                gather_pages(past['c_k'][j], page_table), gather_pages(past['c_v'][j], page_table))
            mix, rows = odd_mixer(h, lp, i, W['c_w_in'][j], W['c_w_out'][j], W['c_lq1'][j], W['c_lk1'][j],
                                  W['c_lq2'][j], W['c_lk2'][j], W['c_subln_w'][j], W['rel_bias'])
            new[5].append(rows[0])
            new[6].append(rows[1])
        x = x + g2 * mix
        h = rms_norm(x, W['norm_w'][i, 2]) * (1 + sc3) + sh3
        x = x + 0.5 * g3 * swiglu(h, W['ffn_w1'][i, 1], W['ffn_w2'][i, 1])
    return rms_norm(x, W['final_norm_w']), [jnp.stack(lst) for lst in new]


def setup_inputs(seed: int = 0) -> dict:
    key = jax.random.key(seed)
    ks = iter(jax.random.split(key, 40))
    f32 = jnp.float32

    def nrm(shape, scale):
        return jax.random.normal(next(ks), shape, f32) * scale

    def gain(shape):
        return 1.0 + nrm(shape, 0.02)

    n_pages = PAST_LEN // PAGE_SIZE
    n_used = DEC_BATCH * n_pages
    n_phys = n_used + (n_used + 3) // 4
    page_table = jax.random.permutation(next(ks), n_phys)[:n_used].reshape(DEC_BATCH, n_pages).astype(jnp.int32)
    return {
        'x_prompt': nrm((BATCH, SEQ, D_MODEL), 1.0),
        'x_sample': nrm((DEC_BATCH, DEC_SEQ, D_MODEL), 1.0),
        'c_prompt': nrm((BATCH, D_MODEL), 1.0),
        'c_sample': nrm((DEC_BATCH, D_MODEL), 1.0),
        'cache_A_k': nrm((N_AB, n_phys, PAGE_SIZE, A_KV_HEADS, A_HEAD_DIM), 1.0),
        'cache_A_v': nrm((N_AB, n_phys, PAGE_SIZE, A_KV_HEADS, A_HEAD_DIM), 1.0),
        'cache_A_kidx': nrm((N_AB, n_phys, PAGE_SIZE, IDX_DIM), 1.0),
        'cache_C_k': nrm((N_C, n_phys, PAGE_SIZE, C_HEADS, 2 * C_HEAD_DIM), 1.0),
        'cache_C_v': nrm((N_C, n_phys, PAGE_SIZE, C_HEADS, 2 * C_HEAD_DIM), 1.0),
        'state_B_ssm': nrm((N_AB, DEC_BATCH, B_HEADS, B_KEY_DIM, B_VAL_DIM), 0.1),
        'state_B_conv': nrm((N_AB, DEC_BATCH, CONV_W - 1, B_CONV_DIM), 1.0),
        'page_table': page_table,
        'rel_bias': nrm((NUM_BUCKETS, A_HEADS + C_HEADS), 0.3),
        'ada_w': nrm((DEPTH, D_MODEL, N_MOD * D_MODEL), 0.5 * D_MODEL ** -0.5),
        'ada_b': nrm((DEPTH, N_MOD * D_MODEL), 0.02),
        'norm_w': gain((DEPTH, 3, D_MODEL)),
        'final_norm_w': gain((D_MODEL,)),
        'ffn_w1': nrm((DEPTH, 2, D_MODEL, 2 * D_FF), D_MODEL ** -0.5),
        'ffn_w2': nrm((DEPTH, 2, D_FF, D_MODEL), D_FF ** -0.5),
        'ab_w_in': nrm((N_AB, D_MODEL, AB_IN), D_MODEL ** -0.5),
        'ab_w_out': nrm((N_AB, AB_OUT, D_MODEL), AB_OUT ** -0.5),
        'gdn_conv_w': nrm((N_AB, CONV_W, B_CONV_DIM), 0.5),
        'gdn_a_log': jnp.log(jax.random.uniform(next(ks), (N_AB, B_HEADS), f32, 1.0, 16.0)),
        'gdn_dt_bias': nrm((N_AB, B_HEADS), 0.1),
        'gdn_norm_w': gain((N_AB, B_VAL_DIM)),
        'c_w_in': nrm((N_C, D_MODEL, C_IN), D_MODEL ** -0.5),
        'c_w_out': nrm((N_C, C_V, D_MODEL), C_V ** -0.5),
        'c_lambda_q1': nrm((N_C, C_HEAD_DIM), 0.1),
        'c_lambda_k1': nrm((N_C, C_HEAD_DIM), 0.1),
        'c_lambda_q2': nrm((N_C, C_HEAD_DIM), 0.1),
        'c_lambda_k2': nrm((N_C, C_HEAD_DIM), 0.1),
        'c_subln_w': gain((N_C, 2 * C_HEAD_DIM)),
    }


def reference(x_prompt, x_sample, c_prompt, c_sample, cache_A_k, cache_A_v, cache_A_kidx, cache_C_k, cache_C_v,
              state_B_ssm, state_B_conv, page_table, rel_bias, ada_w, ada_b, norm_w, final_norm_w, ffn_w1, ffn_w2,
              ab_w_in, ab_w_out, gdn_conv_w, gdn_a_log, gdn_dt_bias, gdn_norm_w, c_w_in, c_w_out,
              c_lambda_q1, c_lambda_k1, c_lambda_q2, c_lambda_k2, c_subln_w):
    W = {'rel_bias': rel_bias, 'ada_w': ada_w, 'ada_b': ada_b, 'norm_w': norm_w, 'final_norm_w': final_norm_w,
         'ffn_w1': ffn_w1, 'ffn_w2': ffn_w2, 'ab_w_in': ab_w_in, 'ab_w_out': ab_w_out,
         'gdn_conv_w': gdn_conv_w, 'gdn_a_log': gdn_a_log, 'gdn_dt_bias': gdn_dt_bias, 'gdn_norm_w': gdn_norm_w,
         'c_w_in': c_w_in, 'c_w_out': c_w_out, 'c_lq1': c_lambda_q1, 'c_lk1': c_lambda_k1,
         'c_lq2': c_lambda_q2, 'c_lk2': c_lambda_k2, 'c_subln_w': c_subln_w}
    past = {'a_k': cache_A_k, 'a_v': cache_A_v, 'a_kidx': cache_A_kidx, 'c_k': cache_C_k, 'c_v': cache_C_v,
            'b_ssm': state_B_ssm, 'b_conv': state_B_conv}
    y_prompt, sp = trunk(x_prompt, c_prompt, None, None, W)
    y_sample, ss = trunk(x_sample, c_sample, past, page_table, W)
    p_a_k, p_a_v, p_a_kidx, p_b_ssm, p_b_conv, p_c_k, p_c_v = sp
    s_a_k, s_a_v, s_a_kidx, s_b_ssm, s_b_conv, s_c_k, s_c_v = ss
    return (y_prompt, y_sample, p_a_k, p_a_v, p_a_kidx, p_b_ssm, p_b_conv, p_c_k, p_c_v,
            s_a_k, s_a_v, s_a_kidx, s_b_ssm, s_b_conv, s_c_k, s_c_v)
```

```python
import functools
import math

import numpy as np
import jax
import jax.numpy as jnp
from jax import lax
from jax.experimental import pallas as pl
from jax.experimental.pallas import tpu as pltpu

F32 = jnp.float32
BF16 = jnp.bfloat16
I32 = jnp.int32

A_HEADS = 8
A_KV_HEADS = 2
A_GROUP = A_HEADS // A_KV_HEADS
A_HEAD_DIM = 64
IDX_HEADS = 4
IDX_DIM = 64
TOPK_MAX = 256
B_HEADS = 4
B_KEY_DIM = 128
B_VAL_DIM = 128
CONV_W = 4
CHUNK = 64
C_HEADS = 8
C_HEAD_DIM = 64
NUM_BUCKETS = 32
MAX_DISTANCE = 128
N_MOD = 9
EPS = 1e-6

A_Q = A_HEADS * A_HEAD_DIM
A_KV = A_KV_HEADS * A_HEAD_DIM
B_K = B_HEADS * B_KEY_DIM
B_V = B_HEADS * B_VAL_DIM
B_CONV_DIM = 2 * B_K + B_V
C_QK = C_HEADS * 2 * C_HEAD_DIM
C_V = C_HEADS * 2 * C_HEAD_DIM

LANES = 128
SUBLANES = 8
VMEM_LIMIT = 56 * 1024 * 1024

BLK = 128
NEG = -1e30

SM_KI = 0
SM_WI = IDX_DIM
SM_A = SM_WI + IDX_HEADS
SM_B = SM_A + B_HEADS

_NT = (((1,), (1,)), ((), ()))
_NN = (((1,), (0,)), ((), ()))
_TN = (((0,), (0,)), ((), ()))


def _cparams(sem):
    return pltpu.CompilerParams(dimension_semantics=sem, vmem_limit_bytes=VMEM_LIMIT)


def _dg(a, b, dims=_NN):
    return lax.dot_general(a, b, dims, preferred_element_type=F32)


def _split2(x):
    hi = x.astype(BF16)
    lo = (x - hi.astype(F32)).astype(BF16)
    return hi, lo


def _dot3(a, b, dims=_NN):
    ah, al = _split2(a)
    bh, bl = _split2(b)
    return _dg(ah, bh, dims) + (_dg(ah, bl, dims) + _dg(al, bh, dims))


def _dot_exact_lhs(a_bf, b, dims=_NN):
    b1 = b.astype(BF16)
    r1 = b - b1.astype(F32)
    b2 = r1.astype(BF16)
    b3 = (r1 - b2.astype(F32)).astype(BF16)
    return _dg(a_bf, b1, dims) + (_dg(a_bf, b2, dims) + _dg(a_bf, b3, dims))


def _silu(x):
    return x * jax.nn.sigmoid(x)


def _softplus(x):
    return jnp.maximum(x, 0.0) + jnp.log(1.0 + jnp.exp(-jnp.abs(x)))


def _norm_mod(x, nw, sc, sh):
    ms = jnp.mean(x * x, axis=-1, keepdims=True)
    return (x * lax.rsqrt(ms + EPS) * nw) * (1.0 + sc) + sh


def _mod_kernel(c_ref, w_ref, b_ref, o_ref):
    s = _silu(c_ref[...]).astype(BF16)
    o_ref[0] = _dg(s, w_ref[0].astype(BF16)) + b_ref[0]


def _modulation(c_all, ada_w, ada_b):
    depth, d, n = ada_w.shape
    m = c_all.shape[0]
    tn = 1024
    return pl.pallas_call(
        _mod_kernel,
        grid=(depth, n // tn),
        in_specs=[pl.BlockSpec((m, d), lambda i, j: (0, 0)),
                  pl.BlockSpec((1, d, tn), lambda i, j: (i, 0, j)),
                  pl.BlockSpec((1, 1, tn), lambda i, j: (i, 0, j))],
        out_specs=pl.BlockSpec((1, m, tn), lambda i, j: (i, 0, j)),
        out_shape=jax.ShapeDtypeStruct((depth, m, n), F32),
        compiler_params=_cparams(("arbitrary", "arbitrary")),
        name="adaln_mod",
    )(c_all, ada_w, ada_b.reshape(depth, 1, n))


def _row_tile(t):
    return min(512, t)


def _mod_spec(mod, tm, m):
    r = mod.shape[1]
    d = mod.shape[2] // N_MOD
    if r == 1:
        return pl.BlockSpec((1, 1, d), lambda s, t: (s, 0, m))
    return pl.BlockSpec((1, tm, d), lambda s, t: (s, t, m))


def _resident(shape):
    nd = len(shape)
    return pl.BlockSpec(shape, lambda s, t: (0,) * nd, pipeline_mode=pl.Buffered(1))


def _ffn_kernel(x_ref, sh_ref, sc_ref, g_ref, nw_ref, w1_ref, w2_ref, o_ref, acc_ref, *, fc):
    x = x_ref[0]
    hb = _norm_mod(x, nw_ref[...], sc_ref[0], sh_ref[0]).astype(BF16)
    f = w2_ref.shape[0]
    for c in range(f // fc):
        gt = _dg(hb, w1_ref[:, c * fc:(c + 1) * fc])
        up = _dg(hb, w1_ref[:, f + c * fc:f + (c + 1) * fc])
        a = (_silu(gt) * up).astype(BF16)
        contrib = _dg(a, w2_ref[c * fc:(c + 1) * fc, :])
        if c == 0:
            acc_ref[...] = contrib
        else:
            acc_ref[...] += contrib
    o_ref[0] = x + (0.5 * g_ref[0]) * acc_ref[...]


def _ffn(x, mod, m0, nw, w1b, w2b):
    s, t, d = x.shape
    tm = _row_tile(t)
    f = w2b.shape[0]
    fc = 256 if f % 256 == 0 else LANES
    xs = pl.BlockSpec((1, tm, d), lambda si, ti: (si, ti, 0))
    return pl.pallas_call(
        functools.partial(_ffn_kernel, fc=fc),
        grid=(s, t // tm),
        in_specs=[xs, _mod_spec(mod, tm, m0), _mod_spec(mod, tm, m0 + 1), _mod_spec(mod, tm, m0 + 2),
                  _resident((1, d)), _resident(w1b.shape), _resident(w2b.shape)],
        out_specs=xs,
        out_shape=jax.ShapeDtypeStruct(x.shape, F32),
        scratch_shapes=[pltpu.VMEM((tm, d), F32)],
        compiler_params=_cparams(("arbitrary", "arbitrary")),
        name="ffn",
    )(x, mod, mod, mod, nw.reshape(1, d), w1b, w2b)


def _inproj_kernel(x_ref, sh_ref, sc_ref, nw_ref, w_ref, *o_refs, widths):
    hb = _norm_mod(x_ref[0], nw_ref[...], sc_ref[0], sh_ref[0]).astype(BF16)
    off = 0
    for o_ref, wd in zip(o_refs, widths):
        o_ref[0] = _dg(hb, w_ref[:, off:off + wd])
        off += wd


def _inproj(x, mod, m0, nw, wb, widths):
    s, t, d = x.shape
    tm = _row_tile(t)
    xs = pl.BlockSpec((1, tm, d), lambda si, ti: (si, ti, 0))
    return pl.pallas_call(
        functools.partial(_inproj_kernel, widths=tuple(widths)),
        grid=(s, t // tm),
        in_specs=[xs, _mod_spec(mod, tm, m0), _mod_spec(mod, tm, m0 + 1), _resident((1, d)), _resident(wb.shape)],
        out_specs=[pl.BlockSpec((1, tm, wd), lambda si, ti: (si, ti, 0)) for wd in widths],
        out_shape=[jax.ShapeDtypeStruct((s, t, wd), F32) for wd in widths],
        compiler_params=_cparams(("arbitrary", "arbitrary")),
        name="inproj",
    )(x, mod, mod, nw.reshape(1, d), wb)


def _outproj_kernel(*refs, widths):
    n = len(widths)
    x_ref, g_ref, w_ref = refs[0], refs[1], refs[2]
    a_refs = refs[3:3 + n]
    o_ref = refs[3 + n]
    acc = None
    off = 0
    for a_ref, wd in zip(a_refs, widths):
        part = _dg(a_ref[0].astype(BF16), w_ref[off:off + wd, :])
        acc = part if acc is None else acc + part
        off += wd
    o_ref[0] = x_ref[0] + g_ref[0] * acc


def _outproj(x, mod, mg, wb, parts):
    s, t, d = x.shape
    tm = _row_tile(t)
    widths = tuple(p.shape[-1] for p in parts)
    xs = pl.BlockSpec((1, tm, d), lambda si, ti: (si, ti, 0))
    return pl.pallas_call(
        functools.partial(_outproj_kernel, widths=widths),
        grid=(s, t // tm),
        in_specs=[xs, _mod_spec(mod, tm, mg), _resident(wb.shape)]
                 + [pl.BlockSpec((1, tm, wd), lambda si, ti: (si, ti, 0)) for wd in widths],
        out_specs=xs,
        out_shape=jax.ShapeDtypeStruct(x.shape, F32),
        compiler_params=_cparams(("arbitrary", "arbitrary")),
        name="outproj",
    )(x, mod, wb, *parts)


def _final_norm_kernel(x_ref, w_ref, o_ref):
    x = x_ref[0]
    ms = jnp.mean(x * x, axis=-1, keepdims=True)
    o_ref[0] = x * lax.rsqrt(ms + EPS) * w_ref[...]


def _final_norm(x, w):
    s, t, d = x.shape
    tm = _row_tile(t)
    xs = pl.BlockSpec((1, tm, d), lambda si, ti: (si, ti, 0))
    return pl.pallas_call(
        _final_norm_kernel,
        grid=(s, t // tm),
        in_specs=[xs, pl.BlockSpec((1, d), lambda si, ti: (0, 0))],
        out_specs=xs,
        out_shape=jax.ShapeDtypeStruct(x.shape, F32),
        compiler_params=_cparams(("arbitrary", "arbitrary")),
        name="final_norm",
    )(x, w.reshape(1, d))


def _t5_bucket(dist):
    n = jnp.maximum(dist, 0)
    max_exact = NUM_BUCKETS // 2
    nf = jnp.maximum(n, 1).astype(F32)
    large = max_exact + (jnp.log(nf / max_exact) / math.log(MAX_DISTANCE / max_exact)
                         * (NUM_BUCKETS - max_exact)).astype(I32)
    large = jnp.minimum(large, NUM_BUCKETS - 1)
    return jnp.where(n < max_exact, n, large)


def _bias_tiles(bias_cols):
    assert BLK >= MAX_DISTANCE
    i = jnp.arange(BLK, dtype=I32)[:, None]
    j = jnp.arange(BLK, dtype=I32)[None, :]
    tiles = [bias_cols[_t5_bucket(i - j + (2 - t) * BLK)] for t in range(3)]
    return jnp.moveaxis(jnp.stack(tiles, axis=0), 3, 0)


def _bias_decode(bias_cols, q_pos, n_blocks):
    k_pos = jnp.arange(n_blocks * BLK, dtype=I32)
    b = bias_cols[_t5_bucket(q_pos - k_pos)]
    b = jnp.where((k_pos <= q_pos)[:, None], b, NEG)
    return jnp.moveaxis(b.reshape(n_blocks, BLK, -1), 2, 1)


def _score_keys(s):
    bits = pltpu.bitcast(s, I32)
    return jnp.where(bits < 0, bits ^ jnp.int32(0x7FFFFFFF), bits)


def _select_topk(key_ref, nkb, topk, idx_bits):
    rows = key_ref.shape[1]
    col = lax.broadcasted_iota(I32, (rows, BLK), 1)

    def count(pred):
        def body(kb, acc):
            return acc + jnp.where(pred(key_ref[kb], kb * BLK + col), 1.0, 0.0)
        acc = lax.fori_loop(0, nkb, body, jnp.zeros((rows, BLK), F32))
        return jnp.sum(acc, axis=1, keepdims=True)

    kf = float(topk)

    def bit_body(i, lo):
        cand = lo + lax.shift_left(jnp.int32(1), jnp.int32(31) - i)
        cnt = count(lambda k, _: k >= cand)
        return jnp.where(cnt >= kf, cand, lo)

    thr = lax.fori_loop(0, 32, bit_body, jnp.full((rows, 1), -2 ** 31, I32))
    need = kf - count(lambda k, _: k > thr)

    def idx_body(i, p):
        cand = p + lax.shift_left(jnp.int32(1), jnp.int32(idx_bits - 1) - i)
        cnt = count(lambda k, ix: jnp.logical_and(k == thr, ix < cand))
        return jnp.where(cnt < need, cand, p)

    cut = lax.fori_loop(0, idx_bits, idx_body, jnp.zeros((rows, 1), I32))
    return thr, cut


def _in_topk(key, idx, thr, cut):
    return jnp.logical_or(key > thr, jnp.logical_and(key == thr, idx <= cut))


def _dsa_kernel(q_ref, qi_ref, sm_ref, kf_ref, vf_ref, smf_ref, bias_ref, o_ref, key_ref, msk_ref,
                *, topk, idx_bits):
    qb = pl.program_id(1)
    nkb = qb + 1
    row = lax.broadcasted_iota(I32, (BLK, BLK), 0)
    col = lax.broadcasted_iota(I32, (BLK, BLK), 1)
    q_pos = qb * BLK + row

    qi = qi_ref[0]
    wi = sm_ref[0][:, SM_WI:SM_WI + IDX_HEADS] * (IDX_HEADS ** -0.5)
    qi_split = [_split2(qi[:, h * IDX_DIM:(h + 1) * IDX_DIM]) for h in range(IDX_HEADS)]
    wi_cols = [wi[:, h:h + 1] for h in range(IDX_HEADS)]

    def score_body(kb, carry):
        off = pl.multiple_of(kb * BLK, BLK)
        ki_h, ki_l = _split2(smf_ref[0, pl.ds(off, BLK), :][:, SM_KI:SM_KI + IDX_DIM])
        s = jnp.zeros((BLK, BLK), F32)
        for h in range(IDX_HEADS):
            qh, ql = qi_split[h]
            d = _dg(qh, ki_h, _NT) + (_dg(qh, ki_l, _NT) + _dg(ql, ki_h, _NT))
            s = s + wi_cols[h] * jnp.maximum(d * (IDX_DIM ** -0.5), 0.0)
        s = jnp.where(kb * BLK + col <= q_pos, s, -jnp.inf)
        key_ref[kb] = _score_keys(s)
        return carry

    lax.fori_loop(0, nkb, score_body, 0)

    @pl.when(nkb * BLK <= topk)
    def _():
        def body(kb, carry):
            msk_ref[kb] = jnp.where(kb * BLK + col <= q_pos, 0.0, NEG)
            return carry
        lax.fori_loop(0, nkb, body, 0)

    @pl.when(nkb * BLK > topk)
    def _():
        thr, cut = _select_topk(key_ref, nkb, topk, idx_bits)

        def body(kb, carry):
            k_pos = kb * BLK + col
            sel = _in_topk(key_ref[kb], k_pos, thr, cut)
            msk_ref[kb] = jnp.where(k_pos <= q_pos, jnp.where(sel, 0.0, NEG), NEG)
            return carry
        lax.fori_loop(0, nkb, body, 0)

    q = q_ref[0] * (A_HEAD_DIM ** -0.5)
    gq = A_GROUP * BLK
    for n in range(A_KV_HEADS):
        qs = jnp.concatenate([q[:, (n * A_GROUP + g) * A_HEAD_DIM:(n * A_GROUP + g + 1) * A_HEAD_DIM]
                              for g in range(A_GROUP)], axis=0).astype(BF16)

        def att_body(kb, carry, n=n, qs=qs):
            m, l, acc = carry
            off = pl.multiple_of(kb * BLK, BLK)
            kblk = kf_ref[0, pl.ds(off, BLK), :][:, n * A_HEAD_DIM:(n + 1) * A_HEAD_DIM].astype(BF16)
            vblk = vf_ref[0, pl.ds(off, BLK), :][:, n * A_HEAD_DIM:(n + 1) * A_HEAD_DIM].astype(BF16)
            t = jnp.clip(kb - qb + 2, 0, 2)
            mk = msk_ref[kb]
            s = _dg(qs, kblk, _NT) + bias_ref[n, t] + jnp.concatenate([mk] * A_GROUP, axis=0)
            m_new = jnp.maximum(m, jnp.max(s, axis=1, keepdims=True))
            alpha = jnp.exp(m - m_new)
            p = jnp.exp(s - m_new)
            l = alpha * l + jnp.sum(p, axis=1, keepdims=True)
            acc = alpha * acc + _dg(p.astype(BF16), vblk)
            return m_new, l, acc

        m0 = jnp.full((gq, 1), NEG, F32)
        l0 = jnp.zeros((gq, 1), F32)
        a0 = jnp.zeros((gq, A_HEAD_DIM), F32)
        _, l, acc = lax.fori_loop(0, nkb, att_body, (m0, l0, a0))
        o = acc / l
        for g in range(A_GROUP):
            h = n * A_GROUP + g
            o_ref[0, :, h * A_HEAD_DIM:(h + 1) * A_HEAD_DIM] = o[g * BLK:(g + 1) * BLK, :]


def _dsa_prompt(q, qi, sm, k, v, bias_tiles, topk):
    b, t, _ = q.shape
    nb = t // BLK
    idx_bits = max(1, int(math.ceil(math.log2(t))))
    bt = bias_tiles.reshape(A_KV_HEADS, A_GROUP, 3, BLK, BLK)
    bt = jnp.moveaxis(bt, 1, 2).reshape(A_KV_HEADS, 3, A_GROUP * BLK, BLK)
    blk = lambda w: pl.BlockSpec((1, BLK, w), lambda bi, qb: (bi, qb, 0))
    full = lambda w: pl.BlockSpec((1, t, w), lambda bi, qb: (bi, 0, 0))
    return pl.pallas_call(
        functools.partial(_dsa_kernel, topk=topk, idx_bits=idx_bits),
        grid=(b, nb),
        in_specs=[blk(A_Q), blk(IDX_HEADS * IDX_DIM), blk(LANES), full(A_KV), full(A_KV), full(LANES),
                  pl.BlockSpec(bt.shape, lambda bi, qb: (0, 0, 0, 0))],
        out_specs=blk(A_Q),
        out_shape=jax.ShapeDtypeStruct((b, t, A_Q), F32),
        scratch_shapes=[pltpu.VMEM((nb, BLK, BLK), I32), pltpu.VMEM((nb, BLK, BLK), F32)],
        compiler_params=_cparams(("arbitrary", "arbitrary")),
        name="dsa_prompt",
    )(q, qi, sm, k, v, sm, bt)


def _gdn_kernel(qkv_ref, z_ref, sm_ref, cb_ref, s0_ref, cw_ref, par_ref, nw_ref, o_ref, sout_ref,
                ext_ref, st_ref, *, n_valid):
    t = pl.program_id(1)
    nt = pl.num_programs(1)
    c = CHUNK
    halo = SUBLANES

    @pl.when(t == 0)
    def _():
        ext_ref[0:halo, :] = cb_ref[0]
        st_ref[...] = s0_ref[0]

    x = qkv_ref[0]
    ext_ref[halo:halo + c, :] = x
    conv = None
    for j in range(CONV_W):
        start = halo - (CONV_W - 1) + j
        term = ext_ref[start:start + c, :] * cw_ref[j:j + 1, :]
        conv = term if conv is None else conv + term
    ext_ref[0:halo, :] = x[c - halo:c, :]
    conv = _silu(conv)

    rowc = lax.broadcasted_iota(I32, (c, 1), 0)
    valid = (t * c + rowc) < n_valid
    ri = lax.broadcasted_iota(I32, (c, c), 0)
    ci = lax.broadcasted_iota(I32, (c, c), 1)
    tri = ri >= ci
    stri = ri > ci
    eye = jnp.where(ri == ci, 1.0, 0.0)
    tri_bf = jnp.where(tri, 1.0, 0.0).astype(BF16)
    ones_bf = jnp.ones((c, c), BF16)

    sm = sm_ref[0]
    g_all = -jnp.exp(par_ref[0:1, :]) * _softplus(sm + par_ref[1:2, :])
    g_all = jnp.where(valid, g_all, 0.0)
    gc_all = _dot_exact_lhs(tri_bf, g_all)
    beta_all = jnp.where(valid, jax.nn.sigmoid(sm), 0.0)
    zz = z_ref[0]
    nw = nw_ref[...]

    for h in range(B_HEADS):
        qh = conv[:, h * B_KEY_DIM:(h + 1) * B_KEY_DIM]
        kh = conv[:, B_K + h * B_KEY_DIM:B_K + (h + 1) * B_KEY_DIM]
        vh = conv[:, 2 * B_K + h * B_VAL_DIM:2 * B_K + (h + 1) * B_VAL_DIM]
        qh = qh * lax.rsqrt(jnp.sum(qh * qh, axis=-1, keepdims=True) + EPS) * (B_KEY_DIM ** -0.5)
        kh = kh * lax.rsqrt(jnp.sum(kh * kh, axis=-1, keepdims=True) + EPS)
        qh = jnp.where(valid, qh, 0.0)
        kh = jnp.where(valid, kh, 0.0)
        vh = jnp.where(valid, vh, 0.0)
        gc = gc_all[:, SM_A + h:SM_A + h + 1]
        beta = beta_all[:, SM_B + h:SM_B + h + 1]
        gc_row = _dot_exact_lhs(ones_bf, eye * gc)
        decay = jnp.exp(jnp.where(tri, gc - gc_row, -jnp.inf))
        kb = kh * beta
        vb = vh * beta
        a_mat = jnp.where(stri, _dot3(kb, kh, _NT) * decay, 0.0)
        tm = eye - a_mat
        pw = _dot3(a_mat, a_mat)
        steps = int(math.log2(c))
        for j in range(1, steps):
            tm = tm + _dot3(tm, pw)
            if j < steps - 1:
                pw = _dot3(pw, pw)
        eg = jnp.exp(gc)
        u = _dot3(tm, vb)
        w = _dot3(tm, kb * eg)
        a_qk = jnp.where(tri, _dot3(qh, kh, _NT) * decay, 0.0)

        s_old = st_ref[h]
        v_new = u - _dot3(w, s_old)
        o = _dot3(qh * eg, s_old) + _dot3(a_qk, v_new)
        g_last = gc[c - 1:c, :]
        s_new = s_old * jnp.exp(g_last) + _dot3(kh * jnp.exp(g_last - gc), v_new, _TN)
        st_ref[h] = s_new

        ms = jnp.mean(o * o, axis=-1, keepdims=True)
        zh = zz[:, h * B_VAL_DIM:(h + 1) * B_VAL_DIM]
        o_ref[0, :, h * B_VAL_DIM:(h + 1) * B_VAL_DIM] = (o * lax.rsqrt(ms + EPS) * nw) * _silu(zh)

    @pl.when(t == nt - 1)
    def _():
        sout_ref[0] = st_ref[...]


def _gdn(qkv, z, sm, conv_buf8, s0, conv_w, a_log, dt_bias, norm_w, n_valid):
    b, t, _ = qkv.shape
    nt = t // CHUNK
    par = jnp.zeros((SUBLANES, LANES), F32)
    par = par.at[0, SM_A:SM_A + B_HEADS].set(a_log).at[1, SM_A:SM_A + B_HEADS].set(dt_bias)
    tok = lambda w: pl.BlockSpec((1, CHUNK, w), lambda bi, ti: (bi, ti, 0))
    const2 = lambda shp: pl.BlockSpec(shp, lambda bi, ti: (0, 0))
    st_spec = pl.BlockSpec((1, B_HEADS, B_KEY_DIM, B_VAL_DIM), lambda bi, ti: (bi, 0, 0, 0))
    o, s_out = pl.pallas_call(
        functools.partial(_gdn_kernel, n_valid=n_valid),
        grid=(b, nt),
        in_specs=[tok(B_CONV_DIM), tok(B_V), tok(LANES),
                  pl.BlockSpec((1, SUBLANES, B_CONV_DIM), lambda bi, ti: (bi, 0, 0)),
                  st_spec, const2((CONV_W, B_CONV_DIM)), const2((SUBLANES, LANES)), const2((1, B_VAL_DIM))],
        out_specs=[tok(B_V), st_spec],
        out_shape=[jax.ShapeDtypeStruct((b, t, B_V), F32),
                   jax.ShapeDtypeStruct((b, B_HEADS, B_KEY_DIM, B_VAL_DIM), F32)],
        scratch_shapes=[pltpu.VMEM((SUBLANES + CHUNK, B_CONV_DIM), F32),
                        pltpu.VMEM((B_HEADS, B_KEY_DIM, B_VAL_DIM), F32)],
        compiler_params=_cparams(("arbitrary", "arbitrary")),
        name="gdn",
    )(qkv, z, sm, conv_buf8, s0, conv_w, par, norm_w.reshape(1, B_VAL_DIM))
    return o, s_out


def _diff_kernel(lam_ref, q_ref, k_ref, v_ref, bias_ref, sw_ref, o_ref, *, lam_init):
    qb = pl.program_id(2)
    nkb = qb + 1
    row = lax.broadcasted_iota(I32, (BLK, BLK), 0)
    col = lax.broadcasted_iota(I32, (BLK, BLK), 1)
    q = q_ref[0] * (C_HEAD_DIM ** -0.5)
    q2 = [q[:, cc * C_HEAD_DIM:(cc + 1) * C_HEAD_DIM].astype(BF16) for cc in range(2)]

    def body(kb, carry):
        off = pl.multiple_of(kb * BLK, BLK)
        kblk = k_ref[0, pl.ds(off, BLK), :]
        vblk = v_ref[0, pl.ds(off, BLK), :].astype(BF16)
        t = jnp.clip(kb - qb + 2, 0, 2)
        bias = jnp.where(kb * BLK + col <= qb * BLK + row, bias_ref[0, t], NEG)
        out = []
        for cc in range(2):
            m, l, acc = carry[3 * cc:3 * cc + 3]
            s = _dg(q2[cc], kblk[:, cc * C_HEAD_DIM:(cc + 1) * C_HEAD_DIM].astype(BF16), _NT) + bias
            m_new = jnp.maximum(m, jnp.max(s, axis=1, keepdims=True))
            alpha = jnp.exp(m - m_new)
            p = jnp.exp(s - m_new)
            l = alpha * l + jnp.sum(p, axis=1, keepdims=True)
            acc = alpha * acc + _dg(p.astype(BF16), vblk)
            out += [m_new, l, acc]
        return tuple(out)

    init = (jnp.full((BLK, 1), NEG, F32), jnp.zeros((BLK, 1), F32), jnp.zeros((BLK, 2 * C_HEAD_DIM), F32)) * 2
    _, l0, a0, _, l1, a1 = lax.fori_loop(0, nkb, body, init)
    o = a0 / l0 - lam_ref[0] * (a1 / l1)
    ms = jnp.mean(o * o, axis=-1, keepdims=True)
    o_ref[0] = (o * lax.rsqrt(ms + EPS) * sw_ref[...]) * (1.0 - lam_init)


def _diff_prompt(q, k, v, bias_tiles, lam, subln_w, lam_init):
    b, t, _ = q.shape
    nb = t // BLK
    hd = 2 * C_HEAD_DIM
    return pl.pallas_call(
        functools.partial(_diff_kernel, lam_init=lam_init),
        grid=(b, C_HEADS, nb),
        in_specs=[pl.BlockSpec(memory_space=pltpu.SMEM),
                  pl.BlockSpec((1, BLK, hd), lambda bi, h, qb: (bi, qb, h)),
                  pl.BlockSpec((1, t, hd), lambda bi, h, qb: (bi, 0, h)),
                  pl.BlockSpec((1, t, hd), lambda bi, h, qb: (bi, 0, h)),
                  pl.BlockSpec((1, 3, BLK, BLK), lambda bi, h, qb: (h, 0, 0, 0)),
                  pl.BlockSpec((1, hd), lambda bi, h, qb: (0, 0))],
        out_specs=pl.BlockSpec((1, BLK, hd), lambda bi, h, qb: (bi, qb, h)),
        out_shape=jax.ShapeDtypeStruct((b, t, C_V), F32),
        compiler_params=_cparams(("arbitrary", "arbitrary", "arbitrary")),
        name="diff_prompt",
    )(lam.reshape(1), q, k, v, bias_tiles, subln_w.reshape(1, hd))


def _diff_dec_kernel(pt_ref, lam_ref, q_ref, kn_ref, vn_ref, kp_ref, vp_ref, bias_ref, sw_ref, o_ref,
                     m_ref, l_ref, acc_ref, *, n_pages, lam_init):
    p = pl.program_id(1)
    nr = 2 * C_HEADS
    hd = 2 * C_HEAD_DIM

    @pl.when(p == 0)
    def _():
        m_ref[...] = jnp.full(m_ref.shape, NEG, F32)
        l_ref[...] = jnp.zeros(l_ref.shape, F32)
        acc_ref[...] = jnp.zeros(acc_ref.shape, F32)

    q = q_ref[0] * (C_HEAD_DIM ** -0.5)
    rowi = lax.broadcasted_iota(I32, (nr, C_QK), 0)
    lane = lax.broadcasted_iota(I32, (nr, C_QK), 1)
    qt = jnp.where(lane // C_HEAD_DIM == rowi, q, 0.0).astype(BF16)

    def update(kblk, vblk):
        s = _dg(qt, kblk.astype(BF16), _NT) + bias_ref[0]
        m = m_ref[...]
        m_new = jnp.maximum(m, jnp.max(s, axis=1, keepdims=True))
        alpha = jnp.exp(m - m_new)
        pr = jnp.exp(s - m_new)
        l_ref[...] = alpha * l_ref[...] + jnp.sum(pr, axis=1, keepdims=True)
        acc_ref[...] = alpha * acc_ref[...] + _dg(pr.astype(BF16), vblk.astype(BF16))
        m_ref[...] = m_new

    @pl.when(p < n_pages)
    def _():
        update(kp_ref[0, 0], vp_ref[0, 0])

    @pl.when(p == n_pages)
    def _():
        update(jnp.broadcast_to(kn_ref[0], (BLK, C_QK)), jnp.broadcast_to(vn_ref[0], (BLK, C_V)))
        on = acc_ref[...] / l_ref[...]
        sw = sw_ref[...]
        for h in range(C_HEADS):
            o = on[2 * h:2 * h + 1, h * hd:(h + 1) * hd] - lam_ref[0] * on[2 * h + 1:2 * h + 2, h * hd:(h + 1) * hd]
            ms = jnp.mean(o * o, axis=-1, keepdims=True)
            o_ref[0, :, h * hd:(h + 1) * hd] = (o * lax.rsqrt(ms + EPS) * sw) * (1.0 - lam_init)


def _diff_decode(q, k_new, v_new, cache_k, cache_v, j, pt_flat, n_pages, bias_dec, lam, subln_w, lam_init):
    b = q.shape[0]
    hd = 2 * C_HEAD_DIM
    row = pl.BlockSpec((1, 1, C_QK), lambda bi, p, pt: (bi, 0, 0))
    page = pl.BlockSpec((1, 1, BLK, C_QK),
                        lambda bi, p, pt: (j, pt[bi * n_pages + jnp.minimum(p, n_pages - 1)], 0, 0))
    grid_spec = pltpu.PrefetchScalarGridSpec(
        num_scalar_prefetch=1,
        grid=(b, n_pages + 1),
        in_specs=[pl.BlockSpec(memory_space=pltpu.SMEM), row, row, row, page, page,
                  pl.BlockSpec((1, 2 * C_HEADS, BLK), lambda bi, p, pt: (p, 0, 0)),
                  pl.BlockSpec((1, hd), lambda bi, p, pt: (0, 0))],
        out_specs=row,
        scratch_shapes=[pltpu.VMEM((2 * C_HEADS, 1), F32), pltpu.VMEM((2 * C_HEADS, 1), F32),
                        pltpu.VMEM((2 * C_HEADS, C_V), F32)],
    )
    return pl.pallas_call(
        functools.partial(_diff_dec_kernel, n_pages=n_pages, lam_init=lam_init),
        grid_spec=grid_spec,
        out_shape=jax.ShapeDtypeStruct((b, 1, C_V), F32),
        compiler_params=_cparams(("arbitrary", "arbitrary")),
        name="diff_decode",
    )(pt_flat, lam.reshape(1), q, k_new, v_new, cache_k, cache_v, bias_dec, subln_w.reshape(1, hd))


def _idx_dec_kernel(pt_ref, qi_ref, sm_ref, kp_ref, o_ref, *, n_pages):
    p = pl.program_id(1)
    qi = qi_ref[0]
    sm = sm_ref[0]
    rowi = lax.broadcasted_iota(I32, (SUBLANES, IDX_HEADS * IDX_DIM), 0)
    lane = lax.broadcasted_iota(I32, (SUBLANES, IDX_HEADS * IDX_DIM), 1)
    qsel = jnp.where(lane // IDX_DIM == rowi, qi, 0.0)
    qt = qsel[:, 0:IDX_DIM]
    for h in range(1, IDX_HEADS):
        qt = qt + qsel[:, h * IDX_DIM:(h + 1) * IDX_DIM]
    r8 = lax.broadcasted_iota(I32, (SUBLANES, LANES), 0)
    l8 = lax.broadcasted_iota(I32, (SUBLANES, LANES), 1)
    wsel = jnp.where(jnp.logical_and(r8 < IDX_HEADS, l8 == r8 + SM_WI), sm, 0.0)
    wcol = jnp.sum(wsel, axis=1, keepdims=True) * (IDX_HEADS ** -0.5)

    def emit(ki):
        d = _dot3(qt, ki, _NT)
        rel = jnp.maximum(d * (IDX_DIM ** -0.5), 0.0)
        o_ref[0, 0] = jnp.sum(wcol * rel, axis=0, keepdims=True)

    @pl.when(p < n_pages)
    def _():
        emit(kp_ref[0, 0])

    @pl.when(p == n_pages)
    def _():
        emit(jnp.broadcast_to(sm[:, SM_KI:SM_KI + IDX_DIM], (BLK, IDX_DIM)))


def _idx_decode(qi, sm, cache_kidx, j, pt_flat, n_pages):
    b = qi.shape[0]
    grid_spec = pltpu.PrefetchScalarGridSpec(
        num_scalar_prefetch=1,
        grid=(b, n_pages + 1),
        in_specs=[pl.BlockSpec((1, 1, IDX_HEADS * IDX_DIM), lambda bi, p, pt: (bi, 0, 0)),
                  pl.BlockSpec((1, 1, LANES), lambda bi, p, pt: (bi, 0, 0)),
                  pl.BlockSpec((1, 1, BLK, IDX_DIM),
                               lambda bi, p, pt: (j, pt[bi * n_pages + jnp.minimum(p, n_pages - 1)], 0, 0))],
        out_specs=pl.BlockSpec((1, 1, 1, BLK), lambda bi, p, pt: (bi, p, 0, 0)),
    )
    return pl.pallas_call(
        functools.partial(_idx_dec_kernel, n_pages=n_pages),
        grid_spec=grid_spec,
        out_shape=jax.ShapeDtypeStruct((b, n_pages + 1, 1, BLK), F32),
        compiler_params=_cparams(("arbitrary", "arbitrary")),
        name="idx_decode",
    )(pt_flat, qi, sm, cache_kidx)


def _sel_dec_kernel(s_ref, o_ref, key_ref, *, n_valid, topk, idx_bits):
    nkb, rows, _ = s_ref.shape
    col = lax.broadcasted_iota(I32, (rows, BLK), 1)
    for kb in range(nkb):
        s = jnp.where(kb * BLK + col < n_valid, s_ref[kb], -jnp.inf)
        key_ref[kb] = _score_keys(s)
    thr, cut = _select_topk(key_ref, nkb, topk, idx_bits)
    for kb in range(nkb):
        k_pos = kb * BLK + col
        sel = _in_topk(key_ref[kb], k_pos, thr, cut)
        o_ref[kb] = jnp.where(k_pos < n_valid, jnp.where(sel, 0.0, NEG), NEG)


def _sel_decode(scores, n_valid, topk):
    nkb, rows, _ = scores.shape
    idx_bits = max(1, int(math.ceil(math.log2(nkb * BLK))))
    return pl.pallas_call(
        functools.partial(_sel_dec_kernel, n_valid=n_valid, topk=topk, idx_bits=idx_bits),
        out_shape=jax.ShapeDtypeStruct(scores.shape, F32),
        scratch_shapes=[pltpu.VMEM(scores.shape, I32)],
        compiler_params=pltpu.CompilerParams(vmem_limit_bytes=VMEM_LIMIT),
        name="sel_decode",
    )(scores)


def _dsa_dec_kernel(pt_ref, q_ref, kn_ref, vn_ref, kp_ref, vp_ref, msk_ref, bias_ref, o_ref,
                    m_ref, l_ref, acc_ref, *, n_pages):
    p = pl.program_id(1)

    @pl.when(p == 0)
    def _():
        m_ref[...] = jnp.full(m_ref.shape, NEG, F32)
        l_ref[...] = jnp.zeros(l_ref.shape, F32)
        acc_ref[...] = jnp.zeros(acc_ref.shape, F32)

    q = q_ref[0] * (A_HEAD_DIM ** -0.5)
    rowi = lax.broadcasted_iota(I32, (A_HEADS, A_Q), 0)
    lane = lax.broadcasted_iota(I32, (A_HEADS, A_Q), 1)
    qsel = jnp.where(lane // A_HEAD_DIM == rowi, q, 0.0)
    halves = []
    for n in range(A_KV_HEADS):
        acc = None
        for g in range(A_GROUP):
            h = n * A_GROUP + g
            part = qsel[:, h * A_HEAD_DIM:(h + 1) * A_HEAD_DIM]
            acc = part if acc is None else acc + part
        halves.append(acc)
    qt = jnp.concatenate(halves, axis=1).astype(BF16)

    def update(kblk, vblk):
        s = _dg(qt, kblk.astype(BF16), _NT) + bias_ref[0] + msk_ref[0, 0]
        m = m_ref[...]
        m_new = jnp.maximum(m, jnp.max(s, axis=1, keepdims=True))
        alpha = jnp.exp(m - m_new)
        pr = jnp.exp(s - m_new)
        l_ref[...] = alpha * l_ref[...] + jnp.sum(pr, axis=1, keepdims=True)
        acc_ref[...] = alpha * acc_ref[...] + _dg(pr.astype(BF16), vblk.astype(BF16))
        m_ref[...] = m_new

    @pl.when(p < n_pages)
    def _():
        update(kp_ref[0, 0], vp_ref[0, 0])

    @pl.when(p == n_pages)
    def _():
        update(jnp.broadcast_to(kn_ref[0], (BLK, A_KV)), jnp.broadcast_to(vn_ref[0], (BLK, A_KV)))
        on = acc_ref[...] / l_ref[...]
        for h in range(A_HEADS):
            n = h // A_GROUP
            o_ref[0, :, h * A_HEAD_DIM:(h + 1) * A_HEAD_DIM] = on[h:h + 1, n * A_HEAD_DIM:(n + 1) * A_HEAD_DIM]


def _dsa_decode(q, k_new, v_new, cache_k, cache_v, j, pt_flat, n_pages, mask, bias_dec):
    b = q.shape[0]
    page = pl.BlockSpec((1, 1, BLK, A_KV),
                        lambda bi, p, pt: (j, pt[bi * n_pages + jnp.minimum(p, n_pages - 1)], 0, 0))
    rowspec = lambda w: pl.BlockSpec((1, 1, w), lambda bi, p, pt: (bi, 0, 0))
    grid_spec = pltpu.PrefetchScalarGridSpec(
        num_scalar_prefetch=1,
        grid=(b, n_pages + 1),
        in_specs=[rowspec(A_Q), rowspec(A_KV), rowspec(A_KV), page, page,
                  pl.BlockSpec((1, 1, 1, BLK), lambda bi, p, pt: (p, bi, 0, 0)),
                  pl.BlockSpec((1, A_HEADS, BLK), lambda bi, p, pt: (p, 0, 0))],
        out_specs=rowspec(A_Q),
        scratch_shapes=[pltpu.VMEM((A_HEADS, 1), F32), pltpu.VMEM((A_HEADS, 1), F32),
                        pltpu.VMEM((A_HEADS, A_KV), F32)],
    )
    return pl.pallas_call(
        functools.partial(_dsa_dec_kernel, n_pages=n_pages),
        grid_spec=grid_spec,
        out_shape=jax.ShapeDtypeStruct((b, 1, A_Q), F32),
        compiler_params=_cparams(("arbitrary", "arbitrary")),
        name="dsa_decode",
    )(pt_flat, q, k_new, v_new, cache_k, cache_v, mask, bias_dec)


def _even_weights(w_in):
    sizes = (A_Q, A_KV, A_KV, IDX_HEADS * IDX_DIM, IDX_DIM, IDX_HEADS, B_CONV_DIM, B_V, B_HEADS, B_HEADS)
    offs = np.concatenate([[0], np.cumsum(sizes)])
    seg = lambda i: w_in[:, int(offs[i]):int(offs[i + 1])]
    pad = LANES - (IDX_DIM + IDX_HEADS + 2 * B_HEADS)
    small = jnp.concatenate([seg(4), seg(5), seg(8), seg(9), jnp.zeros((w_in.shape[0], pad), w_in.dtype)], axis=1)
    return jnp.concatenate([seg(0), seg(1), seg(2), seg(3), small, seg(6), seg(7)], axis=1).astype(BF16)


_EVEN_WIDTHS = (A_Q, A_KV, A_KV, IDX_HEADS * IDX_DIM, LANES, B_CONV_DIM, B_V)
_ODD_WIDTHS = (C_QK, C_QK, C_V)


def _pad_rows(a, rows):
    return jnp.pad(a, ((0, 0), (0, rows - a.shape[1]), (0, 0)))


def kernel(x_prompt, x_sample, c_prompt, c_sample, cache_A_k, cache_A_v, cache_A_kidx, cache_C_k, cache_C_v, state_B_ssm, state_B_conv, page_table, rel_bias, ada_w, ada_b, norm_w, final_norm_w, ffn_w1, ffn_w2, ab_w_in, ab_w_out, gdn_conv_w, gdn_a_log, gdn_dt_bias, gdn_norm_w, c_w_in, c_w_out, c_lambda_q1, c_lambda_k1, c_lambda_q2, c_lambda_k2, c_subln_w):
    depth = ada_w.shape[0]
    bp, tp, d = x_prompt.shape
    bs, ts, _ = x_sample.shape
    assert ts == 1 and tp % BLK == 0 and tp % CHUNK == 0
    n_pages = page_table.shape[1]
    page = cache_A_k.shape[2]
    assert page == BLK
    n_phys = cache_A_k.shape[1]
    past_len = n_pages * page
    topk_p = min(TOPK_MAX, tp // 4)
    topk_s = min(TOPK_MAX, (past_len + ts) // 4)
    pt_flat = page_table.reshape(-1).astype(I32)

    n_c = bp + bs
    n_c_pad = -(-n_c // SUBLANES) * SUBLANES
    c_all = jnp.pad(jnp.concatenate([c_prompt, c_sample], axis=0), ((0, n_c_pad - n_c), (0, 0)))
    mod_all = _modulation(c_all, ada_w, ada_b)

    w1b = ffn_w1.astype(BF16)
    w2b = ffn_w2.astype(BF16)
    ab_in_b = [_even_weights(ab_w_in[j]) for j in range(ab_w_in.shape[0])]
    ab_out_b = ab_w_out.astype(BF16)
    c_in_b = c_w_in.astype(BF16)
    c_out_b = c_w_out.astype(BF16)

    bias_a = _bias_tiles(rel_bias[:, :A_HEADS])
    bias_c = _bias_tiles(rel_bias[:, A_HEADS:])
    bias_a_dec = _bias_decode(rel_bias[:, :A_HEADS], past_len, n_pages + 1)
    bias_c_dec = jnp.repeat(_bias_decode(rel_bias[:, A_HEADS:], past_len, n_pages + 1), 2, axis=1)

    cache_a_k = cache_A_k.reshape(cache_A_k.shape[0], n_phys, page, A_KV)
    cache_a_v = cache_A_v.reshape(cache_A_v.shape[0], n_phys, page, A_KV)
    cache_c_k = cache_C_k.reshape(cache_C_k.shape[0], n_phys, page, C_QK)
    cache_c_v = cache_C_v.reshape(cache_C_v.shape[0], n_phys, page, C_V)

    xp = x_prompt
    xs = x_sample.reshape(1, bs, d)
    new_p = [[] for _ in range(7)]
    new_s = [[] for _ in range(7)]

    for i in range(depth):
        j = i // 2
        mod_p = mod_all[i, :bp].reshape(bp, 1, N_MOD * d)
        mod_s = mod_all[i, bp:bp + bs].reshape(1, bs, N_MOD * d)
        xp = _ffn(xp, mod_p, 0, norm_w[i, 0], w1b[i, 0], w2b[i, 0])
        xs = _ffn(xs, mod_s, 0, norm_w[i, 0], w1b[i, 0], w2b[i, 0])
        if i % 2 == 0:
            q, k, v, qi, sm, qkv, z = _inproj(xp, mod_p, 3, norm_w[i, 1], ab_in_b[j], _EVEN_WIDTHS)
            o_a = _dsa_prompt(q, qi, sm, k, v, bias_a, topk_p)
            o_b, s_new = _gdn(qkv, z, sm, jnp.zeros((bp, SUBLANES, B_CONV_DIM), F32),
                              jnp.zeros((bp, B_HEADS, B_KEY_DIM, B_VAL_DIM), F32),
                              gdn_conv_w[j], gdn_a_log[j], gdn_dt_bias[j], gdn_norm_w[j], tp)
            xp = _outproj(xp, mod_p, 5, ab_out_b[j], [o_a, o_b])
            new_p[0].append(k.reshape(bp, tp, A_KV_HEADS, A_HEAD_DIM))
            new_p[1].append(v.reshape(bp, tp, A_KV_HEADS, A_HEAD_DIM))
            new_p[2].append(sm[:, :, SM_KI:SM_KI + IDX_DIM])
            new_p[3].append(s_new)
            new_p[4].append(qkv[:, tp - (CONV_W - 1):, :])
            q, k, v, qi, sm, qkv, z = _inproj(xs, mod_s, 3, norm_w[i, 1], ab_in_b[j], _EVEN_WIDTHS)
            as_rows = lambda a: a.reshape(bs, 1, a.shape[-1])
            scores = _idx_decode(as_rows(qi), as_rows(sm), cache_A_kidx, j, pt_flat, n_pages)
            scores = jnp.moveaxis(scores.reshape(bs, n_pages + 1, BLK), 1, 0)
            mask = _sel_decode(scores, past_len + 1, topk_s).reshape(n_pages + 1, bs, 1, BLK)
            o_a = _dsa_decode(as_rows(q), as_rows(k), as_rows(v), cache_a_k, cache_a_v, j, pt_flat, n_pages,
                              mask, bias_a_dec)
            conv_buf = state_B_conv[j]
            cb8 = jnp.pad(conv_buf, ((0, 0), (SUBLANES - (CONV_W - 1), 0), (0, 0)))
            o_b, s_new = _gdn(_pad_rows(as_rows(qkv), CHUNK), _pad_rows(as_rows(z), CHUNK),
                              _pad_rows(as_rows(sm), CHUNK), cb8, state_B_ssm[j],
                              gdn_conv_w[j], gdn_a_log[j], gdn_dt_bias[j], gdn_norm_w[j], 1)
            xs = _outproj(xs, mod_s, 5, ab_out_b[j], [o_a.reshape(1, bs, A_Q), o_b[:, 0, :].reshape(1, bs, B_V)])
            new_s[0].append(k.reshape(bs, 1, A_KV_HEADS, A_HEAD_DIM))
            new_s[1].append(v.reshape(bs, 1, A_KV_HEADS, A_HEAD_DIM))
            new_s[2].append(sm.reshape(bs, 1, LANES)[:, :, SM_KI:SM_KI + IDX_DIM])
            new_s[3].append(s_new)
            new_s[4].append(jnp.concatenate([conv_buf, as_rows(qkv)], axis=1)[:, 1:, :])
        else:
            lam_init = 0.8 - 0.6 * math.exp(-0.3 * i)
            lam = (jnp.exp(jnp.sum(c_lambda_q1[j] * c_lambda_k1[j]))
                   - jnp.exp(jnp.sum(c_lambda_q2[j] * c_lambda_k2[j])) + lam_init).astype(F32)
            q, k, v = _inproj(xp, mod_p, 3, norm_w[i, 1], c_in_b[j], _ODD_WIDTHS)
            o = _diff_prompt(q, k, v, bias_c, lam, c_subln_w[j], lam_init)
            xp = _outproj(xp, mod_p, 5, c_out_b[j], [o])
            new_p[5].append(k.reshape(bp, tp, C_HEADS, 2 * C_HEAD_DIM))
            new_p[6].append(v.reshape(bp, tp, C_HEADS, 2 * C_HEAD_DIM))
            q, k, v = _inproj(xs, mod_s, 3, norm_w[i, 1], c_in_b[j], _ODD_WIDTHS)
            as_rows = lambda a: a.reshape(bs, 1, a.shape[-1])
            o = _diff_decode(as_rows(q), as_rows(k), as_rows(v), cache_c_k, cache_c_v, j, pt_flat, n_pages,
                             bias_c_dec, lam, c_subln_w[j], lam_init)
            xs = _outproj(xs, mod_s, 5, c_out_b[j], [o.reshape(1, bs, C_V)])
            new_s[5].append(k.reshape(bs, 1, C_HEADS, 2 * C_HEAD_DIM))
            new_s[6].append(v.reshape(bs, 1, C_HEADS, 2 * C_HEAD_DIM))
        xp = _ffn(xp, mod_p, 6, norm_w[i, 2], w1b[i, 1], w2b[i, 1])
        xs = _ffn(xs, mod_s, 6, norm_w[i, 2], w1b[i, 1], w2b[i, 1])

    y_prompt = _final_norm(xp, final_norm_w)
    y_sample = _final_norm(xs, final_norm_w).reshape(bs, 1, d)
    sp = [jnp.stack(lst) for lst in new_p]
    ss = [jnp.stack(lst) for lst in new_s]
    return (y_prompt, y_sample, *sp, *ss)
```

```python
import functools
import math

import numpy as np
import jax
import jax.numpy as jnp
from jax import lax
from jax.experimental import pallas as pl
from jax.experimental.pallas import tpu as pltpu

F32 = jnp.float32
BF16 = jnp.bfloat16
I32 = jnp.int32

A_HEADS = 8
A_KV_HEADS = 2
A_GROUP = A_HEADS // A_KV_HEADS
A_HEAD_DIM = 64
IDX_HEADS = 4
IDX_DIM = 64
TOPK_MAX = 256
B_HEADS = 4
B_KEY_DIM = 128
B_VAL_DIM = 128
CONV_W = 4
CHUNK = 64
C_HEADS = 8
C_HEAD_DIM = 64
NUM_BUCKETS = 32
MAX_DISTANCE = 128
N_MOD = 9
EPS = 1e-6

A_Q = A_HEADS * A_HEAD_DIM
A_KV = A_KV_HEADS * A_HEAD_DIM
B_K = B_HEADS * B_KEY_DIM
B_V = B_HEADS * B_VAL_DIM
B_CONV_DIM = 2 * B_K + B_V
C_QK = C_HEADS * 2 * C_HEAD_DIM
C_V = C_HEADS * 2 * C_HEAD_DIM
C_HD = 2 * C_HEAD_DIM

LANES = 128
SUBLANES = 8
VMEM_LIMIT = 56 * 1024 * 1024

BLK = 128
TQ = 256
NEG = -1e30

SM_KI = 0
SM_WI = IDX_DIM
SM_A = SM_WI + IDX_HEADS
SM_B = SM_A + B_HEADS

_NT = (((1,), (1,)), ((), ()))
_NN = (((1,), (0,)), ((), ()))
_TN = (((0,), (0,)), ((), ()))


def _cparams(sem):
    return pltpu.CompilerParams(dimension_semantics=sem, vmem_limit_bytes=VMEM_LIMIT)


def _dg(a, b, dims=_NN):
    return lax.dot_general(a, b, dims, preferred_element_type=F32)


def _dot1(a, b, dims=_NN):
    return _dg(a.astype(BF16), b.astype(BF16), dims)


def _split2(x):
    hi = x.astype(BF16)
    lo = (x - hi.astype(F32)).astype(BF16)
    return hi, lo


def _split3(x):
    b1 = x.astype(BF16)
    r1 = x - b1.astype(F32)
    b2 = r1.astype(BF16)
    b3 = (r1 - b2.astype(F32)).astype(BF16)
    return b1, b2, b3


def _dot3(a, b, dims=_NN):
    ah, al = _split2(a)
    bh, bl = _split2(b)
    return _dg(ah, bh, dims) + (_dg(ah, bl, dims) + _dg(al, bh, dims))


def _dot_exact_lhs(a_bf, b, dims=_NN):
    b1, b2, b3 = _split3(b)
    return _dg(a_bf, b1, dims) + (_dg(a_bf, b2, dims) + _dg(a_bf, b3, dims))


def _dot_exact_rhs(a, b_bf, dims=_NN):
    a1, a2, a3 = _split3(a)
    return _dg(a1, b_bf, dims) + (_dg(a2, b_bf, dims) + _dg(a3, b_bf, dims))


def _silu(x):
    return x * jax.nn.sigmoid(x)


def _softplus(x):
    return jnp.maximum(x, 0.0) + jnp.log(1.0 + jnp.exp(-jnp.abs(x)))


def _norm_mod(x, nw, sc, sh):
    ms = jnp.mean(x * x, axis=-1, keepdims=True)
    return (x * lax.rsqrt(ms + EPS) * nw) * (1.0 + sc) + sh


def _softmax_step(s, m, l, acc, pv):
    m_new = jnp.maximum(m, jnp.max(s, axis=1, keepdims=True))
    alpha = jnp.exp(m - m_new)
    p = jnp.exp(s - m_new)
    return m_new, alpha * l + jnp.sum(p, axis=1, keepdims=True), alpha * acc + pv(p)


def _mod_kernel(c_ref, w_ref, b_ref, o_ref):
    s = _silu(c_ref[...]).astype(BF16)
    o_ref[0] = _dg(s, w_ref[0].astype(BF16)) + b_ref[0]


def _modulation(c_all, ada_w, ada_b):
    depth, d, n = ada_w.shape
    m = c_all.shape[0]
    tn = 1024
    return pl.pallas_call(
        _mod_kernel,
        grid=(depth, n // tn),
        in_specs=[pl.BlockSpec((m, d), lambda i, j: (0, 0)),
                  pl.BlockSpec((1, d, tn), lambda i, j: (i, 0, j)),
                  pl.BlockSpec((1, 1, tn), lambda i, j: (i, 0, j))],
        out_specs=pl.BlockSpec((1, m, tn), lambda i, j: (i, 0, j)),
        out_shape=jax.ShapeDtypeStruct((depth, m, n), F32),
        compiler_params=_cparams(("arbitrary", "arbitrary")),
        name="adaln_mod",
    )(c_all, ada_w, ada_b.reshape(depth, 1, n))


def _row_tile(t):
    return min(512, t)


def _mod_spec(mod, tm, m):
    r = mod.shape[1]
    d = mod.shape[2] // N_MOD
    if r == 1:
        return pl.BlockSpec((1, 1, d), lambda s, t: (s, 0, m))
    return pl.BlockSpec((1, tm, d), lambda s, t: (s, t, m))


def _resident(shape):
    nd = len(shape)
    return pl.BlockSpec(shape, lambda s, t: (0,) * nd, pipeline_mode=pl.Buffered(1))


def _ffn_kernel(x_ref, sh_ref, sc_ref, g_ref, nw_ref, w1_ref, w2_ref, o_ref, acc_ref, *, fc):
    x = x_ref[0]
    hb = _norm_mod(x, nw_ref[...], sc_ref[0], sh_ref[0]).astype(BF16)
    f = w2_ref.shape[0]
    for c in range(f // fc):
        gt = _dg(hb, w1_ref[:, c * fc:(c + 1) * fc])
        up = _dg(hb, w1_ref[:, f + c * fc:f + (c + 1) * fc])
        a = (_silu(gt) * up).astype(BF16)
        contrib = _dg(a, w2_ref[c * fc:(c + 1) * fc, :])
        if c == 0:
            acc_ref[...] = contrib
        else:
            acc_ref[...] += contrib
    o_ref[0] = x + (0.5 * g_ref[0]) * acc_ref[...]


def _ffn(x, mod, m0, nw, w1b, w2b):
    s, t, d = x.shape
    tm = _row_tile(t)
    f = w2b.shape[0]
    fc = 256 if f % 256 == 0 else LANES
    xs = pl.BlockSpec((1, tm, d), lambda si, ti: (si, ti, 0))
    return pl.pallas_call(
        functools.partial(_ffn_kernel, fc=fc),
        grid=(s, t // tm),
        in_specs=[xs, _mod_spec(mod, tm, m0), _mod_spec(mod, tm, m0 + 1), _mod_spec(mod, tm, m0 + 2),
                  _resident((1, d)), _resident(w1b.shape), _resident(w2b.shape)],
        out_specs=xs,
        out_shape=jax.ShapeDtypeStruct(x.shape, F32),
        scratch_shapes=[pltpu.VMEM((tm, d), F32)],
        compiler_params=_cparams(("arbitrary", "arbitrary")),
        name="ffn",
    )(x, mod, mod, mod, nw.reshape(1, d), w1b, w2b)


def _inproj_kernel(x_ref, sh_ref, sc_ref, nw_ref, w_ref, *o_refs, outs):
    hb = _norm_mod(x_ref[0], nw_ref[...], sc_ref[0], sh_ref[0]).astype(BF16)
    done = {}
    for o_ref, (off, wd, dt) in zip(o_refs, outs):
        if (off, wd) not in done:
            done[(off, wd)] = _dg(hb, w_ref[:, off:off + wd])
        o_ref[0] = done[(off, wd)].astype(dt)


def _inproj(x, mod, m0, nw, wb, outs):
    s, t, d = x.shape
    tm = _row_tile(t)
    xs = pl.BlockSpec((1, tm, d), lambda si, ti: (si, ti, 0))
    return pl.pallas_call(
        functools.partial(_inproj_kernel, outs=tuple(outs)),
        grid=(s, t // tm),
        in_specs=[xs, _mod_spec(mod, tm, m0), _mod_spec(mod, tm, m0 + 1), _resident((1, d)), _resident(wb.shape)],
        out_specs=[pl.BlockSpec((1, tm, wd), lambda si, ti: (si, ti, 0)) for _, wd, _ in outs],
        out_shape=[jax.ShapeDtypeStruct((s, t, wd), dt) for _, wd, dt in outs],
        compiler_params=_cparams(("arbitrary", "arbitrary")),
        name="inproj",
    )(x, mod, mod, nw.reshape(1, d), wb)


def _outproj_kernel(*refs, widths):
    n = len(widths)
    x_ref, g_ref, w_ref = refs[0], refs[1], refs[2]
    a_refs = refs[3:3 + n]
    o_ref = refs[3 + n]
    acc = None
    off = 0
    for a_ref, wd in zip(a_refs, widths):
        part = _dg(a_ref[0].astype(BF16), w_ref[off:off + wd, :])
        acc = part if acc is None else acc + part
        off += wd
    o_ref[0] = x_ref[0] + g_ref[0] * acc


def _outproj(x, mod, mg, wb, parts):
    s, t, d = x.shape
    tm = _row_tile(t)
    widths = tuple(p.shape[-1] for p in parts)
    xs = pl.BlockSpec((1, tm, d), lambda si, ti: (si, ti, 0))
    return pl.pallas_call(
        functools.partial(_outproj_kernel, widths=widths),
        grid=(s, t // tm),
        in_specs=[xs, _mod_spec(mod, tm, mg), _resident(wb.shape)]
                 + [pl.BlockSpec((1, tm, wd), lambda si, ti: (si, ti, 0)) for wd in widths],
        out_specs=xs,
        out_shape=jax.ShapeDtypeStruct(x.shape, F32),
        compiler_params=_cparams(("arbitrary", "arbitrary")),
        name="outproj",
    )(x, mod, wb, *parts)


def _final_norm_kernel(x_ref, w_ref, o_ref):
    x = x_ref[0]
    ms = jnp.mean(x * x, axis=-1, keepdims=True)
    o_ref[0] = x * lax.rsqrt(ms + EPS) * w_ref[...]


def _final_norm(x, w):
    s, t, d = x.shape
    tm = _row_tile(t)
    xs = pl.BlockSpec((1, tm, d), lambda si, ti: (si, ti, 0))
    return pl.pallas_call(
        _final_norm_kernel,
        grid=(s, t // tm),
        in_specs=[xs, pl.BlockSpec((1, d), lambda si, ti: (0, 0))],
        out_specs=xs,
        out_shape=jax.ShapeDtypeStruct(x.shape, F32),
        compiler_params=_cparams(("arbitrary", "arbitrary")),
        name="final_norm",
    )(x, w.reshape(1, d))


def _t5_bucket(dist):
    n = jnp.maximum(dist, 0)
    max_exact = NUM_BUCKETS // 2
    nf = jnp.maximum(n, 1).astype(F32)
    large = max_exact + (jnp.log(nf / max_exact) / math.log(MAX_DISTANCE / max_exact)
                         * (NUM_BUCKETS - max_exact)).astype(I32)
    large = jnp.minimum(large, NUM_BUCKETS - 1)
    return jnp.where(n < max_exact, n, large)


def _bias_tiles(bias_cols, blk, causal):
    assert blk >= MAX_DISTANCE
    i = jnp.arange(blk, dtype=I32)[:, None]
    j = jnp.arange(blk, dtype=I32)[None, :]
    tiles = []
    for t in range(3):
        dist = i - j + (2 - t) * blk
        tile = bias_cols[_t5_bucket(dist)]
        if causal:
            tile = jnp.where((dist >= 0)[:, :, None], tile, NEG)
        tiles.append(tile)
    return jnp.moveaxis(jnp.stack(tiles, axis=0), 3, 0)


def _bias_decode(bias_cols, q_pos, n_blocks):
    k_pos = jnp.arange(n_blocks * BLK, dtype=I32)
    b = bias_cols[_t5_bucket(q_pos - k_pos)]
    b = jnp.where((k_pos <= q_pos)[:, None], b, NEG)
    return jnp.moveaxis(b.reshape(n_blocks, BLK, -1), 2, 1)


def _score_keys(s):
    bits = pltpu.bitcast(s, I32)
    return jnp.where(bits < 0, bits ^ jnp.int32(0x7FFFFFFF), bits)


def _select_topk(key_ref, nkb, topk, idx_bits):
    rows = key_ref.shape[1]
    col = lax.broadcasted_iota(I32, (rows, BLK), 1)

    def count(pred):
        def body(kb, acc):
            return acc + jnp.where(pred(key_ref[kb], kb * BLK + col), 1.0, 0.0)
        acc = lax.fori_loop(0, nkb, body, jnp.zeros((rows, BLK), F32))
        return jnp.sum(acc, axis=1, keepdims=True)

    kf = float(topk)

    def bit_body(i, lo):
        cand = lo + lax.shift_left(jnp.int32(1), jnp.int32(31) - i)
        cnt = count(lambda k, _: k >= cand)
        return jnp.where(cnt >= kf, cand, lo)

    thr = lax.fori_loop(0, 32, bit_body, jnp.full((rows, 1), -2 ** 31, I32))
    need = kf - count(lambda k, _: k > thr)

    def idx_body(i, p):
        cand = p + lax.shift_left(jnp.int32(1), jnp.int32(idx_bits - 1) - i)
        cnt = count(lambda k, ix: jnp.logical_and(k == thr, ix < cand))
        return jnp.where(cnt < need, cand, p)

    cut = lax.fori_loop(0, idx_bits, idx_body, jnp.zeros((rows, 1), I32))
    return thr, cut


def _in_topk(key, idx, thr, cut):
    return jnp.logical_or(key > thr, jnp.logical_and(key == thr, idx <= cut))


def _dsa_kernel(q_ref, qi_ref, sm_ref, kf_ref, vf_ref, smf_ref, bias_ref, o_ref, key_ref, msk_ref,
                *, topk, idx_bits):
    qb = pl.program_id(1)
    nkb = qb + 1
    row = lax.broadcasted_iota(I32, (BLK, BLK), 0)
    col = lax.broadcasted_iota(I32, (BLK, BLK), 1)
    q_pos = qb * BLK + row

    qi = qi_ref[0]
    wi = sm_ref[0][:, SM_WI:SM_WI + IDX_HEADS] * (IDX_HEADS ** -0.5)
    qi_split = [_split2(qi[:, h * IDX_DIM:(h + 1) * IDX_DIM]) for h in range(IDX_HEADS)]
    wi_cols = [wi[:, h:h + 1] for h in range(IDX_HEADS)]

    def score_body(kb, carry):
        off = pl.multiple_of(kb * BLK, BLK)
        ki_h, ki_l = _split2(smf_ref[0, pl.ds(off, BLK), :][:, SM_KI:SM_KI + IDX_DIM])
        s = jnp.zeros((BLK, BLK), F32)
        for h in range(IDX_HEADS):
            qh, ql = qi_split[h]
            d = _dg(qh, ki_h, _NT) + (_dg(qh, ki_l, _NT) + _dg(ql, ki_h, _NT))
            s = s + wi_cols[h] * jnp.maximum(d * (IDX_DIM ** -0.5), 0.0)
        s = jnp.where(kb * BLK + col <= q_pos, s, -jnp.inf)
        key_ref[kb] = _score_keys(s)
        return carry

    lax.fori_loop(0, nkb, score_body, 0)

    @pl.when(nkb * BLK <= topk)
    def _():
        def body(kb, carry):
            msk_ref[kb] = jnp.where(kb * BLK + col <= q_pos, 0.0, NEG)
            return carry
        lax.fori_loop(0, nkb, body, 0)

    @pl.when(nkb * BLK > topk)
    def _():
        thr, cut = _select_topk(key_ref, nkb, topk, idx_bits)

        def body(kb, carry):
            k_pos = kb * BLK + col
            sel = _in_topk(key_ref[kb], k_pos, thr, cut)
            msk_ref[kb] = jnp.where(k_pos <= q_pos, jnp.where(sel, 0.0, NEG), NEG)
            return carry
        lax.fori_loop(0, nkb, body, 0)

    q = q_ref[0] * (A_HEAD_DIM ** -0.5)
    gq = A_GROUP * BLK
    ns = range(A_KV_HEADS)
    qs = [jnp.concatenate([q[:, (n * A_GROUP + g) * A_HEAD_DIM:(n * A_GROUP + g + 1) * A_HEAD_DIM]
                           for g in range(A_GROUP)], axis=0).astype(BF16) for n in ns]

    def att_body(kb, carry):
        off = pl.multiple_of(kb * BLK, BLK)
        kblk = kf_ref[0, pl.ds(off, BLK), :].astype(BF16)
        vblk = vf_ref[0, pl.ds(off, BLK), :].astype(BF16)
        t = jnp.clip(kb - qb + 2, 0, 2)
        mk = msk_ref[kb]
        mk4 = jnp.concatenate([mk] * A_GROUP, axis=0)
        ss = [_dg(qs[n], kblk[:, n * A_HEAD_DIM:(n + 1) * A_HEAD_DIM], _NT) + bias_ref[n, t] + mk4 for n in ns]
        m_new = [jnp.maximum(carry[3 * n], jnp.max(ss[n], axis=1, keepdims=True)) for n in ns]
        ps = [jnp.exp(ss[n] - m_new[n]) for n in ns]
        pvs = [_dg(ps[n].astype(BF16), vblk[:, n * A_HEAD_DIM:(n + 1) * A_HEAD_DIM]) for n in ns]
        out = []
        for n in ns:
            alpha = jnp.exp(carry[3 * n] - m_new[n])
            out += [m_new[n], alpha * carry[3 * n + 1] + jnp.sum(ps[n], axis=1, keepdims=True),
                    alpha * carry[3 * n + 2] + pvs[n]]
        return tuple(out)

    init = (jnp.full((gq, 1), NEG, F32), jnp.zeros((gq, 1), F32), jnp.zeros((gq, A_HEAD_DIM), F32)) * A_KV_HEADS
    res = lax.fori_loop(0, nkb, att_body, init)
    for n in ns:
        o = res[3 * n + 2] / res[3 * n + 1]
        for g in range(A_GROUP):
            h = n * A_GROUP + g
            o_ref[0, :, h * A_HEAD_DIM:(h + 1) * A_HEAD_DIM] = o[g * BLK:(g + 1) * BLK, :]


def _dsa_prompt(q, qi, sm, k, v, bias_tiles, topk):
    b, t, _ = q.shape
    nb = t // BLK
    idx_bits = max(1, int(math.ceil(math.log2(t))))
    bt = bias_tiles.reshape(A_KV_HEADS, A_GROUP, 3, BLK, BLK)
    bt = jnp.moveaxis(bt, 1, 2).reshape(A_KV_HEADS, 3, A_GROUP * BLK, BLK)
    blk = lambda w: pl.BlockSpec((1, BLK, w), lambda bi, qb: (bi, qb, 0))
    full = lambda w: pl.BlockSpec((1, t, w), lambda bi, qb: (bi, 0, 0))
    return pl.pallas_call(
        functools.partial(_dsa_kernel, topk=topk, idx_bits=idx_bits),
        grid=(b, nb),
        in_specs=[blk(A_Q), blk(IDX_HEADS * IDX_DIM), blk(LANES), full(A_KV), full(A_KV), full(LANES),
                  pl.BlockSpec(bt.shape, lambda bi, qb: (0, 0, 0, 0))],
        out_specs=blk(A_Q),
        out_shape=jax.ShapeDtypeStruct((b, t, A_Q), F32),
        scratch_shapes=[pltpu.VMEM((nb, BLK, BLK), I32), pltpu.VMEM((nb, BLK, BLK), F32)],
        compiler_params=_cparams(("arbitrary", "arbitrary")),
        name="dsa_prompt",
    )(q, qi, sm, k, v, sm, bt)


def _gdn_kernel(qkv_ref, z_ref, sm_ref, s0_ref, cw_ref, par_ref, nw_ref, o_ref, sout_ref, ext_ref, st_ref):
    t = pl.program_id(1)
    nt = pl.num_programs(1)
    c = CHUNK
    halo = SUBLANES

    @pl.when(t == 0)
    def _():
        ext_ref[0:halo, :] = jnp.zeros((halo, B_CONV_DIM), F32)
        st_ref[...] = s0_ref[0]

    x = qkv_ref[0]
    ext_ref[halo:halo + c, :] = x
    conv = None
    for j in range(CONV_W):
        start = halo - (CONV_W - 1) + j
        term = ext_ref[start:start + c, :] * cw_ref[j:j + 1, :]
        conv = term if conv is None else conv + term
    ext_ref[0:halo, :] = x[c - halo:c, :]
    conv = _silu(conv)

    ri = lax.broadcasted_iota(I32, (c, c), 0)
    ci = lax.broadcasted_iota(I32, (c, c), 1)
    tri = ri >= ci
    stri = ri > ci
    eye = jnp.where(ri == ci, 1.0, 0.0)
    tri_bf = jnp.where(tri, 1.0, 0.0).astype(BF16)
    ones_bf = jnp.ones((c, c), BF16)

    sm = sm_ref[0]
    g_all = -jnp.exp(par_ref[0:1, :]) * _softplus(sm + par_ref[1:2, :])
    gc_all = _dot_exact_lhs(tri_bf, g_all)
    beta_all = jax.nn.sigmoid(sm)
    zz = z_ref[0]
    nw = nw_ref[...]
    gcs = [gc_all[:, SM_A + h:SM_A + h + 1] for h in range(B_HEADS)]
    gc_rows = _dot_exact_lhs(ones_bf, jnp.concatenate([eye * gc for gc in gcs], axis=1))

    hs = range(B_HEADS)
    qs = [conv[:, h * B_KEY_DIM:(h + 1) * B_KEY_DIM] for h in hs]
    ks = [conv[:, B_K + h * B_KEY_DIM:B_K + (h + 1) * B_KEY_DIM] for h in hs]
    vs = [conv[:, 2 * B_K + h * B_VAL_DIM:2 * B_K + (h + 1) * B_VAL_DIM] for h in hs]
    qs = [q * lax.rsqrt(jnp.sum(q * q, axis=-1, keepdims=True) + EPS) * (B_KEY_DIM ** -0.5) for q in qs]
    ks = [k * lax.rsqrt(jnp.sum(k * k, axis=-1, keepdims=True) + EPS) for k in ks]
    betas = [beta_all[:, SM_B + h:SM_B + h + 1] for h in hs]
    decays = [jnp.exp(jnp.where(tri, gcs[h] - gc_rows[:, h * c:(h + 1) * c], -jnp.inf)) for h in hs]
    kbs = [ks[h] * betas[h] for h in hs]
    vbs = [vs[h] * betas[h] for h in hs]
    k_bf = [k.astype(BF16) for k in ks]
    a_mats = [jnp.where(stri, _dg(kbs[h].astype(BF16), k_bf[h], _NT) * decays[h], 0.0) for h in hs]
    tms = [eye - a for a in a_mats]
    pws = [_dot3(a, a) for a in a_mats]
    steps = int(math.log2(c))
    for j in range(1, steps):
        tms = [tm + _dot3(tm, pw) for tm, pw in zip(tms, pws)]
        if j < steps - 1:
            pws = [_dot3(pw, pw) for pw in pws]
    egs = [jnp.exp(gc) for gc in gcs]
    tm_bf = [tm.astype(BF16) for tm in tms]
    us = [_dg(tm_bf[h], vbs[h].astype(BF16)) for h in hs]
    ws = [_dg(tm_bf[h], (kbs[h] * egs[h]).astype(BF16)) for h in hs]
    a_qks = [jnp.where(tri, _dg(qs[h].astype(BF16), k_bf[h], _NT) * decays[h], 0.0) for h in hs]

    s_olds = [st_ref[h] for h in hs]
    s_bf = [s.astype(BF16) for s in s_olds]
    v_news = [us[h] - _dg(ws[h].astype(BF16), s_bf[h]) for h in hs]
    os_ = [_dg((qs[h] * egs[h]).astype(BF16), s_bf[h]) + _dot1(a_qks[h], v_news[h]) for h in hs]
    g_lasts = [gc[c - 1:c, :] for gc in gcs]
    for h in hs:
        st_ref[h] = (s_olds[h] * jnp.exp(g_lasts[h])
                     + _dot1(ks[h] * jnp.exp(g_lasts[h] - gcs[h]), v_news[h], _TN))
    for h in hs:
        o = os_[h]
        ms = jnp.mean(o * o, axis=-1, keepdims=True)
        zh = zz[:, h * B_VAL_DIM:(h + 1) * B_VAL_DIM]
        o_ref[0, :, h * B_VAL_DIM:(h + 1) * B_VAL_DIM] = (o * lax.rsqrt(ms + EPS) * nw) * _silu(zh)

    @pl.when(t == nt - 1)
    def _():
        sout_ref[0] = st_ref[...]


def _gdn_params(a_log, dt_bias):
    par = jnp.zeros((SUBLANES, LANES), F32)
    return par.at[0, SM_A:SM_A + B_HEADS].set(a_log).at[1, SM_A:SM_A + B_HEADS].set(dt_bias)


def _gdn_prompt(qkv, z, sm, s0, conv_w, a_log, dt_bias, norm_w):
    b, t, _ = qkv.shape
    nt = t // CHUNK
    tok = lambda w: pl.BlockSpec((1, CHUNK, w), lambda bi, ti: (bi, ti, 0))
    const2 = lambda shp: pl.BlockSpec(shp, lambda bi, ti: (0, 0))
    st_spec = pl.BlockSpec((1, B_HEADS, B_KEY_DIM, B_VAL_DIM), lambda bi, ti: (bi, 0, 0, 0))
    o, s_out = pl.pallas_call(
        _gdn_kernel,
        grid=(b, nt),
        in_specs=[tok(B_CONV_DIM), tok(B_V), tok(LANES), st_spec,
                  const2((CONV_W, B_CONV_DIM)), const2((SUBLANES, LANES)), const2((1, B_VAL_DIM))],
        out_specs=[tok(B_V), st_spec],
        out_shape=[jax.ShapeDtypeStruct((b, t, B_V), F32),
                   jax.ShapeDtypeStruct((b, B_HEADS, B_KEY_DIM, B_VAL_DIM), F32)],
        scratch_shapes=[pltpu.VMEM((SUBLANES + CHUNK, B_CONV_DIM), F32),
                        pltpu.VMEM((B_HEADS, B_KEY_DIM, B_VAL_DIM), F32)],
        compiler_params=_cparams(("arbitrary", "arbitrary")),
        name="gdn_prompt",
    )(qkv, z, sm, s0, conv_w, _gdn_params(a_log, dt_bias), norm_w.reshape(1, B_VAL_DIM))
    return o, s_out


GDN_DEC_ROWS = 16


def _gdn_dec_kernel(qkv_ref, z_ref, sm_ref, cb_ref, s_ref, cw_ref, par_ref, nw_ref, o_ref, so_ref, oraw_ref):
    r = GDN_DEC_ROWS
    conv = qkv_ref[...] * cw_ref[CONV_W - 1:CONV_W, :]
    for j in range(CONV_W - 1):
        conv = conv + cb_ref[0, j] * cw_ref[j:j + 1, :]
    conv = _silu(conv)
    sm = sm_ref[...]
    g_all = -jnp.exp(par_ref[0:1, :]) * _softplus(sm + par_ref[1:2, :])
    beta_all = jax.nn.sigmoid(sm)
    ri = lax.broadcasted_iota(I32, (r, LANES), 0)
    ci = lax.broadcasted_iota(I32, (r, LANES), 1)
    eye_bf = jnp.where(ri == ci, 1.0, 0.0).astype(BF16)

    for h in range(B_HEADS):
        qh = conv[:, h * B_KEY_DIM:(h + 1) * B_KEY_DIM]
        kh = conv[:, B_K + h * B_KEY_DIM:B_K + (h + 1) * B_KEY_DIM]
        vh = conv[:, 2 * B_K + h * B_VAL_DIM:2 * B_K + (h + 1) * B_VAL_DIM]
        qh = qh * lax.rsqrt(jnp.sum(qh * qh, axis=-1, keepdims=True) + EPS) * (B_KEY_DIM ** -0.5)
        kh = kh * lax.rsqrt(jnp.sum(kh * kh, axis=-1, keepdims=True) + EPS)
        eg = jnp.exp(g_all[:, SM_A + h:SM_A + h + 1])
        beta = beta_all[:, SM_B + h:SM_B + h + 1]
        qk = jnp.sum(qh * kh, axis=-1, keepdims=True)
        k_t = _dot_exact_rhs(kh, eye_bf, _TN)
        q_t = _dot_exact_rhs(qh, eye_bf, _TN)
        for s in range(r):
            st = s_ref[0, s, h]
            kc = k_t[:, s:s + 1]
            qc = q_t[:, s:s + 1]
            k_s = jnp.sum(st * kc, axis=0, keepdims=True)
            q_s = jnp.sum(st * qc, axis=0, keepdims=True)
            eg_s = eg[s:s + 1, :]
            v_new = beta[s:s + 1, :] * (vh[s:s + 1, :] - eg_s * k_s)
            oraw_ref[s:s + 1, h * B_VAL_DIM:(h + 1) * B_VAL_DIM] = eg_s * q_s + qk[s:s + 1, :] * v_new
            so_ref[s, h] = st * eg_s + kc * v_new

    zz = z_ref[...]
    nw = nw_ref[...]
    for h in range(B_HEADS):
        o = oraw_ref[:, h * B_VAL_DIM:(h + 1) * B_VAL_DIM]
        ms = jnp.mean(o * o, axis=-1, keepdims=True)
        zh = zz[:, h * B_VAL_DIM:(h + 1) * B_VAL_DIM]
        o_ref[:, h * B_VAL_DIM:(h + 1) * B_VAL_DIM] = (o * lax.rsqrt(ms + EPS) * nw) * _silu(zh)


def _gdn_decode(qkv, z, sm, conv_t, state, j, conv_w, a_log, dt_bias, norm_w):
    b = qkv.shape[0]
    r = GDN_DEC_ROWS
    assert b % r == 0
    rows = lambda w: pl.BlockSpec((r, w), lambda i: (i, 0))
    const2 = lambda shp: pl.BlockSpec(shp, lambda i: (0, 0))
    o, s_out = pl.pallas_call(
        _gdn_dec_kernel,
        grid=(b // r,),
        in_specs=[rows(B_CONV_DIM), rows(B_V), rows(LANES),
                  pl.BlockSpec((1, CONV_W - 1, r, B_CONV_DIM), lambda i: (j, 0, i, 0)),
                  pl.BlockSpec((1, r, B_HEADS, B_KEY_DIM, B_VAL_DIM), lambda i: (j, i, 0, 0, 0)),
                  const2((CONV_W, B_CONV_DIM)), const2((SUBLANES, LANES)), const2((1, B_VAL_DIM))],
        out_specs=[rows(B_V), pl.BlockSpec((r, B_HEADS, B_KEY_DIM, B_VAL_DIM), lambda i: (i, 0, 0, 0))],
        out_shape=[jax.ShapeDtypeStruct((b, B_V), F32),
                   jax.ShapeDtypeStruct((b, B_HEADS, B_KEY_DIM, B_VAL_DIM), F32)],
        scratch_shapes=[pltpu.VMEM((r, B_V), F32)],
        compiler_params=_cparams(("arbitrary",)),
        name="gdn_decode",
    )(qkv, z, sm, conv_t, state, conv_w, _gdn_params(a_log, dt_bias), norm_w.reshape(1, B_VAL_DIM))
    return o, s_out


DIFF_HEADS_PER_STEP = 2


def _diff_kernel(lam_ref, q_ref, k_ref, v_ref, bias_ref, sw_ref, o_ref, *, lam_init):
    qb = pl.program_id(2)
    nkb = qb + 1
    nh = DIFF_HEADS_PER_STEP
    lane = lax.broadcasted_iota(I32, (TQ, C_HD), 1)
    q2 = []
    for h in range(nh):
        qh = q_ref[0, :, h * C_HD:(h + 1) * C_HD] * (C_HEAD_DIM ** -0.5)
        zero = jnp.zeros_like(qh)
        q2.append(jnp.concatenate([jnp.where(lane < C_HEAD_DIM, qh, zero),
                                   jnp.where(lane >= C_HEAD_DIM, qh, zero)], axis=0))

    def body(kb, carry):
        off = pl.multiple_of(kb * TQ, TQ)
        t = jnp.clip(kb - qb + 2, 0, 2)
        hs = range(nh)
        ss = [_dg(q2[h], k_ref[0, pl.ds(off, TQ), h * C_HD:(h + 1) * C_HD], _NT) for h in hs]
        ss = [(ss[h].reshape(2, TQ, TQ) + bias_ref[h, t][None]).reshape(2 * TQ, TQ) for h in hs]
        m_new = [jnp.maximum(carry[3 * h], jnp.max(ss[h], axis=1, keepdims=True)) for h in hs]
        ps = [jnp.exp(ss[h] - m_new[h]) for h in hs]
        pvs = [_dg(ps[h].astype(BF16), v_ref[0, pl.ds(off, TQ), h * C_HD:(h + 1) * C_HD]) for h in hs]
        out = []
        for h in hs:
            alpha = jnp.exp(carry[3 * h] - m_new[h])
            out += [m_new[h], alpha * carry[3 * h + 1] + jnp.sum(ps[h], axis=1, keepdims=True),
                    alpha * carry[3 * h + 2] + pvs[h]]
        return tuple(out)

    init = (jnp.full((2 * TQ, 1), NEG, F32), jnp.zeros((2 * TQ, 1), F32), jnp.zeros((2 * TQ, C_HD), F32)) * nh
    res = lax.fori_loop(0, nkb, body, init)
    for h in range(nh):
        _, l, acc = res[3 * h:3 * h + 3]
        on = acc / l
        o = on[0:TQ] - lam_ref[0] * on[TQ:2 * TQ]
        ms = jnp.mean(o * o, axis=-1, keepdims=True)
        o_ref[0, :, h * C_HD:(h + 1) * C_HD] = (o * lax.rsqrt(ms + EPS) * sw_ref[...]) * (1.0 - lam_init)


def _diff_prompt(q, k, v, bias_tiles, lam, subln_w, lam_init):
    b, t, _ = q.shape
    nh = DIFF_HEADS_PER_STEP
    w = nh * C_HD
    return pl.pallas_call(
        functools.partial(_diff_kernel, lam_init=lam_init),
        grid=(b, C_HEADS // nh, t // TQ),
        in_specs=[pl.BlockSpec(memory_space=pltpu.SMEM),
                  pl.BlockSpec((1, TQ, w), lambda bi, h, qb: (bi, qb, h)),
                  pl.BlockSpec((1, t, w), lambda bi, h, qb: (bi, 0, h)),
                  pl.BlockSpec((1, t, w), lambda bi, h, qb: (bi, 0, h)),
                  pl.BlockSpec((nh, 3, TQ, TQ), lambda bi, h, qb: (h, 0, 0, 0)),
                  pl.BlockSpec((1, C_HD), lambda bi, h, qb: (0, 0))],
        out_specs=pl.BlockSpec((1, TQ, w), lambda bi, h, qb: (bi, qb, h)),
        out_shape=jax.ShapeDtypeStruct((b, t, C_V), F32),
        compiler_params=_cparams(("arbitrary", "arbitrary", "arbitrary")),
        name="diff_prompt",
    )(lam.reshape(1), q, k, v, bias_tiles, subln_w.reshape(1, C_HD))


DIFF_DEC_PAGES = 4


def _diff_dec_kernel(pt_ref, lam_ref, q_ref, kn_ref, vn_ref, *rest, n_steps, lam_init):
    pp = DIFF_DEC_PAGES
    kps, vps = rest[:pp], rest[pp:2 * pp]
    bias_ref, bnew_ref, sw_ref, o_ref, m_ref, l_ref, acc_ref = rest[2 * pp:]
    p = pl.program_id(1)
    nr = 2 * C_HEADS

    @pl.when(p == 0)
    def _():
        m_ref[...] = jnp.full(m_ref.shape, NEG, F32)
        l_ref[...] = jnp.zeros(l_ref.shape, F32)
        acc_ref[...] = jnp.zeros(acc_ref.shape, F32)

    q8 = q_ref[0] * (C_HEAD_DIM ** -0.5)
    lane = lax.broadcasted_iota(I32, (C_HEADS, C_HD), 1)
    q_lo = jnp.where(lane < C_HEAD_DIM, q8, 0.0)
    q_hi = jnp.where(lane >= C_HEAD_DIM, q8, 0.0)
    qm = jnp.concatenate([q_lo, q_hi], axis=0).astype(BF16)

    rows = BLK * C_HEADS
    s = jnp.concatenate([_dg(qm, kps[i][0, 0].astype(BF16), _NT) + bias_ref[i] for i in range(pp)], axis=1)

    def pv(pr):
        prb = pr.astype(BF16)
        out = None
        for i in range(pp):
            part = _dg(prb[:, i * rows:(i + 1) * rows], vps[i][0, 0].astype(BF16))
            out = part if out is None else out + part
        return out

    m, l, acc = _softmax_step(s, m_ref[...], l_ref[...], acc_ref[...], pv)
    m_ref[...] = m
    l_ref[...] = l
    acc_ref[...] = acc

    @pl.when(p == n_steps - 1)
    def _():
        kn = kn_ref[0]
        s_new = jnp.concatenate([jnp.sum(q_lo * kn, axis=1, keepdims=True),
                                 jnp.sum(q_hi * kn, axis=1, keepdims=True)], axis=0) + bnew_ref[:, 0:1]
        vn2 = jnp.concatenate([vn_ref[0], vn_ref[0]], axis=0)
        _, l, acc = _softmax_step(s_new, m_ref[...], l_ref[...], acc_ref[...], lambda pr: pr * vn2)
        on = acc / l
        o = on[0:C_HEADS] - lam_ref[0] * on[C_HEADS:nr]
        ms = jnp.mean(o * o, axis=-1, keepdims=True)
        o_ref[0] = (o * lax.rsqrt(ms + EPS) * sw_ref[...]) * (1.0 - lam_init)


def _diff_decode(q8, k8, v8, cache_k, cache_v, j, pt_flat, n_pages, bias_pages, bias_new, lam, subln_w, lam_init):
    b = q8.shape[0]
    pp = DIFF_DEC_PAGES
    assert n_pages % pp == 0
    n_steps = n_pages // pp
    rows = BLK * C_HEADS
    row = pl.BlockSpec((1, C_HEADS, C_HD), lambda bi, p, pt: (bi, 0, 0))
    page = lambda i: pl.BlockSpec((1, 1, rows, C_HD),
                                  lambda bi, p, pt: (j, pt[bi * n_pages + p * pp + i], 0, 0))
    grid_spec = pltpu.PrefetchScalarGridSpec(
        num_scalar_prefetch=1,
        grid=(b, n_steps),
        in_specs=[pl.BlockSpec(memory_space=pltpu.SMEM), row, row, row]
                 + [page(i) for i in range(pp)] + [page(i) for i in range(pp)]
                 + [pl.BlockSpec((pp, 2 * C_HEADS, rows), lambda bi, p, pt: (p, 0, 0)),
                    pl.BlockSpec((2 * C_HEADS, LANES), lambda bi, p, pt: (0, 0)),
                    pl.BlockSpec((1, C_HD), lambda bi, p, pt: (0, 0))],
        out_specs=row,
        scratch_shapes=[pltpu.VMEM((2 * C_HEADS, 1), F32), pltpu.VMEM((2 * C_HEADS, 1), F32),
                        pltpu.VMEM((2 * C_HEADS, C_HD), F32)],
    )
    return pl.pallas_call(
        functools.partial(_diff_dec_kernel, n_steps=n_steps, lam_init=lam_init),
        grid_spec=grid_spec,
        out_shape=jax.ShapeDtypeStruct((b, C_HEADS, C_HD), F32),
        compiler_params=_cparams(("arbitrary", "arbitrary")),
        name="diff_decode",
    )(pt_flat, lam.reshape(1), q8, k8, v8, *([cache_k] * pp), *([cache_v] * pp),
      bias_pages, bias_new, subln_w.reshape(1, C_HD))


def _diff_decode_bias(bias_cols, q_pos, n_pages):
    bd = _bias_decode(bias_cols, q_pos, n_pages + 1)
    past = jnp.moveaxis(bd[:n_pages], 1, 2)
    same = jnp.eye(C_HEADS, dtype=bool)
    tab = jnp.where(same[None, :, None, :], past[:, None, :, :], NEG)
    tab = tab.reshape(n_pages, C_HEADS, BLK * C_HEADS)
    tab = jnp.concatenate([tab, tab], axis=1)
    new = bd[n_pages, :, 0]
    new = jnp.broadcast_to(jnp.concatenate([new, new])[:, None], (2 * C_HEADS, LANES))
    return tab, new


def _idx_dec_kernel(pt_ref, qi_ref, sm_ref, *rest, n_pages):
    kps = rest[:n_pages]
    o_ref = rest[n_pages]
    qi = qi_ref[0]
    sm = sm_ref[0]
    rowi = lax.broadcasted_iota(I32, (SUBLANES, IDX_HEADS * IDX_DIM), 0)
    lane = lax.broadcasted_iota(I32, (SUBLANES, IDX_HEADS * IDX_DIM), 1)
    qsel = jnp.where(lane // IDX_DIM == rowi, qi, 0.0)
    qt = qsel[:, 0:IDX_DIM]
    for h in range(1, IDX_HEADS):
        qt = qt + qsel[:, h * IDX_DIM:(h + 1) * IDX_DIM]
    r8 = lax.broadcasted_iota(I32, (SUBLANES, LANES), 0)
    l8 = lax.broadcasted_iota(I32, (SUBLANES, LANES), 1)
    wsel = jnp.where(jnp.logical_and(r8 < IDX_HEADS, l8 == r8 + SM_WI), sm, 0.0)
    wcol = jnp.sum(wsel, axis=1, keepdims=True) * (IDX_HEADS ** -0.5)
    qh, ql = _split2(qt)
    for p in range(n_pages):
        kh, kl = _split2(kps[p][0, 0])
        d = _dg(qh, kh) + (_dg(qh, kl) + _dg(ql, kh))
        rel = jnp.maximum(d * (IDX_DIM ** -0.5), 0.0)
        o_ref[0, p] = jnp.sum(wcol * rel, axis=0, keepdims=True)
    d_new = jnp.sum(qt * sm[:, SM_KI:SM_KI + IDX_DIM], axis=1, keepdims=True)
    s_new = jnp.sum(wcol * jnp.maximum(d_new * (IDX_DIM ** -0.5), 0.0), axis=0, keepdims=True)
    o_ref[0, n_pages] = jnp.broadcast_to(s_new, (1, BLK))


def _idx_decode(qi, sm, cache_kidx_t, j, pt_flat, n_pages):
    b = qi.shape[0]
    page = lambda p: pl.BlockSpec((1, 1, IDX_DIM, BLK), lambda bi, pt: (j, pt[bi * n_pages + p], 0, 0))
    grid_spec = pltpu.PrefetchScalarGridSpec(
        num_scalar_prefetch=1,
        grid=(b,),
        in_specs=[pl.BlockSpec((1, 1, IDX_HEADS * IDX_DIM), lambda bi, pt: (bi, 0, 0)),
                  pl.BlockSpec((1, 1, LANES), lambda bi, pt: (bi, 0, 0))]
                 + [page(p) for p in range(n_pages)],
        out_specs=pl.BlockSpec((1, n_pages + 1, 1, BLK), lambda bi, pt: (bi, 0, 0, 0)),
    )
    return pl.pallas_call(
        functools.partial(_idx_dec_kernel, n_pages=n_pages),
        grid_spec=grid_spec,
        out_shape=jax.ShapeDtypeStruct((b, n_pages + 1, 1, BLK), F32),
        compiler_params=_cparams(("arbitrary",)),
        name="idx_decode",
    )(pt_flat, qi, sm, *([cache_kidx_t] * n_pages))


def _sel_dec_kernel(s_ref, o_ref, key_ref, *, n_valid, topk, idx_bits):
    nkb, rows, _ = s_ref.shape
    col = lax.broadcasted_iota(I32, (rows, BLK), 1)
    for kb in range(nkb):
        s = jnp.where(kb * BLK + col < n_valid, s_ref[kb], -jnp.inf)
        key_ref[kb] = _score_keys(s)
    thr, cut = _select_topk(key_ref, nkb, topk, idx_bits)
    for kb in range(nkb):
        k_pos = kb * BLK + col
        sel = _in_topk(key_ref[kb], k_pos, thr, cut)
        o_ref[kb] = jnp.where(k_pos < n_valid, jnp.where(sel, 0.0, NEG), NEG)


def _sel_decode(scores, n_valid, topk):
    nkb, rows, _ = scores.shape
    idx_bits = max(1, int(math.ceil(math.log2(nkb * BLK))))
    return pl.pallas_call(
        functools.partial(_sel_dec_kernel, n_valid=n_valid, topk=topk, idx_bits=idx_bits),
        out_shape=jax.ShapeDtypeStruct(scores.shape, F32),
        scratch_shapes=[pltpu.VMEM(scores.shape, I32)],
        compiler_params=pltpu.CompilerParams(vmem_limit_bytes=VMEM_LIMIT),
        name="sel_decode",
    )(scores)


def _dsa_dec_kernel(pt_ref, q_ref, kn_ref, vn_ref, *rest, n_pages):
    kps, vps = rest[:n_pages], rest[n_pages:2 * n_pages]
    msk_ref, bias_ref, o_ref = rest[2 * n_pages:]

    q = q_ref[0] * (A_HEAD_DIM ** -0.5)
    rowi = lax.broadcasted_iota(I32, (A_HEADS, A_Q), 0)
    lane = lax.broadcasted_iota(I32, (A_HEADS, A_Q), 1)
    qsel = jnp.where(lane // A_HEAD_DIM == rowi, q, 0.0)
    halves = []
    for n in range(A_KV_HEADS):
        acc = None
        for g in range(A_GROUP):
            h = n * A_GROUP + g
            part = qsel[:, h * A_HEAD_DIM:(h + 1) * A_HEAD_DIM]
            acc = part if acc is None else acc + part
        halves.append(acc)
    qt = jnp.concatenate(halves, axis=1)
    qt_bf = qt.astype(BF16)

    s = jnp.concatenate([_dg(qt_bf, kps[i][0, 0].astype(BF16)) + bias_ref[i] + msk_ref[i, 0]
                         for i in range(n_pages)], axis=1)
    s_new = (jnp.sum(qt * kn_ref[0], axis=1, keepdims=True) + bias_ref[n_pages][:, 0:1]
             + msk_ref[n_pages, 0][:, 0:1])
    m = jnp.maximum(jnp.max(s, axis=1, keepdims=True), s_new)
    p = jnp.exp(s - m)
    p_new = jnp.exp(s_new - m)
    l = jnp.sum(p, axis=1, keepdims=True) + p_new
    pb = p.astype(BF16)
    acc = p_new * vn_ref[0]
    for i in range(n_pages):
        acc = acc + _dg(pb[:, i * BLK:(i + 1) * BLK], vps[i][0, 0].astype(BF16), _NT)
    on = acc / l
    for h in range(A_HEADS):
        n = h // A_GROUP
        o_ref[0, :, h * A_HEAD_DIM:(h + 1) * A_HEAD_DIM] = on[h:h + 1, n * A_HEAD_DIM:(n + 1) * A_HEAD_DIM]


def _dsa_decode(q, k_new, v_new, cache_kt, cache_vt, j, pt_flat, n_pages, mask, bias_dec):
    b = q.shape[0]
    page = lambda i: pl.BlockSpec((1, 1, A_KV, BLK), lambda bi, pt: (j, pt[bi * n_pages + i], 0, 0))
    rowspec = lambda w: pl.BlockSpec((1, 1, w), lambda bi, pt: (bi, 0, 0))
    grid_spec = pltpu.PrefetchScalarGridSpec(
        num_scalar_prefetch=1,
        grid=(b,),
        in_specs=[rowspec(A_Q), rowspec(A_KV), rowspec(A_KV)]
                 + [page(i) for i in range(n_pages)] + [page(i) for i in range(n_pages)]
                 + [pl.BlockSpec((n_pages + 1, 1, 1, BLK), lambda bi, pt: (0, bi, 0, 0)),
                    pl.BlockSpec((n_pages + 1, A_HEADS, BLK), lambda bi, pt: (0, 0, 0))],
        out_specs=rowspec(A_Q),
    )
    return pl.pallas_call(
        functools.partial(_dsa_dec_kernel, n_pages=n_pages),
        grid_spec=grid_spec,
        out_shape=jax.ShapeDtypeStruct((b, 1, A_Q), F32),
        compiler_params=_cparams(("arbitrary",)),
        name="dsa_decode",
    )(pt_flat, q, k_new, v_new, *([cache_kt] * n_pages), *([cache_vt] * n_pages), mask, bias_dec)


def _even_weights(w_in):
    sizes = (A_Q, A_KV, A_KV, IDX_HEADS * IDX_DIM, IDX_DIM, IDX_HEADS, B_CONV_DIM, B_V, B_HEADS, B_HEADS)
    offs = np.concatenate([[0], np.cumsum(sizes)])
    seg = lambda i: w_in[:, int(offs[i]):int(offs[i + 1])]
    pad = LANES - (IDX_DIM + IDX_HEADS + 2 * B_HEADS)
    small = jnp.concatenate([seg(4), seg(5), seg(8), seg(9), jnp.zeros((w_in.shape[0], pad), w_in.dtype)], axis=1)
    return jnp.concatenate([seg(0), seg(1), seg(2), seg(3), small, seg(6), seg(7)], axis=1).astype(BF16)


def _seq_outs(widths, dtype=F32):
    outs, off = [], 0
    for wd in widths:
        outs.append((off, wd, dtype))
        off += wd
    return tuple(outs)


_EVEN_OUTS = _seq_outs((A_Q, A_KV, A_KV, IDX_HEADS * IDX_DIM, LANES, B_CONV_DIM, B_V))
_ODD_OUTS = _seq_outs((C_QK, C_QK, C_V))
_ODD_OUTS_PROMPT = _ODD_OUTS[1:] + _seq_outs((C_QK, C_QK, C_V), BF16)


def kernel(x_prompt, x_sample, c_prompt, c_sample, cache_A_k, cache_A_v, cache_A_kidx, cache_C_k, cache_C_v, state_B_ssm, state_B_conv, page_table, rel_bias, ada_w, ada_b, norm_w, final_norm_w, ffn_w1, ffn_w2, ab_w_in, ab_w_out, gdn_conv_w, gdn_a_log, gdn_dt_bias, gdn_norm_w, c_w_in, c_w_out, c_lambda_q1, c_lambda_k1, c_lambda_q2, c_lambda_k2, c_subln_w):
    depth = ada_w.shape[0]
    bp, tp, d = x_prompt.shape
    bs, ts, _ = x_sample.shape
    assert ts == 1 and tp % TQ == 0 and tp % CHUNK == 0
    n_pages = page_table.shape[1]
    page = cache_A_k.shape[2]
    assert page == BLK
    n_phys = cache_A_k.shape[1]
    past_len = n_pages * page
    topk_p = min(TOPK_MAX, tp // 4)
    topk_s = min(TOPK_MAX, (past_len + ts) // 4)
    pt_flat = page_table.reshape(-1).astype(I32)

    n_c = bp + bs
    n_c_pad = -(-n_c // SUBLANES) * SUBLANES
    c_all = jnp.pad(jnp.concatenate([c_prompt, c_sample], axis=0), ((0, n_c_pad - n_c), (0, 0)))
    mod_all = _modulation(c_all, ada_w, ada_b)

    w1b = ffn_w1.astype(BF16)
    w2b = ffn_w2.astype(BF16)
    ab_in_b = [_even_weights(ab_w_in[j]) for j in range(ab_w_in.shape[0])]
    ab_out_b = ab_w_out.astype(BF16)
    c_in_b = c_w_in.astype(BF16)
    c_out_b = c_w_out.astype(BF16)

    bias_a = _bias_tiles(rel_bias[:, :A_HEADS], BLK, False)
    bias_c = _bias_tiles(rel_bias[:, A_HEADS:], TQ, True)
    bias_a_dec = _bias_decode(rel_bias[:, :A_HEADS], past_len, n_pages + 1)
    bias_c_pages, bias_c_new = _diff_decode_bias(rel_bias[:, A_HEADS:], past_len, n_pages)

    n_ab = cache_A_k.shape[0]
    cache_a_kt = jnp.transpose(cache_A_k, (0, 1, 3, 4, 2)).reshape(n_ab, n_phys, A_KV, page)
    cache_a_vt = jnp.transpose(cache_A_v, (0, 1, 3, 4, 2)).reshape(n_ab, n_phys, A_KV, page)
    cache_a_it = jnp.transpose(cache_A_kidx, (0, 1, 3, 2))
    cache_c_k = cache_C_k.reshape(cache_C_k.shape[0], n_phys, page * C_HEADS, C_HD)
    cache_c_v = cache_C_v.reshape(cache_C_v.shape[0], n_phys, page * C_HEADS, C_HD)
    conv_t = jnp.transpose(state_B_conv, (0, 2, 1, 3))

    xp = x_prompt
    xs = x_sample.reshape(1, bs, d)
    new_p = [[] for _ in range(7)]
    new_s = [[] for _ in range(7)]

    for i in range(depth):
        j = i // 2
        mod_p = mod_all[i, :bp].reshape(bp, 1, N_MOD * d)
        mod_s = mod_all[i, bp:bp + bs].reshape(1, bs, N_MOD * d)
        xp = _ffn(xp, mod_p, 0, norm_w[i, 0], w1b[i, 0], w2b[i, 0])
        xs = _ffn(xs, mod_s, 0, norm_w[i, 0], w1b[i, 0], w2b[i, 0])
        if i % 2 == 0:
            q, k, v, qi, sm, qkv, z = _inproj(xp, mod_p, 3, norm_w[i, 1], ab_in_b[j], _EVEN_OUTS)
            o_a = _dsa_prompt(q, qi, sm, k, v, bias_a, topk_p)
            o_b, s_new = _gdn_prompt(qkv, z, sm, jnp.zeros((bp, B_HEADS, B_KEY_DIM, B_VAL_DIM), F32),
                                     gdn_conv_w[j], gdn_a_log[j], gdn_dt_bias[j], gdn_norm_w[j])
            xp = _outproj(xp, mod_p, 5, ab_out_b[j], [o_a, o_b])
            new_p[0].append(k.reshape(bp, tp, A_KV_HEADS, A_HEAD_DIM))
            new_p[1].append(v.reshape(bp, tp, A_KV_HEADS, A_HEAD_DIM))
            new_p[2].append(sm[:, :, SM_KI:SM_KI + IDX_DIM])
            new_p[3].append(s_new)
            new_p[4].append(qkv[:, tp - (CONV_W - 1):, :])
            q, k, v, qi, sm, qkv, z = _inproj(xs, mod_s, 3, norm_w[i, 1], ab_in_b[j], _EVEN_OUTS)
            as_rows = lambda a: a.reshape(bs, 1, a.shape[-1])
            scores = _idx_decode(as_rows(qi), as_rows(sm), cache_a_it, j, pt_flat, n_pages)
            scores = jnp.moveaxis(scores.reshape(bs, n_pages + 1, BLK), 1, 0)
            mask = _sel_decode(scores, past_len + 1, topk_s).reshape(n_pages + 1, bs, 1, BLK)
            o_a = _dsa_decode(as_rows(q), as_rows(k), as_rows(v), cache_a_kt, cache_a_vt, j, pt_flat, n_pages,
                              mask, bias_a_dec)
            o_b, s_new = _gdn_decode(qkv.reshape(bs, B_CONV_DIM), z.reshape(bs, B_V), sm.reshape(bs, LANES),
                                     conv_t, state_B_ssm, j, gdn_conv_w[j], gdn_a_log[j], gdn_dt_bias[j],
                                     gdn_norm_w[j])
            xs = _outproj(xs, mod_s, 5, ab_out_b[j], [o_a.reshape(1, bs, A_Q), o_b.reshape(1, bs, B_V)])
            new_s[0].append(k.reshape(bs, 1, A_KV_HEADS, A_HEAD_DIM))
            new_s[1].append(v.reshape(bs, 1, A_KV_HEADS, A_HEAD_DIM))
            new_s[2].append(sm.reshape(bs, 1, LANES)[:, :, SM_KI:SM_KI + IDX_DIM])
            new_s[3].append(s_new)
            new_s[4].append(jnp.concatenate([state_B_conv[j], as_rows(qkv)], axis=1)[:, 1:, :])
        else:
            lam_init = 0.8 - 0.6 * math.exp(-0.3 * i)
            lam = (jnp.exp(jnp.sum(c_lambda_q1[j] * c_lambda_k1[j]))
                   - jnp.exp(jnp.sum(c_lambda_q2[j] * c_lambda_k2[j])) + lam_init).astype(F32)
            k, v, qb16, kb16, vb16 = _inproj(xp, mod_p, 3, norm_w[i, 1], c_in_b[j], _ODD_OUTS_PROMPT)
            o = _diff_prompt(qb16, kb16, vb16, bias_c, lam, c_subln_w[j], lam_init)
            xp = _outproj(xp, mod_p, 5, c_out_b[j], [o])
            new_p[5].append(k.reshape(bp, tp, C_HEADS, C_HD))
            new_p[6].append(v.reshape(bp, tp, C_HEADS, C_HD))
            q, k, v = _inproj(xs, mod_s, 3, norm_w[i, 1], c_in_b[j], _ODD_OUTS)
            as_heads = lambda a: a.reshape(bs, C_HEADS, C_HD)
            o = _diff_decode(as_heads(q), as_heads(k), as_heads(v), cache_c_k, cache_c_v, j, pt_flat, n_pages,
                             bias_c_pages, bias_c_new, lam, c_subln_w[j], lam_init)
            xs = _outproj(xs, mod_s, 5, c_out_b[j], [o.reshape(1, bs, C_V)])
            new_s[5].append(k.reshape(bs, 1, C_HEADS, C_HD))
            new_s[6].append(v.reshape(bs, 1, C_HEADS, C_HD))
        xp = _ffn(xp, mod_p, 6, norm_w[i, 2], w1b[i, 1], w2b[i, 1])
        xs = _ffn(xs, mod_s, 6, norm_w[i, 2], w1b[i, 1], w2b[i, 1])

    y_prompt = _final_norm(xp, final_norm_w)
    y_sample = _final_norm(xs, final_norm_w).reshape(bs, 1, d)
    sp = [jnp.stack(lst) for lst in new_p]
    ss = [jnp.stack(lst) for lst in new_s]
    return (y_prompt, y_sample, *sp, *ss)
```

```python
import functools
import math

import numpy as np
import jax
import jax.numpy as jnp
from jax import lax
from jax.experimental import pallas as pl
from jax.experimental.pallas import tpu as pltpu

F32 = jnp.float32
BF16 = jnp.bfloat16
I32 = jnp.int32

A_HEADS = 8
A_KV_HEADS = 2
A_GROUP = A_HEADS // A_KV_HEADS
A_HEAD_DIM = 64
IDX_HEADS = 4
IDX_DIM = 64
TOPK_MAX = 256
B_HEADS = 4
B_KEY_DIM = 128
B_VAL_DIM = 128
CONV_W = 4
CHUNK = 64
C_HEADS = 8
C_HEAD_DIM = 64
NUM_BUCKETS = 32
MAX_DISTANCE = 128
N_MOD = 9
EPS = 1e-6

A_Q = A_HEADS * A_HEAD_DIM
A_KV = A_KV_HEADS * A_HEAD_DIM
B_K = B_HEADS * B_KEY_DIM
B_V = B_HEADS * B_VAL_DIM
B_CONV_DIM = 2 * B_K + B_V
C_QK = C_HEADS * 2 * C_HEAD_DIM
C_V = C_HEADS * 2 * C_HEAD_DIM
C_HD = 2 * C_HEAD_DIM

LANES = 128
SUBLANES = 8
VMEM_LIMIT = 56 * 1024 * 1024

BLK = 128
TQ = 256
NEG = -1e30

SM_KI = 0
SM_WI = IDX_DIM
SM_A = SM_WI + IDX_HEADS
SM_B = SM_A + B_HEADS

_NT = (((1,), (1,)), ((), ()))
_NN = (((1,), (0,)), ((), ()))
_TN = (((0,), (0,)), ((), ()))


def _cparams(sem):
    return pltpu.CompilerParams(dimension_semantics=sem, vmem_limit_bytes=VMEM_LIMIT)


def _dg(a, b, dims=_NN):
    return lax.dot_general(a, b, dims, preferred_element_type=F32)


def _dot1(a, b, dims=_NN):
    return _dg(a.astype(BF16), b.astype(BF16), dims)


def _split2(x):
    hi = x.astype(BF16)
    lo = (x - hi.astype(F32)).astype(BF16)
    return hi, lo


def _split3(x):
    b1 = x.astype(BF16)
    r1 = x - b1.astype(F32)
    b2 = r1.astype(BF16)
    b3 = (r1 - b2.astype(F32)).astype(BF16)
    return b1, b2, b3


def _dot3(a, b, dims=_NN):
    ah, al = _split2(a)
    bh, bl = _split2(b)
    return _dg(ah, bh, dims) + (_dg(ah, bl, dims) + _dg(al, bh, dims))


def _dot_exact_lhs(a_bf, b, dims=_NN):
    b1, b2, b3 = _split3(b)
    return _dg(a_bf, b1, dims) + (_dg(a_bf, b2, dims) + _dg(a_bf, b3, dims))


def _dot_exact_rhs(a, b_bf, dims=_NN):
    a1, a2, a3 = _split3(a)
    return _dg(a1, b_bf, dims) + (_dg(a2, b_bf, dims) + _dg(a3, b_bf, dims))


def _silu(x):
    return x * jax.nn.sigmoid(x)


def _softplus(x):
    return jnp.maximum(x, 0.0) + jnp.log(1.0 + jnp.exp(-jnp.abs(x)))


def _norm_mod(x, nw, sc, sh):
    ms = jnp.mean(x * x, axis=-1, keepdims=True)
    return (x * lax.rsqrt(ms + EPS) * nw) * (1.0 + sc) + sh


def _softmax_step(s, m, l, acc, pv):
    m_new = jnp.maximum(m, jnp.max(s, axis=1, keepdims=True))
    alpha = jnp.exp(m - m_new)
    p = jnp.exp(s - m_new)
    return m_new, alpha * l + jnp.sum(p, axis=1, keepdims=True), alpha * acc + pv(p)


def _mod_kernel(c_ref, w_ref, b_ref, o_ref):
    s = _silu(c_ref[...]).astype(BF16)
    o_ref[0] = _dg(s, w_ref[0].astype(BF16)) + b_ref[0]


def _modulation(c_all, ada_w, ada_b):
    depth, d, n = ada_w.shape
    m = c_all.shape[0]
    tn = 1024
    return pl.pallas_call(
        _mod_kernel,
        grid=(depth, n // tn),
        in_specs=[pl.BlockSpec((m, d), lambda i, j: (0, 0)),
                  pl.BlockSpec((1, d, tn), lambda i, j: (i, 0, j)),
                  pl.BlockSpec((1, 1, tn), lambda i, j: (i, 0, j))],
        out_specs=pl.BlockSpec((1, m, tn), lambda i, j: (i, 0, j)),
        out_shape=jax.ShapeDtypeStruct((depth, m, n), F32),
        compiler_params=_cparams(("arbitrary", "arbitrary")),
        name="adaln_mod",
    )(c_all, ada_w, ada_b.reshape(depth, 1, n))


def _row_tile(t):
    return min(512, t)


def _mod_spec(mod, tm, m):
    r = mod.shape[1]
    d = mod.shape[2] // N_MOD
    if r == 1:
        return pl.BlockSpec((1, 1, d), lambda s, t: (s, 0, m))
    return pl.BlockSpec((1, tm, d), lambda s, t: (s, t, m))


def _resident(shape):
    nd = len(shape)
    return pl.BlockSpec(shape, lambda s, t: (0,) * nd, pipeline_mode=pl.Buffered(1))


def _ffn_kernel(x_ref, sh_ref, sc_ref, g_ref, nw_ref, w1_ref, w2_ref, o_ref, acc_ref, *, fc):
    x = x_ref[0]
    hb = _norm_mod(x, nw_ref[...], sc_ref[0], sh_ref[0]).astype(BF16)
    f = w2_ref.shape[0]
    for c in range(f // fc):
        gt = _dg(hb, w1_ref[:, c * fc:(c + 1) * fc])
        up = _dg(hb, w1_ref[:, f + c * fc:f + (c + 1) * fc])
        a = (_silu(gt) * up).astype(BF16)
        contrib = _dg(a, w2_ref[c * fc:(c + 1) * fc, :])
        if c == 0:
            acc_ref[...] = contrib
        else:
            acc_ref[...] += contrib
    o_ref[0] = x + (0.5 * g_ref[0]) * acc_ref[...]


def _ffn(x, mod, m0, nw, w1b, w2b):
    s, t, d = x.shape
    tm = _row_tile(t)
    f = w2b.shape[0]
    fc = 256 if f % 256 == 0 else LANES
    xs = pl.BlockSpec((1, tm, d), lambda si, ti: (si, ti, 0))
    return pl.pallas_call(
        functools.partial(_ffn_kernel, fc=fc),
        grid=(s, t // tm),
        in_specs=[xs, _mod_spec(mod, tm, m0), _mod_spec(mod, tm, m0 + 1), _mod_spec(mod, tm, m0 + 2),
                  _resident((1, d)), _resident(w1b.shape), _resident(w2b.shape)],
        out_specs=xs,
        out_shape=jax.ShapeDtypeStruct(x.shape, F32),
        scratch_shapes=[pltpu.VMEM((tm, d), F32)],
        compiler_params=_cparams(("arbitrary", "arbitrary")),
        name="ffn",
    )(x, mod, mod, mod, nw.reshape(1, d), w1b, w2b)


def _inproj_kernel(x_ref, sh_ref, sc_ref, nw_ref, w_ref, *o_refs, outs):
    hb = _norm_mod(x_ref[0], nw_ref[...], sc_ref[0], sh_ref[0]).astype(BF16)
    done = {}
    for o_ref, (off, wd, dt) in zip(o_refs, outs):
        if (off, wd) not in done:
            done[(off, wd)] = _dg(hb, w_ref[:, off:off + wd])
        o_ref[0] = done[(off, wd)].astype(dt)


def _inproj(x, mod, m0, nw, wb, outs):
    s, t, d = x.shape
    tm = _row_tile(t)
    xs = pl.BlockSpec((1, tm, d), lambda si, ti: (si, ti, 0))
    return pl.pallas_call(
        functools.partial(_inproj_kernel, outs=tuple(outs)),
        grid=(s, t // tm),
        in_specs=[xs, _mod_spec(mod, tm, m0), _mod_spec(mod, tm, m0 + 1), _resident((1, d)), _resident(wb.shape)],
        out_specs=[pl.BlockSpec((1, tm, wd), lambda si, ti: (si, ti, 0)) for _, wd, _ in outs],
        out_shape=[jax.ShapeDtypeStruct((s, t, wd), dt) for _, wd, dt in outs],
        compiler_params=_cparams(("arbitrary", "arbitrary")),
        name="inproj",
    )(x, mod, mod, nw.reshape(1, d), wb)


def _outproj_kernel(*refs, widths):
    n = len(widths)
    x_ref, g_ref, w_ref = refs[0], refs[1], refs[2]
    a_refs = refs[3:3 + n]
    o_ref = refs[3 + n]
    acc = None
    off = 0
    for a_ref, wd in zip(a_refs, widths):
        part = _dg(a_ref[0].astype(BF16), w_ref[off:off + wd, :])
        acc = part if acc is None else acc + part
        off += wd
    o_ref[0] = x_ref[0] + g_ref[0] * acc


def _outproj(x, mod, mg, wb, parts):
    s, t, d = x.shape
    tm = _row_tile(t)
    widths = tuple(p.shape[-1] for p in parts)
    xs = pl.BlockSpec((1, tm, d), lambda si, ti: (si, ti, 0))
    return pl.pallas_call(
        functools.partial(_outproj_kernel, widths=widths),
        grid=(s, t // tm),
        in_specs=[xs, _mod_spec(mod, tm, mg), _resident(wb.shape)]
                 + [pl.BlockSpec((1, tm, wd), lambda si, ti: (si, ti, 0)) for wd in widths],
        out_specs=xs,
        out_shape=jax.ShapeDtypeStruct(x.shape, F32),
        compiler_params=_cparams(("arbitrary", "arbitrary")),
        name="outproj",
    )(x, mod, wb, *parts)


def _final_norm_kernel(x_ref, w_ref, o_ref):
    x = x_ref[0]
    ms = jnp.mean(x * x, axis=-1, keepdims=True)
    o_ref[0] = x * lax.rsqrt(ms + EPS) * w_ref[...]


def _final_norm(x, w):
    s, t, d = x.shape
    tm = _row_tile(t)
    xs = pl.BlockSpec((1, tm, d), lambda si, ti: (si, ti, 0))
    return pl.pallas_call(
        _final_norm_kernel,
        grid=(s, t // tm),
        in_specs=[xs, pl.BlockSpec((1, d), lambda si, ti: (0, 0))],
        out_specs=xs,
        out_shape=jax.ShapeDtypeStruct(x.shape, F32),
        compiler_params=_cparams(("arbitrary", "arbitrary")),
        name="final_norm",
    )(x, w.reshape(1, d))


def _t5_bucket(dist):
    n = jnp.maximum(dist, 0)
    max_exact = NUM_BUCKETS // 2
    nf = jnp.maximum(n, 1).astype(F32)
    large = max_exact + (jnp.log(nf / max_exact) / math.log(MAX_DISTANCE / max_exact)
                         * (NUM_BUCKETS - max_exact)).astype(I32)
    large = jnp.minimum(large, NUM_BUCKETS - 1)
    return jnp.where(n < max_exact, n, large)


def _bias_tiles(bias_cols, blk, causal):
    assert blk >= MAX_DISTANCE
    n = blk
    h = bias_cols.shape[1]
    tiles = []
    for t in range(3):
        off = (2 - t) * n
        dist = n - 1 + off - jnp.arange(2 * n, dtype=I32)
        g = bias_cols[_t5_bucket(dist)]
        if causal:
            g = jnp.where((dist >= 0)[:, None], g, NEG)
        x = jnp.broadcast_to(g.T[:, None, :], (h, n, 2 * n))
        x = jnp.pad(x, ((0, 0), (0, 0), (0, 1))).reshape(h, n * (2 * n + 1))[:, :2 * n * n].reshape(h, n, 2 * n)
        tiles.append(x[:, :, n - 1:2 * n - 1])
    return jnp.stack(tiles, axis=1)


def _bias_decode(bias_cols, q_pos, n_blocks):
    k_pos = jnp.arange(n_blocks * BLK, dtype=I32)
    b = bias_cols[_t5_bucket(q_pos - k_pos)]
    b = jnp.where((k_pos <= q_pos)[:, None], b, NEG)
    return jnp.moveaxis(b.reshape(n_blocks, BLK, -1), 2, 1)


def _score_keys(s):
    bits = pltpu.bitcast(s, I32)
    return jnp.where(bits < 0, bits ^ jnp.int32(0x7FFFFFFF), bits)


def _select_topk(key_ref, nkb, topk, idx_bits):
    rows = key_ref.shape[1]
    col = lax.broadcasted_iota(I32, (rows, BLK), 1)

    def count(pred):
        def body(kb, acc):
            return acc + jnp.where(pred(key_ref[kb], kb * BLK + col), 1.0, 0.0)
        acc = lax.fori_loop(0, nkb, body, jnp.zeros((rows, BLK), F32))
        return jnp.sum(acc, axis=1, keepdims=True)

    kf = float(topk)

    def bit_body(i, carry):
        lo, cnt_lo = carry
        cand = lo + lax.shift_left(jnp.int32(1), jnp.int32(31) - i)
        cnt = count(lambda k, _: k >= cand)
        take = cnt >= kf
        return jnp.where(take, cand, lo), jnp.where(take, cnt, cnt_lo)

    total = jnp.zeros((rows, 1), F32) + jnp.asarray(nkb * BLK, F32)
    thr, cnt_ge = lax.fori_loop(0, 32, bit_body, (jnp.full((rows, 1), -2 ** 31, I32), total))
    need = kf - count(lambda k, _: k > thr)

    has_ties = jnp.max(cnt_ge) > kf

    def idx_body(i, p):
        cand = p + lax.shift_left(jnp.int32(1), jnp.int32(idx_bits - 1) - i)
        cnt = count(lambda k, ix: jnp.logical_and(k == thr, ix < cand))
        return jnp.where(cnt < need, cand, p)

    cut0 = jnp.zeros((rows, 1), I32) + jnp.where(has_ties, 0, 2 ** idx_bits - 1)
    cut = lax.fori_loop(0, jnp.where(has_ties, idx_bits, 0), idx_body, cut0)
    return thr, cut


def _in_topk(key, idx, thr, cut):
    return jnp.logical_or(key > thr, jnp.logical_and(key == thr, idx <= cut))


def _dsa_kernel(q_ref, qi_ref, sm_ref, kf_ref, vf_ref, smf_ref, bias_ref, o_ref, key_ref, msk_ref,
                *, topk, idx_bits):
    qb = pl.program_id(1)
    nkb = qb + 1
    row = lax.broadcasted_iota(I32, (BLK, BLK), 0)
    col = lax.broadcasted_iota(I32, (BLK, BLK), 1)
    q_pos = qb * BLK + row

    qi = qi_ref[0]
    wi = sm_ref[0][:, SM_WI:SM_WI + IDX_HEADS] * (IDX_HEADS ** -0.5)
    qi_st = jnp.concatenate([qi[:, h * IDX_DIM:(h + 1) * IDX_DIM] for h in range(IDX_HEADS)], axis=0).astype(BF16)
    wi_st = jnp.concatenate([wi[:, h:h + 1] for h in range(IDX_HEADS)], axis=0)

    def score_body(kb, carry):
        off = pl.multiple_of(kb * BLK, BLK)
        ki = smf_ref[0, pl.ds(off, BLK), :][:, SM_KI:SM_KI + IDX_DIM].astype(BF16)
        r = wi_st * jnp.maximum(_dg(qi_st, ki, _NT) * (IDX_DIM ** -0.5), 0.0)
        s = r[0:BLK]
        for h in range(1, IDX_HEADS):
            s = s + r[h * BLK:(h + 1) * BLK]
        s = jnp.where(kb * BLK + col <= q_pos, s, -jnp.inf)
        key_ref[kb] = _score_keys(s)
        return carry

    lax.fori_loop(0, nkb, score_body, 0)

    @pl.when(nkb * BLK <= topk)
    def _():
        def body(kb, carry):
            msk_ref[kb] = jnp.where(kb * BLK + col <= q_pos, 0.0, NEG)
            return carry
        lax.fori_loop(0, nkb, body, 0)

    @pl.when(nkb * BLK > topk)
    def _():
        thr, cut = _select_topk(key_ref, nkb, topk, idx_bits)

        def body(kb, carry):
            k_pos = kb * BLK + col
            sel = _in_topk(key_ref[kb], k_pos, thr, cut)
            msk_ref[kb] = jnp.where(k_pos <= q_pos, jnp.where(sel, 0.0, NEG), NEG)
            return carry
        lax.fori_loop(0, nkb, body, 0)

    q = q_ref[0] * (A_HEAD_DIM ** -0.5)
    gq = A_GROUP * BLK
    ns = range(A_KV_HEADS)
    qs = [jnp.concatenate([q[:, (n * A_GROUP + g) * A_HEAD_DIM:(n * A_GROUP + g + 1) * A_HEAD_DIM]
                           for g in range(A_GROUP)], axis=0).astype(BF16) for n in ns]

    def att_body(kb, carry):
        off = pl.multiple_of(kb * BLK, BLK)
        kblk = kf_ref[0, pl.ds(off, BLK), :].astype(BF16)
        vblk = vf_ref[0, pl.ds(off, BLK), :].astype(BF16)
        t = jnp.clip(kb - qb + 2, 0, 2)
        mk = msk_ref[kb]
        mk4 = jnp.concatenate([mk] * A_GROUP, axis=0)
        ss = [_dg(qs[n], kblk[:, n * A_HEAD_DIM:(n + 1) * A_HEAD_DIM], _NT) + bias_ref[n, t] + mk4 for n in ns]
        m_new = [jnp.maximum(carry[3 * n], jnp.max(ss[n], axis=1, keepdims=True)) for n in ns]
        ps = [jnp.exp(ss[n] - m_new[n]) for n in ns]
        pvs = [_dg(ps[n].astype(BF16), vblk[:, n * A_HEAD_DIM:(n + 1) * A_HEAD_DIM]) for n in ns]
        out = []
        for n in ns:
            alpha = jnp.exp(carry[3 * n] - m_new[n])
            out += [m_new[n], alpha * carry[3 * n + 1] + jnp.sum(ps[n], axis=1, keepdims=True),
                    alpha * carry[3 * n + 2] + pvs[n]]
        return tuple(out)

    init = (jnp.full((gq, 1), NEG, F32), jnp.zeros((gq, 1), F32), jnp.zeros((gq, A_HEAD_DIM), F32)) * A_KV_HEADS
    res = lax.fori_loop(0, nkb, att_body, init)
    for n in ns:
        o = res[3 * n + 2] / res[3 * n + 1]
        for g in range(A_GROUP):
            h = n * A_GROUP + g
            o_ref[0, :, h * A_HEAD_DIM:(h + 1) * A_HEAD_DIM] = o[g * BLK:(g + 1) * BLK, :]


def _dsa_prompt(q, qi, sm, k, v, bias_tiles, topk):
    b, t, _ = q.shape
    nb = t // BLK
    idx_bits = max(1, int(math.ceil(math.log2(t))))
    bt = bias_tiles.reshape(A_KV_HEADS, A_GROUP, 3, BLK, BLK)
    bt = jnp.moveaxis(bt, 1, 2).reshape(A_KV_HEADS, 3, A_GROUP * BLK, BLK)
    blk = lambda w: pl.BlockSpec((1, BLK, w), lambda bi, qb: (bi, qb, 0))
    full = lambda w: pl.BlockSpec((1, t, w), lambda bi, qb: (bi, 0, 0))
    return pl.pallas_call(
        functools.partial(_dsa_kernel, topk=topk, idx_bits=idx_bits),
        grid=(b, nb),
        in_specs=[blk(A_Q), blk(IDX_HEADS * IDX_DIM), blk(LANES), full(A_KV), full(A_KV), full(LANES),
                  pl.BlockSpec(bt.shape, lambda bi, qb: (0, 0, 0, 0))],
        out_specs=blk(A_Q),
        out_shape=jax.ShapeDtypeStruct((b, t, A_Q), F32),
        scratch_shapes=[pltpu.VMEM((nb, BLK, BLK), I32), pltpu.VMEM((nb, BLK, BLK), F32)],
        compiler_params=_cparams(("arbitrary", "arbitrary")),
        name="dsa_prompt",
    )(q, qi, sm, k, v, sm, bt)


def _gdn_kernel(qkv_ref, z_ref, sm_ref, s0_ref, cw_ref, par_ref, nw_ref, o_ref, sout_ref, ext_ref, st_ref):
    t = pl.program_id(1)
    nt = pl.num_programs(1)
    c = CHUNK
    halo = SUBLANES

    @pl.when(t == 0)
    def _():
        ext_ref[0:halo, :] = jnp.zeros((halo, B_CONV_DIM), F32)
        st_ref[...] = s0_ref[0]

    x = qkv_ref[0]
    ext_ref[halo:halo + c, :] = x
    conv = None
    for j in range(CONV_W):
        start = halo - (CONV_W - 1) + j
        term = ext_ref[start:start + c, :] * cw_ref[j:j + 1, :]
        conv = term if conv is None else conv + term
    ext_ref[0:halo, :] = x[c - halo:c, :]
    conv = _silu(conv)

    ri = lax.broadcasted_iota(I32, (c, c), 0)
    ci = lax.broadcasted_iota(I32, (c, c), 1)
    tri = ri >= ci
    stri = ri > ci
    eye = jnp.where(ri == ci, 1.0, 0.0)
    tri_bf = jnp.where(tri, 1.0, 0.0).astype(BF16)
    ones_bf = jnp.ones((c, c), BF16)

    sm = sm_ref[0]
    g_all = -jnp.exp(par_ref[0:1, :]) * _softplus(sm + par_ref[1:2, :])
    gc_all = _dot_exact_lhs(tri_bf, g_all)
    beta_all = jax.nn.sigmoid(sm)
    zz = z_ref[0]
    nw = nw_ref[...]
    gcs = [gc_all[:, SM_A + h:SM_A + h + 1] for h in range(B_HEADS)]
    gc_rows = _dot_exact_lhs(ones_bf, jnp.concatenate([eye * gc for gc in gcs], axis=1))

    hs = range(B_HEADS)
    qs = [conv[:, h * B_KEY_DIM:(h + 1) * B_KEY_DIM] for h in hs]
    ks = [conv[:, B_K + h * B_KEY_DIM:B_K + (h + 1) * B_KEY_DIM] for h in hs]
    vs = [conv[:, 2 * B_K + h * B_VAL_DIM:2 * B_K + (h + 1) * B_VAL_DIM] for h in hs]
    qs = [q * lax.rsqrt(jnp.sum(q * q, axis=-1, keepdims=True) + EPS) * (B_KEY_DIM ** -0.5) for q in qs]
    ks = [k * lax.rsqrt(jnp.sum(k * k, axis=-1, keepdims=True) + EPS) for k in ks]
    betas = [beta_all[:, SM_B + h:SM_B + h + 1] for h in hs]
    decays = [jnp.exp(jnp.where(tri, gcs[h] - gc_rows[:, h * c:(h + 1) * c], -jnp.inf)) for h in hs]
    kbs = [ks[h] * betas[h] for h in hs]
    vbs = [vs[h] * betas[h] for h in hs]
    k_bf = [k.astype(BF16) for k in ks]
    a_mats = [jnp.where(stri, _dg(kbs[h].astype(BF16), k_bf[h], _NT) * decays[h], 0.0) for h in hs]
    tms = [eye - a for a in a_mats]
    pws = [_dot3(a, a) for a in a_mats]
    steps = int(math.log2(c))
    for j in range(1, steps):
        tms = [tm + _dot3(tm, pw) for tm, pw in zip(tms, pws)]
        if j < steps - 1:
            pws = [_dot3(pw, pw) for pw in pws]
    egs = [jnp.exp(gc) for gc in gcs]
    tm_bf = [tm.astype(BF16) for tm in tms]
    us = [_dg(tm_bf[h], vbs[h].astype(BF16)) for h in hs]
    ws = [_dg(tm_bf[h], (kbs[h] * egs[h]).astype(BF16)) for h in hs]
    a_qks = [jnp.where(tri, _dg(qs[h].astype(BF16), k_bf[h], _NT) * decays[h], 0.0) for h in hs]

    s_olds = [st_ref[h] for h in hs]
    s_bf = [s.astype(BF16) for s in s_olds]
    v_news = [us[h] - _dg(ws[h].astype(BF16), s_bf[h]) for h in hs]
    os_ = [_dg((qs[h] * egs[h]).astype(BF16), s_bf[h]) + _dot1(a_qks[h], v_news[h]) for h in hs]
    g_lasts = [gc[c - 1:c, :] for gc in gcs]
    for h in hs:
        st_ref[h] = (s_olds[h] * jnp.exp(g_lasts[h])
                     + _dot1(ks[h] * jnp.exp(g_lasts[h] - gcs[h]), v_news[h], _TN))
    for h in hs:
        o = os_[h]
        ms = jnp.mean(o * o, axis=-1, keepdims=True)
        zh = zz[:, h * B_VAL_DIM:(h + 1) * B_VAL_DIM]
        o_ref[0, :, h * B_VAL_DIM:(h + 1) * B_VAL_DIM] = (o * lax.rsqrt(ms + EPS) * nw) * _silu(zh)

    @pl.when(t == nt - 1)
    def _():
        sout_ref[0] = st_ref[...]


def _gdn_params(a_log, dt_bias):
    par = jnp.zeros((SUBLANES, LANES), F32)
    return par.at[0, SM_A:SM_A + B_HEADS].set(a_log).at[1, SM_A:SM_A + B_HEADS].set(dt_bias)


def _gdn_prompt(qkv, z, sm, s0, conv_w, a_log, dt_bias, norm_w):
    b, t, _ = qkv.shape
    nt = t // CHUNK
    tok = lambda w: pl.BlockSpec((1, CHUNK, w), lambda bi, ti: (bi, ti, 0))
    const2 = lambda shp: pl.BlockSpec(shp, lambda bi, ti: (0, 0))
    st_spec = pl.BlockSpec((1, B_HEADS, B_KEY_DIM, B_VAL_DIM), lambda bi, ti: (bi, 0, 0, 0))
    o, s_out = pl.pallas_call(
        _gdn_kernel,
        grid=(b, nt),
        in_specs=[tok(B_CONV_DIM), tok(B_V), tok(LANES), st_spec,
                  const2((CONV_W, B_CONV_DIM)), const2((SUBLANES, LANES)), const2((1, B_VAL_DIM))],
        out_specs=[tok(B_V), st_spec],
        out_shape=[jax.ShapeDtypeStruct((b, t, B_V), F32),
                   jax.ShapeDtypeStruct((b, B_HEADS, B_KEY_DIM, B_VAL_DIM), F32)],
        scratch_shapes=[pltpu.VMEM((SUBLANES + CHUNK, B_CONV_DIM), F32),
                        pltpu.VMEM((B_HEADS, B_KEY_DIM, B_VAL_DIM), F32)],
        compiler_params=_cparams(("arbitrary", "arbitrary")),
        name="gdn_prompt",
    )(qkv, z, sm, s0, conv_w, _gdn_params(a_log, dt_bias), norm_w.reshape(1, B_VAL_DIM))
    return o, s_out


GDN_DEC_ROWS = 16


def _gdn_dec_kernel(qkv_ref, z_ref, sm_ref, cb_ref, s_ref, cw_ref, par_ref, nw_ref, o_ref, so_ref, oraw_ref):
    r = GDN_DEC_ROWS
    conv = qkv_ref[...] * cw_ref[CONV_W - 1:CONV_W, :]
    for j in range(CONV_W - 1):
        conv = conv + cb_ref[0, j] * cw_ref[j:j + 1, :]
    conv = _silu(conv)
    sm = sm_ref[...]
    g_all = -jnp.exp(par_ref[0:1, :]) * _softplus(sm + par_ref[1:2, :])
    beta_all = jax.nn.sigmoid(sm)
    ri = lax.broadcasted_iota(I32, (r, LANES), 0)
    ci = lax.broadcasted_iota(I32, (r, LANES), 1)
    eye_bf = jnp.where(ri == ci, 1.0, 0.0).astype(BF16)

    for h in range(B_HEADS):
        qh = conv[:, h * B_KEY_DIM:(h + 1) * B_KEY_DIM]
        kh = conv[:, B_K + h * B_KEY_DIM:B_K + (h + 1) * B_KEY_DIM]
        vh = conv[:, 2 * B_K + h * B_VAL_DIM:2 * B_K + (h + 1) * B_VAL_DIM]
        qh = qh * lax.rsqrt(jnp.sum(qh * qh, axis=-1, keepdims=True) + EPS) * (B_KEY_DIM ** -0.5)
        kh = kh * lax.rsqrt(jnp.sum(kh * kh, axis=-1, keepdims=True) + EPS)
        eg = jnp.exp(g_all[:, SM_A + h:SM_A + h + 1])
        beta = beta_all[:, SM_B + h:SM_B + h + 1]
        qk = jnp.sum(qh * kh, axis=-1, keepdims=True)
        k_t = _dot_exact_rhs(kh, eye_bf, _TN)
        q_t = _dot_exact_rhs(qh, eye_bf, _TN)
        for s in range(r):
            st = s_ref[0, s, h]
            kc = k_t[:, s:s + 1]
            qc = q_t[:, s:s + 1]
            k_s = jnp.sum(st * kc, axis=0, keepdims=True)
            q_s = jnp.sum(st * qc, axis=0, keepdims=True)
            eg_s = eg[s:s + 1, :]
            v_new = beta[s:s + 1, :] * (vh[s:s + 1, :] - eg_s * k_s)
            oraw_ref[s:s + 1, h * B_VAL_DIM:(h + 1) * B_VAL_DIM] = eg_s * q_s + qk[s:s + 1, :] * v_new
            so_ref[s, h] = st * eg_s + kc * v_new

    zz = z_ref[...]
    nw = nw_ref[...]
    for h in range(B_HEADS):
        o = oraw_ref[:, h * B_VAL_DIM:(h + 1) * B_VAL_DIM]
        ms = jnp.mean(o * o, axis=-1, keepdims=True)
        zh = zz[:, h * B_VAL_DIM:(h + 1) * B_VAL_DIM]
        o_ref[:, h * B_VAL_DIM:(h + 1) * B_VAL_DIM] = (o * lax.rsqrt(ms + EPS) * nw) * _silu(zh)


def _gdn_decode(qkv, z, sm, conv_t, state, j, conv_w, a_log, dt_bias, norm_w):
    b = qkv.shape[0]
    r = GDN_DEC_ROWS
    assert b % r == 0
    rows = lambda w: pl.BlockSpec((r, w), lambda i: (i, 0))
    const2 = lambda shp: pl.BlockSpec(shp, lambda i: (0, 0))
    o, s_out = pl.pallas_call(
        _gdn_dec_kernel,
        grid=(b // r,),
        in_specs=[rows(B_CONV_DIM), rows(B_V), rows(LANES),
                  pl.BlockSpec((1, CONV_W - 1, r, B_CONV_DIM), lambda i: (j, 0, i, 0)),
                  pl.BlockSpec((1, r, B_HEADS, B_KEY_DIM, B_VAL_DIM), lambda i: (j, i, 0, 0, 0)),
                  const2((CONV_W, B_CONV_DIM)), const2((SUBLANES, LANES)), const2((1, B_VAL_DIM))],
        out_specs=[rows(B_V), pl.BlockSpec((r, B_HEADS, B_KEY_DIM, B_VAL_DIM), lambda i: (i, 0, 0, 0))],
        out_shape=[jax.ShapeDtypeStruct((b, B_V), F32),
                   jax.ShapeDtypeStruct((b, B_HEADS, B_KEY_DIM, B_VAL_DIM), F32)],
        scratch_shapes=[pltpu.VMEM((r, B_V), F32)],
        compiler_params=_cparams(("arbitrary",)),
        name="gdn_decode",
    )(qkv, z, sm, conv_t, state, conv_w, _gdn_params(a_log, dt_bias), norm_w.reshape(1, B_VAL_DIM))
    return o, s_out


DIFF_HEADS_PER_STEP = 2


def _diff_kernel(lam_ref, q_ref, k_ref, v_ref, bias_ref, sw_ref, o_ref, *, lam_init):
    qb = pl.program_id(2)
    nkb = qb + 1
    nh = DIFF_HEADS_PER_STEP
    lane = lax.broadcasted_iota(I32, (TQ, C_HD), 1)
    q2 = []
    for h in range(nh):
        qh = q_ref[0, :, h * C_HD:(h + 1) * C_HD] * (C_HEAD_DIM ** -0.5)
        zero = jnp.zeros_like(qh)
        q2.append(jnp.concatenate([jnp.where(lane < C_HEAD_DIM, qh, zero),
                                   jnp.where(lane >= C_HEAD_DIM, qh, zero)], axis=0))

    def body(kb, carry):
        off = pl.multiple_of(kb * TQ, TQ)
        t = jnp.clip(kb - qb + 2, 0, 2)
        hs = range(nh)
        ss = [_dg(q2[h], k_ref[0, pl.ds(off, TQ), h * C_HD:(h + 1) * C_HD], _NT) for h in hs]
        ss = [(ss[h].reshape(2, TQ, TQ) + bias_ref[h, t][None]).reshape(2 * TQ, TQ) for h in hs]
        m_new = [jnp.maximum(carry[3 * h], jnp.max(ss[h], axis=1, keepdims=True)) for h in hs]
        ps = [jnp.exp(ss[h] - m_new[h]) for h in hs]
        pvs = [_dg(ps[h].astype(BF16), v_ref[0, pl.ds(off, TQ), h * C_HD:(h + 1) * C_HD]) for h in hs]
        out = []
        for h in hs:
            alpha = jnp.exp(carry[3 * h] - m_new[h])
            out += [m_new[h], alpha * carry[3 * h + 1] + jnp.sum(ps[h], axis=1, keepdims=True),
                    alpha * carry[3 * h + 2] + pvs[h]]
        return tuple(out)

    init = (jnp.full((2 * TQ, 1), NEG, F32), jnp.zeros((2 * TQ, 1), F32), jnp.zeros((2 * TQ, C_HD), F32)) * nh
    res = lax.fori_loop(0, nkb, body, init)
    for h in range(nh):
        _, l, acc = res[3 * h:3 * h + 3]
        on = acc / l
        o = on[0:TQ] - lam_ref[0] * on[TQ:2 * TQ]
        ms = jnp.mean(o * o, axis=-1, keepdims=True)
        o_ref[0, :, h * C_HD:(h + 1) * C_HD] = (o * lax.rsqrt(ms + EPS) * sw_ref[...]) * (1.0 - lam_init)


def _diff_prompt(q, k, v, bias_tiles, lam, subln_w, lam_init):
    b, t, _ = q.shape
    nh = DIFF_HEADS_PER_STEP
    w = nh * C_HD
    return pl.pallas_call(
        functools.partial(_diff_kernel, lam_init=lam_init),
        grid=(b, C_HEADS // nh, t // TQ),
        in_specs=[pl.BlockSpec(memory_space=pltpu.SMEM),
                  pl.BlockSpec((1, TQ, w), lambda bi, h, qb: (bi, qb, h)),
                  pl.BlockSpec((1, t, w), lambda bi, h, qb: (bi, 0, h)),
                  pl.BlockSpec((1, t, w), lambda bi, h, qb: (bi, 0, h)),
                  pl.BlockSpec((nh, 3, TQ, TQ), lambda bi, h, qb: (h, 0, 0, 0)),
                  pl.BlockSpec((1, C_HD), lambda bi, h, qb: (0, 0))],
        out_specs=pl.BlockSpec((1, TQ, w), lambda bi, h, qb: (bi, qb, h)),
        out_shape=jax.ShapeDtypeStruct((b, t, C_V), F32),
        compiler_params=_cparams(("arbitrary", "arbitrary", "arbitrary")),
        name="diff_prompt",
    )(lam.reshape(1), q, k, v, bias_tiles, subln_w.reshape(1, C_HD))


DIFF_DEC_PAGES = 8


def _diff_dec_kernel(pt_ref, lam_ref, q_ref, kn_ref, vn_ref, *rest, n_steps, lam_init):
    pp = DIFF_DEC_PAGES
    kps, vps = rest[:pp], rest[pp:2 * pp]
    bias_ref, bnew_ref, sw_ref, o_ref, m_ref, l_ref, acc_ref = rest[2 * pp:]
    p = pl.program_id(1)
    nr = 2 * C_HEADS

    @pl.when(p == 0)
    def _():
        m_ref[...] = jnp.full(m_ref.shape, NEG, F32)
        l_ref[...] = jnp.zeros(l_ref.shape, F32)
        acc_ref[...] = jnp.zeros(acc_ref.shape, F32)

    q8 = q_ref[0] * (C_HEAD_DIM ** -0.5)
    lane = lax.broadcasted_iota(I32, (C_HEADS, C_HD), 1)
    q_lo = jnp.where(lane < C_HEAD_DIM, q8, 0.0)
    q_hi = jnp.where(lane >= C_HEAD_DIM, q8, 0.0)
    qm = jnp.concatenate([q_lo, q_hi], axis=0).astype(BF16)

    rows = BLK * C_HEADS
    half = pp // 2
    cs = range(2)
    ss = [jnp.concatenate([_dg(qm, kps[c * half + i][0, 0].astype(BF16), _NT) + bias_ref[c * half + i]
                           for i in range(half)], axis=1) for c in cs]
    m_old = [m_ref[c] for c in cs]
    m_new = [jnp.maximum(m_old[c], jnp.max(ss[c], axis=1, keepdims=True)) for c in cs]
    ps = [jnp.exp(ss[c] - m_new[c]) for c in cs]
    pb = [pr.astype(BF16) for pr in ps]
    pvs = []
    for c in cs:
        out = None
        for i in range(half):
            part = _dg(pb[c][:, i * rows:(i + 1) * rows], vps[c * half + i][0, 0].astype(BF16))
            out = part if out is None else out + part
        pvs.append(out)
    for c in cs:
        alpha = jnp.exp(m_old[c] - m_new[c])
        l_ref[c] = alpha * l_ref[c] + jnp.sum(ps[c], axis=1, keepdims=True)
        acc_ref[c] = alpha * acc_ref[c] + pvs[c]
        m_ref[c] = m_new[c]

    @pl.when(p == n_steps - 1)
    def _():
        kn = kn_ref[0]
        s_new = jnp.concatenate([jnp.sum(q_lo * kn, axis=1, keepdims=True),
                                 jnp.sum(q_hi * kn, axis=1, keepdims=True)], axis=0) + bnew_ref[:, 0:1]
        vn2 = jnp.concatenate([vn_ref[0], vn_ref[0]], axis=0)
        m = jnp.maximum(jnp.maximum(m_ref[0], m_ref[1]), s_new)
        w0 = jnp.exp(m_ref[0] - m)
        w1 = jnp.exp(m_ref[1] - m)
        p_new = jnp.exp(s_new - m)
        l = w0 * l_ref[0] + w1 * l_ref[1] + p_new
        acc = w0 * acc_ref[0] + w1 * acc_ref[1] + p_new * vn2
        on = acc / l
        o = on[0:C_HEADS] - lam_ref[0] * on[C_HEADS:nr]
        ms = jnp.mean(o * o, axis=-1, keepdims=True)
        o_ref[0] = (o * lax.rsqrt(ms + EPS) * sw_ref[...]) * (1.0 - lam_init)


def _diff_decode(q8, k8, v8, cache_k, cache_v, j, pt_flat, n_pages, bias_pages, bias_new, lam, subln_w, lam_init):
    b = q8.shape[0]
    pp = DIFF_DEC_PAGES
    assert n_pages % pp == 0
    n_steps = n_pages // pp
    rows = BLK * C_HEADS
    row = pl.BlockSpec((1, C_HEADS, C_HD), lambda bi, p, pt: (bi, 0, 0))
    page = lambda i: pl.BlockSpec((1, 1, rows, C_HD),
                                  lambda bi, p, pt: (j, pt[bi * n_pages + p * pp + i], 0, 0))
    grid_spec = pltpu.PrefetchScalarGridSpec(
        num_scalar_prefetch=1,
        grid=(b, n_steps),
        in_specs=[pl.BlockSpec(memory_space=pltpu.SMEM), row, row, row]
                 + [page(i) for i in range(pp)] + [page(i) for i in range(pp)]
                 + [pl.BlockSpec((pp, 2 * C_HEADS, rows), lambda bi, p, pt: (p, 0, 0)),
                    pl.BlockSpec((2 * C_HEADS, LANES), lambda bi, p, pt: (0, 0)),
                    pl.BlockSpec((1, C_HD), lambda bi, p, pt: (0, 0))],
        out_specs=row,
        scratch_shapes=[pltpu.VMEM((2, 2 * C_HEADS, 1), F32), pltpu.VMEM((2, 2 * C_HEADS, 1), F32),
                        pltpu.VMEM((2, 2 * C_HEADS, C_HD), F32)],
    )
    return pl.pallas_call(
        functools.partial(_diff_dec_kernel, n_steps=n_steps, lam_init=lam_init),
        grid_spec=grid_spec,
        out_shape=jax.ShapeDtypeStruct((b, C_HEADS, C_HD), F32),
        compiler_params=_cparams(("arbitrary", "arbitrary")),
        name="diff_decode",
    )(pt_flat, lam.reshape(1), q8, k8, v8, *([cache_k] * pp), *([cache_v] * pp),
      bias_pages, bias_new, subln_w.reshape(1, C_HD))


def _diff_decode_bias(bias_cols, q_pos, n_pages):
    bd = _bias_decode(bias_cols, q_pos, n_pages + 1)
    past = jnp.moveaxis(bd[:n_pages], 1, 2)
    same = jnp.eye(C_HEADS, dtype=bool)
    tab = jnp.where(same[None, :, None, :], past[:, None, :, :], NEG)
    tab = tab.reshape(n_pages, C_HEADS, BLK * C_HEADS)
    tab = jnp.concatenate([tab, tab], axis=1)
    new = bd[n_pages, :, 0]
    new = jnp.broadcast_to(jnp.concatenate([new, new])[:, None], (2 * C_HEADS, LANES))
    return tab, new


def _idx_dec_kernel(pt_ref, qi_ref, sm_ref, *rest, n_pages):
    kps = rest[:n_pages]
    o_ref = rest[n_pages]
    qi = qi_ref[0]
    sm = sm_ref[0]
    rowi = lax.broadcasted_iota(I32, (SUBLANES, IDX_HEADS * IDX_DIM), 0)
    lane = lax.broadcasted_iota(I32, (SUBLANES, IDX_HEADS * IDX_DIM), 1)
    qsel = jnp.where(lane // IDX_DIM == rowi, qi, 0.0)
    qt = qsel[:, 0:IDX_DIM]
    for h in range(1, IDX_HEADS):
        qt = qt + qsel[:, h * IDX_DIM:(h + 1) * IDX_DIM]
    r8 = lax.broadcasted_iota(I32, (SUBLANES, LANES), 0)
    l8 = lax.broadcasted_iota(I32, (SUBLANES, LANES), 1)
    wsel = jnp.where(jnp.logical_and(r8 < IDX_HEADS, l8 == r8 + SM_WI), sm, 0.0)
    wcol = jnp.sum(wsel, axis=1, keepdims=True) * (IDX_HEADS ** -0.5)
    qh, ql = _split2(qt)
    for p in range(n_pages):
        kh, kl = _split2(kps[p][0, 0])
        d = _dg(qh, kh) + (_dg(qh, kl) + _dg(ql, kh))
        rel = jnp.maximum(d * (IDX_DIM ** -0.5), 0.0)
        o_ref[0, p] = jnp.sum(wcol * rel, axis=0, keepdims=True)
    d_new = jnp.sum(qt * sm[:, SM_KI:SM_KI + IDX_DIM], axis=1, keepdims=True)
    s_new = jnp.sum(wcol * jnp.maximum(d_new * (IDX_DIM ** -0.5), 0.0), axis=0, keepdims=True)
    o_ref[0, n_pages] = jnp.broadcast_to(s_new, (1, BLK))


def _idx_decode(qi, sm, cache_kidx_t, j, pt_flat, n_pages):
    b = qi.shape[0]
    page = lambda p: pl.BlockSpec((1, 1, IDX_DIM, BLK), lambda bi, pt: (j, pt[bi * n_pages + p], 0, 0))
    grid_spec = pltpu.PrefetchScalarGridSpec(
        num_scalar_prefetch=1,
        grid=(b,),
        in_specs=[pl.BlockSpec((1, 1, IDX_HEADS * IDX_DIM), lambda bi, pt: (bi, 0, 0)),
                  pl.BlockSpec((1, 1, LANES), lambda bi, pt: (bi, 0, 0))]
                 + [page(p) for p in range(n_pages)],
        out_specs=pl.BlockSpec((1, n_pages + 1, 1, BLK), lambda bi, pt: (bi, 0, 0, 0)),
    )
    return pl.pallas_call(
        functools.partial(_idx_dec_kernel, n_pages=n_pages),
        grid_spec=grid_spec,
        out_shape=jax.ShapeDtypeStruct((b, n_pages + 1, 1, BLK), F32),
        compiler_params=_cparams(("arbitrary",)),
        name="idx_decode",
    )(pt_flat, qi, sm, *([cache_kidx_t] * n_pages))


def _sel_dec_kernel(s_ref, o_ref, key_ref, *, n_valid, topk, idx_bits):
    nkb, rows, _ = s_ref.shape
    col = lax.broadcasted_iota(I32, (rows, BLK), 1)
    for kb in range(nkb):
        s = jnp.where(kb * BLK + col < n_valid, s_ref[kb], -jnp.inf)
        key_ref[kb] = _score_keys(s)
    thr, cut = _select_topk(key_ref, nkb, topk, idx_bits)
    for kb in range(nkb):
        k_pos = kb * BLK + col
        sel = _in_topk(key_ref[kb], k_pos, thr, cut)
        o_ref[kb] = jnp.where(k_pos < n_valid, jnp.where(sel, 0.0, NEG), NEG)


def _sel_decode(scores, n_valid, topk):
    nkb, rows, _ = scores.shape
    idx_bits = max(1, int(math.ceil(math.log2(nkb * BLK))))
    return pl.pallas_call(
        functools.partial(_sel_dec_kernel, n_valid=n_valid, topk=topk, idx_bits=idx_bits),
        out_shape=jax.ShapeDtypeStruct(scores.shape, F32),
        scratch_shapes=[pltpu.VMEM(scores.shape, I32)],
        compiler_params=pltpu.CompilerParams(vmem_limit_bytes=VMEM_LIMIT),
        name="sel_decode",
    )(scores)


def _dsa_dec_kernel(pt_ref, q_ref, kn_ref, vn_ref, *rest, n_pages):
    kps, vps = rest[:n_pages], rest[n_pages:2 * n_pages]
    msk_ref, bias_ref, o_ref = rest[2 * n_pages:]

    q = q_ref[0] * (A_HEAD_DIM ** -0.5)
    rowi = lax.broadcasted_iota(I32, (A_HEADS, A_Q), 0)
    lane = lax.broadcasted_iota(I32, (A_HEADS, A_Q), 1)
    qsel = jnp.where(lane // A_HEAD_DIM == rowi, q, 0.0)
    halves = []
    for n in range(A_KV_HEADS):
        acc = None
        for g in range(A_GROUP):
            h = n * A_GROUP + g
            part = qsel[:, h * A_HEAD_DIM:(h + 1) * A_HEAD_DIM]
            acc = part if acc is None else acc + part
        halves.append(acc)
    qt = jnp.concatenate(halves, axis=1)
    qt_bf = qt.astype(BF16)

    s = jnp.concatenate([_dg(qt_bf, kps[i][0, 0].astype(BF16)) + bias_ref[i] + msk_ref[i, 0]
                         for i in range(n_pages)], axis=1)
    s_new = (jnp.sum(qt * kn_ref[0], axis=1, keepdims=True) + bias_ref[n_pages][:, 0:1]
             + msk_ref[n_pages, 0][:, 0:1])
    m = jnp.maximum(jnp.max(s, axis=1, keepdims=True), s_new)
    p = jnp.exp(s - m)
    p_new = jnp.exp(s_new - m)
    l = jnp.sum(p, axis=1, keepdims=True) + p_new
    pb = p.astype(BF16)
    acc = p_new * vn_ref[0]
    for i in range(n_pages):
        acc = acc + _dg(pb[:, i * BLK:(i + 1) * BLK], vps[i][0, 0].astype(BF16), _NT)
    on = acc / l
    for h in range(A_HEADS):
        n = h // A_GROUP
        o_ref[0, :, h * A_HEAD_DIM:(h + 1) * A_HEAD_DIM] = on[h:h + 1, n * A_HEAD_DIM:(n + 1) * A_HEAD_DIM]


def _dsa_decode(q, k_new, v_new, cache_kt, cache_vt, j, pt_flat, n_pages, mask, bias_dec):
    b = q.shape[0]
    page = lambda i: pl.BlockSpec((1, 1, A_KV, BLK), lambda bi, pt: (j, pt[bi * n_pages + i], 0, 0))
    rowspec = lambda w: pl.BlockSpec((1, 1, w), lambda bi, pt: (bi, 0, 0))
    grid_spec = pltpu.PrefetchScalarGridSpec(
        num_scalar_prefetch=1,
        grid=(b,),
        in_specs=[rowspec(A_Q), rowspec(A_KV), rowspec(A_KV)]
                 + [page(i) for i in range(n_pages)] + [page(i) for i in range(n_pages)]
                 + [pl.BlockSpec((n_pages + 1, 1, 1, BLK), lambda bi, pt: (0, bi, 0, 0)),
                    pl.BlockSpec((n_pages + 1, A_HEADS, BLK), lambda bi, pt: (0, 0, 0))],
        out_specs=rowspec(A_Q),
    )
    return pl.pallas_call(
        functools.partial(_dsa_dec_kernel, n_pages=n_pages),
        grid_spec=grid_spec,
        out_shape=jax.ShapeDtypeStruct((b, 1, A_Q), F32),
        compiler_params=_cparams(("arbitrary",)),
        name="dsa_decode",
    )(pt_flat, q, k_new, v_new, *([cache_kt] * n_pages), *([cache_vt] * n_pages), mask, bias_dec)


def _even_weights(w_in):
    sizes = (A_Q, A_KV, A_KV, IDX_HEADS * IDX_DIM, IDX_DIM, IDX_HEADS, B_CONV_DIM, B_V, B_HEADS, B_HEADS)
    offs = np.concatenate([[0], np.cumsum(sizes)])
    seg = lambda i: w_in[:, int(offs[i]):int(offs[i + 1])]
    pad = LANES - (IDX_DIM + IDX_HEADS + 2 * B_HEADS)
    small = jnp.concatenate([seg(4), seg(5), seg(8), seg(9), jnp.zeros((w_in.shape[0], pad), w_in.dtype)], axis=1)
    return jnp.concatenate([seg(0), seg(1), seg(2), seg(3), small, seg(6), seg(7)], axis=1).astype(BF16)


def _seq_outs(widths, dtype=F32):
    outs, off = [], 0
    for wd in widths:
        outs.append((off, wd, dtype))
        off += wd
    return tuple(outs)


_EVEN_OUTS = _seq_outs((A_Q, A_KV, A_KV, IDX_HEADS * IDX_DIM, LANES, B_CONV_DIM, B_V))
_ODD_OUTS = _seq_outs((C_QK, C_QK, C_V))
_ODD_OUTS_PROMPT = _ODD_OUTS[1:] + _seq_outs((C_QK, C_QK, C_V), BF16)


def kernel(x_prompt, x_sample, c_prompt, c_sample, cache_A_k, cache_A_v, cache_A_kidx, cache_C_k, cache_C_v, state_B_ssm, state_B_conv, page_table, rel_bias, ada_w, ada_b, norm_w, final_norm_w, ffn_w1, ffn_w2, ab_w_in, ab_w_out, gdn_conv_w, gdn_a_log, gdn_dt_bias, gdn_norm_w, c_w_in, c_w_out, c_lambda_q1, c_lambda_k1, c_lambda_q2, c_lambda_k2, c_subln_w):
    depth = ada_w.shape[0]
    bp, tp, d = x_prompt.shape
    bs, ts, _ = x_sample.shape
    assert ts == 1 and tp % TQ == 0 and tp % CHUNK == 0
    n_pages = page_table.shape[1]
    page = cache_A_k.shape[2]
    assert page == BLK
    n_phys = cache_A_k.shape[1]
    past_len = n_pages * page
    topk_p = min(TOPK_MAX, tp // 4)
    topk_s = min(TOPK_MAX, (past_len + ts) // 4)
    pt_flat = page_table.reshape(-1).astype(I32)

    n_c = bp + bs
    n_c_pad = -(-n_c // SUBLANES) * SUBLANES
    c_all = jnp.pad(jnp.concatenate([c_prompt, c_sample], axis=0), ((0, n_c_pad - n_c), (0, 0)))
    mod_all = _modulation(c_all, ada_w, ada_b)

    w1b = ffn_w1.astype(BF16)
    w2b = ffn_w2.astype(BF16)
    ab_in_b = [_even_weights(ab_w_in[j]) for j in range(ab_w_in.shape[0])]
    ab_out_b = ab_w_out.astype(BF16)
    c_in_b = c_w_in.astype(BF16)
    c_out_b = c_w_out.astype(BF16)

    bias_a = _bias_tiles(rel_bias[:, :A_HEADS], BLK, False)
    bias_c = _bias_tiles(rel_bias[:, A_HEADS:], TQ, True)
    bias_a_dec = _bias_decode(rel_bias[:, :A_HEADS], past_len, n_pages + 1)
    bias_c_pages, bias_c_new = _diff_decode_bias(rel_bias[:, A_HEADS:], past_len, n_pages)

    n_ab = cache_A_k.shape[0]
    cache_a_kt = jnp.transpose(cache_A_k, (0, 1, 3, 4, 2)).reshape(n_ab, n_phys, A_KV, page)
    cache_a_vt = jnp.transpose(cache_A_v, (0, 1, 3, 4, 2)).reshape(n_ab, n_phys, A_KV, page)
    cache_a_it = jnp.transpose(cache_A_kidx, (0, 1, 3, 2))
    cache_c_k = cache_C_k.reshape(cache_C_k.shape[0], n_phys, page * C_HEADS, C_HD)
    cache_c_v = cache_C_v.reshape(cache_C_v.shape[0], n_phys, page * C_HEADS, C_HD)
    conv_t = jnp.transpose(state_B_conv, (0, 2, 1, 3))

    xp = x_prompt
    xs = x_sample.reshape(1, bs, d)
    new_p = [[] for _ in range(7)]
    new_s = [[] for _ in range(7)]

    for i in range(depth):
        j = i // 2
        mod_p = mod_all[i, :bp].reshape(bp, 1, N_MOD * d)
        mod_s = mod_all[i, bp:bp + bs].reshape(1, bs, N_MOD * d)
        xp = _ffn(xp, mod_p, 0, norm_w[i, 0], w1b[i, 0], w2b[i, 0])
        xs = _ffn(xs, mod_s, 0, norm_w[i, 0], w1b[i, 0], w2b[i, 0])
        if i % 2 == 0:
            q, k, v, qi, sm, qkv, z = _inproj(xp, mod_p, 3, norm_w[i, 1], ab_in_b[j], _EVEN_OUTS)
            o_a = _dsa_prompt(q, qi, sm, k, v, bias_a, topk_p)
            o_b, s_new = _gdn_prompt(qkv, z, sm, jnp.zeros((bp, B_HEADS, B_KEY_DIM, B_VAL_DIM), F32),
                                     gdn_conv_w[j], gdn_a_log[j], gdn_dt_bias[j], gdn_norm_w[j])
            xp = _outproj(xp, mod_p, 5, ab_out_b[j], [o_a, o_b])
            new_p[0].append(k.reshape(bp, tp, A_KV_HEADS, A_HEAD_DIM))
            new_p[1].append(v.reshape(bp, tp, A_KV_HEADS, A_HEAD_DIM))
            new_p[2].append(sm[:, :, SM_KI:SM_KI + IDX_DIM])
            new_p[3].append(s_new)
            new_p[4].append(qkv[:, tp - (CONV_W - 1):, :])
            q, k, v, qi, sm, qkv, z = _inproj(xs, mod_s, 3, norm_w[i, 1], ab_in_b[j], _EVEN_OUTS)
            as_rows = lambda a: a.reshape(bs, 1, a.shape[-1])
            scores = _idx_decode(as_rows(qi), as_rows(sm), cache_a_it, j, pt_flat, n_pages)
            scores = jnp.moveaxis(scores.reshape(bs, n_pages + 1, BLK), 1, 0)
            mask = _sel_decode(scores, past_len + 1, topk_s).reshape(n_pages + 1, bs, 1, BLK)
            o_a = _dsa_decode(as_rows(q), as_rows(k), as_rows(v), cache_a_kt, cache_a_vt, j, pt_flat, n_pages,
                              mask, bias_a_dec)
            o_b, s_new = _gdn_decode(qkv.reshape(bs, B_CONV_DIM), z.reshape(bs, B_V), sm.reshape(bs, LANES),
                                     conv_t, state_B_ssm, j, gdn_conv_w[j], gdn_a_log[j], gdn_dt_bias[j],
                                     gdn_norm_w[j])
            xs = _outproj(xs, mod_s, 5, ab_out_b[j], [o_a.reshape(1, bs, A_Q), o_b.reshape(1, bs, B_V)])
            new_s[0].append(k.reshape(bs, 1, A_KV_HEADS, A_HEAD_DIM))
            new_s[1].append(v.reshape(bs, 1, A_KV_HEADS, A_HEAD_DIM))
            new_s[2].append(sm.reshape(bs, 1, LANES)[:, :, SM_KI:SM_KI + IDX_DIM])
            new_s[3].append(s_new)
            new_s[4].append(jnp.concatenate([state_B_conv[j], as_rows(qkv)], axis=1)[:, 1:, :])
        else:
            lam_init = 0.8 - 0.6 * math.exp(-0.3 * i)
            lam = (jnp.exp(jnp.sum(c_lambda_q1[j] * c_lambda_k1[j]))
                   - jnp.exp(jnp.sum(c_lambda_q2[j] * c_lambda_k2[j])) + lam_init).astype(F32)
            k, v, qb16, kb16, vb16 = _inproj(xp, mod_p, 3, norm_w[i, 1], c_in_b[j], _ODD_OUTS_PROMPT)
            o = _diff_prompt(qb16, kb16, vb16, bias_c, lam, c_subln_w[j], lam_init)
            xp = _outproj(xp, mod_p, 5, c_out_b[j], [o])
            new_p[5].append(k.reshape(bp, tp, C_HEADS, C_HD))
            new_p[6].append(v.reshape(bp, tp, C_HEADS, C_HD))
            q, k, v = _inproj(xs, mod_s, 3, norm_w[i, 1], c_in_b[j], _ODD_OUTS)
            as_heads = lambda a: a.reshape(bs, C_HEADS, C_HD)
            o = _diff_decode(as_heads(q), as_heads(k), as_heads(v), cache_c_k, cache_c_v, j, pt_flat, n_pages,
                             bias_c_pages, bias_c_new, lam, c_subln_w[j], lam_init)
            xs = _outproj(xs, mod_s, 5, c_out_b[j], [o.reshape(1, bs, C_V)])
            new_s[5].append(k.reshape(bs, 1, C_HEADS, C_HD))
            new_s[6].append(v.reshape(bs, 1, C_HEADS, C_HD))
        xp = _ffn(xp, mod_p, 6, norm_w[i, 2], w1b[i, 1], w2b[i, 1])
        xs = _ffn(xs, mod_s, 6, norm_w[i, 2], w1b[i, 1], w2b[i, 1])

    y_prompt = _final_norm(xp, final_norm_w)
    y_sample = _final_norm(xs, final_norm_w).reshape(bs, 1, d)
    sp = [jnp.stack(lst) for lst in new_p]
    ss = [jnp.stack(lst) for lst in new_s]
    return (y_prompt, y_sample, *sp, *ss)
```

```python
import functools
import math

import numpy as np
import jax
import jax.numpy as jnp
from jax import lax
from jax.experimental import pallas as pl
from jax.experimental.pallas import tpu as pltpu

F32 = jnp.float32
BF16 = jnp.bfloat16
I32 = jnp.int32

A_HEADS = 8
A_KV_HEADS = 2
A_GROUP = A_HEADS // A_KV_HEADS
A_HEAD_DIM = 64
IDX_HEADS = 4
IDX_DIM = 64
TOPK_MAX = 256
B_HEADS = 4
B_KEY_DIM = 128
B_VAL_DIM = 128
CONV_W = 4
CHUNK = 64
C_HEADS = 8
C_HEAD_DIM = 64
NUM_BUCKETS = 32
MAX_DISTANCE = 128
N_MOD = 9
EPS = 1e-6

A_Q = A_HEADS * A_HEAD_DIM
A_KV = A_KV_HEADS * A_HEAD_DIM
B_K = B_HEADS * B_KEY_DIM
B_V = B_HEADS * B_VAL_DIM
B_CONV_DIM = 2 * B_K + B_V
C_QK = C_HEADS * 2 * C_HEAD_DIM
C_V = C_HEADS * 2 * C_HEAD_DIM
C_HD = 2 * C_HEAD_DIM

LANES = 128
SUBLANES = 8
VMEM_LIMIT = 56 * 1024 * 1024

BLK = 128
TQ = 256
NEG = -1e30

SM_KI = 0
SM_WI = IDX_DIM
SM_A = SM_WI + IDX_HEADS
SM_B = SM_A + B_HEADS

_NT = (((1,), (1,)), ((), ()))
_NN = (((1,), (0,)), ((), ()))
_TN = (((0,), (0,)), ((), ()))


def _cparams(sem):
    return pltpu.CompilerParams(dimension_semantics=sem, vmem_limit_bytes=VMEM_LIMIT)


def _dg(a, b, dims=_NN):
    return lax.dot_general(a, b, dims, preferred_element_type=F32)


def _dot1(a, b, dims=_NN):
    return _dg(a.astype(BF16), b.astype(BF16), dims)


def _split2(x):
    hi = x.astype(BF16)
    lo = (x - hi.astype(F32)).astype(BF16)
    return hi, lo


def _split3(x):
    b1 = x.astype(BF16)
    r1 = x - b1.astype(F32)
    b2 = r1.astype(BF16)
    b3 = (r1 - b2.astype(F32)).astype(BF16)
    return b1, b2, b3


def _dot3(a, b, dims=_NN):
    ah, al = _split2(a)
    bh, bl = _split2(b)
    return _dg(ah, bh, dims) + (_dg(ah, bl, dims) + _dg(al, bh, dims))


def _dot_exact_lhs(a_bf, b, dims=_NN):
    b1, b2, b3 = _split3(b)
    return _dg(a_bf, b1, dims) + (_dg(a_bf, b2, dims) + _dg(a_bf, b3, dims))


def _dot_exact_rhs(a, b_bf, dims=_NN):
    a1, a2, a3 = _split3(a)
    return _dg(a1, b_bf, dims) + (_dg(a2, b_bf, dims) + _dg(a3, b_bf, dims))


def _silu(x):
    return x * jax.nn.sigmoid(x)


def _softplus(x):
    return jnp.maximum(x, 0.0) + jnp.log(1.0 + jnp.exp(-jnp.abs(x)))


def _norm_mod(x, nw, sc, sh):
    ms = jnp.mean(x * x, axis=-1, keepdims=True)
    return (x * lax.rsqrt(ms + EPS) * nw) * (1.0 + sc) + sh


def _softmax_step(s, m, l, acc, pv):
    m_new = jnp.maximum(m, jnp.max(s, axis=1, keepdims=True))
    alpha = jnp.exp(m - m_new)
    p = jnp.exp(s - m_new)
    return m_new, alpha * l + jnp.sum(p, axis=1, keepdims=True), alpha * acc + pv(p)


def _mod_kernel(c_ref, w_ref, b_ref, o_ref):
    s = _silu(c_ref[...]).astype(BF16)
    o_ref[0] = _dg(s, w_ref[0].astype(BF16)) + b_ref[0]


def _modulation(c_all, ada_w, ada_b):
    depth, d, n = ada_w.shape
    m = c_all.shape[0]
    tn = 1024
    return pl.pallas_call(
        _mod_kernel,
        grid=(depth, n // tn),
        in_specs=[pl.BlockSpec((m, d), lambda i, j: (0, 0)),
                  pl.BlockSpec((1, d, tn), lambda i, j: (i, 0, j)),
                  pl.BlockSpec((1, 1, tn), lambda i, j: (i, 0, j))],
        out_specs=pl.BlockSpec((1, m, tn), lambda i, j: (i, 0, j)),
        out_shape=jax.ShapeDtypeStruct((depth, m, n), F32),
        compiler_params=_cparams(("arbitrary", "arbitrary")),
        name="adaln_mod",
    )(c_all, ada_w, ada_b.reshape(depth, 1, n))


def _row_tile(t):
    return min(512, t)


def _mod_spec(mod, tm, m):
    r = mod.shape[1]
    d = mod.shape[2] // N_MOD
    if r == 1:
        return pl.BlockSpec((1, 1, d), lambda s, t: (s, 0, m))
    return pl.BlockSpec((1, tm, d), lambda s, t: (s, t, m))


def _resident(shape):
    nd = len(shape)
    return pl.BlockSpec(shape, lambda s, t: (0,) * nd, pipeline_mode=pl.Buffered(1))


def _ffn_kernel(x_ref, sh_ref, sc_ref, g_ref, nw_ref, w1_ref, w2_ref, o_ref, acc_ref, *, fc):
    x = x_ref[0]
    hb = _norm_mod(x, nw_ref[...], sc_ref[0], sh_ref[0]).astype(BF16)
    f = w2_ref.shape[0]
    for c in range(f // fc):
        gt = _dg(hb, w1_ref[:, c * fc:(c + 1) * fc])
        up = _dg(hb, w1_ref[:, f + c * fc:f + (c + 1) * fc])
        a = (_silu(gt) * up).astype(BF16)
        contrib = _dg(a, w2_ref[c * fc:(c + 1) * fc, :])
        if c == 0:
            acc_ref[...] = contrib
        else:
            acc_ref[...] += contrib
    o_ref[0] = x + (0.5 * g_ref[0]) * acc_ref[...]


def _ffn(x, mod, m0, nw, w1b, w2b):
    s, t, d = x.shape
    tm = _row_tile(t)
    f = w2b.shape[0]
    fc = 256 if f % 256 == 0 else LANES
    xs = pl.BlockSpec((1, tm, d), lambda si, ti: (si, ti, 0))
    return pl.pallas_call(
        functools.partial(_ffn_kernel, fc=fc),
        grid=(s, t // tm),
        in_specs=[xs, _mod_spec(mod, tm, m0), _mod_spec(mod, tm, m0 + 1), _mod_spec(mod, tm, m0 + 2),
                  _resident((1, d)), _resident(w1b.shape), _resident(w2b.shape)],
        out_specs=xs,
        out_shape=jax.ShapeDtypeStruct(x.shape, F32),
        scratch_shapes=[pltpu.VMEM((tm, d), F32)],
        compiler_params=_cparams(("arbitrary", "arbitrary")),
        name="ffn",
    )(x, mod, mod, mod, nw.reshape(1, d), w1b, w2b)


def _inproj_kernel(x_ref, sh_ref, sc_ref, nw_ref, w_ref, *o_refs, outs):
    hb = _norm_mod(x_ref[0], nw_ref[...], sc_ref[0], sh_ref[0]).astype(BF16)
    done = {}
    for o_ref, (off, wd, dt) in zip(o_refs, outs):
        if (off, wd) not in done:
            done[(off, wd)] = _dg(hb, w_ref[:, off:off + wd])
        o_ref[0] = done[(off, wd)].astype(dt)


def _inproj(x, mod, m0, nw, wb, outs):
    s, t, d = x.shape
    tm = _row_tile(t)
    xs = pl.BlockSpec((1, tm, d), lambda si, ti: (si, ti, 0))
    return pl.pallas_call(
        functools.partial(_inproj_kernel, outs=tuple(outs)),
        grid=(s, t // tm),
        in_specs=[xs, _mod_spec(mod, tm, m0), _mod_spec(mod, tm, m0 + 1), _resident((1, d)), _resident(wb.shape)],
        out_specs=[pl.BlockSpec((1, tm, wd), lambda si, ti: (si, ti, 0)) for _, wd, _ in outs],
        out_shape=[jax.ShapeDtypeStruct((s, t, wd), dt) for _, wd, dt in outs],
        compiler_params=_cparams(("arbitrary", "arbitrary")),
        name="inproj",
    )(x, mod, mod, nw.reshape(1, d), wb)


def _outproj_kernel(*refs, widths):
    n = len(widths)
    x_ref, g_ref, w_ref = refs[0], refs[1], refs[2]
    a_refs = refs[3:3 + n]
    o_ref = refs[3 + n]
    acc = None
    off = 0
    for a_ref, wd in zip(a_refs, widths):
        part = _dg(a_ref[0].astype(BF16), w_ref[off:off + wd, :])
        acc = part if acc is None else acc + part
        off += wd
    o_ref[0] = x_ref[0] + g_ref[0] * acc


def _outproj(x, mod, mg, wb, parts):
    s, t, d = x.shape
    tm = _row_tile(t)
    widths = tuple(p.shape[-1] for p in parts)
    xs = pl.BlockSpec((1, tm, d), lambda si, ti: (si, ti, 0))
    return pl.pallas_call(
        functools.partial(_outproj_kernel, widths=widths),
        grid=(s, t // tm),
        in_specs=[xs, _mod_spec(mod, tm, mg), _resident(wb.shape)]
                 + [pl.BlockSpec((1, tm, wd), lambda si, ti: (si, ti, 0)) for wd in widths],
        out_specs=xs,
        out_shape=jax.ShapeDtypeStruct(x.shape, F32),
        compiler_params=_cparams(("arbitrary", "arbitrary")),
        name="outproj",
    )(x, mod, wb, *parts)


def _final_norm_kernel(x_ref, w_ref, o_ref):
    x = x_ref[0]
    ms = jnp.mean(x * x, axis=-1, keepdims=True)
    o_ref[0] = x * lax.rsqrt(ms + EPS) * w_ref[...]


def _final_norm(x, w):
    s, t, d = x.shape
    tm = _row_tile(t)
    xs = pl.BlockSpec((1, tm, d), lambda si, ti: (si, ti, 0))
    return pl.pallas_call(
        _final_norm_kernel,
        grid=(s, t // tm),
        in_specs=[xs, pl.BlockSpec((1, d), lambda si, ti: (0, 0))],
        out_specs=xs,
        out_shape=jax.ShapeDtypeStruct(x.shape, F32),
        compiler_params=_cparams(("arbitrary", "arbitrary")),
        name="final_norm",
    )(x, w.reshape(1, d))


def _t5_bucket(dist):
    n = jnp.maximum(dist, 0)
    max_exact = NUM_BUCKETS // 2
    nf = jnp.maximum(n, 1).astype(F32)
    large = max_exact + (jnp.log(nf / max_exact) / math.log(MAX_DISTANCE / max_exact)
                         * (NUM_BUCKETS - max_exact)).astype(I32)
    large = jnp.minimum(large, NUM_BUCKETS - 1)
    return jnp.where(n < max_exact, n, large)


def _bias_tiles(bias_cols, blk, causal):
    assert blk >= MAX_DISTANCE
    n = blk
    h = bias_cols.shape[1]
    tiles = []
    for t in range(3):
        off = (2 - t) * n
        dist = n - 1 + off - jnp.arange(2 * n, dtype=I32)
        g = bias_cols[_t5_bucket(dist)]
        if causal:
            g = jnp.where((dist >= 0)[:, None], g, NEG)
        x = jnp.broadcast_to(g.T[:, None, :], (h, n, 2 * n))
        x = jnp.pad(x, ((0, 0), (0, 0), (0, 1))).reshape(h, n * (2 * n + 1))[:, :2 * n * n].reshape(h, n, 2 * n)
        tiles.append(x[:, :, n - 1:2 * n - 1])
    return jnp.stack(tiles, axis=1)


def _bias_decode(bias_cols, q_pos, n_blocks):
    k_pos = jnp.arange(n_blocks * BLK, dtype=I32)
    b = bias_cols[_t5_bucket(q_pos - k_pos)]
    b = jnp.where((k_pos <= q_pos)[:, None], b, NEG)
    return jnp.moveaxis(b.reshape(n_blocks, BLK, -1), 2, 1)


def _score_keys(s):
    bits = pltpu.bitcast(s, I32)
    return jnp.where(bits < 0, bits ^ jnp.int32(0x7FFFFFFF), bits)


def _select_topk(key_ref, nkb, topk, idx_bits, key_axis=1):
    tile = tuple(key_ref.shape[1:])
    assert tile[key_axis] == BLK
    stat = tuple(1 if a == key_axis else n for a, n in enumerate(tile))
    rows = stat
    col = lax.broadcasted_iota(I32, tile, key_axis)

    def count(pred):
        def body(kb, acc):
            return acc + jnp.where(pred(key_ref[kb], kb * BLK + col), 1.0, 0.0)
        acc = lax.fori_loop(0, nkb, body, jnp.zeros(tile, F32))
        return jnp.sum(acc, axis=key_axis, keepdims=True)

    kf = float(topk)

    def bit_body(i, carry):
        lo, cnt_lo = carry
        cand = lo + lax.shift_left(jnp.int32(1), jnp.int32(31) - i)
        cnt = count(lambda k, _: k >= cand)
        take = cnt >= kf
        return jnp.where(take, cand, lo), jnp.where(take, cnt, cnt_lo)

    total = jnp.zeros(rows, F32) + jnp.asarray(nkb * BLK, F32)
    thr, cnt_ge = lax.fori_loop(0, 32, bit_body, (jnp.full(rows, -2 ** 31, I32), total))
    need = kf - count(lambda k, _: k > thr)

    has_ties = jnp.max(cnt_ge) > kf

    def idx_body(i, p):
        cand = p + lax.shift_left(jnp.int32(1), jnp.int32(idx_bits - 1) - i)
        cnt = count(lambda k, ix: jnp.logical_and(k == thr, ix < cand))
        return jnp.where(cnt < need, cand, p)

    cut0 = jnp.zeros(rows, I32) + jnp.where(has_ties, 0, 2 ** idx_bits - 1)
    cut = lax.fori_loop(0, jnp.where(has_ties, idx_bits, 0), idx_body, cut0)
    return thr, cut


def _in_topk(key, idx, thr, cut):
    return jnp.logical_or(key > thr, jnp.logical_and(key == thr, idx <= cut))


def _dsa_kernel(q_ref, qi_ref, sm_ref, kf_ref, vf_ref, smf_ref, bias_ref, o_ref, key_ref, msk_ref,
                *, topk, idx_bits):
    qb = pl.program_id(1)
    nkb = qb + 1
    key_i = lax.broadcasted_iota(I32, (BLK, BLK), 0)
    q_pos = qb * BLK + lax.broadcasted_iota(I32, (BLK, BLK), 1)

    qi = qi_ref[0]
    sm_t = sm_ref[0].T
    qi_st = jnp.concatenate([qi[:, h * IDX_DIM:(h + 1) * IDX_DIM] for h in range(IDX_HEADS)], axis=0).astype(BF16)
    wi_rows = [sm_t[SM_WI + h:SM_WI + h + 1, :] * (IDX_HEADS ** -0.5) for h in range(IDX_HEADS)]

    def score_body(kb, carry):
        off = pl.multiple_of(kb * BLK, BLK)
        ki = smf_ref[0, pl.ds(off, BLK), :][:, SM_KI:SM_KI + IDX_DIM].astype(BF16)
        r = jnp.maximum(_dg(ki, qi_st, _NT) * (IDX_DIM ** -0.5), 0.0)
        s = wi_rows[0] * r[:, 0:BLK]
        for h in range(1, IDX_HEADS):
            s = s + wi_rows[h] * r[:, h * BLK:(h + 1) * BLK]
        s = jnp.where(kb * BLK + key_i <= q_pos, s, -jnp.inf)
        key_ref[kb] = _score_keys(s)
        return carry

    lax.fori_loop(0, nkb, score_body, 0)

    @pl.when(nkb * BLK <= topk)
    def _():
        def body(kb, carry):
            msk_ref[kb] = jnp.where(kb * BLK + key_i <= q_pos, 0.0, NEG)
            return carry
        lax.fori_loop(0, nkb, body, 0)

    @pl.when(nkb * BLK > topk)
    def _():
        thr, cut = _select_topk(key_ref, nkb, topk, idx_bits, key_axis=0)

        def body(kb, carry):
            k_pos = kb * BLK + key_i
            sel = _in_topk(key_ref[kb], k_pos, thr, cut)
            msk_ref[kb] = jnp.where(k_pos <= q_pos, jnp.where(sel, 0.0, NEG), NEG)
            return carry
        lax.fori_loop(0, nkb, body, 0)

    q = q_ref[0] * (A_HEAD_DIM ** -0.5)
    gq = A_GROUP * BLK
    ns = range(A_KV_HEADS)
    qs = [jnp.concatenate([q[:, (n * A_GROUP + g) * A_HEAD_DIM:(n * A_GROUP + g + 1) * A_HEAD_DIM]
                           for g in range(A_GROUP)], axis=0).astype(BF16) for n in ns]

    def att_body(kb, carry):
        off = pl.multiple_of(kb * BLK, BLK)
        kblk = kf_ref[0, pl.ds(off, BLK), :].astype(BF16)
        vblk = vf_ref[0, pl.ds(off, BLK), :].astype(BF16)
        t = jnp.clip(kb - qb + 2, 0, 2)
        mk = msk_ref[kb]
        mk4 = jnp.concatenate([mk] * A_GROUP, axis=1)
        ss = [_dg(kblk[:, n * A_HEAD_DIM:(n + 1) * A_HEAD_DIM], qs[n], _NT) + bias_ref[n, t] + mk4 for n in ns]
        m_new = [jnp.maximum(carry[3 * n], jnp.max(ss[n], axis=0, keepdims=True)) for n in ns]
        ps = [jnp.exp(ss[n] - m_new[n]) for n in ns]
        pvs = [_dg(vblk[:, n * A_HEAD_DIM:(n + 1) * A_HEAD_DIM], ps[n].astype(BF16), _TN) for n in ns]
        out = []
        for n in ns:
            alpha = jnp.exp(carry[3 * n] - m_new[n])
            out += [m_new[n], alpha * carry[3 * n + 1] + jnp.sum(ps[n], axis=0, keepdims=True),
                    alpha * carry[3 * n + 2] + pvs[n]]
        return tuple(out)

    init = (jnp.full((1, gq), NEG, F32), jnp.zeros((1, gq), F32), jnp.zeros((A_HEAD_DIM, gq), F32)) * A_KV_HEADS
    res = lax.fori_loop(0, nkb, att_body, init)
    for n in ns:
        o_t = res[3 * n + 2] / res[3 * n + 1]
        for g in range(0, A_GROUP, 2):
            h = n * A_GROUP + g
            pair = jnp.concatenate([o_t[:, g * BLK:(g + 1) * BLK], o_t[:, (g + 1) * BLK:(g + 2) * BLK]], axis=0)
            o_ref[0, :, h * A_HEAD_DIM:(h + 2) * A_HEAD_DIM] = pair.T


def _dsa_prompt(q, qi, sm, k, v, bias_tiles, topk):
    b, t, _ = q.shape
    nb = t // BLK
    idx_bits = max(1, int(math.ceil(math.log2(t))))
    bt = bias_tiles.reshape(A_KV_HEADS, A_GROUP, 3, BLK, BLK)
    bt = jnp.transpose(bt, (0, 2, 4, 1, 3)).reshape(A_KV_HEADS, 3, BLK, A_GROUP * BLK)
    blk = lambda w: pl.BlockSpec((1, BLK, w), lambda bi, qb: (bi, qb, 0))
    full = lambda w: pl.BlockSpec((1, t, w), lambda bi, qb: (bi, 0, 0))
    return pl.pallas_call(
        functools.partial(_dsa_kernel, topk=topk, idx_bits=idx_bits),
        grid=(b, nb),
        in_specs=[blk(A_Q), blk(IDX_HEADS * IDX_DIM), blk(LANES), full(A_KV), full(A_KV), full(LANES),
                  pl.BlockSpec(bt.shape, lambda bi, qb: (0, 0, 0, 0))],
        out_specs=blk(A_Q),
        out_shape=jax.ShapeDtypeStruct((b, t, A_Q), F32),
        scratch_shapes=[pltpu.VMEM((nb, BLK, BLK), I32), pltpu.VMEM((nb, BLK, BLK), F32)],
        compiler_params=_cparams(("arbitrary", "arbitrary")),
        name="dsa_prompt",
    )(q, qi, sm, k, v, sm, bt)


def _gdn_kernel(qkv_ref, z_ref, sm_ref, s0_ref, cw_ref, par_ref, nw_ref, o_ref, sout_ref, ext_ref, st_ref):
    t = pl.program_id(1)
    nt = pl.num_programs(1)
    c = CHUNK
    halo = SUBLANES

    @pl.when(t == 0)
    def _():
        ext_ref[0:halo, :] = jnp.zeros((halo, B_CONV_DIM), F32)
        st_ref[...] = s0_ref[0]

    x = qkv_ref[0]
    ext_ref[halo:halo + c, :] = x
    conv = None
    for j in range(CONV_W):
        start = halo - (CONV_W - 1) + j
        term = ext_ref[start:start + c, :] * cw_ref[j:j + 1, :]
        conv = term if conv is None else conv + term
    ext_ref[0:halo, :] = x[c - halo:c, :]
    conv = _silu(conv)

    ri = lax.broadcasted_iota(I32, (c, c), 0)
    ci = lax.broadcasted_iota(I32, (c, c), 1)
    tri = ri >= ci
    stri = ri > ci
    eye = jnp.where(ri == ci, 1.0, 0.0)
    tri_bf = jnp.where(tri, 1.0, 0.0).astype(BF16)
    ones_bf = jnp.ones((c, c), BF16)

    sm = sm_ref[0]
    g_all = -jnp.exp(par_ref[0:1, :]) * _softplus(sm + par_ref[1:2, :])
    gc_all = _dot_exact_lhs(tri_bf, g_all)
    beta_all = jax.nn.sigmoid(sm)
    zz = z_ref[0]
    nw = nw_ref[...]
    gcs = [gc_all[:, SM_A + h:SM_A + h + 1] for h in range(B_HEADS)]
    gc_rows = _dot_exact_lhs(ones_bf, jnp.concatenate([eye * gc for gc in gcs], axis=1))

    hs = range(B_HEADS)
    qs = [conv[:, h * B_KEY_DIM:(h + 1) * B_KEY_DIM] for h in hs]
    ks = [conv[:, B_K + h * B_KEY_DIM:B_K + (h + 1) * B_KEY_DIM] for h in hs]
    vs = [conv[:, 2 * B_K + h * B_VAL_DIM:2 * B_K + (h + 1) * B_VAL_DIM] for h in hs]
    qs = [q * lax.rsqrt(jnp.sum(q * q, axis=-1, keepdims=True) + EPS) * (B_KEY_DIM ** -0.5) for q in qs]
    ks = [k * lax.rsqrt(jnp.sum(k * k, axis=-1, keepdims=True) + EPS) for k in ks]
    betas = [beta_all[:, SM_B + h:SM_B + h + 1] for h in hs]
    decays = [jnp.exp(jnp.where(tri, gcs[h] - gc_rows[:, h * c:(h + 1) * c], -jnp.inf)) for h in hs]
    kbs = [ks[h] * betas[h] for h in hs]
    vbs = [vs[h] * betas[h] for h in hs]
    k_bf = [k.astype(BF16) for k in ks]
    a_mats = [jnp.where(stri, _dg(kbs[h].astype(BF16), k_bf[h], _NT) * decays[h], 0.0) for h in hs]
    tms = [eye - a for a in a_mats]
    pws = [_dot3(a, a) for a in a_mats]
    steps = int(math.log2(c))
    for j in range(1, steps):
        tms = [tm + _dot3(tm, pw) for tm, pw in zip(tms, pws)]
        if j < steps - 1:
            pws = [_dot3(pw, pw) for pw in pws]
    egs = [jnp.exp(gc) for gc in gcs]
    tm_bf = [tm.astype(BF16) for tm in tms]
    us = [_dg(tm_bf[h], vbs[h].astype(BF16)) for h in hs]
    ws = [_dg(tm_bf[h], (kbs[h] * egs[h]).astype(BF16)) for h in hs]
    a_qks = [jnp.where(tri, _dg(qs[h].astype(BF16), k_bf[h], _NT) * decays[h], 0.0) for h in hs]

    s_olds = [st_ref[h] for h in hs]
    s_bf = [s.astype(BF16) for s in s_olds]
    v_news = [us[h] - _dg(ws[h].astype(BF16), s_bf[h]) for h in hs]
    os_ = [_dg((qs[h] * egs[h]).astype(BF16), s_bf[h]) + _dot1(a_qks[h], v_news[h]) for h in hs]
    g_lasts = [gc[c - 1:c, :] for gc in gcs]
    for h in hs:
        st_ref[h] = (s_olds[h] * jnp.exp(g_lasts[h])
                     + _dot1(ks[h] * jnp.exp(g_lasts[h] - gcs[h]), v_news[h], _TN))
    for h in hs:
        o = os_[h]
        ms = jnp.mean(o * o, axis=-1, keepdims=True)
        zh = zz[:, h * B_VAL_DIM:(h + 1) * B_VAL_DIM]
        o_ref[0, :, h * B_VAL_DIM:(h + 1) * B_VAL_DIM] = (o * lax.rsqrt(ms + EPS) * nw) * _silu(zh)

    @pl.when(t == nt - 1)
    def _():
        sout_ref[0] = st_ref[...]


def _gdn_params(a_log, dt_bias):
    par = jnp.zeros((SUBLANES, LANES), F32)
    return par.at[0, SM_A:SM_A + B_HEADS].set(a_log).at[1, SM_A:SM_A + B_HEADS].set(dt_bias)


def _gdn_prompt(qkv, z, sm, s0, conv_w, a_log, dt_bias, norm_w):
    b, t, _ = qkv.shape
    nt = t // CHUNK
    tok = lambda w: pl.BlockSpec((1, CHUNK, w), lambda bi, ti: (bi, ti, 0))
    const2 = lambda shp: pl.BlockSpec(shp, lambda bi, ti: (0, 0))
    st_spec = pl.BlockSpec((1, B_HEADS, B_KEY_DIM, B_VAL_DIM), lambda bi, ti: (bi, 0, 0, 0))
    o, s_out = pl.pallas_call(
        _gdn_kernel,
        grid=(b, nt),
        in_specs=[tok(B_CONV_DIM), tok(B_V), tok(LANES), st_spec,
                  const2((CONV_W, B_CONV_DIM)), const2((SUBLANES, LANES)), const2((1, B_VAL_DIM))],
        out_specs=[tok(B_V), st_spec],
        out_shape=[jax.ShapeDtypeStruct((b, t, B_V), F32),
                   jax.ShapeDtypeStruct((b, B_HEADS, B_KEY_DIM, B_VAL_DIM), F32)],
        scratch_shapes=[pltpu.VMEM((SUBLANES + CHUNK, B_CONV_DIM), F32),
                        pltpu.VMEM((B_HEADS, B_KEY_DIM, B_VAL_DIM), F32)],
        compiler_params=_cparams(("arbitrary", "arbitrary")),
        name="gdn_prompt",
    )(qkv, z, sm, s0, conv_w, _gdn_params(a_log, dt_bias), norm_w.reshape(1, B_VAL_DIM))
    return o, s_out


GDN_DEC_ROWS = 16


def _gdn_dec_kernel(qkv_ref, z_ref, sm_ref, cb_ref, s_ref, cw_ref, par_ref, nw_ref, o_ref, so_ref, oraw_ref):
    r = GDN_DEC_ROWS
    conv = qkv_ref[...] * cw_ref[CONV_W - 1:CONV_W, :]
    for j in range(CONV_W - 1):
        conv = conv + cb_ref[0, j] * cw_ref[j:j + 1, :]
    conv = _silu(conv)
    sm = sm_ref[...]
    g_all = -jnp.exp(par_ref[0:1, :]) * _softplus(sm + par_ref[1:2, :])
    beta_all = jax.nn.sigmoid(sm)
    ri = lax.broadcasted_iota(I32, (r, LANES), 0)
    ci = lax.broadcasted_iota(I32, (r, LANES), 1)
    eye_bf = jnp.where(ri == ci, 1.0, 0.0).astype(BF16)

    for h in range(B_HEADS):
        qh = conv[:, h * B_KEY_DIM:(h + 1) * B_KEY_DIM]
        kh = conv[:, B_K + h * B_KEY_DIM:B_K + (h + 1) * B_KEY_DIM]
        vh = conv[:, 2 * B_K + h * B_VAL_DIM:2 * B_K + (h + 1) * B_VAL_DIM]
        qh = qh * lax.rsqrt(jnp.sum(qh * qh, axis=-1, keepdims=True) + EPS) * (B_KEY_DIM ** -0.5)
        kh = kh * lax.rsqrt(jnp.sum(kh * kh, axis=-1, keepdims=True) + EPS)
        eg = jnp.exp(g_all[:, SM_A + h:SM_A + h + 1])
        beta = beta_all[:, SM_B + h:SM_B + h + 1]
        qk = jnp.sum(qh * kh, axis=-1, keepdims=True)
        k_t = _dot_exact_rhs(kh, eye_bf, _TN)
        q_t = _dot_exact_rhs(qh, eye_bf, _TN)
        for s in range(r):
            st = s_ref[0, s, h]
            kc = k_t[:, s:s + 1]
            qc = q_t[:, s:s + 1]
            k_s = jnp.sum(st * kc, axis=0, keepdims=True)
            q_s = jnp.sum(st * qc, axis=0, keepdims=True)
            eg_s = eg[s:s + 1, :]
            v_new = beta[s:s + 1, :] * (vh[s:s + 1, :] - eg_s * k_s)
            oraw_ref[s:s + 1, h * B_VAL_DIM:(h + 1) * B_VAL_DIM] = eg_s * q_s + qk[s:s + 1, :] * v_new
            so_ref[s, h] = st * eg_s + kc * v_new

    zz = z_ref[...]
    nw = nw_ref[...]
    for h in range(B_HEADS):
        o = oraw_ref[:, h * B_VAL_DIM:(h + 1) * B_VAL_DIM]
        ms = jnp.mean(o * o, axis=-1, keepdims=True)
        zh = zz[:, h * B_VAL_DIM:(h + 1) * B_VAL_DIM]
        o_ref[:, h * B_VAL_DIM:(h + 1) * B_VAL_DIM] = (o * lax.rsqrt(ms + EPS) * nw) * _silu(zh)


def _gdn_decode(qkv, z, sm, conv_t, state, j, conv_w, a_log, dt_bias, norm_w):
    b = qkv.shape[0]
    r = GDN_DEC_ROWS
    assert b % r == 0
    rows = lambda w: pl.BlockSpec((r, w), lambda i: (i, 0))
    const2 = lambda shp: pl.BlockSpec(shp, lambda i: (0, 0))
    o, s_out = pl.pallas_call(
        _gdn_dec_kernel,
        grid=(b // r,),
        in_specs=[rows(B_CONV_DIM), rows(B_V), rows(LANES),
                  pl.BlockSpec((1, CONV_W - 1, r, B_CONV_DIM), lambda i: (j, 0, i, 0)),
                  pl.BlockSpec((1, r, B_HEADS, B_KEY_DIM, B_VAL_DIM), lambda i: (j, i, 0, 0, 0)),
                  const2((CONV_W, B_CONV_DIM)), const2((SUBLANES, LANES)), const2((1, B_VAL_DIM))],
        out_specs=[rows(B_V), pl.BlockSpec((r, B_HEADS, B_KEY_DIM, B_VAL_DIM), lambda i: (i, 0, 0, 0))],
        out_shape=[jax.ShapeDtypeStruct((b, B_V), F32),
                   jax.ShapeDtypeStruct((b, B_HEADS, B_KEY_DIM, B_VAL_DIM), F32)],
        scratch_shapes=[pltpu.VMEM((r, B_V), F32)],
        compiler_params=_cparams(("arbitrary",)),
        name="gdn_decode",
    )(qkv, z, sm, conv_t, state, conv_w, _gdn_params(a_log, dt_bias), norm_w.reshape(1, B_VAL_DIM))
    return o, s_out


DIFF_HEADS_PER_STEP = 2


def _diff_kernel(lam_ref, q_ref, k_ref, v_ref, bias_ref, sw_ref, o_ref, *, lam_init):
    qb = pl.program_id(2)
    nkb = qb + 1
    nh = DIFF_HEADS_PER_STEP
    lane = lax.broadcasted_iota(I32, (TQ, C_HD), 1)
    q2 = []
    for h in range(nh):
        qh = q_ref[0, :, h * C_HD:(h + 1) * C_HD] * (C_HEAD_DIM ** -0.5)
        zero = jnp.zeros_like(qh)
        q2.append(jnp.concatenate([jnp.where(lane < C_HEAD_DIM, qh, zero),
                                   jnp.where(lane >= C_HEAD_DIM, qh, zero)], axis=0))

    def body(kb, carry):
        off = pl.multiple_of(kb * TQ, TQ)
        t = jnp.clip(kb - qb + 2, 0, 2)
        hs = range(nh)
        ss = [_dg(k_ref[0, pl.ds(off, TQ), h * C_HD:(h + 1) * C_HD], q2[h], _NT) for h in hs]
        ss = [ss[h] + jnp.concatenate([bias_ref[h, t]] * 2, axis=1) for h in hs]
        m_new = [jnp.maximum(carry[3 * h], jnp.max(ss[h], axis=0, keepdims=True)) for h in hs]
        ps = [jnp.exp(ss[h] - m_new[h]) for h in hs]
        pvs = [_dg(v_ref[0, pl.ds(off, TQ), h * C_HD:(h + 1) * C_HD], ps[h].astype(BF16), _TN) for h in hs]
        out = []
        for h in hs:
            alpha = jnp.exp(carry[3 * h] - m_new[h])
            out += [m_new[h], alpha * carry[3 * h + 1] + jnp.sum(ps[h], axis=0, keepdims=True),
                    alpha * carry[3 * h + 2] + pvs[h]]
        return tuple(out)

    init = (jnp.full((1, 2 * TQ), NEG, F32), jnp.zeros((1, 2 * TQ), F32), jnp.zeros((C_HD, 2 * TQ), F32)) * nh
    res = lax.fori_loop(0, nkb, body, init)
    for h in range(nh):
        _, l, acc = res[3 * h:3 * h + 3]
        on = acc / l
        o = (on[:, 0:TQ] - lam_ref[0] * on[:, TQ:2 * TQ]).T
        ms = jnp.mean(o * o, axis=-1, keepdims=True)
        o_ref[0, :, h * C_HD:(h + 1) * C_HD] = (o * lax.rsqrt(ms + EPS) * sw_ref[...]) * (1.0 - lam_init)


def _diff_prompt(q, k, v, bias_tiles, lam, subln_w, lam_init):
    b, t, _ = q.shape
    nh = DIFF_HEADS_PER_STEP
    w = nh * C_HD
    return pl.pallas_call(
        functools.partial(_diff_kernel, lam_init=lam_init),
        grid=(b, C_HEADS // nh, t // TQ),
        in_specs=[pl.BlockSpec(memory_space=pltpu.SMEM),
                  pl.BlockSpec((1, TQ, w), lambda bi, h, qb: (bi, qb, h)),
                  pl.BlockSpec((1, t, w), lambda bi, h, qb: (bi, 0, h)),
                  pl.BlockSpec((1, t, w), lambda bi, h, qb: (bi, 0, h)),
                  pl.BlockSpec((nh, 3, TQ, TQ), lambda bi, h, qb: (h, 0, 0, 0)),
                  pl.BlockSpec((1, C_HD), lambda bi, h, qb: (0, 0))],
        out_specs=pl.BlockSpec((1, TQ, w), lambda bi, h, qb: (bi, qb, h)),
        out_shape=jax.ShapeDtypeStruct((b, t, C_V), F32),
        compiler_params=_cparams(("arbitrary", "arbitrary", "arbitrary")),
        name="diff_prompt",
    )(lam.reshape(1), q, k, v, bias_tiles, subln_w.reshape(1, C_HD))


DIFF_DEC_PAGES = 8


def _diff_dec_kernel(pt_ref, lam_ref, q_ref, kn_ref, vn_ref, *rest, n_steps, lam_init):
    pp = DIFF_DEC_PAGES
    kps, vps = rest[:pp], rest[pp:2 * pp]
    bias_ref, bnew_ref, sw_ref, o_ref, m_ref, l_ref, acc_ref = rest[2 * pp:]
    p = pl.program_id(1)
    nr = 2 * C_HEADS

    @pl.when(p == 0)
    def _():
        m_ref[...] = jnp.full(m_ref.shape, NEG, F32)
        l_ref[...] = jnp.zeros(l_ref.shape, F32)
        acc_ref[...] = jnp.zeros(acc_ref.shape, F32)

    q8 = q_ref[0] * (C_HEAD_DIM ** -0.5)
    lane = lax.broadcasted_iota(I32, (C_HEADS, C_HD), 1)
    q_lo = jnp.where(lane < C_HEAD_DIM, q8, 0.0)
    q_hi = jnp.where(lane >= C_HEAD_DIM, q8, 0.0)
    qm = jnp.concatenate([q_lo, q_hi], axis=0).astype(BF16)

    rows = BLK * C_HEADS
    half = pp // 2
    cs = range(2)
    ss = [jnp.concatenate([_dg(qm, kps[c * half + i][0, 0].astype(BF16), _NT) + bias_ref[c * half + i]
                           for i in range(half)], axis=1) for c in cs]
    m_old = [m_ref[c] for c in cs]
    m_new = [jnp.maximum(m_old[c], jnp.max(ss[c], axis=1, keepdims=True)) for c in cs]
    ps = [jnp.exp(ss[c] - m_new[c]) for c in cs]
    pb = [pr.astype(BF16) for pr in ps]
    pvs = []
    for c in cs:
        out = None
        for i in range(half):
            part = _dg(pb[c][:, i * rows:(i + 1) * rows], vps[c * half + i][0, 0].astype(BF16))
            out = part if out is None else out + part
        pvs.append(out)
    for c in cs:
        alpha = jnp.exp(m_old[c] - m_new[c])
        l_ref[c] = alpha * l_ref[c] + jnp.sum(ps[c], axis=1, keepdims=True)
        acc_ref[c] = alpha * acc_ref[c] + pvs[c]
        m_ref[c] = m_new[c]

    @pl.when(p == n_steps - 1)
    def _():
        kn = kn_ref[0]
        s_new = jnp.concatenate([jnp.sum(q_lo * kn, axis=1, keepdims=True),
                                 jnp.sum(q_hi * kn, axis=1, keepdims=True)], axis=0) + bnew_ref[:, 0:1]
        vn2 = jnp.concatenate([vn_ref[0], vn_ref[0]], axis=0)
        m = jnp.maximum(jnp.maximum(m_ref[0], m_ref[1]), s_new)
        w0 = jnp.exp(m_ref[0] - m)
        w1 = jnp.exp(m_ref[1] - m)
        p_new = jnp.exp(s_new - m)
        l = w0 * l_ref[0] + w1 * l_ref[1] + p_new
        acc = w0 * acc_ref[0] + w1 * acc_ref[1] + p_new * vn2
        on = acc / l
        o = on[0:C_HEADS] - lam_ref[0] * on[C_HEADS:nr]
        ms = jnp.mean(o * o, axis=-1, keepdims=True)
        o_ref[0] = (o * lax.rsqrt(ms + EPS) * sw_ref[...]) * (1.0 - lam_init)


def _diff_decode(q8, k8, v8, cache_k, cache_v, j, pt_flat, n_pages, bias_pages, bias_new, lam, subln_w, lam_init):
    b = q8.shape[0]
    pp = DIFF_DEC_PAGES
    assert n_pages % pp == 0
    n_steps = n_pages // pp
    rows = BLK * C_HEADS
    row = pl.BlockSpec((1, C_HEADS, C_HD), lambda bi, p, pt: (bi, 0, 0))
    page = lambda i: pl.BlockSpec((1, 1, rows, C_HD),
                                  lambda bi, p, pt: (j, pt[bi * n_pages + p * pp + i], 0, 0))
    grid_spec = pltpu.PrefetchScalarGridSpec(
        num_scalar_prefetch=1,
        grid=(b, n_steps),
        in_specs=[pl.BlockSpec(memory_space=pltpu.SMEM), row, row, row]
                 + [page(i) for i in range(pp)] + [page(i) for i in range(pp)]
                 + [pl.BlockSpec((pp, 2 * C_HEADS, rows), lambda bi, p, pt: (p, 0, 0)),
                    pl.BlockSpec((2 * C_HEADS, LANES), lambda bi, p, pt: (0, 0)),
                    pl.BlockSpec((1, C_HD), lambda bi, p, pt: (0, 0))],
        out_specs=row,
        scratch_shapes=[pltpu.VMEM((2, 2 * C_HEADS, 1), F32), pltpu.VMEM((2, 2 * C_HEADS, 1), F32),
                        pltpu.VMEM((2, 2 * C_HEADS, C_HD), F32)],
    )
    return pl.pallas_call(
        functools.partial(_diff_dec_kernel, n_steps=n_steps, lam_init=lam_init),
        grid_spec=grid_spec,
        out_shape=jax.ShapeDtypeStruct((b, C_HEADS, C_HD), F32),
        compiler_params=_cparams(("arbitrary", "arbitrary")),
        name="diff_decode",
    )(pt_flat, lam.reshape(1), q8, k8, v8, *([cache_k] * pp), *([cache_v] * pp),
      bias_pages, bias_new, subln_w.reshape(1, C_HD))


def _diff_decode_bias(bias_cols, q_pos, n_pages):
    bd = _bias_decode(bias_cols, q_pos, n_pages + 1)
    past = jnp.moveaxis(bd[:n_pages], 1, 2)
    same = jnp.eye(C_HEADS, dtype=bool)
    tab = jnp.where(same[None, :, None, :], past[:, None, :, :], NEG)
    tab = tab.reshape(n_pages, C_HEADS, BLK * C_HEADS)
    tab = jnp.concatenate([tab, tab], axis=1)
    new = bd[n_pages, :, 0]
    new = jnp.broadcast_to(jnp.concatenate([new, new])[:, None], (2 * C_HEADS, LANES))
    return tab, new


def _idx_dec_kernel(pt_ref, qi_ref, sm_ref, *rest, n_pages):
    kps = rest[:n_pages]
    o_ref = rest[n_pages]
    qi = qi_ref[0]
    sm = sm_ref[0]
    rowi = lax.broadcasted_iota(I32, (SUBLANES, IDX_HEADS * IDX_DIM), 0)
    lane = lax.broadcasted_iota(I32, (SUBLANES, IDX_HEADS * IDX_DIM), 1)
    qsel = jnp.where(lane // IDX_DIM == rowi, qi, 0.0)
    qt = qsel[:, 0:IDX_DIM]
    for h in range(1, IDX_HEADS):
        qt = qt + qsel[:, h * IDX_DIM:(h + 1) * IDX_DIM]
    r8 = lax.broadcasted_iota(I32, (SUBLANES, LANES), 0)
    l8 = lax.broadcasted_iota(I32, (SUBLANES, LANES), 1)
    wsel = jnp.where(jnp.logical_and(r8 < IDX_HEADS, l8 == r8 + SM_WI), sm, 0.0)
    wcol = jnp.sum(wsel, axis=1, keepdims=True) * (IDX_HEADS ** -0.5)
    qh, ql = _split2(qt)
    for p in range(n_pages):
        kh, kl = _split2(kps[p][0, 0])
        d = _dg(qh, kh) + (_dg(qh, kl) + _dg(ql, kh))
        rel = jnp.maximum(d * (IDX_DIM ** -0.5), 0.0)
        o_ref[0, p] = jnp.sum(wcol * rel, axis=0, keepdims=True)
    d_new = jnp.sum(qt * sm[:, SM_KI:SM_KI + IDX_DIM], axis=1, keepdims=True)
    s_new = jnp.sum(wcol * jnp.maximum(d_new * (IDX_DIM ** -0.5), 0.0), axis=0, keepdims=True)
    o_ref[0, n_pages] = jnp.broadcast_to(s_new, (1, BLK))


def _idx_decode(qi, sm, cache_kidx_t, j, pt_flat, n_pages):
    b = qi.shape[0]
    page = lambda p: pl.BlockSpec((1, 1, IDX_DIM, BLK), lambda bi, pt: (j, pt[bi * n_pages + p], 0, 0))
    grid_spec = pltpu.PrefetchScalarGridSpec(
        num_scalar_prefetch=1,
        grid=(b,),
        in_specs=[pl.BlockSpec((1, 1, IDX_HEADS * IDX_DIM), lambda bi, pt: (bi, 0, 0)),
                  pl.BlockSpec((1, 1, LANES), lambda bi, pt: (bi, 0, 0))]
                 + [page(p) for p in range(n_pages)],
        out_specs=pl.BlockSpec((1, n_pages + 1, 1, BLK), lambda bi, pt: (bi, 0, 0, 0)),
    )
    return pl.pallas_call(
        functools.partial(_idx_dec_kernel, n_pages=n_pages),
        grid_spec=grid_spec,
        out_shape=jax.ShapeDtypeStruct((b, n_pages + 1, 1, BLK), F32),
        compiler_params=_cparams(("arbitrary",)),
        name="idx_decode",
    )(pt_flat, qi, sm, *([cache_kidx_t] * n_pages))


def _sel_dec_kernel(s_ref, o_ref, key_ref, *, n_valid, topk, idx_bits):
    nkb, rows, _ = s_ref.shape
    col = lax.broadcasted_iota(I32, (rows, BLK), 1)
    for kb in range(nkb):
        s = jnp.where(kb * BLK + col < n_valid, s_ref[kb], -jnp.inf)
        key_ref[kb] = _score_keys(s)
    thr, cut = _select_topk(key_ref, nkb, topk, idx_bits)
    for kb in range(nkb):
        k_pos = kb * BLK + col
        sel = _in_topk(key_ref[kb], k_pos, thr, cut)
        o_ref[kb] = jnp.where(k_pos < n_valid, jnp.where(sel, 0.0, NEG), NEG)


def _sel_decode(scores, n_valid, topk):
    nkb, rows, _ = scores.shape
    idx_bits = max(1, int(math.ceil(math.log2(nkb * BLK))))
    return pl.pallas_call(
        functools.partial(_sel_dec_kernel, n_valid=n_valid, topk=topk, idx_bits=idx_bits),
        out_shape=jax.ShapeDtypeStruct(scores.shape, F32),
        scratch_shapes=[pltpu.VMEM(scores.shape, I32)],
        compiler_params=pltpu.CompilerParams(vmem_limit_bytes=VMEM_LIMIT),
        name="sel_decode",
    )(scores)


def _dsa_dec_kernel(pt_ref, q_ref, kn_ref, vn_ref, *rest, n_pages):
    kps, vps = rest[:n_pages], rest[n_pages:2 * n_pages]
    msk_ref, bias_ref, o_ref = rest[2 * n_pages:]

    q = q_ref[0] * (A_HEAD_DIM ** -0.5)
    rowi = lax.broadcasted_iota(I32, (A_HEADS, A_Q), 0)
    lane = lax.broadcasted_iota(I32, (A_HEADS, A_Q), 1)
    qsel = jnp.where(lane // A_HEAD_DIM == rowi, q, 0.0)
    halves = []
    for n in range(A_KV_HEADS):
        acc = None
        for g in range(A_GROUP):
            h = n * A_GROUP + g
            part = qsel[:, h * A_HEAD_DIM:(h + 1) * A_HEAD_DIM]
            acc = part if acc is None else acc + part
        halves.append(acc)
    qt = jnp.concatenate(halves, axis=1)
    qt_bf = qt.astype(BF16)

    s = jnp.concatenate([_dg(qt_bf, kps[i][0, 0].astype(BF16)) + bias_ref[i] + msk_ref[i, 0]
                         for i in range(n_pages)], axis=1)
    s_new = (jnp.sum(qt * kn_ref[0], axis=1, keepdims=True) + bias_ref[n_pages][:, 0:1]
             + msk_ref[n_pages, 0][:, 0:1])
    m = jnp.maximum(jnp.max(s, axis=1, keepdims=True), s_new)
    p = jnp.exp(s - m)
    p_new = jnp.exp(s_new - m)
    l = jnp.sum(p, axis=1, keepdims=True) + p_new
    pb = p.astype(BF16)
    acc = p_new * vn_ref[0]
    for i in range(n_pages):
        acc = acc + _dg(pb[:, i * BLK:(i + 1) * BLK], vps[i][0, 0].astype(BF16), _NT)
    on = acc / l
    for h in range(A_HEADS):
        n = h // A_GROUP
        o_ref[0, :, h * A_HEAD_DIM:(h + 1) * A_HEAD_DIM] = on[h:h + 1, n * A_HEAD_DIM:(n + 1) * A_HEAD_DIM]


def _dsa_decode(q, k_new, v_new, cache_kt, cache_vt, j, pt_flat, n_pages, mask, bias_dec):
    b = q.shape[0]
    page = lambda i: pl.BlockSpec((1, 1, A_KV, BLK), lambda bi, pt: (j, pt[bi * n_pages + i], 0, 0))
    rowspec = lambda w: pl.BlockSpec((1, 1, w), lambda bi, pt: (bi, 0, 0))
    grid_spec = pltpu.PrefetchScalarGridSpec(
        num_scalar_prefetch=1,
        grid=(b,),
        in_specs=[rowspec(A_Q), rowspec(A_KV), rowspec(A_KV)]
                 + [page(i) for i in range(n_pages)] + [page(i) for i in range(n_pages)]
                 + [pl.BlockSpec((n_pages + 1, 1, 1, BLK), lambda bi, pt: (0, bi, 0, 0)),
                    pl.BlockSpec((n_pages + 1, A_HEADS, BLK), lambda bi, pt: (0, 0, 0))],
        out_specs=rowspec(A_Q),
    )
    return pl.pallas_call(
        functools.partial(_dsa_dec_kernel, n_pages=n_pages),
        grid_spec=grid_spec,
        out_shape=jax.ShapeDtypeStruct((b, 1, A_Q), F32),
        compiler_params=_cparams(("arbitrary",)),
        name="dsa_decode",
    )(pt_flat, q, k_new, v_new, *([cache_kt] * n_pages), *([cache_vt] * n_pages), mask, bias_dec)


def _even_weights(w_in):
    sizes = (A_Q, A_KV, A_KV, IDX_HEADS * IDX_DIM, IDX_DIM, IDX_HEADS, B_CONV_DIM, B_V, B_HEADS, B_HEADS)
    offs = np.concatenate([[0], np.cumsum(sizes)])
    seg = lambda i: w_in[:, int(offs[i]):int(offs[i + 1])]
    pad = LANES - (IDX_DIM + IDX_HEADS + 2 * B_HEADS)
    small = jnp.concatenate([seg(4), seg(5), seg(8), seg(9), jnp.zeros((w_in.shape[0], pad), w_in.dtype)], axis=1)
    return jnp.concatenate([seg(0), seg(1), seg(2), seg(3), small, seg(6), seg(7)], axis=1).astype(BF16)


def _seq_outs(widths, dtype=F32):
    outs, off = [], 0
    for wd in widths:
        outs.append((off, wd, dtype))
        off += wd
    return tuple(outs)


_EVEN_OUTS = _seq_outs((A_Q, A_KV, A_KV, IDX_HEADS * IDX_DIM, LANES, B_CONV_DIM, B_V))
_ODD_OUTS = _seq_outs((C_QK, C_QK, C_V))
_ODD_OUTS_PROMPT = _ODD_OUTS[1:] + _seq_outs((C_QK, C_QK, C_V), BF16)


def kernel(x_prompt, x_sample, c_prompt, c_sample, cache_A_k, cache_A_v, cache_A_kidx, cache_C_k, cache_C_v, state_B_ssm, state_B_conv, page_table, rel_bias, ada_w, ada_b, norm_w, final_norm_w, ffn_w1, ffn_w2, ab_w_in, ab_w_out, gdn_conv_w, gdn_a_log, gdn_dt_bias, gdn_norm_w, c_w_in, c_w_out, c_lambda_q1, c_lambda_k1, c_lambda_q2, c_lambda_k2, c_subln_w):
    depth = ada_w.shape[0]
    bp, tp, d = x_prompt.shape
    bs, ts, _ = x_sample.shape
    assert ts == 1 and tp % TQ == 0 and tp % CHUNK == 0
    n_pages = page_table.shape[1]
    page = cache_A_k.shape[2]
    assert page == BLK
    n_phys = cache_A_k.shape[1]
    past_len = n_pages * page
    topk_p = min(TOPK_MAX, tp // 4)
    topk_s = min(TOPK_MAX, (past_len + ts) // 4)
    pt_flat = page_table.reshape(-1).astype(I32)

    n_c = bp + bs
    n_c_pad = -(-n_c // SUBLANES) * SUBLANES
    c_all = jnp.pad(jnp.concatenate([c_prompt, c_sample], axis=0), ((0, n_c_pad - n_c), (0, 0)))
    mod_all = _modulation(c_all, ada_w, ada_b)

    w1b = ffn_w1.astype(BF16)
    w2b = ffn_w2.astype(BF16)
    ab_in_b = [_even_weights(ab_w_in[j]) for j in range(ab_w_in.shape[0])]
    ab_out_b = ab_w_out.astype(BF16)
    c_in_b = c_w_in.astype(BF16)
    c_out_b = c_w_out.astype(BF16)

    bias_a = _bias_tiles(rel_bias[:, :A_HEADS], BLK, False)
    bias_c = jnp.swapaxes(_bias_tiles(rel_bias[:, A_HEADS:], TQ, True), 2, 3)
    bias_a_dec = _bias_decode(rel_bias[:, :A_HEADS], past_len, n_pages + 1)
    bias_c_pages, bias_c_new = _diff_decode_bias(rel_bias[:, A_HEADS:], past_len, n_pages)

    n_ab = cache_A_k.shape[0]
    cache_a_kt = jnp.transpose(cache_A_k, (0, 1, 3, 4, 2)).reshape(n_ab, n_phys, A_KV, page)
    cache_a_vt = jnp.transpose(cache_A_v, (0, 1, 3, 4, 2)).reshape(n_ab, n_phys, A_KV, page)
    cache_a_it = jnp.transpose(cache_A_kidx, (0, 1, 3, 2))
    cache_c_k = cache_C_k.reshape(cache_C_k.shape[0], n_phys, page * C_HEADS, C_HD)
    cache_c_v = cache_C_v.reshape(cache_C_v.shape[0], n_phys, page * C_HEADS, C_HD)
    conv_t = jnp.transpose(state_B_conv, (0, 2, 1, 3))

    xp = x_prompt
    xs = x_sample.reshape(1, bs, d)
    new_p = [[] for _ in range(7)]
    new_s = [[] for _ in range(7)]

    for i in range(depth):
        j = i // 2
        mod_p = mod_all[i, :bp].reshape(bp, 1, N_MOD * d)
        mod_s = mod_all[i, bp:bp + bs].reshape(1, bs, N_MOD * d)
        xp = _ffn(xp, mod_p, 0, norm_w[i, 0], w1b[i, 0], w2b[i, 0])
        xs = _ffn(xs, mod_s, 0, norm_w[i, 0], w1b[i, 0], w2b[i, 0])
        if i % 2 == 0:
            q, k, v, qi, sm, qkv, z = _inproj(xp, mod_p, 3, norm_w[i, 1], ab_in_b[j], _EVEN_OUTS)
            o_a = _dsa_prompt(q, qi, sm, k, v, bias_a, topk_p)
            o_b, s_new = _gdn_prompt(qkv, z, sm, jnp.zeros((bp, B_HEADS, B_KEY_DIM, B_VAL_DIM), F32),
                                     gdn_conv_w[j], gdn_a_log[j], gdn_dt_bias[j], gdn_norm_w[j])
            xp = _outproj(xp, mod_p, 5, ab_out_b[j], [o_a, o_b])
            new_p[0].append(k.reshape(bp, tp, A_KV_HEADS, A_HEAD_DIM))
            new_p[1].append(v.reshape(bp, tp, A_KV_HEADS, A_HEAD_DIM))
            new_p[2].append(sm[:, :, SM_KI:SM_KI + IDX_DIM])
            new_p[3].append(s_new)
            new_p[4].append(qkv[:, tp - (CONV_W - 1):, :])
            q, k, v, qi, sm, qkv, z = _inproj(xs, mod_s, 3, norm_w[i, 1], ab_in_b[j], _EVEN_OUTS)
            as_rows = lambda a: a.reshape(bs, 1, a.shape[-1])
            scores = _idx_decode(as_rows(qi), as_rows(sm), cache_a_it, j, pt_flat, n_pages)
            scores = jnp.moveaxis(scores.reshape(bs, n_pages + 1, BLK), 1, 0)
            mask = _sel_decode(scores, past_len + 1, topk_s).reshape(n_pages + 1, bs, 1, BLK)
            o_a = _dsa_decode(as_rows(q), as_rows(k), as_rows(v), cache_a_kt, cache_a_vt, j, pt_flat, n_pages,
                              mask, bias_a_dec)
            o_b, s_new = _gdn_decode(qkv.reshape(bs, B_CONV_DIM), z.reshape(bs, B_V), sm.reshape(bs, LANES),
                                     conv_t, state_B_ssm, j, gdn_conv_w[j], gdn_a_log[j], gdn_dt_bias[j],
                                     gdn_norm_w[j])
            xs = _outproj(xs, mod_s, 5, ab_out_b[j], [o_a.reshape(1, bs, A_Q), o_b.reshape(1, bs, B_V)])
            new_s[0].append(k.reshape(bs, 1, A_KV_HEADS, A_HEAD_DIM))
            new_s[1].append(v.reshape(bs, 1, A_KV_HEADS, A_HEAD_DIM))
            new_s[2].append(sm.reshape(bs, 1, LANES)[:, :, SM_KI:SM_KI + IDX_DIM])
            new_s[3].append(s_new)
            new_s[4].append(jnp.concatenate([state_B_conv[j], as_rows(qkv)], axis=1)[:, 1:, :])
        else:
            lam_init = 0.8 - 0.6 * math.exp(-0.3 * i)
            lam = (jnp.exp(jnp.sum(c_lambda_q1[j] * c_lambda_k1[j]))
                   - jnp.exp(jnp.sum(c_lambda_q2[j] * c_lambda_k2[j])) + lam_init).astype(F32)
            k, v, qb16, kb16, vb16 = _inproj(xp, mod_p, 3, norm_w[i, 1], c_in_b[j], _ODD_OUTS_PROMPT)
            o = _diff_prompt(qb16, kb16, vb16, bias_c, lam, c_subln_w[j], lam_init)
            xp = _outproj(xp, mod_p, 5, c_out_b[j], [o])
            new_p[5].append(k.reshape(bp, tp, C_HEADS, C_HD))
            new_p[6].append(v.reshape(bp, tp, C_HEADS, C_HD))
            q, k, v = _inproj(xs, mod_s, 3, norm_w[i, 1], c_in_b[j], _ODD_OUTS)
            as_heads = lambda a: a.reshape(bs, C_HEADS, C_HD)
            o = _diff_decode(as_heads(q), as_heads(k), as_heads(v), cache_c_k, cache_c_v, j, pt_flat, n_pages,
                             bias_c_pages, bias_c_new, lam, c_subln_w[j], lam_init)
            xs = _outproj(xs, mod_s, 5, c_out_b[j], [o.reshape(1, bs, C_V)])
            new_s[5].append(k.reshape(bs, 1, C_HEADS, C_HD))
            new_s[6].append(v.reshape(bs, 1, C_HEADS, C_HD))
        xp = _ffn(xp, mod_p, 6, norm_w[i, 2], w1b[i, 1], w2b[i, 1])
        xs = _ffn(xs, mod_s, 6, norm_w[i, 2], w1b[i, 1], w2b[i, 1])

    y_prompt = _final_norm(xp, final_norm_w)
    y_sample = _final_norm(xs, final_norm_w).reshape(bs, 1, d)
    sp = [jnp.stack(lst) for lst in new_p]
    ss = [jnp.stack(lst) for lst in new_s]
    return (y_prompt, y_sample, *sp, *ss)
```

```python
import functools
import math

import numpy as np
import jax
import jax.numpy as jnp
from jax import lax
from jax.experimental import pallas as pl
from jax.experimental.pallas import tpu as pltpu

F32 = jnp.float32
BF16 = jnp.bfloat16
I32 = jnp.int32

A_HEADS = 8
A_KV_HEADS = 2
A_GROUP = A_HEADS // A_KV_HEADS
A_HEAD_DIM = 64
IDX_HEADS = 4
IDX_DIM = 64
TOPK_MAX = 256
B_HEADS = 4
B_KEY_DIM = 128
B_VAL_DIM = 128
CONV_W = 4
CHUNK = 64
C_HEADS = 8
C_HEAD_DIM = 64
NUM_BUCKETS = 32
MAX_DISTANCE = 128
N_MOD = 9
EPS = 1e-6

A_Q = A_HEADS * A_HEAD_DIM
A_KV = A_KV_HEADS * A_HEAD_DIM
B_K = B_HEADS * B_KEY_DIM
B_V = B_HEADS * B_VAL_DIM
B_CONV_DIM = 2 * B_K + B_V
C_QK = C_HEADS * 2 * C_HEAD_DIM
C_V = C_HEADS * 2 * C_HEAD_DIM
C_HD = 2 * C_HEAD_DIM

LANES = 128
SUBLANES = 8
VMEM_LIMIT = 56 * 1024 * 1024

BLK = 128
TQ = 256
NEG = -1e30

SM_KI = 0
SM_WI = IDX_DIM
SM_A = SM_WI + IDX_HEADS
SM_B = SM_A + B_HEADS

_NT = (((1,), (1,)), ((), ()))
_NN = (((1,), (0,)), ((), ()))
_TN = (((0,), (0,)), ((), ()))


def _cparams(sem):
    return pltpu.CompilerParams(dimension_semantics=sem, vmem_limit_bytes=VMEM_LIMIT)


def _dg(a, b, dims=_NN):
    return lax.dot_general(a, b, dims, preferred_element_type=F32)


def _dot1(a, b, dims=_NN):
    return _dg(a.astype(BF16), b.astype(BF16), dims)


def _split2(x):
    hi = x.astype(BF16)
    lo = (x - hi.astype(F32)).astype(BF16)
    return hi, lo


def _split3(x):
    b1 = x.astype(BF16)
    r1 = x - b1.astype(F32)
    b2 = r1.astype(BF16)
    b3 = (r1 - b2.astype(F32)).astype(BF16)
    return b1, b2, b3


def _dot3(a, b, dims=_NN):
    ah, al = _split2(a)
    bh, bl = _split2(b)
    return _dg(ah, bh, dims) + (_dg(ah, bl, dims) + _dg(al, bh, dims))


def _dot_exact_lhs(a_bf, b, dims=_NN):
    b1, b2, b3 = _split3(b)
    return _dg(a_bf, b1, dims) + (_dg(a_bf, b2, dims) + _dg(a_bf, b3, dims))


def _dot_exact_rhs(a, b_bf, dims=_NN):
    a1, a2, a3 = _split3(a)
    return _dg(a1, b_bf, dims) + (_dg(a2, b_bf, dims) + _dg(a3, b_bf, dims))


def _silu(x):
    return x * jax.nn.sigmoid(x)


def _softplus(x):
    return jnp.maximum(x, 0.0) + jnp.log(1.0 + jnp.exp(-jnp.abs(x)))


def _norm_mod(x, nw, sc, sh):
    ms = jnp.mean(x * x, axis=-1, keepdims=True)
    return (x * lax.rsqrt(ms + EPS) * nw) * (1.0 + sc) + sh


def _softmax_step(s, m, l, acc, pv):
    m_new = jnp.maximum(m, jnp.max(s, axis=1, keepdims=True))
    alpha = jnp.exp(m - m_new)
    p = jnp.exp(s - m_new)
    return m_new, alpha * l + jnp.sum(p, axis=1, keepdims=True), alpha * acc + pv(p)


def _mod_kernel(c_ref, w_ref, b_ref, o_ref):
    s = _silu(c_ref[...]).astype(BF16)
    o_ref[0] = _dg(s, w_ref[0].astype(BF16)) + b_ref[0]


def _modulation(c_all, ada_w, ada_b):
    depth, d, n = ada_w.shape
    m = c_all.shape[0]
    tn = 1024
    return pl.pallas_call(
        _mod_kernel,
        grid=(depth, n // tn),
        in_specs=[pl.BlockSpec((m, d), lambda i, j: (0, 0)),
                  pl.BlockSpec((1, d, tn), lambda i, j: (i, 0, j)),
                  pl.BlockSpec((1, 1, tn), lambda i, j: (i, 0, j))],
        out_specs=pl.BlockSpec((1, m, tn), lambda i, j: (i, 0, j)),
        out_shape=jax.ShapeDtypeStruct((depth, m, n), F32),
        compiler_params=_cparams(("arbitrary", "arbitrary")),
        name="adaln_mod",
    )(c_all, ada_w, ada_b.reshape(depth, 1, n))


def _row_tile(t):
    return min(512, t)


def _mod_spec(mod, tm, m):
    r = mod.shape[1]
    d = mod.shape[2] // N_MOD
    if r == 1:
        return pl.BlockSpec((1, 1, d), lambda s, t: (s, 0, m))
    return pl.BlockSpec((1, tm, d), lambda s, t: (s, t, m))


def _resident(shape):
    nd = len(shape)
    return pl.BlockSpec(shape, lambda s, t: (0,) * nd, pipeline_mode=pl.Buffered(1))


def _ffn_kernel(x_ref, sh_ref, sc_ref, g_ref, nw_ref, w1_ref, w2_ref, o_ref, acc_ref, *, fc):
    x = x_ref[0]
    hb = _norm_mod(x, nw_ref[...], sc_ref[0], sh_ref[0]).astype(BF16)
    f = w2_ref.shape[0]
    for c in range(f // fc):
        gt = _dg(hb, w1_ref[:, c * fc:(c + 1) * fc])
        up = _dg(hb, w1_ref[:, f + c * fc:f + (c + 1) * fc])
        a = (_silu(gt) * up).astype(BF16)
        contrib = _dg(a, w2_ref[c * fc:(c + 1) * fc, :])
        if c == 0:
            acc_ref[...] = contrib
        else:
            acc_ref[...] += contrib
    o_ref[0] = x + (0.5 * g_ref[0]) * acc_ref[...]


def _ffn(x, mod, m0, nw, w1b, w2b):
    s, t, d = x.shape
    tm = _row_tile(t)
    f = w2b.shape[0]
    fc = 256 if f % 256 == 0 else LANES
    xs = pl.BlockSpec((1, tm, d), lambda si, ti: (si, ti, 0))
    return pl.pallas_call(
        functools.partial(_ffn_kernel, fc=fc),
        grid=(s, t // tm),
        in_specs=[xs, _mod_spec(mod, tm, m0), _mod_spec(mod, tm, m0 + 1), _mod_spec(mod, tm, m0 + 2),
                  _resident((1, d)), _resident(w1b.shape), _resident(w2b.shape)],
        out_specs=xs,
        out_shape=jax.ShapeDtypeStruct(x.shape, F32),
        scratch_shapes=[pltpu.VMEM((tm, d), F32)],
        compiler_params=_cparams(("arbitrary", "arbitrary")),
        name="ffn",
    )(x, mod, mod, mod, nw.reshape(1, d), w1b, w2b)


def _inproj_kernel(x_ref, sh_ref, sc_ref, nw_ref, w_ref, *o_refs, outs):
    hb = _norm_mod(x_ref[0], nw_ref[...], sc_ref[0], sh_ref[0]).astype(BF16)
    done = {}
    for o_ref, (off, wd, dt) in zip(o_refs, outs):
        if (off, wd) not in done:
            done[(off, wd)] = _dg(hb, w_ref[:, off:off + wd])
        o_ref[0] = done[(off, wd)].astype(dt)


def _inproj(x, mod, m0, nw, wb, outs):
    s, t, d = x.shape
    tm = _row_tile(t)
    xs = pl.BlockSpec((1, tm, d), lambda si, ti: (si, ti, 0))
    return pl.pallas_call(
        functools.partial(_inproj_kernel, outs=tuple(outs)),
        grid=(s, t // tm),
        in_specs=[xs, _mod_spec(mod, tm, m0), _mod_spec(mod, tm, m0 + 1), _resident((1, d)), _resident(wb.shape)],
        out_specs=[pl.BlockSpec((1, tm, wd), lambda si, ti: (si, ti, 0)) for _, wd, _ in outs],
        out_shape=[jax.ShapeDtypeStruct((s, t, wd), dt) for _, wd, dt in outs],
        compiler_params=_cparams(("arbitrary", "arbitrary")),
        name="inproj",
    )(x, mod, mod, nw.reshape(1, d), wb)


def _outproj_kernel(*refs, widths):
    n = len(widths)
    x_ref, g_ref, w_ref = refs[0], refs[1], refs[2]
    a_refs = refs[3:3 + n]
    o_ref = refs[3 + n]
    acc = None
    off = 0
    for a_ref, wd in zip(a_refs, widths):
        part = _dg(a_ref[0].astype(BF16), w_ref[off:off + wd, :])
        acc = part if acc is None else acc + part
        off += wd
    o_ref[0] = x_ref[0] + g_ref[0] * acc


def _outproj(x, mod, mg, wb, parts):
    s, t, d = x.shape
    tm = _row_tile(t)
    widths = tuple(p.shape[-1] for p in parts)
    xs = pl.BlockSpec((1, tm, d), lambda si, ti: (si, ti, 0))
    return pl.pallas_call(
        functools.partial(_outproj_kernel, widths=widths),
        grid=(s, t // tm),
        in_specs=[xs, _mod_spec(mod, tm, mg), _resident(wb.shape)]
                 + [pl.BlockSpec((1, tm, wd), lambda si, ti: (si, ti, 0)) for wd in widths],
        out_specs=xs,
        out_shape=jax.ShapeDtypeStruct(x.shape, F32),
        compiler_params=_cparams(("arbitrary", "arbitrary")),
        name="outproj",
    )(x, mod, wb, *parts)


def _final_norm_kernel(x_ref, w_ref, o_ref):
    x = x_ref[0]
    ms = jnp.mean(x * x, axis=-1, keepdims=True)
    o_ref[0] = x * lax.rsqrt(ms + EPS) * w_ref[...]


def _final_norm(x, w):
    s, t, d = x.shape
    tm = _row_tile(t)
    xs = pl.BlockSpec((1, tm, d), lambda si, ti: (si, ti, 0))
    return pl.pallas_call(
        _final_norm_kernel,
        grid=(s, t // tm),
        in_specs=[xs, pl.BlockSpec((1, d), lambda si, ti: (0, 0))],
        out_specs=xs,
        out_shape=jax.ShapeDtypeStruct(x.shape, F32),
        compiler_params=_cparams(("arbitrary", "arbitrary")),
        name="final_norm",
    )(x, w.reshape(1, d))


def _t5_bucket(dist):
    n = jnp.maximum(dist, 0)
    max_exact = NUM_BUCKETS // 2
    nf = jnp.maximum(n, 1).astype(F32)
    large = max_exact + (jnp.log(nf / max_exact) / math.log(MAX_DISTANCE / max_exact)
                         * (NUM_BUCKETS - max_exact)).astype(I32)
    large = jnp.minimum(large, NUM_BUCKETS - 1)
    return jnp.where(n < max_exact, n, large)


def _bias_tiles(bias_cols, blk, causal):
    assert blk >= MAX_DISTANCE
    n = blk
    h = bias_cols.shape[1]
    tiles = []
    for t in range(3):
        off = (2 - t) * n
        dist = n - 1 + off - jnp.arange(2 * n, dtype=I32)
        g = bias_cols[_t5_bucket(dist)]
        if causal:
            g = jnp.where((dist >= 0)[:, None], g, NEG)
        x = jnp.broadcast_to(g.T[:, None, :], (h, n, 2 * n))
        x = jnp.pad(x, ((0, 0), (0, 0), (0, 1))).reshape(h, n * (2 * n + 1))[:, :2 * n * n].reshape(h, n, 2 * n)
        tiles.append(x[:, :, n - 1:2 * n - 1])
    return jnp.stack(tiles, axis=1)


def _bias_decode(bias_cols, q_pos, n_blocks):
    k_pos = jnp.arange(n_blocks * BLK, dtype=I32)
    b = bias_cols[_t5_bucket(q_pos - k_pos)]
    b = jnp.where((k_pos <= q_pos)[:, None], b, NEG)
    return jnp.moveaxis(b.reshape(n_blocks, BLK, -1), 2, 1)


def _score_keys(s):
    bits = pltpu.bitcast(s, I32)
    return jnp.where(bits < 0, bits ^ jnp.int32(0x7FFFFFFF), bits)


def _select_topk(key_ref, nkb, topk, idx_bits, key_axis=1, pairs=False):
    tile = tuple(key_ref.shape[1:])
    assert tile[key_axis] == BLK
    stat = tuple(1 if a == key_axis else n for a, n in enumerate(tile))
    rows = stat
    col = lax.broadcasted_iota(I32, tile, key_axis)

    def count(pred):
        if pairs:
            def body(i, acc):
                kb = 2 * i
                return (acc + jnp.where(pred(key_ref[kb], kb * BLK + col), 1.0, 0.0)
                        + jnp.where(pred(key_ref[kb + 1], (kb + 1) * BLK + col), 1.0, 0.0))
            acc = lax.fori_loop(0, (nkb + 1) // 2, body, jnp.zeros(tile, F32))
        else:
            def body(kb, acc):
                return acc + jnp.where(pred(key_ref[kb], kb * BLK + col), 1.0, 0.0)
            acc = lax.fori_loop(0, nkb, body, jnp.zeros(tile, F32))
        return jnp.sum(acc, axis=key_axis, keepdims=True)

    kf = float(topk)

    def bit_body(i, carry):
        lo, cnt_lo = carry
        cand = lo + lax.shift_left(jnp.int32(1), jnp.int32(31) - i)
        cnt = count(lambda k, _: k >= cand)
        take = cnt >= kf
        return jnp.where(take, cand, lo), jnp.where(take, cnt, cnt_lo)

    n_visited = 2 * ((nkb + 1) // 2) if pairs else nkb
    total = jnp.zeros(rows, F32) + jnp.asarray(n_visited * BLK, F32)
    thr, cnt_ge = lax.fori_loop(0, 32, bit_body, (jnp.full(rows, -2 ** 31, I32), total))
    need = kf - count(lambda k, _: k > thr)

    has_ties = jnp.max(cnt_ge) > kf

    def idx_body(i, p):
        cand = p + lax.shift_left(jnp.int32(1), jnp.int32(idx_bits - 1) - i)
        cnt = count(lambda k, ix: jnp.logical_and(k == thr, ix < cand))
        return jnp.where(cnt < need, cand, p)

    cut0 = jnp.zeros(rows, I32) + jnp.where(has_ties, 0, 2 ** idx_bits - 1)
    cut = lax.fori_loop(0, jnp.where(has_ties, idx_bits, 0), idx_body, cut0)
    return thr, cut


def _in_topk(key, idx, thr, cut):
    return jnp.logical_or(key > thr, jnp.logical_and(key == thr, idx <= cut))


def _dsa_kernel(q_ref, qi_ref, sm_ref, kf_ref, vf_ref, smf_ref, bias_ref, o_ref, key_ref, msk_ref,
                *, topk, idx_bits):
    qb = pl.program_id(1)
    nkb = qb + 1
    key_i = lax.broadcasted_iota(I32, (BLK, BLK), 0)
    q_pos = qb * BLK + lax.broadcasted_iota(I32, (BLK, BLK), 1)

    qi = qi_ref[0]
    sm_t = sm_ref[0].T
    qi_st = jnp.concatenate([qi[:, h * IDX_DIM:(h + 1) * IDX_DIM] for h in range(IDX_HEADS)], axis=0).astype(BF16)
    wi_rows = [sm_t[SM_WI + h:SM_WI + h + 1, :] * (IDX_HEADS ** -0.5) for h in range(IDX_HEADS)]

    def score_body(i, carry):
        kbs = (2 * i, 2 * i + 1)
        kis = [smf_ref[0, pl.ds(pl.multiple_of(kb * BLK, BLK), BLK), :][:, SM_KI:SM_KI + IDX_DIM].astype(BF16)
               for kb in kbs]
        rs = [jnp.maximum(_dg(ki, qi_st, _NT) * (IDX_DIM ** -0.5), 0.0) for ki in kis]
        for kb, r in zip(kbs, rs):
            s = wi_rows[0] * r[:, 0:BLK]
            for h in range(1, IDX_HEADS):
                s = s + wi_rows[h] * r[:, h * BLK:(h + 1) * BLK]
            s = jnp.where(kb * BLK + key_i <= q_pos, s, -jnp.inf)
            key_ref[kb] = _score_keys(s)
        return carry

    lax.fori_loop(0, (nkb + 1) // 2, score_body, 0)

    @pl.when(nkb * BLK <= topk)
    def _():
        def body(kb, carry):
            msk_ref[kb] = jnp.where(kb * BLK + key_i <= q_pos, 0.0, NEG)
            return carry
        lax.fori_loop(0, nkb, body, 0)

    @pl.when(nkb * BLK > topk)
    def _():
        thr, cut = _select_topk(key_ref, nkb, topk, idx_bits, key_axis=0, pairs=True)

        def body(kb, carry):
            k_pos = kb * BLK + key_i
            sel = _in_topk(key_ref[kb], k_pos, thr, cut)
            msk_ref[kb] = jnp.where(k_pos <= q_pos, jnp.where(sel, 0.0, NEG), NEG)
            return carry
        lax.fori_loop(0, nkb, body, 0)

    q = q_ref[0] * (A_HEAD_DIM ** -0.5)
    gq = A_GROUP * BLK
    ns = range(A_KV_HEADS)
    qs = [jnp.concatenate([q[:, (n * A_GROUP + g) * A_HEAD_DIM:(n * A_GROUP + g + 1) * A_HEAD_DIM]
                           for g in range(A_GROUP)], axis=0).astype(BF16) for n in ns]

    def logits(kb):
        off = pl.multiple_of(kb * BLK, BLK)
        kblk = kf_ref[0, pl.ds(off, BLK), :].astype(BF16)
        t = jnp.clip(kb - qb + 2, 0, 2)
        mk4 = jnp.concatenate([msk_ref[kb]] * A_GROUP, axis=1)
        return [_dg(kblk[:, n * A_HEAD_DIM:(n + 1) * A_HEAD_DIM], qs[n], _NT) + bias_ref[n, t] + mk4 for n in ns]

    def att_body(kb, carry):
        ss = carry[3 * A_KV_HEADS:]
        ss_next = logits(jnp.minimum(kb + 1, nkb - 1))
        off = pl.multiple_of(kb * BLK, BLK)
        vblk = vf_ref[0, pl.ds(off, BLK), :].astype(BF16)
        m_new = [jnp.maximum(carry[3 * n], jnp.max(ss[n], axis=0, keepdims=True)) for n in ns]
        ps = [jnp.exp(ss[n] - m_new[n]) for n in ns]
        pvs = [_dg(vblk[:, n * A_HEAD_DIM:(n + 1) * A_HEAD_DIM], ps[n].astype(BF16), _TN) for n in ns]
        out = []
        for n in ns:
            alpha = jnp.exp(carry[3 * n] - m_new[n])
            out += [m_new[n], alpha * carry[3 * n + 1] + jnp.sum(ps[n], axis=0, keepdims=True),
                    alpha * carry[3 * n + 2] + pvs[n]]
        return tuple(out) + tuple(ss_next)

    init = (jnp.full((1, gq), NEG, F32), jnp.zeros((1, gq), F32), jnp.zeros((A_HEAD_DIM, gq), F32)) * A_KV_HEADS
    res = lax.fori_loop(0, nkb, att_body, init + tuple(logits(0)))
    for n in ns:
        o_t = res[3 * n + 2] / res[3 * n + 1]
        for g in range(0, A_GROUP, 2):
            h = n * A_GROUP + g
            pair = jnp.concatenate([o_t[:, g * BLK:(g + 1) * BLK], o_t[:, (g + 1) * BLK:(g + 2) * BLK]], axis=0)
            o_ref[0, :, h * A_HEAD_DIM:(h + 2) * A_HEAD_DIM] = pair.T


def _dsa_prompt(q, qi, sm, k, v, bias_tiles, topk):
    b, t, _ = q.shape
    nb = t // BLK
    assert nb % 2 == 0
    idx_bits = max(1, int(math.ceil(math.log2(t))))
    bt = bias_tiles.reshape(A_KV_HEADS, A_GROUP, 3, BLK, BLK)
    bt = jnp.transpose(bt, (0, 2, 4, 1, 3)).reshape(A_KV_HEADS, 3, BLK, A_GROUP * BLK)
    blk = lambda w: pl.BlockSpec((1, BLK, w), lambda bi, qb: (bi, qb, 0))
    full = lambda w: pl.BlockSpec((1, t, w), lambda bi, qb: (bi, 0, 0))
    return pl.pallas_call(
        functools.partial(_dsa_kernel, topk=topk, idx_bits=idx_bits),
        grid=(b, nb),
        in_specs=[blk(A_Q), blk(IDX_HEADS * IDX_DIM), blk(LANES), full(A_KV), full(A_KV), full(LANES),
                  pl.BlockSpec(bt.shape, lambda bi, qb: (0, 0, 0, 0))],
        out_specs=blk(A_Q),
        out_shape=jax.ShapeDtypeStruct((b, t, A_Q), F32),
        scratch_shapes=[pltpu.VMEM((nb, BLK, BLK), I32), pltpu.VMEM((nb, BLK, BLK), F32)],
        compiler_params=_cparams(("arbitrary", "arbitrary")),
        name="dsa_prompt",
    )(q, qi, sm, k, v, sm, bt)


def _gdn_kernel(qkv_ref, z_ref, sm_ref, s0_ref, cw_ref, par_ref, nw_ref, o_ref, sout_ref, ext_ref, st_ref):
    t = pl.program_id(1)
    nt = pl.num_programs(1)
    c = CHUNK
    halo = SUBLANES

    @pl.when(t == 0)
    def _():
        ext_ref[0:halo, :] = jnp.zeros((halo, B_CONV_DIM), F32)
        st_ref[...] = s0_ref[0]

    x = qkv_ref[0]
    ext_ref[halo:halo + c, :] = x
    conv = None
    for j in range(CONV_W):
        start = halo - (CONV_W - 1) + j
        term = ext_ref[start:start + c, :] * cw_ref[j:j + 1, :]
        conv = term if conv is None else conv + term
    ext_ref[0:halo, :] = x[c - halo:c, :]
    conv = _silu(conv)

    ri = lax.broadcasted_iota(I32, (c, c), 0)
    ci = lax.broadcasted_iota(I32, (c, c), 1)
    tri = ri >= ci
    stri = ri > ci
    eye = jnp.where(ri == ci, 1.0, 0.0)
    tri_bf = jnp.where(tri, 1.0, 0.0).astype(BF16)
    ones_bf = jnp.ones((c, c), BF16)

    sm = sm_ref[0]
    g_all = -jnp.exp(par_ref[0:1, :]) * _softplus(sm + par_ref[1:2, :])
    gc_all = _dot_exact_lhs(tri_bf, g_all)
    beta_all = jax.nn.sigmoid(sm)
    zz = z_ref[0]
    nw = nw_ref[...]
    gcs = [gc_all[:, SM_A + h:SM_A + h + 1] for h in range(B_HEADS)]
    gc_rows = _dot_exact_lhs(ones_bf, jnp.concatenate([eye * gc for gc in gcs], axis=1))

    hs = range(B_HEADS)
    qs = [conv[:, h * B_KEY_DIM:(h + 1) * B_KEY_DIM] for h in hs]
    ks = [conv[:, B_K + h * B_KEY_DIM:B_K + (h + 1) * B_KEY_DIM] for h in hs]
    vs = [conv[:, 2 * B_K + h * B_VAL_DIM:2 * B_K + (h + 1) * B_VAL_DIM] for h in hs]
    qs = [q * lax.rsqrt(jnp.sum(q * q, axis=-1, keepdims=True) + EPS) * (B_KEY_DIM ** -0.5) for q in qs]
    ks = [k * lax.rsqrt(jnp.sum(k * k, axis=-1, keepdims=True) + EPS) for k in ks]
    betas = [beta_all[:, SM_B + h:SM_B + h + 1] for h in hs]
    decays = [jnp.exp(jnp.where(tri, gcs[h] - gc_rows[:, h * c:(h + 1) * c], -jnp.inf)) for h in hs]
    kbs = [ks[h] * betas[h] for h in hs]
    vbs = [vs[h] * betas[h] for h in hs]
    k_bf = [k.astype(BF16) for k in ks]
    a_mats = [jnp.where(stri, _dg(kbs[h].astype(BF16), k_bf[h], _NT) * decays[h], 0.0) for h in hs]
    tms = [eye - a for a in a_mats]
    pws = [_dot3(a, a) for a in a_mats]
    steps = int(math.log2(c))
    for j in range(1, steps):
        tms = [tm + _dot3(tm, pw) for tm, pw in zip(tms, pws)]
        if j < steps - 1:
            pws = [_dot3(pw, pw) for pw in pws]
    egs = [jnp.exp(gc) for gc in gcs]
    tm_bf = [tm.astype(BF16) for tm in tms]
    us = [_dg(tm_bf[h], vbs[h].astype(BF16)) for h in hs]
    ws = [_dg(tm_bf[h], (kbs[h] * egs[h]).astype(BF16)) for h in hs]
    a_qks = [jnp.where(tri, _dg(qs[h].astype(BF16), k_bf[h], _NT) * decays[h], 0.0) for h in hs]

    s_olds = [st_ref[h] for h in hs]
    s_bf = [s.astype(BF16) for s in s_olds]
    v_news = [us[h] - _dg(ws[h].astype(BF16), s_bf[h]) for h in hs]
    os_ = [_dg((qs[h] * egs[h]).astype(BF16), s_bf[h]) + _dot1(a_qks[h], v_news[h]) for h in hs]
    g_lasts = [gc[c - 1:c, :] for gc in gcs]
    for h in hs:
        st_ref[h] = (s_olds[h] * jnp.exp(g_lasts[h])
                     + _dot1(ks[h] * jnp.exp(g_lasts[h] - gcs[h]), v_news[h], _TN))
    for h in hs:
        o = os_[h]
        ms = jnp.mean(o * o, axis=-1, keepdims=True)
        zh = zz[:, h * B_VAL_DIM:(h + 1) * B_VAL_DIM]
        o_ref[0, :, h * B_VAL_DIM:(h + 1) * B_VAL_DIM] = (o * lax.rsqrt(ms + EPS) * nw) * _silu(zh)

    @pl.when(t == nt - 1)
    def _():
        sout_ref[0] = st_ref[...]


def _gdn_params(a_log, dt_bias):
    par = jnp.zeros((SUBLANES, LANES), F32)
    return par.at[0, SM_A:SM_A + B_HEADS].set(a_log).at[1, SM_A:SM_A + B_HEADS].set(dt_bias)


def _gdn_prompt(qkv, z, sm, s0, conv_w, a_log, dt_bias, norm_w):
    b, t, _ = qkv.shape
    nt = t // CHUNK
    tok = lambda w: pl.BlockSpec((1, CHUNK, w), lambda bi, ti: (bi, ti, 0))
    const2 = lambda shp: pl.BlockSpec(shp, lambda bi, ti: (0, 0))
    st_spec = pl.BlockSpec((1, B_HEADS, B_KEY_DIM, B_VAL_DIM), lambda bi, ti: (bi, 0, 0, 0))
    o, s_out = pl.pallas_call(
        _gdn_kernel,
        grid=(b, nt),
        in_specs=[tok(B_CONV_DIM), tok(B_V), tok(LANES), st_spec,
                  const2((CONV_W, B_CONV_DIM)), const2((SUBLANES, LANES)), const2((1, B_VAL_DIM))],
        out_specs=[tok(B_V), st_spec],
        out_shape=[jax.ShapeDtypeStruct((b, t, B_V), F32),
                   jax.ShapeDtypeStruct((b, B_HEADS, B_KEY_DIM, B_VAL_DIM), F32)],
        scratch_shapes=[pltpu.VMEM((SUBLANES + CHUNK, B_CONV_DIM), F32),
                        pltpu.VMEM((B_HEADS, B_KEY_DIM, B_VAL_DIM), F32)],
        compiler_params=_cparams(("arbitrary", "arbitrary")),
        name="gdn_prompt",
    )(qkv, z, sm, s0, conv_w, _gdn_params(a_log, dt_bias), norm_w.reshape(1, B_VAL_DIM))
    return o, s_out


GDN_DEC_ROWS = 16


def _gdn_dec_kernel(qkv_ref, z_ref, sm_ref, cb_ref, s_ref, cw_ref, par_ref, nw_ref, o_ref, so_ref, oraw_ref):
    r = GDN_DEC_ROWS
    conv = qkv_ref[...] * cw_ref[CONV_W - 1:CONV_W, :]
    for j in range(CONV_W - 1):
        conv = conv + cb_ref[0, j] * cw_ref[j:j + 1, :]
    conv = _silu(conv)
    sm = sm_ref[...]
    g_all = -jnp.exp(par_ref[0:1, :]) * _softplus(sm + par_ref[1:2, :])
    beta_all = jax.nn.sigmoid(sm)
    ri = lax.broadcasted_iota(I32, (r, LANES), 0)
    ci = lax.broadcasted_iota(I32, (r, LANES), 1)
    eye_bf = jnp.where(ri == ci, 1.0, 0.0).astype(BF16)

    for h in range(B_HEADS):
        qh = conv[:, h * B_KEY_DIM:(h + 1) * B_KEY_DIM]
        kh = conv[:, B_K + h * B_KEY_DIM:B_K + (h + 1) * B_KEY_DIM]
        vh = conv[:, 2 * B_K + h * B_VAL_DIM:2 * B_K + (h + 1) * B_VAL_DIM]
        qh = qh * lax.rsqrt(jnp.sum(qh * qh, axis=-1, keepdims=True) + EPS) * (B_KEY_DIM ** -0.5)
        kh = kh * lax.rsqrt(jnp.sum(kh * kh, axis=-1, keepdims=True) + EPS)
        eg = jnp.exp(g_all[:, SM_A + h:SM_A + h + 1])
        beta = beta_all[:, SM_B + h:SM_B + h + 1]
        qk = jnp.sum(qh * kh, axis=-1, keepdims=True)
        k_t = _dot_exact_rhs(kh, eye_bf, _TN)
        q_t = _dot_exact_rhs(qh, eye_bf, _TN)
        for s in range(r):
            st = s_ref[0, s, h]
            kc = k_t[:, s:s + 1]
            qc = q_t[:, s:s + 1]
            k_s = jnp.sum(st * kc, axis=0, keepdims=True)
            q_s = jnp.sum(st * qc, axis=0, keepdims=True)
            eg_s = eg[s:s + 1, :]
            v_new = beta[s:s + 1, :] * (vh[s:s + 1, :] - eg_s * k_s)
            oraw_ref[s:s + 1, h * B_VAL_DIM:(h + 1) * B_VAL_DIM] = eg_s * q_s + qk[s:s + 1, :] * v_new
            so_ref[s, h] = st * eg_s + kc * v_new

    zz = z_ref[...]
    nw = nw_ref[...]
    for h in range(B_HEADS):
        o = oraw_ref[:, h * B_VAL_DIM:(h + 1) * B_VAL_DIM]
        ms = jnp.mean(o * o, axis=-1, keepdims=True)
        zh = zz[:, h * B_VAL_DIM:(h + 1) * B_VAL_DIM]
        o_ref[:, h * B_VAL_DIM:(h + 1) * B_VAL_DIM] = (o * lax.rsqrt(ms + EPS) * nw) * _silu(zh)


def _gdn_decode(qkv, z, sm, conv_t, state, j, conv_w, a_log, dt_bias, norm_w):
    b = qkv.shape[0]
    r = GDN_DEC_ROWS
    assert b % r == 0
    rows = lambda w: pl.BlockSpec((r, w), lambda i: (i, 0))
    const2 = lambda shp: pl.BlockSpec(shp, lambda i: (0, 0))
    o, s_out = pl.pallas_call(
        _gdn_dec_kernel,
        grid=(b // r,),
        in_specs=[rows(B_CONV_DIM), rows(B_V), rows(LANES),
                  pl.BlockSpec((1, CONV_W - 1, r, B_CONV_DIM), lambda i: (j, 0, i, 0)),
                  pl.BlockSpec((1, r, B_HEADS, B_KEY_DIM, B_VAL_DIM), lambda i: (j, i, 0, 0, 0)),
                  const2((CONV_W, B_CONV_DIM)), const2((SUBLANES, LANES)), const2((1, B_VAL_DIM))],
        out_specs=[rows(B_V), pl.BlockSpec((r, B_HEADS, B_KEY_DIM, B_VAL_DIM), lambda i: (i, 0, 0, 0))],
        out_shape=[jax.ShapeDtypeStruct((b, B_V), F32),
                   jax.ShapeDtypeStruct((b, B_HEADS, B_KEY_DIM, B_VAL_DIM), F32)],
        scratch_shapes=[pltpu.VMEM((r, B_V), F32)],
        compiler_params=_cparams(("arbitrary",)),
        name="gdn_decode",
    )(qkv, z, sm, conv_t, state, conv_w, _gdn_params(a_log, dt_bias), norm_w.reshape(1, B_VAL_DIM))
    return o, s_out


DIFF_HEADS_PER_STEP = 2


def _diff_kernel(lam_ref, q_ref, k_ref, v_ref, bias_ref, sw_ref, o_ref, *, lam_init):
    qb = pl.program_id(2)
    nkb = qb + 1
    nh = DIFF_HEADS_PER_STEP
    lane = lax.broadcasted_iota(I32, (TQ, C_HD), 1)
    q2 = []
    for h in range(nh):
        qh = q_ref[0, :, h * C_HD:(h + 1) * C_HD] * (C_HEAD_DIM ** -0.5)
        zero = jnp.zeros_like(qh)
        q2.append(jnp.concatenate([jnp.where(lane < C_HEAD_DIM, qh, zero),
                                   jnp.where(lane >= C_HEAD_DIM, qh, zero)], axis=0))

    hs = range(nh)

    def logits(kb):
        off = pl.multiple_of(kb * TQ, TQ)
        t = jnp.clip(kb - qb + 2, 0, 2)
        ss = [_dg(k_ref[0, pl.ds(off, TQ), h * C_HD:(h + 1) * C_HD], q2[h], _NT) for h in hs]
        return [ss[h] + jnp.concatenate([bias_ref[h, t]] * 2, axis=1) for h in hs]

    def body(kb, carry):
        ss = logits(kb)
        off = pl.multiple_of(kb * TQ, TQ)
        m_new = [jnp.maximum(carry[3 * h], jnp.max(ss[h], axis=0, keepdims=True)) for h in hs]
        ps = [jnp.exp(ss[h] - m_new[h]) for h in hs]
        pvs = [_dg(v_ref[0, pl.ds(off, TQ), h * C_HD:(h + 1) * C_HD], ps[h].astype(BF16), _TN) for h in hs]
        out = []
        for h in hs:
            alpha = jnp.exp(carry[3 * h] - m_new[h])
            out += [m_new[h], alpha * carry[3 * h + 1] + jnp.sum(ps[h], axis=0, keepdims=True),
                    alpha * carry[3 * h + 2] + pvs[h]]
        return tuple(out)

    init = (jnp.full((1, 2 * TQ), NEG, F32), jnp.zeros((1, 2 * TQ), F32), jnp.zeros((C_HD, 2 * TQ), F32)) * nh
    res = lax.fori_loop(0, nkb, body, init)
    for h in range(nh):
        _, l, acc = res[3 * h:3 * h + 3]
        on = acc / l
        o = (on[:, 0:TQ] - lam_ref[0] * on[:, TQ:2 * TQ]).T
        ms = jnp.mean(o * o, axis=-1, keepdims=True)
        o_ref[0, :, h * C_HD:(h + 1) * C_HD] = (o * lax.rsqrt(ms + EPS) * sw_ref[...]) * (1.0 - lam_init)


def _diff_prompt(q, k, v, bias_tiles, lam, subln_w, lam_init):
    b, t, _ = q.shape
    nh = DIFF_HEADS_PER_STEP
    w = nh * C_HD
    return pl.pallas_call(
        functools.partial(_diff_kernel, lam_init=lam_init),
        grid=(b, C_HEADS // nh, t // TQ),
        in_specs=[pl.BlockSpec(memory_space=pltpu.SMEM),
                  pl.BlockSpec((1, TQ, w), lambda bi, h, qb: (bi, qb, h)),
                  pl.BlockSpec((1, t, w), lambda bi, h, qb: (bi, 0, h)),
                  pl.BlockSpec((1, t, w), lambda bi, h, qb: (bi, 0, h)),
                  pl.BlockSpec((nh, 3, TQ, TQ), lambda bi, h, qb: (h, 0, 0, 0)),
                  pl.BlockSpec((1, C_HD), lambda bi, h, qb: (0, 0))],
        out_specs=pl.BlockSpec((1, TQ, w), lambda bi, h, qb: (bi, qb, h)),
        out_shape=jax.ShapeDtypeStruct((b, t, C_V), F32),
        compiler_params=_cparams(("arbitrary", "arbitrary", "arbitrary")),
        name="diff_prompt",
    )(lam.reshape(1), q, k, v, bias_tiles, subln_w.reshape(1, C_HD))


DIFF_DEC_PAGES = 8


def _diff_dec_kernel(pt_ref, lam_ref, q_ref, kn_ref, vn_ref, *rest, n_steps, lam_init):
    pp = DIFF_DEC_PAGES
    kps, vps = rest[:pp], rest[pp:2 * pp]
    bias_ref, bnew_ref, sw_ref, o_ref, m_ref, l_ref, acc_ref = rest[2 * pp:]
    p = pl.program_id(1)
    nr = 2 * C_HEADS

    @pl.when(p == 0)
    def _():
        m_ref[...] = jnp.full(m_ref.shape, NEG, F32)
        l_ref[...] = jnp.zeros(l_ref.shape, F32)
        acc_ref[...] = jnp.zeros(acc_ref.shape, F32)

    q8 = q_ref[0] * (C_HEAD_DIM ** -0.5)
    lane = lax.broadcasted_iota(I32, (C_HEADS, C_HD), 1)
    q_lo = jnp.where(lane < C_HEAD_DIM, q8, 0.0)
    q_hi = jnp.where(lane >= C_HEAD_DIM, q8, 0.0)
    qm = jnp.concatenate([q_lo, q_hi], axis=0).astype(BF16)

    rows = BLK * C_HEADS
    half = pp // 2
    cs = range(2)
    ss = [jnp.concatenate([_dg(qm, kps[c * half + i][0, 0].astype(BF16), _NT) + bias_ref[c * half + i]
                           for i in range(half)], axis=1) for c in cs]
    m_old = [m_ref[c] for c in cs]
    m_new = [jnp.maximum(m_old[c], jnp.max(ss[c], axis=1, keepdims=True)) for c in cs]
    ps = [jnp.exp(ss[c] - m_new[c]) for c in cs]
    pb = [pr.astype(BF16) for pr in ps]
    pvs = []
    for c in cs:
        out = None
        for i in range(half):
            part = _dg(pb[c][:, i * rows:(i + 1) * rows], vps[c * half + i][0, 0].astype(BF16))
            out = part if out is None else out + part
        pvs.append(out)
    for c in cs:
        alpha = jnp.exp(m_old[c] - m_new[c])
        l_ref[c] = alpha * l_ref[c] + jnp.sum(ps[c], axis=1, keepdims=True)
        acc_ref[c] = alpha * acc_ref[c] + pvs[c]
        m_ref[c] = m_new[c]

    @pl.when(p == n_steps - 1)
    def _():
        kn = kn_ref[0]
        s_new = jnp.concatenate([jnp.sum(q_lo * kn, axis=1, keepdims=True),
                                 jnp.sum(q_hi * kn, axis=1, keepdims=True)], axis=0) + bnew_ref[:, 0:1]
        vn2 = jnp.concatenate([vn_ref[0], vn_ref[0]], axis=0)
        m = jnp.maximum(jnp.maximum(m_ref[0], m_ref[1]), s_new)
        w0 = jnp.exp(m_ref[0] - m)
        w1 = jnp.exp(m_ref[1] - m)
        p_new = jnp.exp(s_new - m)
        l = w0 * l_ref[0] + w1 * l_ref[1] + p_new
        acc = w0 * acc_ref[0] + w1 * acc_ref[1] + p_new * vn2
        on = acc / l
        o = on[0:C_HEADS] - lam_ref[0] * on[C_HEADS:nr]
        ms = jnp.mean(o * o, axis=-1, keepdims=True)
        o_ref[0] = (o * lax.rsqrt(ms + EPS) * sw_ref[...]) * (1.0 - lam_init)


def _diff_decode(q8, k8, v8, cache_k, cache_v, j, pt_flat, n_pages, bias_pages, bias_new, lam, subln_w, lam_init):
    b = q8.shape[0]
    pp = DIFF_DEC_PAGES
    assert n_pages % pp == 0
    n_steps = n_pages // pp
    rows = BLK * C_HEADS
    row = pl.BlockSpec((1, C_HEADS, C_HD), lambda bi, p, pt: (bi, 0, 0))
    page = lambda i: pl.BlockSpec((1, 1, rows, C_HD),
                                  lambda bi, p, pt: (j, pt[bi * n_pages + p * pp + i], 0, 0))
    grid_spec = pltpu.PrefetchScalarGridSpec(
        num_scalar_prefetch=1,
        grid=(b, n_steps),
        in_specs=[pl.BlockSpec(memory_space=pltpu.SMEM), row, row, row]
                 + [page(i) for i in range(pp)] + [page(i) for i in range(pp)]
                 + [pl.BlockSpec((pp, 2 * C_HEADS, rows), lambda bi, p, pt: (p, 0, 0)),
                    pl.BlockSpec((2 * C_HEADS, LANES), lambda bi, p, pt: (0, 0)),
                    pl.BlockSpec((1, C_HD), lambda bi, p, pt: (0, 0))],
        out_specs=row,
        scratch_shapes=[pltpu.VMEM((2, 2 * C_HEADS, 1), F32), pltpu.VMEM((2, 2 * C_HEADS, 1), F32),
                        pltpu.VMEM((2, 2 * C_HEADS, C_HD), F32)],
    )
    return pl.pallas_call(
        functools.partial(_diff_dec_kernel, n_steps=n_steps, lam_init=lam_init),
        grid_spec=grid_spec,
        out_shape=jax.ShapeDtypeStruct((b, C_HEADS, C_HD), F32),
        compiler_params=_cparams(("arbitrary", "arbitrary")),
        name="diff_decode",
    )(pt_flat, lam.reshape(1), q8, k8, v8, *([cache_k] * pp), *([cache_v] * pp),
      bias_pages, bias_new, subln_w.reshape(1, C_HD))


def _diff_decode_bias(bias_cols, q_pos, n_pages):
    bd = _bias_decode(bias_cols, q_pos, n_pages + 1)
    past = jnp.moveaxis(bd[:n_pages], 1, 2)
    same = jnp.eye(C_HEADS, dtype=bool)
    tab = jnp.where(same[None, :, None, :], past[:, None, :, :], NEG)
    tab = tab.reshape(n_pages, C_HEADS, BLK * C_HEADS)
    tab = jnp.concatenate([tab, tab], axis=1)
    new = bd[n_pages, :, 0]
    new = jnp.broadcast_to(jnp.concatenate([new, new])[:, None], (2 * C_HEADS, LANES))
    return tab, new


def _idx_dec_kernel(pt_ref, qi_ref, sm_ref, *rest, n_pages):
    kps = rest[:n_pages]
    o_ref = rest[n_pages]
    qi = qi_ref[0]
    sm = sm_ref[0]
    rowi = lax.broadcasted_iota(I32, (SUBLANES, IDX_HEADS * IDX_DIM), 0)
    lane = lax.broadcasted_iota(I32, (SUBLANES, IDX_HEADS * IDX_DIM), 1)
    qsel = jnp.where(lane // IDX_DIM == rowi, qi, 0.0)
    qt = qsel[:, 0:IDX_DIM]
    for h in range(1, IDX_HEADS):
        qt = qt + qsel[:, h * IDX_DIM:(h + 1) * IDX_DIM]
    r8 = lax.broadcasted_iota(I32, (SUBLANES, LANES), 0)
    l8 = lax.broadcasted_iota(I32, (SUBLANES, LANES), 1)
    wsel = jnp.where(jnp.logical_and(r8 < IDX_HEADS, l8 == r8 + SM_WI), sm, 0.0)
    wcol = jnp.sum(wsel, axis=1, keepdims=True) * (IDX_HEADS ** -0.5)
    qh, ql = _split2(qt)
    for p in range(n_pages):
        kh, kl = _split2(kps[p][0, 0])
        d = _dg(qh, kh) + (_dg(qh, kl) + _dg(ql, kh))
        rel = jnp.maximum(d * (IDX_DIM ** -0.5), 0.0)
        o_ref[0, p] = jnp.sum(wcol * rel, axis=0, keepdims=True)
    d_new = jnp.sum(qt * sm[:, SM_KI:SM_KI + IDX_DIM], axis=1, keepdims=True)
    s_new = jnp.sum(wcol * jnp.maximum(d_new * (IDX_DIM ** -0.5), 0.0), axis=0, keepdims=True)
    o_ref[0, n_pages] = jnp.broadcast_to(s_new, (1, BLK))


def _idx_decode(qi, sm, cache_kidx_t, j, pt_flat, n_pages):
    b = qi.shape[0]
    page = lambda p: pl.BlockSpec((1, 1, IDX_DIM, BLK), lambda bi, pt: (j, pt[bi * n_pages + p], 0, 0))
    grid_spec = pltpu.PrefetchScalarGridSpec(
        num_scalar_prefetch=1,
        grid=(b,),
        in_specs=[pl.BlockSpec((1, 1, IDX_HEADS * IDX_DIM), lambda bi, pt: (bi, 0, 0)),
                  pl.BlockSpec((1, 1, LANES), lambda bi, pt: (bi, 0, 0))]
                 + [page(p) for p in range(n_pages)],
        out_specs=pl.BlockSpec((1, n_pages + 1, 1, BLK), lambda bi, pt: (bi, 0, 0, 0)),
    )
    return pl.pallas_call(
        functools.partial(_idx_dec_kernel, n_pages=n_pages),
        grid_spec=grid_spec,
        out_shape=jax.ShapeDtypeStruct((b, n_pages + 1, 1, BLK), F32),
        compiler_params=_cparams(("arbitrary",)),
        name="idx_decode",
    )(pt_flat, qi, sm, *([cache_kidx_t] * n_pages))


def _sel_dec_kernel(s_ref, o_ref, key_ref, *, n_valid, topk, idx_bits):
    nkb, rows, _ = s_ref.shape
    col = lax.broadcasted_iota(I32, (rows, BLK), 1)
    for kb in range(nkb):
        s = jnp.where(kb * BLK + col < n_valid, s_ref[kb], -jnp.inf)
        key_ref[kb] = _score_keys(s)
    thr, cut = _select_topk(key_ref, nkb, topk, idx_bits)
    for kb in range(nkb):
        k_pos = kb * BLK + col
        sel = _in_topk(key_ref[kb], k_pos, thr, cut)
        o_ref[kb] = jnp.where(k_pos < n_valid, jnp.where(sel, 0.0, NEG), NEG)


def _sel_decode(scores, n_valid, topk):
    nkb, rows, _ = scores.shape
    idx_bits = max(1, int(math.ceil(math.log2(nkb * BLK))))
    return pl.pallas_call(
        functools.partial(_sel_dec_kernel, n_valid=n_valid, topk=topk, idx_bits=idx_bits),
        out_shape=jax.ShapeDtypeStruct(scores.shape, F32),
        scratch_shapes=[pltpu.VMEM(scores.shape, I32)],
        compiler_params=pltpu.CompilerParams(vmem_limit_bytes=VMEM_LIMIT),
        name="sel_decode",
    )(scores)


def _dsa_dec_kernel(pt_ref, q_ref, kn_ref, vn_ref, *rest, n_pages):
    kps, vps = rest[:n_pages], rest[n_pages:2 * n_pages]
    msk_ref, bias_ref, o_ref = rest[2 * n_pages:]

    q = q_ref[0] * (A_HEAD_DIM ** -0.5)
    rowi = lax.broadcasted_iota(I32, (A_HEADS, A_Q), 0)
    lane = lax.broadcasted_iota(I32, (A_HEADS, A_Q), 1)
    qsel = jnp.where(lane // A_HEAD_DIM == rowi, q, 0.0)
    halves = []
    for n in range(A_KV_HEADS):
        acc = None
        for g in range(A_GROUP):
            h = n * A_GROUP + g
            part = qsel[:, h * A_HEAD_DIM:(h + 1) * A_HEAD_DIM]
            acc = part if acc is None else acc + part
        halves.append(acc)
    qt = jnp.concatenate(halves, axis=1)
    qt_bf = qt.astype(BF16)

    s = jnp.concatenate([_dg(qt_bf, kps[i][0, 0].astype(BF16)) + bias_ref[i] + msk_ref[i, 0]
                         for i in range(n_pages)], axis=1)
    s_new = (jnp.sum(qt * kn_ref[0], axis=1, keepdims=True) + bias_ref[n_pages][:, 0:1]
             + msk_ref[n_pages, 0][:, 0:1])
    m = jnp.maximum(jnp.max(s, axis=1, keepdims=True), s_new)
    p = jnp.exp(s - m)
    p_new = jnp.exp(s_new - m)
    l = jnp.sum(p, axis=1, keepdims=True) + p_new
    pb = p.astype(BF16)
    acc = p_new * vn_ref[0]
    for i in range(n_pages):
        acc = acc + _dg(pb[:, i * BLK:(i + 1) * BLK], vps[i][0, 0].astype(BF16), _NT)
    on = acc / l
    for h in range(A_HEADS):
        n = h // A_GROUP
        o_ref[0, :, h * A_HEAD_DIM:(h + 1) * A_HEAD_DIM] = on[h:h + 1, n * A_HEAD_DIM:(n + 1) * A_HEAD_DIM]


def _dsa_decode(q, k_new, v_new, cache_kt, cache_vt, j, pt_flat, n_pages, mask, bias_dec):
    b = q.shape[0]
    page = lambda i: pl.BlockSpec((1, 1, A_KV, BLK), lambda bi, pt: (j, pt[bi * n_pages + i], 0, 0))
    rowspec = lambda w: pl.BlockSpec((1, 1, w), lambda bi, pt: (bi, 0, 0))
    grid_spec = pltpu.PrefetchScalarGridSpec(
        num_scalar_prefetch=1,
        grid=(b,),
        in_specs=[rowspec(A_Q), rowspec(A_KV), rowspec(A_KV)]
                 + [page(i) for i in range(n_pages)] + [page(i) for i in range(n_pages)]
                 + [pl.BlockSpec((n_pages + 1, 1, 1, BLK), lambda bi, pt: (0, bi, 0, 0)),
                    pl.BlockSpec((n_pages + 1, A_HEADS, BLK), lambda bi, pt: (0, 0, 0))],
        out_specs=rowspec(A_Q),
    )
    return pl.pallas_call(
        functools.partial(_dsa_dec_kernel, n_pages=n_pages),
        grid_spec=grid_spec,
        out_shape=jax.ShapeDtypeStruct((b, 1, A_Q), F32),
        compiler_params=_cparams(("arbitrary",)),
        name="dsa_decode",
    )(pt_flat, q, k_new, v_new, *([cache_kt] * n_pages), *([cache_vt] * n_pages), mask, bias_dec)


def _even_weights(w_in):
    sizes = (A_Q, A_KV, A_KV, IDX_HEADS * IDX_DIM, IDX_DIM, IDX_HEADS, B_CONV_DIM, B_V, B_HEADS, B_HEADS)
    offs = np.concatenate([[0], np.cumsum(sizes)])
    seg = lambda i: w_in[:, int(offs[i]):int(offs[i + 1])]
    pad = LANES - (IDX_DIM + IDX_HEADS + 2 * B_HEADS)
    small = jnp.concatenate([seg(4), seg(5), seg(8), seg(9), jnp.zeros((w_in.shape[0], pad), w_in.dtype)], axis=1)
    return jnp.concatenate([seg(0), seg(1), seg(2), seg(3), small, seg(6), seg(7)], axis=1).astype(BF16)


def _seq_outs(widths, dtype=F32):
    outs, off = [], 0
    for wd in widths:
        outs.append((off, wd, dtype))
        off += wd
    return tuple(outs)


_EVEN_OUTS = _seq_outs((A_Q, A_KV, A_KV, IDX_HEADS * IDX_DIM, LANES, B_CONV_DIM, B_V))
_ODD_OUTS = _seq_outs((C_QK, C_QK, C_V))
_ODD_OUTS_PROMPT = _ODD_OUTS[1:] + _seq_outs((C_QK, C_QK, C_V), BF16)


def kernel(x_prompt, x_sample, c_prompt, c_sample, cache_A_k, cache_A_v, cache_A_kidx, cache_C_k, cache_C_v, state_B_ssm, state_B_conv, page_table, rel_bias, ada_w, ada_b, norm_w, final_norm_w, ffn_w1, ffn_w2, ab_w_in, ab_w_out, gdn_conv_w, gdn_a_log, gdn_dt_bias, gdn_norm_w, c_w_in, c_w_out, c_lambda_q1, c_lambda_k1, c_lambda_q2, c_lambda_k2, c_subln_w):
    depth = ada_w.shape[0]
    bp, tp, d = x_prompt.shape
    bs, ts, _ = x_sample.shape
    assert ts == 1 and tp % TQ == 0 and tp % CHUNK == 0
    n_pages = page_table.shape[1]
    page = cache_A_k.shape[2]
    assert page == BLK
    n_phys = cache_A_k.shape[1]
    past_len = n_pages * page
    topk_p = min(TOPK_MAX, tp // 4)
    topk_s = min(TOPK_MAX, (past_len + ts) // 4)
    pt_flat = page_table.reshape(-1).astype(I32)

    n_c = bp + bs
    n_c_pad = -(-n_c // SUBLANES) * SUBLANES
    c_all = jnp.pad(jnp.concatenate([c_prompt, c_sample], axis=0), ((0, n_c_pad - n_c), (0, 0)))
    mod_all = _modulation(c_all, ada_w, ada_b)

    w1b = ffn_w1.astype(BF16)
    w2b = ffn_w2.astype(BF16)
    ab_in_b = [_even_weights(ab_w_in[j]) for j in range(ab_w_in.shape[0])]
    ab_out_b = ab_w_out.astype(BF16)
    c_in_b = c_w_in.astype(BF16)
    c_out_b = c_w_out.astype(BF16)

    bias_a = _bias_tiles(rel_bias[:, :A_HEADS], BLK, False)
    bias_c = jnp.swapaxes(_bias_tiles(rel_bias[:, A_HEADS:], TQ, True), 2, 3)
    bias_a_dec = _bias_decode(rel_bias[:, :A_HEADS], past_len, n_pages + 1)
    bias_c_pages, bias_c_new = _diff_decode_bias(rel_bias[:, A_HEADS:], past_len, n_pages)

    n_ab = cache_A_k.shape[0]
    cache_a_kt = jnp.transpose(cache_A_k, (0, 1, 3, 4, 2)).reshape(n_ab, n_phys, A_KV, page)
    cache_a_vt = jnp.transpose(cache_A_v, (0, 1, 3, 4, 2)).reshape(n_ab, n_phys, A_KV, page)
    cache_a_it = jnp.transpose(cache_A_kidx, (0, 1, 3, 2))
    cache_c_k = cache_C_k.reshape(cache_C_k.shape[0], n_phys, page * C_HEADS, C_HD)
    cache_c_v = cache_C_v.reshape(cache_C_v.shape[0], n_phys, page * C_HEADS, C_HD)
    conv_t = jnp.transpose(state_B_conv, (0, 2, 1, 3))

    xp = x_prompt
    xs = x_sample.reshape(1, bs, d)
    new_p = [[] for _ in range(7)]
    new_s = [[] for _ in range(7)]

    for i in range(depth):
        j = i // 2
        mod_p = mod_all[i, :bp].reshape(bp, 1, N_MOD * d)
        mod_s = mod_all[i, bp:bp + bs].reshape(1, bs, N_MOD * d)
        xp = _ffn(xp, mod_p, 0, norm_w[i, 0], w1b[i, 0], w2b[i, 0])
        xs = _ffn(xs, mod_s, 0, norm_w[i, 0], w1b[i, 0], w2b[i, 0])
        if i % 2 == 0:
            q, k, v, qi, sm, qkv, z = _inproj(xp, mod_p, 3, norm_w[i, 1], ab_in_b[j], _EVEN_OUTS)
            o_a = _dsa_prompt(q, qi, sm, k, v, bias_a, topk_p)
            o_b, s_new = _gdn_prompt(qkv, z, sm, jnp.zeros((bp, B_HEADS, B_KEY_DIM, B_VAL_DIM), F32),
                                     gdn_conv_w[j], gdn_a_log[j], gdn_dt_bias[j], gdn_norm_w[j])
            xp = _outproj(xp, mod_p, 5, ab_out_b[j], [o_a, o_b])
            new_p[0].append(k.reshape(bp, tp, A_KV_HEADS, A_HEAD_DIM))
            new_p[1].append(v.reshape(bp, tp, A_KV_HEADS, A_HEAD_DIM))
            new_p[2].append(sm[:, :, SM_KI:SM_KI + IDX_DIM])
            new_p[3].append(s_new)
            new_p[4].append(qkv[:, tp - (CONV_W - 1):, :])
            q, k, v, qi, sm, qkv, z = _inproj(xs, mod_s, 3, norm_w[i, 1], ab_in_b[j], _EVEN_OUTS)
            as_rows = lambda a: a.reshape(bs, 1, a.shape[-1])
            scores = _idx_decode(as_rows(qi), as_rows(sm), cache_a_it, j, pt_flat, n_pages)
            scores = jnp.moveaxis(scores.reshape(bs, n_pages + 1, BLK), 1, 0)
            mask = _sel_decode(scores, past_len + 1, topk_s).reshape(n_pages + 1, bs, 1, BLK)
            o_a = _dsa_decode(as_rows(q), as_rows(k), as_rows(v), cache_a_kt, cache_a_vt, j, pt_flat, n_pages,
                              mask, bias_a_dec)
            o_b, s_new = _gdn_decode(qkv.reshape(bs, B_CONV_DIM), z.reshape(bs, B_V), sm.reshape(bs, LANES),
                                     conv_t, state_B_ssm, j, gdn_conv_w[j], gdn_a_log[j], gdn_dt_bias[j],
                                     gdn_norm_w[j])
            xs = _outproj(xs, mod_s, 5, ab_out_b[j], [o_a.reshape(1, bs, A_Q), o_b.reshape(1, bs, B_V)])
            new_s[0].append(k.reshape(bs, 1, A_KV_HEADS, A_HEAD_DIM))
            new_s[1].append(v.reshape(bs, 1, A_KV_HEADS, A_HEAD_DIM))
            new_s[2].append(sm.reshape(bs, 1, LANES)[:, :, SM_KI:SM_KI + IDX_DIM])
            new_s[3].append(s_new)
            new_s[4].append(jnp.concatenate([state_B_conv[j], as_rows(qkv)], axis=1)[:, 1:, :])
        else:
            lam_init = 0.8 - 0.6 * math.exp(-0.3 * i)
            lam = (jnp.exp(jnp.sum(c_lambda_q1[j] * c_lambda_k1[j]))
                   - jnp.exp(jnp.sum(c_lambda_q2[j] * c_lambda_k2[j])) + lam_init).astype(F32)
            k, v, qb16, kb16, vb16 = _inproj(xp, mod_p, 3, norm_w[i, 1], c_in_b[j], _ODD_OUTS_PROMPT)
            o = _diff_prompt(qb16, kb16, vb16, bias_c, lam, c_subln_w[j], lam_init)
            xp = _outproj(xp, mod_p, 5, c_out_b[j], [o])
            new_p[5].append(k.reshape(bp, tp, C_HEADS, C_HD))
            new_p[6].append(v.reshape(bp, tp, C_HEADS, C_HD))
            q, k, v = _inproj(xs, mod_s, 3, norm_w[i, 1], c_in_b[j], _ODD_OUTS)
            as_heads = lambda a: a.reshape(bs, C_HEADS, C_HD)
            o = _diff_decode(as_heads(q), as_heads(k), as_heads(v), cache_c_k, cache_c_v, j, pt_flat, n_pages,
                             bias_c_pages, bias_c_new, lam, c_subln_w[j], lam_init)
            xs = _outproj(xs, mod_s, 5, c_out_b[j], [o.reshape(1, bs, C_V)])
            new_s[5].append(k.reshape(bs, 1, C_HEADS, C_HD))
            new_s[6].append(v.reshape(bs, 1, C_HEADS, C_HD))
        xp = _ffn(xp, mod_p, 6, norm_w[i, 2], w1b[i, 1], w2b[i, 1])
        xs = _ffn(xs, mod_s, 6, norm_w[i, 2], w1b[i, 1], w2b[i, 1])

    y_prompt = _final_norm(xp, final_norm_w)
    y_sample = _final_norm(xs, final_norm_w).reshape(bs, 1, d)
    sp = [jnp.stack(lst) for lst in new_p]
    ss = [jnp.stack(lst) for lst in new_s]
    return (y_prompt, y_sample, *sp, *ss)
```

```python
import functools
import math

import numpy as np
import jax
import jax.numpy as jnp
from jax import lax
from jax.experimental import pallas as pl
from jax.experimental.pallas import tpu as pltpu

F32 = jnp.float32
BF16 = jnp.bfloat16
I32 = jnp.int32

A_HEADS = 8
A_KV_HEADS = 2
A_GROUP = A_HEADS // A_KV_HEADS
A_HEAD_DIM = 64
IDX_HEADS = 4
IDX_DIM = 64
TOPK_MAX = 256
B_HEADS = 4
B_KEY_DIM = 128
B_VAL_DIM = 128
CONV_W = 4
CHUNK = 64
C_HEADS = 8
C_HEAD_DIM = 64
NUM_BUCKETS = 32
MAX_DISTANCE = 128
N_MOD = 9
EPS = 1e-6

A_Q = A_HEADS * A_HEAD_DIM
A_KV = A_KV_HEADS * A_HEAD_DIM
B_K = B_HEADS * B_KEY_DIM
B_V = B_HEADS * B_VAL_DIM
B_CONV_DIM = 2 * B_K + B_V
C_QK = C_HEADS * 2 * C_HEAD_DIM
C_V = C_HEADS * 2 * C_HEAD_DIM
C_HD = 2 * C_HEAD_DIM

LANES = 128
SUBLANES = 8
VMEM_LIMIT = 56 * 1024 * 1024

BLK = 128
TQ = 256
NEG = -1e30

SM_KI = 0
SM_WI = IDX_DIM
SM_A = SM_WI + IDX_HEADS
SM_B = SM_A + B_HEADS

_NT = (((1,), (1,)), ((), ()))
_NN = (((1,), (0,)), ((), ()))
_TN = (((0,), (0,)), ((), ()))


def _cparams(sem):
    return pltpu.CompilerParams(dimension_semantics=sem, vmem_limit_bytes=VMEM_LIMIT)


def _dg(a, b, dims=_NN):
    return lax.dot_general(a, b, dims, preferred_element_type=F32)


def _dot1(a, b, dims=_NN):
    return _dg(a.astype(BF16), b.astype(BF16), dims)


def _split2(x):
    hi = x.astype(BF16)
    lo = (x - hi.astype(F32)).astype(BF16)
    return hi, lo


def _split3(x):
    b1 = x.astype(BF16)
    r1 = x - b1.astype(F32)
    b2 = r1.astype(BF16)
    b3 = (r1 - b2.astype(F32)).astype(BF16)
    return b1, b2, b3


def _dot3(a, b, dims=_NN):
    ah, al = _split2(a)
    bh, bl = _split2(b)
    return _dg(ah, bh, dims) + (_dg(ah, bl, dims) + _dg(al, bh, dims))


def _dot_exact_lhs(a_bf, b, dims=_NN):
    b1, b2, b3 = _split3(b)
    return _dg(a_bf, b1, dims) + (_dg(a_bf, b2, dims) + _dg(a_bf, b3, dims))


def _dot_exact_rhs(a, b_bf, dims=_NN):
    a1, a2, a3 = _split3(a)
    return _dg(a1, b_bf, dims) + (_dg(a2, b_bf, dims) + _dg(a3, b_bf, dims))


def _silu(x):
    return x * jax.nn.sigmoid(x)


def _softplus(x):
    return jnp.maximum(x, 0.0) + jnp.log(1.0 + jnp.exp(-jnp.abs(x)))


def _norm_mod(x, nw, sc, sh):
    ms = jnp.mean(x * x, axis=-1, keepdims=True)
    return (x * lax.rsqrt(ms + EPS) * nw) * (1.0 + sc) + sh


def _softmax_step(s, m, l, acc, pv):
    m_new = jnp.maximum(m, jnp.max(s, axis=1, keepdims=True))
    alpha = jnp.exp(m - m_new)
    p = jnp.exp(s - m_new)
    return m_new, alpha * l + jnp.sum(p, axis=1, keepdims=True), alpha * acc + pv(p)


def _mod_kernel(c_ref, w_ref, b_ref, o_ref):
    s = _silu(c_ref[...]).astype(BF16)
    o_ref[0] = _dg(s, w_ref[0].astype(BF16)) + b_ref[0]


def _modulation(c_all, ada_w, ada_b):
    depth, d, n = ada_w.shape
    m = c_all.shape[0]
    tn = 1024
    return pl.pallas_call(
        _mod_kernel,
        grid=(depth, n // tn),
        in_specs=[pl.BlockSpec((m, d), lambda i, j: (0, 0)),
                  pl.BlockSpec((1, d, tn), lambda i, j: (i, 0, j)),
                  pl.BlockSpec((1, 1, tn), lambda i, j: (i, 0, j))],
        out_specs=pl.BlockSpec((1, m, tn), lambda i, j: (i, 0, j)),
        out_shape=jax.ShapeDtypeStruct((depth, m, n), F32),
        compiler_params=_cparams(("arbitrary", "arbitrary")),
        name="adaln_mod",
    )(c_all, ada_w, ada_b.reshape(depth, 1, n))


def _row_tile(t):
    return min(512, t)


def _mod_spec(mod, tm, m):
    r = mod.shape[1]
    d = mod.shape[2] // N_MOD
    if r == 1:
        return pl.BlockSpec((1, 1, d), lambda s, t: (s, 0, m))
    return pl.BlockSpec((1, tm, d), lambda s, t: (s, t, m))


def _resident(shape):
    nd = len(shape)
    return pl.BlockSpec(shape, lambda s, t: (0,) * nd, pipeline_mode=pl.Buffered(1))


def _ffn_kernel(x_ref, sh_ref, sc_ref, g_ref, nw_ref, w1_ref, w2_ref, o_ref, acc_ref, *, fc):
    x = x_ref[0]
    hb = _norm_mod(x, nw_ref[...], sc_ref[0], sh_ref[0]).astype(BF16)
    f = w2_ref.shape[0]
    for c in range(f // fc):
        gt = _dg(hb, w1_ref[:, c * fc:(c + 1) * fc])
        up = _dg(hb, w1_ref[:, f + c * fc:f + (c + 1) * fc])
        a = (_silu(gt) * up).astype(BF16)
        contrib = _dg(a, w2_ref[c * fc:(c + 1) * fc, :])
        if c == 0:
            acc_ref[...] = contrib
        else:
            acc_ref[...] += contrib
    o_ref[0] = x + (0.5 * g_ref[0]) * acc_ref[...]


def _ffn(x, mod, m0, nw, w1b, w2b):
    s, t, d = x.shape
    tm = _row_tile(t)
    f = w2b.shape[0]
    fc = 256 if f % 256 == 0 else LANES
    xs = pl.BlockSpec((1, tm, d), lambda si, ti: (si, ti, 0))
    return pl.pallas_call(
        functools.partial(_ffn_kernel, fc=fc),
        grid=(s, t // tm),
        in_specs=[xs, _mod_spec(mod, tm, m0), _mod_spec(mod, tm, m0 + 1), _mod_spec(mod, tm, m0 + 2),
                  _resident((1, d)), _resident(w1b.shape), _resident(w2b.shape)],
        out_specs=xs,
        out_shape=jax.ShapeDtypeStruct(x.shape, F32),
        scratch_shapes=[pltpu.VMEM((tm, d), F32)],
        compiler_params=_cparams(("arbitrary", "arbitrary")),
        name="ffn",
    )(x, mod, mod, mod, nw.reshape(1, d), w1b, w2b)


def _inproj_kernel(x_ref, sh_ref, sc_ref, nw_ref, w_ref, *o_refs, outs):
    hb = _norm_mod(x_ref[0], nw_ref[...], sc_ref[0], sh_ref[0]).astype(BF16)
    done = {}
    for o_ref, (off, wd, dt) in zip(o_refs, outs):
        if (off, wd) not in done:
            done[(off, wd)] = _dg(hb, w_ref[:, off:off + wd])
        o_ref[0] = done[(off, wd)].astype(dt)


def _inproj(x, mod, m0, nw, wb, outs):
    s, t, d = x.shape
    tm = _row_tile(t)
    xs = pl.BlockSpec((1, tm, d), lambda si, ti: (si, ti, 0))
    return pl.pallas_call(
        functools.partial(_inproj_kernel, outs=tuple(outs)),
        grid=(s, t // tm),
        in_specs=[xs, _mod_spec(mod, tm, m0), _mod_spec(mod, tm, m0 + 1), _resident((1, d)), _resident(wb.shape)],
        out_specs=[pl.BlockSpec((1, tm, wd), lambda si, ti: (si, ti, 0)) for _, wd, _ in outs],
        out_shape=[jax.ShapeDtypeStruct((s, t, wd), dt) for _, wd, dt in outs],
        compiler_params=_cparams(("arbitrary", "arbitrary")),
        name="inproj",
    )(x, mod, mod, nw.reshape(1, d), wb)


def _outproj_kernel(*refs, widths):
    n = len(widths)
    x_ref, g_ref, w_ref = refs[0], refs[1], refs[2]
    a_refs = refs[3:3 + n]
    o_ref = refs[3 + n]
    acc = None
    off = 0
    for a_ref, wd in zip(a_refs, widths):
        part = _dg(a_ref[0].astype(BF16), w_ref[off:off + wd, :])
        acc = part if acc is None else acc + part
        off += wd
    o_ref[0] = x_ref[0] + g_ref[0] * acc


def _outproj(x, mod, mg, wb, parts):
    s, t, d = x.shape
    tm = _row_tile(t)
    widths = tuple(p.shape[-1] for p in parts)
    xs = pl.BlockSpec((1, tm, d), lambda si, ti: (si, ti, 0))
    return pl.pallas_call(
        functools.partial(_outproj_kernel, widths=widths),
        grid=(s, t // tm),
        in_specs=[xs, _mod_spec(mod, tm, mg), _resident(wb.shape)]
                 + [pl.BlockSpec((1, tm, wd), lambda si, ti: (si, ti, 0)) for wd in widths],
        out_specs=xs,
        out_shape=jax.ShapeDtypeStruct(x.shape, F32),
        compiler_params=_cparams(("arbitrary", "arbitrary")),
        name="outproj",
    )(x, mod, wb, *parts)


def _final_norm_kernel(x_ref, w_ref, o_ref):
    x = x_ref[0]
    ms = jnp.mean(x * x, axis=-1, keepdims=True)
    o_ref[0] = x * lax.rsqrt(ms + EPS) * w_ref[...]


def _final_norm(x, w):
    s, t, d = x.shape
    tm = _row_tile(t)
    xs = pl.BlockSpec((1, tm, d), lambda si, ti: (si, ti, 0))
    return pl.pallas_call(
        _final_norm_kernel,
        grid=(s, t // tm),
        in_specs=[xs, pl.BlockSpec((1, d), lambda si, ti: (0, 0))],
        out_specs=xs,
        out_shape=jax.ShapeDtypeStruct(x.shape, F32),
        compiler_params=_cparams(("arbitrary", "arbitrary")),
        name="final_norm",
    )(x, w.reshape(1, d))


def _t5_bucket(dist):
    n = jnp.maximum(dist, 0)
    max_exact = NUM_BUCKETS // 2
    nf = jnp.maximum(n, 1).astype(F32)
    large = max_exact + (jnp.log(nf / max_exact) / math.log(MAX_DISTANCE / max_exact)
                         * (NUM_BUCKETS - max_exact)).astype(I32)
    large = jnp.minimum(large, NUM_BUCKETS - 1)
    return jnp.where(n < max_exact, n, large)


def _bias_tiles(bias_cols, blk, causal):
    assert blk >= MAX_DISTANCE
    n = blk
    h = bias_cols.shape[1]
    tiles = []
    for t in range(3):
        off = (2 - t) * n
        if t == 0:
            far = bias_cols[_t5_bucket(jnp.full((), 2 * n, I32))]
            tiles.append(jnp.broadcast_to(far[:, None, None], (h, n, n)))
            continue
        dist = n - 1 + off - jnp.arange(2 * n, dtype=I32)
        g = bias_cols[_t5_bucket(dist)]
        if causal:
            g = jnp.where((dist >= 0)[:, None], g, NEG)
        x = jnp.broadcast_to(g.T[:, None, :], (h, n, 2 * n))
        x = jnp.pad(x, ((0, 0), (0, 0), (0, 1))).reshape(h, n * (2 * n + 1))[:, :2 * n * n].reshape(h, n, 2 * n)
        tiles.append(x[:, :, n - 1:2 * n - 1])
    return jnp.stack(tiles, axis=1)


def _bias_decode(bias_cols, q_pos, n_blocks):
    k_pos = jnp.arange(n_blocks * BLK, dtype=I32)
    b = bias_cols[_t5_bucket(q_pos - k_pos)]
    b = jnp.where((k_pos <= q_pos)[:, None], b, NEG)
    return jnp.moveaxis(b.reshape(n_blocks, BLK, -1), 2, 1)


def _score_keys(s):
    bits = pltpu.bitcast(s, I32)
    return jnp.where(bits < 0, bits ^ jnp.int32(0x7FFFFFFF), bits)


def _select_topk(key_ref, nkb, topk, idx_bits, key_axis=1, pairs=False):
    tile = tuple(key_ref.shape[1:])
    assert tile[key_axis] == BLK
    stat = tuple(1 if a == key_axis else n for a, n in enumerate(tile))
    rows = stat
    col = lax.broadcasted_iota(I32, tile, key_axis)

    def count(pred):
        if pairs:
            def body(i, acc):
                kb = 2 * i
                return (acc + jnp.where(pred(key_ref[kb], kb * BLK + col), 1.0, 0.0)
                        + jnp.where(pred(key_ref[kb + 1], (kb + 1) * BLK + col), 1.0, 0.0))
            acc = lax.fori_loop(0, (nkb + 1) // 2, body, jnp.zeros(tile, F32))
        else:
            def body(kb, acc):
                return acc + jnp.where(pred(key_ref[kb], kb * BLK + col), 1.0, 0.0)
            acc = lax.fori_loop(0, nkb, body, jnp.zeros(tile, F32))
        return jnp.sum(acc, axis=key_axis, keepdims=True)

    kf = float(topk)

    def bit_body(i, carry):
        lo, cnt_lo = carry
        cand = lo + lax.shift_left(jnp.int32(1), jnp.int32(31) - i)
        cnt = count(lambda k, _: k >= cand)
        take = cnt >= kf
        return jnp.where(take, cand, lo), jnp.where(take, cnt, cnt_lo)

    n_visited = 2 * ((nkb + 1) // 2) if pairs else nkb
    total = jnp.zeros(rows, F32) + jnp.asarray(n_visited * BLK, F32)
    thr, cnt_ge = lax.fori_loop(0, 32, bit_body, (jnp.full(rows, -2 ** 31, I32), total))
    need = kf - count(lambda k, _: k > thr)

    has_ties = jnp.max(cnt_ge) > kf

    def idx_body(i, p):
        cand = p + lax.shift_left(jnp.int32(1), jnp.int32(idx_bits - 1) - i)
        cnt = count(lambda k, ix: jnp.logical_and(k == thr, ix < cand))
        return jnp.where(cnt < need, cand, p)

    cut0 = jnp.zeros(rows, I32) + jnp.where(has_ties, 0, 2 ** idx_bits - 1)
    cut = lax.fori_loop(0, jnp.where(has_ties, idx_bits, 0), idx_body, cut0)
    return thr, cut


def _in_topk(key, idx, thr, cut):
    return jnp.logical_or(key > thr, jnp.logical_and(key == thr, idx <= cut))


def _dsa_kernel(q_ref, qi_ref, sm_ref, kf_ref, vf_ref, smf_ref, bias_ref, o_ref, key_ref, msk_ref,
                *, topk, idx_bits):
    qb = pl.program_id(1)
    nkb = qb + 1
    key_i = lax.broadcasted_iota(I32, (BLK, BLK), 0)
    q_pos = qb * BLK + lax.broadcasted_iota(I32, (BLK, BLK), 1)

    qi = qi_ref[0]
    sm_t = sm_ref[0].T
    qi_st = jnp.concatenate([qi[:, h * IDX_DIM:(h + 1) * IDX_DIM] for h in range(IDX_HEADS)], axis=0).astype(BF16)
    wi_rows = [sm_t[SM_WI + h:SM_WI + h + 1, :] * (IDX_HEADS ** -0.5) for h in range(IDX_HEADS)]

    def score_body(i, carry):
        kbs = (2 * i, 2 * i + 1)
        kis = [smf_ref[0, pl.ds(pl.multiple_of(kb * BLK, BLK), BLK), :][:, SM_KI:SM_KI + IDX_DIM].astype(BF16)
               for kb in kbs]
        rs = [jnp.maximum(_dg(ki, qi_st, _NT) * (IDX_DIM ** -0.5), 0.0) for ki in kis]
        for kb, r in zip(kbs, rs):
            s = wi_rows[0] * r[:, 0:BLK]
            for h in range(1, IDX_HEADS):
                s = s + wi_rows[h] * r[:, h * BLK:(h + 1) * BLK]
            s = jnp.where(kb * BLK + key_i <= q_pos, s, -jnp.inf)
            key_ref[kb] = _score_keys(s)
        return carry

    lax.fori_loop(0, (nkb + 1) // 2, score_body, 0)

    @pl.when(nkb * BLK <= topk)
    def _():
        def body(kb, carry):
            msk_ref[kb] = jnp.where(kb * BLK + key_i <= q_pos, 0.0, NEG)
            return carry
        lax.fori_loop(0, nkb, body, 0)

    @pl.when(nkb * BLK > topk)
    def _():
        thr, cut = _select_topk(key_ref, nkb, topk, idx_bits, key_axis=0, pairs=True)

        def body(kb, carry):
            k_pos = kb * BLK + key_i
            sel = _in_topk(key_ref[kb], k_pos, thr, cut)
            msk_ref[kb] = jnp.where(k_pos <= q_pos, jnp.where(sel, 0.0, NEG), NEG)
            return carry
        lax.fori_loop(0, nkb, body, 0)

    q = q_ref[0] * (A_HEAD_DIM ** -0.5)
    gq = A_GROUP * BLK
    ns = range(A_KV_HEADS)
    qs = [jnp.concatenate([q[:, (n * A_GROUP + g) * A_HEAD_DIM:(n * A_GROUP + g + 1) * A_HEAD_DIM]
                           for g in range(A_GROUP)], axis=0).astype(BF16) for n in ns]

    def logits(kb):
        off = pl.multiple_of(kb * BLK, BLK)
        kblk = kf_ref[0, pl.ds(off, BLK), :].astype(BF16)
        t = jnp.clip(kb - qb + 2, 0, 2)
        mk4 = jnp.concatenate([msk_ref[kb]] * A_GROUP, axis=1)
        return [_dg(kblk[:, n * A_HEAD_DIM:(n + 1) * A_HEAD_DIM], qs[n], _NT) + bias_ref[n, t] + mk4 for n in ns]

    def att_body(kb, carry):
        ss = carry[3 * A_KV_HEADS:]
        ss_next = logits(jnp.minimum(kb + 1, nkb - 1))
        off = pl.multiple_of(kb * BLK, BLK)
        vblk = vf_ref[0, pl.ds(off, BLK), :].astype(BF16)
        m_new = [jnp.maximum(carry[3 * n], jnp.max(ss[n], axis=0, keepdims=True)) for n in ns]
        ps = [jnp.exp(ss[n] - m_new[n]) for n in ns]
        pvs = [_dg(vblk[:, n * A_HEAD_DIM:(n + 1) * A_HEAD_DIM], ps[n].astype(BF16), _TN) for n in ns]
        out = []
        for n in ns:
            alpha = jnp.exp(carry[3 * n] - m_new[n])
            out += [m_new[n], alpha * carry[3 * n + 1] + jnp.sum(ps[n], axis=0, keepdims=True),
                    alpha * carry[3 * n + 2] + pvs[n]]
        return tuple(out) + tuple(ss_next)

    init = (jnp.full((1, gq), NEG, F32), jnp.zeros((1, gq), F32), jnp.zeros((A_HEAD_DIM, gq), F32)) * A_KV_HEADS
    res = lax.fori_loop(0, nkb, att_body, init + tuple(logits(0)))
    for n in ns:
        o_t = res[3 * n + 2] / res[3 * n + 1]
        for g in range(0, A_GROUP, 2):
            h = n * A_GROUP + g
            pair = jnp.concatenate([o_t[:, g * BLK:(g + 1) * BLK], o_t[:, (g + 1) * BLK:(g + 2) * BLK]], axis=0)
            o_ref[0, :, h * A_HEAD_DIM:(h + 2) * A_HEAD_DIM] = pair.T


def _dsa_prompt(q, qi, sm, k, v, bias_tiles, topk):
    b, t, _ = q.shape
    nb = t // BLK
    assert nb % 2 == 0
    idx_bits = max(1, int(math.ceil(math.log2(t))))
    bt = bias_tiles.reshape(A_KV_HEADS, A_GROUP, 3, BLK, BLK)
    bt = jnp.transpose(bt, (0, 2, 4, 1, 3)).reshape(A_KV_HEADS, 3, BLK, A_GROUP * BLK)
    blk = lambda w: pl.BlockSpec((1, BLK, w), lambda bi, qb: (bi, qb, 0))
    full = lambda w: pl.BlockSpec((1, t, w), lambda bi, qb: (bi, 0, 0))
    return pl.pallas_call(
        functools.partial(_dsa_kernel, topk=topk, idx_bits=idx_bits),
        grid=(b, nb),
        in_specs=[blk(A_Q), blk(IDX_HEADS * IDX_DIM), blk(LANES), full(A_KV), full(A_KV), full(LANES),
                  pl.BlockSpec(bt.shape, lambda bi, qb: (0, 0, 0, 0))],
        out_specs=blk(A_Q),
        out_shape=jax.ShapeDtypeStruct((b, t, A_Q), F32),
        scratch_shapes=[pltpu.VMEM((nb, BLK, BLK), I32), pltpu.VMEM((nb, BLK, BLK), F32)],
        compiler_params=_cparams(("arbitrary", "arbitrary")),
        name="dsa_prompt",
    )(q, qi, sm, k, v, sm, bt)


def _gdn_kernel(qkv_ref, z_ref, sm_ref, s0_ref, cw_ref, par_ref, nw_ref, o_ref, sout_ref, ext_ref, st_ref):
    t = pl.program_id(1)
    nt = pl.num_programs(1)
    c = CHUNK
    halo = SUBLANES

    nbb = qkv_ref.shape[0]
    bbs = range(nbb)

    @pl.when(t == 0)
    def _():
        for bb in bbs:
            ext_ref[bb, 0:halo, :] = jnp.zeros((halo, B_CONV_DIM), F32)
        st_ref[...] = s0_ref[...]

    convs = []
    for bb in bbs:
        x = qkv_ref[bb]
        ext_ref[bb, halo:halo + c, :] = x
        conv = None
        for j in range(CONV_W):
            start = halo - (CONV_W - 1) + j
            term = ext_ref[bb, start:start + c, :] * cw_ref[j:j + 1, :]
            conv = term if conv is None else conv + term
        ext_ref[bb, 0:halo, :] = x[c - halo:c, :]
        convs.append(_silu(conv))

    ri = lax.broadcasted_iota(I32, (c, c), 0)
    ci = lax.broadcasted_iota(I32, (c, c), 1)
    tri = ri >= ci
    stri = ri > ci
    eye = jnp.where(ri == ci, 1.0, 0.0)
    tri_bf = jnp.where(tri, 1.0, 0.0).astype(BF16)
    ones_bf = jnp.ones((c, c), BF16)

    sms = [sm_ref[bb] for bb in bbs]
    g_alls = [-jnp.exp(par_ref[0:1, :]) * _softplus(sm + par_ref[1:2, :]) for sm in sms]
    gc_alls = [_dot_exact_lhs(tri_bf, g) for g in g_alls]
    beta_alls = [jax.nn.sigmoid(sm) for sm in sms]
    nw = nw_ref[...]
    items = [(bb, h) for bb in bbs for h in range(B_HEADS)]
    hs = range(len(items))
    gcs = [gc_alls[bb][:, SM_A + h:SM_A + h + 1] for bb, h in items]
    gc_rows = [_dot_exact_lhs(ones_bf, jnp.concatenate([eye * gcs[bb * B_HEADS + h] for h in range(B_HEADS)],
                                                       axis=1)) for bb in bbs]
    qs = [convs[bb][:, h * B_KEY_DIM:(h + 1) * B_KEY_DIM] for bb, h in items]
    ks = [convs[bb][:, B_K + h * B_KEY_DIM:B_K + (h + 1) * B_KEY_DIM] for bb, h in items]
    vs = [convs[bb][:, 2 * B_K + h * B_VAL_DIM:2 * B_K + (h + 1) * B_VAL_DIM] for bb, h in items]
    qs = [q * lax.rsqrt(jnp.sum(q * q, axis=-1, keepdims=True) + EPS) * (B_KEY_DIM ** -0.5) for q in qs]
    ks = [k * lax.rsqrt(jnp.sum(k * k, axis=-1, keepdims=True) + EPS) for k in ks]
    betas = [beta_alls[bb][:, SM_B + h:SM_B + h + 1] for bb, h in items]
    decays = [jnp.exp(jnp.where(tri, gcs[i] - gc_rows[bb][:, h * c:(h + 1) * c], -jnp.inf))
              for i, (bb, h) in enumerate(items)]
    kbs = [ks[h] * betas[h] for h in hs]
    vbs = [vs[h] * betas[h] for h in hs]
    k_bf = [k.astype(BF16) for k in ks]
    a_mats = [jnp.where(stri, _dg(kbs[h].astype(BF16), k_bf[h], _NT) * decays[h], 0.0) for h in hs]
    tms = [eye - a for a in a_mats]
    pws = [_dot3(a, a) for a in a_mats]
    steps = int(math.log2(c))
    for j in range(1, steps):
        tms = [tm + _dot3(tm, pw) for tm, pw in zip(tms, pws)]
        if j < steps - 1:
            pws = [_dot3(pw, pw) for pw in pws]
    egs = [jnp.exp(gc) for gc in gcs]
    tm_bf = [tm.astype(BF16) for tm in tms]
    us = [_dg(tm_bf[h], vbs[h].astype(BF16)) for h in hs]
    ws = [_dg(tm_bf[h], (kbs[h] * egs[h]).astype(BF16)) for h in hs]
    a_qks = [jnp.where(tri, _dg(qs[h].astype(BF16), k_bf[h], _NT) * decays[h], 0.0) for h in hs]

    s_olds = [st_ref[bb, h] for bb, h in items]
    s_bf = [s.astype(BF16) for s in s_olds]
    v_news = [us[i] - _dg(ws[i].astype(BF16), s_bf[i]) for i in hs]
    os_ = [_dg((qs[i] * egs[i]).astype(BF16), s_bf[i]) + _dot1(a_qks[i], v_news[i]) for i in hs]
    g_lasts = [gc[c - 1:c, :] for gc in gcs]
    for i, (bb, h) in enumerate(items):
        st_ref[bb, h] = (s_olds[i] * jnp.exp(g_lasts[i])
                         + _dot1(ks[i] * jnp.exp(g_lasts[i] - gcs[i]), v_news[i], _TN))
    for i, (bb, h) in enumerate(items):
        o = os_[i]
        ms = jnp.mean(o * o, axis=-1, keepdims=True)
        zh = z_ref[bb, :, h * B_VAL_DIM:(h + 1) * B_VAL_DIM]
        o_ref[bb, :, h * B_VAL_DIM:(h + 1) * B_VAL_DIM] = (o * lax.rsqrt(ms + EPS) * nw) * _silu(zh)

    @pl.when(t == nt - 1)
    def _():
        sout_ref[...] = st_ref[...]


GDN_SEQS_PER_STEP = 2


def _gdn_params(a_log, dt_bias):
    par = jnp.zeros((SUBLANES, LANES), F32)
    return par.at[0, SM_A:SM_A + B_HEADS].set(a_log).at[1, SM_A:SM_A + B_HEADS].set(dt_bias)


def _gdn_prompt(qkv, z, sm, s0, conv_w, a_log, dt_bias, norm_w):
    b, t, _ = qkv.shape
    nt = t // CHUNK
    nbb = GDN_SEQS_PER_STEP if b % GDN_SEQS_PER_STEP == 0 else 1
    tok = lambda w: pl.BlockSpec((nbb, CHUNK, w), lambda bi, ti: (bi, ti, 0))
    const2 = lambda shp: pl.BlockSpec(shp, lambda bi, ti: (0, 0))
    st_spec = pl.BlockSpec((nbb, B_HEADS, B_KEY_DIM, B_VAL_DIM), lambda bi, ti: (bi, 0, 0, 0))
    o, s_out = pl.pallas_call(
        _gdn_kernel,
        grid=(b // nbb, nt),
        in_specs=[tok(B_CONV_DIM), tok(B_V), tok(LANES), st_spec,
                  const2((CONV_W, B_CONV_DIM)), const2((SUBLANES, LANES)), const2((1, B_VAL_DIM))],
        out_specs=[tok(B_V), st_spec],
        out_shape=[jax.ShapeDtypeStruct((b, t, B_V), F32),
                   jax.ShapeDtypeStruct((b, B_HEADS, B_KEY_DIM, B_VAL_DIM), F32)],
        scratch_shapes=[pltpu.VMEM((nbb, SUBLANES + CHUNK, B_CONV_DIM), F32),
                        pltpu.VMEM((nbb, B_HEADS, B_KEY_DIM, B_VAL_DIM), F32)],
        compiler_params=_cparams(("arbitrary", "arbitrary")),
        name="gdn_prompt",
    )(qkv, z, sm, s0, conv_w, _gdn_params(a_log, dt_bias), norm_w.reshape(1, B_VAL_DIM))
    return o, s_out


GDN_DEC_ROWS = 16


def _gdn_dec_kernel(qkv_ref, z_ref, sm_ref, cb_ref, s_ref, cw_ref, par_ref, nw_ref, o_ref, so_ref, oraw_ref):
    r = GDN_DEC_ROWS
    conv = qkv_ref[...] * cw_ref[CONV_W - 1:CONV_W, :]
    for j in range(CONV_W - 1):
        conv = conv + cb_ref[0, j] * cw_ref[j:j + 1, :]
    conv = _silu(conv)
    sm = sm_ref[...]
    g_all = -jnp.exp(par_ref[0:1, :]) * _softplus(sm + par_ref[1:2, :])
    beta_all = jax.nn.sigmoid(sm)
    ri = lax.broadcasted_iota(I32, (r, LANES), 0)
    ci = lax.broadcasted_iota(I32, (r, LANES), 1)
    eye_bf = jnp.where(ri == ci, 1.0, 0.0).astype(BF16)

    for h in range(B_HEADS):
        qh = conv[:, h * B_KEY_DIM:(h + 1) * B_KEY_DIM]
        kh = conv[:, B_K + h * B_KEY_DIM:B_K + (h + 1) * B_KEY_DIM]
        vh = conv[:, 2 * B_K + h * B_VAL_DIM:2 * B_K + (h + 1) * B_VAL_DIM]
        qh = qh * lax.rsqrt(jnp.sum(qh * qh, axis=-1, keepdims=True) + EPS) * (B_KEY_DIM ** -0.5)
        kh = kh * lax.rsqrt(jnp.sum(kh * kh, axis=-1, keepdims=True) + EPS)
        eg = jnp.exp(g_all[:, SM_A + h:SM_A + h + 1])
        beta = beta_all[:, SM_B + h:SM_B + h + 1]
        qk = jnp.sum(qh * kh, axis=-1, keepdims=True)
        k_t = _dot_exact_rhs(kh, eye_bf, _TN)
        q_t = _dot_exact_rhs(qh, eye_bf, _TN)
        for s in range(r):
            st = s_ref[0, s, h]
            kc = k_t[:, s:s + 1]
            qc = q_t[:, s:s + 1]
            k_s = jnp.sum(st * kc, axis=0, keepdims=True)
            q_s = jnp.sum(st * qc, axis=0, keepdims=True)
            eg_s = eg[s:s + 1, :]
            v_new = beta[s:s + 1, :] * (vh[s:s + 1, :] - eg_s * k_s)
            oraw_ref[s:s + 1, h * B_VAL_DIM:(h + 1) * B_VAL_DIM] = eg_s * q_s + qk[s:s + 1, :] * v_new
            so_ref[s, h] = st * eg_s + kc * v_new

    zz = z_ref[...]
    nw = nw_ref[...]
    for h in range(B_HEADS):
        o = oraw_ref[:, h * B_VAL_DIM:(h + 1) * B_VAL_DIM]
        ms = jnp.mean(o * o, axis=-1, keepdims=True)
        zh = zz[:, h * B_VAL_DIM:(h + 1) * B_VAL_DIM]
        o_ref[:, h * B_VAL_DIM:(h + 1) * B_VAL_DIM] = (o * lax.rsqrt(ms + EPS) * nw) * _silu(zh)


def _gdn_decode(qkv, z, sm, conv_t, state, j, conv_w, a_log, dt_bias, norm_w):
    b = qkv.shape[0]
    r = GDN_DEC_ROWS
    assert b % r == 0
    rows = lambda w: pl.BlockSpec((r, w), lambda i: (i, 0))
    const2 = lambda shp: pl.BlockSpec(shp, lambda i: (0, 0))
    o, s_out = pl.pallas_call(
        _gdn_dec_kernel,
        grid=(b // r,),
        in_specs=[rows(B_CONV_DIM), rows(B_V), rows(LANES),
                  pl.BlockSpec((1, CONV_W - 1, r, B_CONV_DIM), lambda i: (j, 0, i, 0)),
                  pl.BlockSpec((1, r, B_HEADS, B_KEY_DIM, B_VAL_DIM), lambda i: (j, i, 0, 0, 0)),
                  const2((CONV_W, B_CONV_DIM)), const2((SUBLANES, LANES)), const2((1, B_VAL_DIM))],
        out_specs=[rows(B_V), pl.BlockSpec((r, B_HEADS, B_KEY_DIM, B_VAL_DIM), lambda i: (i, 0, 0, 0))],
        out_shape=[jax.ShapeDtypeStruct((b, B_V), F32),
                   jax.ShapeDtypeStruct((b, B_HEADS, B_KEY_DIM, B_VAL_DIM), F32)],
        scratch_shapes=[pltpu.VMEM((r, B_V), F32)],
        compiler_params=_cparams(("arbitrary",)),
        name="gdn_decode",
    )(qkv, z, sm, conv_t, state, conv_w, _gdn_params(a_log, dt_bias), norm_w.reshape(1, B_VAL_DIM))
    return o, s_out


DIFF_HEADS_PER_STEP = 2


def _diff_kernel(lam_ref, q_ref, k_ref, v_ref, bias_ref, sw_ref, o_ref, *, lam_init):
    qb = pl.program_id(2)
    nkb = qb + 1
    nh = DIFF_HEADS_PER_STEP
    lane = lax.broadcasted_iota(I32, (TQ, C_HD), 1)
    q2 = []
    for h in range(nh):
        qh = q_ref[0, :, h * C_HD:(h + 1) * C_HD] * (C_HEAD_DIM ** -0.5)
        zero = jnp.zeros_like(qh)
        q2.append(jnp.concatenate([jnp.where(lane < C_HEAD_DIM, qh, zero),
                                   jnp.where(lane >= C_HEAD_DIM, qh, zero)], axis=0))

    hs = range(nh)

    def logits(kb):
        off = pl.multiple_of(kb * TQ, TQ)
        t = jnp.clip(kb - qb + 2, 0, 2)
        ss = [_dg(k_ref[0, pl.ds(off, TQ), h * C_HD:(h + 1) * C_HD], q2[h], _NT) for h in hs]
        return [ss[h] + jnp.concatenate([bias_ref[h, t]] * 2, axis=1) for h in hs]

    def body(kb, carry):
        ss = logits(kb)
        off = pl.multiple_of(kb * TQ, TQ)
        m_new = [jnp.maximum(carry[3 * h], jnp.max(ss[h], axis=0, keepdims=True)) for h in hs]
        ps = [jnp.exp(ss[h] - m_new[h]) for h in hs]
        pvs = [_dg(v_ref[0, pl.ds(off, TQ), h * C_HD:(h + 1) * C_HD], ps[h].astype(BF16), _TN) for h in hs]
        out = []
        for h in hs:
            alpha = jnp.exp(carry[3 * h] - m_new[h])
            out += [m_new[h], alpha * carry[3 * h + 1] + jnp.sum(ps[h], axis=0, keepdims=True),
                    alpha * carry[3 * h + 2] + pvs[h]]
        return tuple(out)

    init = (jnp.full((1, 2 * TQ), NEG, F32), jnp.zeros((1, 2 * TQ), F32), jnp.zeros((C_HD, 2 * TQ), F32)) * nh
    res = lax.fori_loop(0, nkb, body, init)
    for h in range(nh):
        _, l, acc = res[3 * h:3 * h + 3]
        on = acc / l
        o = (on[:, 0:TQ] - lam_ref[0] * on[:, TQ:2 * TQ]).T
        ms = jnp.mean(o * o, axis=-1, keepdims=True)
        o_ref[0, :, h * C_HD:(h + 1) * C_HD] = (o * lax.rsqrt(ms + EPS) * sw_ref[...]) * (1.0 - lam_init)


def _diff_prompt(q, k, v, bias_tiles, lam, subln_w, lam_init):
    b, t, _ = q.shape
    nh = DIFF_HEADS_PER_STEP
    w = nh * C_HD
    return pl.pallas_call(
        functools.partial(_diff_kernel, lam_init=lam_init),
        grid=(b, C_HEADS // nh, t // TQ),
        in_specs=[pl.BlockSpec(memory_space=pltpu.SMEM),
                  pl.BlockSpec((1, TQ, w), lambda bi, h, qb: (bi, qb, h)),
                  pl.BlockSpec((1, t, w), lambda bi, h, qb: (bi, 0, h)),
                  pl.BlockSpec((1, t, w), lambda bi, h, qb: (bi, 0, h)),
                  pl.BlockSpec((nh, 3, TQ, TQ), lambda bi, h, qb: (h, 0, 0, 0)),
                  pl.BlockSpec((1, C_HD), lambda bi, h, qb: (0, 0))],
        out_specs=pl.BlockSpec((1, TQ, w), lambda bi, h, qb: (bi, qb, h)),
        out_shape=jax.ShapeDtypeStruct((b, t, C_V), F32),
        compiler_params=_cparams(("arbitrary", "arbitrary", "arbitrary")),
        name="diff_prompt",
    )(lam.reshape(1), q, k, v, bias_tiles, subln_w.reshape(1, C_HD))


DIFF_DEC_PAGES = 8


def _diff_dec_kernel(pt_ref, lam_ref, q_ref, kn_ref, vn_ref, *rest, n_steps, lam_init):
    pp = DIFF_DEC_PAGES
    kps, vps = rest[:pp], rest[pp:2 * pp]
    bias_ref, bnew_ref, sw_ref, o_ref, m_ref, l_ref, acc_ref = rest[2 * pp:]
    p = pl.program_id(1)
    nr = 2 * C_HEADS

    @pl.when(p == 0)
    def _():
        m_ref[...] = jnp.full(m_ref.shape, NEG, F32)
        l_ref[...] = jnp.zeros(l_ref.shape, F32)
        acc_ref[...] = jnp.zeros(acc_ref.shape, F32)

    q8 = q_ref[0] * (C_HEAD_DIM ** -0.5)
    lane = lax.broadcasted_iota(I32, (C_HEADS, C_HD), 1)
    q_lo = jnp.where(lane < C_HEAD_DIM, q8, 0.0)
    q_hi = jnp.where(lane >= C_HEAD_DIM, q8, 0.0)
    qm = jnp.concatenate([q_lo, q_hi], axis=0).astype(BF16)

    rows = BLK * C_HEADS
    half = pp // 2
    cs = range(2)
    ss = [jnp.concatenate([_dg(qm, kps[c * half + i][0, 0].astype(BF16), _NT) + bias_ref[c * half + i]
                           for i in range(half)], axis=1) for c in cs]
    m_old = [m_ref[c] for c in cs]
    m_new = [jnp.maximum(m_old[c], jnp.max(ss[c], axis=1, keepdims=True)) for c in cs]
    ps = [jnp.exp(ss[c] - m_new[c]) for c in cs]
    pb = [pr.astype(BF16) for pr in ps]
    pvs = []
    for c in cs:
        out = None
        for i in range(half):
            part = _dg(pb[c][:, i * rows:(i + 1) * rows], vps[c * half + i][0, 0].astype(BF16))
            out = part if out is None else out + part
        pvs.append(out)
    for c in cs:
        alpha = jnp.exp(m_old[c] - m_new[c])
        l_ref[c] = alpha * l_ref[c] + jnp.sum(ps[c], axis=1, keepdims=True)
        acc_ref[c] = alpha * acc_ref[c] + pvs[c]
        m_ref[c] = m_new[c]

    @pl.when(p == n_steps - 1)
    def _():
        kn = kn_ref[0]
        s_new = jnp.concatenate([jnp.sum(q_lo * kn, axis=1, keepdims=True),
                                 jnp.sum(q_hi * kn, axis=1, keepdims=True)], axis=0) + bnew_ref[:, 0:1]
        vn2 = jnp.concatenate([vn_ref[0], vn_ref[0]], axis=0)
        m = jnp.maximum(jnp.maximum(m_ref[0], m_ref[1]), s_new)
        w0 = jnp.exp(m_ref[0] - m)
        w1 = jnp.exp(m_ref[1] - m)
        p_new = jnp.exp(s_new - m)
        l = w0 * l_ref[0] + w1 * l_ref[1] + p_new
        acc = w0 * acc_ref[0] + w1 * acc_ref[1] + p_new * vn2
        on = acc / l
        o = on[0:C_HEADS] - lam_ref[0] * on[C_HEADS:nr]
        ms = jnp.mean(o * o, axis=-1, keepdims=True)
        o_ref[0] = (o * lax.rsqrt(ms + EPS) * sw_ref[...]) * (1.0 - lam_init)


def _diff_decode(q8, k8, v8, cache_k, cache_v, j, pt_flat, n_pages, bias_pages, bias_new, lam, subln_w, lam_init):
    b = q8.shape[0]
    pp = DIFF_DEC_PAGES
    assert n_pages % pp == 0
    n_steps = n_pages // pp
    rows = BLK * C_HEADS
    row = pl.BlockSpec((1, C_HEADS, C_HD), lambda bi, p, pt: (bi, 0, 0))
    page = lambda i: pl.BlockSpec((1, 1, rows, C_HD),
                                  lambda bi, p, pt: (j, pt[bi * n_pages + p * pp + i], 0, 0))
    grid_spec = pltpu.PrefetchScalarGridSpec(
        num_scalar_prefetch=1,
        grid=(b, n_steps),
        in_specs=[pl.BlockSpec(memory_space=pltpu.SMEM), row, row, row]
                 + [page(i) for i in range(pp)] + [page(i) for i in range(pp)]
                 + [pl.BlockSpec((pp, 2 * C_HEADS, rows), lambda bi, p, pt: (p, 0, 0)),
                    pl.BlockSpec((2 * C_HEADS, LANES), lambda bi, p, pt: (0, 0)),
                    pl.BlockSpec((1, C_HD), lambda bi, p, pt: (0, 0))],
        out_specs=row,
        scratch_shapes=[pltpu.VMEM((2, 2 * C_HEADS, 1), F32), pltpu.VMEM((2, 2 * C_HEADS, 1), F32),
                        pltpu.VMEM((2, 2 * C_HEADS, C_HD), F32)],
    )
    return pl.pallas_call(
        functools.partial(_diff_dec_kernel, n_steps=n_steps, lam_init=lam_init),
        grid_spec=grid_spec,
        out_shape=jax.ShapeDtypeStruct((b, C_HEADS, C_HD), F32),
        compiler_params=_cparams(("arbitrary", "arbitrary")),
        name="diff_decode",
    )(pt_flat, lam.reshape(1), q8, k8, v8, *([cache_k] * pp), *([cache_v] * pp),
      bias_pages, bias_new, subln_w.reshape(1, C_HD))


def _diff_decode_bias(bias_cols, q_pos, n_pages):
    bd = _bias_decode(bias_cols, q_pos, n_pages + 1)
    past = jnp.moveaxis(bd[:n_pages], 1, 2)
    same = jnp.eye(C_HEADS, dtype=bool)
    tab = jnp.where(same[None, :, None, :], past[:, None, :, :], NEG)
    tab = tab.reshape(n_pages, C_HEADS, BLK * C_HEADS)
    tab = jnp.concatenate([tab, tab], axis=1)
    new = bd[n_pages, :, 0]
    new = jnp.broadcast_to(jnp.concatenate([new, new])[:, None], (2 * C_HEADS, LANES))
    return tab, new


def _idx_dec_kernel(pt_ref, qi_ref, sm_ref, *rest, n_pages):
    kps = rest[:n_pages]
    o_ref = rest[n_pages]
    qi = qi_ref[0]
    sm = sm_ref[0]
    rowi = lax.broadcasted_iota(I32, (SUBLANES, IDX_HEADS * IDX_DIM), 0)
    lane = lax.broadcasted_iota(I32, (SUBLANES, IDX_HEADS * IDX_DIM), 1)
    qsel = jnp.where(lane // IDX_DIM == rowi, qi, 0.0)
    qt = qsel[:, 0:IDX_DIM]
    for h in range(1, IDX_HEADS):
        qt = qt + qsel[:, h * IDX_DIM:(h + 1) * IDX_DIM]
    r8 = lax.broadcasted_iota(I32, (SUBLANES, LANES), 0)
    l8 = lax.broadcasted_iota(I32, (SUBLANES, LANES), 1)
    wsel = jnp.where(jnp.logical_and(r8 < IDX_HEADS, l8 == r8 + SM_WI), sm, 0.0)
    wcol = jnp.sum(wsel, axis=1, keepdims=True) * (IDX_HEADS ** -0.5)
    qh, ql = _split2(qt)
    kh, kl = _split2(jnp.concatenate([kps[p][0, 0] for p in range(n_pages)], axis=1))
    d = _dg(qh, kh) + (_dg(qh, kl) + _dg(ql, kh))
    sc = jnp.sum(wcol * jnp.maximum(d * (IDX_DIM ** -0.5), 0.0), axis=0, keepdims=True)
    for p in range(n_pages):
        o_ref[0, p] = sc[:, p * BLK:(p + 1) * BLK]
    d_new = jnp.sum(qt * sm[:, SM_KI:SM_KI + IDX_DIM], axis=1, keepdims=True)
    s_new = jnp.sum(wcol * jnp.maximum(d_new * (IDX_DIM ** -0.5), 0.0), axis=0, keepdims=True)
    o_ref[0, n_pages] = jnp.broadcast_to(s_new, (1, BLK))


def _idx_decode(qi, sm, cache_kidx_t, j, pt_flat, n_pages):
    b = qi.shape[0]
    page = lambda p: pl.BlockSpec((1, 1, IDX_DIM, BLK), lambda bi, pt: (j, pt[bi * n_pages + p], 0, 0))
    grid_spec = pltpu.PrefetchScalarGridSpec(
        num_scalar_prefetch=1,
        grid=(b,),
        in_specs=[pl.BlockSpec((1, 1, IDX_HEADS * IDX_DIM), lambda bi, pt: (bi, 0, 0)),
                  pl.BlockSpec((1, 1, LANES), lambda bi, pt: (bi, 0, 0))]
                 + [page(p) for p in range(n_pages)],
        out_specs=pl.BlockSpec((1, n_pages + 1, 1, BLK), lambda bi, pt: (bi, 0, 0, 0)),
    )
    return pl.pallas_call(
        functools.partial(_idx_dec_kernel, n_pages=n_pages),
        grid_spec=grid_spec,
        out_shape=jax.ShapeDtypeStruct((b, n_pages + 1, 1, BLK), F32),
        compiler_params=_cparams(("arbitrary",)),
        name="idx_decode",
    )(pt_flat, qi, sm, *([cache_kidx_t] * n_pages))


def _sel_dec_kernel(s_ref, o_ref, key_ref, *, n_valid, topk, idx_bits):
    nkb, rows, _ = s_ref.shape
    col = lax.broadcasted_iota(I32, (rows, BLK), 1)
    for kb in range(nkb):
        s = jnp.where(kb * BLK + col < n_valid, s_ref[kb], -jnp.inf)
        key_ref[kb] = _score_keys(s)
    thr, cut = _select_topk(key_ref, nkb, topk, idx_bits)
    for kb in range(nkb):
        k_pos = kb * BLK + col
        sel = _in_topk(key_ref[kb], k_pos, thr, cut)
        o_ref[kb] = jnp.where(k_pos < n_valid, jnp.where(sel, 0.0, NEG), NEG)


def _sel_decode(scores, n_valid, topk):
    nkb, rows, _ = scores.shape
    idx_bits = max(1, int(math.ceil(math.log2(nkb * BLK))))
    return pl.pallas_call(
        functools.partial(_sel_dec_kernel, n_valid=n_valid, topk=topk, idx_bits=idx_bits),
        out_shape=jax.ShapeDtypeStruct(scores.shape, F32),
        scratch_shapes=[pltpu.VMEM(scores.shape, I32)],
        compiler_params=pltpu.CompilerParams(vmem_limit_bytes=VMEM_LIMIT),
        name="sel_decode",
    )(scores)


def _dsa_dec_kernel(pt_ref, q_ref, kn_ref, vn_ref, *rest, n_pages):
    kps, vps = rest[:n_pages], rest[n_pages:2 * n_pages]
    msk_ref, bias_ref, o_ref = rest[2 * n_pages:]

    q = q_ref[0] * (A_HEAD_DIM ** -0.5)
    rowi = lax.broadcasted_iota(I32, (A_HEADS, A_Q), 0)
    lane = lax.broadcasted_iota(I32, (A_HEADS, A_Q), 1)
    qsel = jnp.where(lane // A_HEAD_DIM == rowi, q, 0.0)
    halves = []
    for n in range(A_KV_HEADS):
        acc = None
        for g in range(A_GROUP):
            h = n * A_GROUP + g
            part = qsel[:, h * A_HEAD_DIM:(h + 1) * A_HEAD_DIM]
            acc = part if acc is None else acc + part
        halves.append(acc)
    qt = jnp.concatenate(halves, axis=1)
    qt_bf = qt.astype(BF16)

    kt = jnp.concatenate([kps[i][0, 0].astype(BF16) for i in range(n_pages)], axis=1)
    vt = jnp.concatenate([vps[i][0, 0].astype(BF16) for i in range(n_pages)], axis=1)
    extra = jnp.concatenate([bias_ref[i] + msk_ref[i, 0] for i in range(n_pages)], axis=1)
    s = _dg(qt_bf, kt) + extra
    s_new = (jnp.sum(qt * kn_ref[0], axis=1, keepdims=True) + bias_ref[n_pages][:, 0:1]
             + msk_ref[n_pages, 0][:, 0:1])
    m = jnp.maximum(jnp.max(s, axis=1, keepdims=True), s_new)
    p = jnp.exp(s - m)
    p_new = jnp.exp(s_new - m)
    l = jnp.sum(p, axis=1, keepdims=True) + p_new
    acc = p_new * vn_ref[0] + _dg(p.astype(BF16), vt, _NT)
    on = acc / l
    for h in range(A_HEADS):
        n = h // A_GROUP
        o_ref[0, :, h * A_HEAD_DIM:(h + 1) * A_HEAD_DIM] = on[h:h + 1, n * A_HEAD_DIM:(n + 1) * A_HEAD_DIM]


def _dsa_decode(q, k_new, v_new, cache_kt, cache_vt, j, pt_flat, n_pages, mask, bias_dec):
    b = q.shape[0]
    page = lambda i: pl.BlockSpec((1, 1, A_KV, BLK), lambda bi, pt: (j, pt[bi * n_pages + i], 0, 0))
    rowspec = lambda w: pl.BlockSpec((1, 1, w), lambda bi, pt: (bi, 0, 0))
    grid_spec = pltpu.PrefetchScalarGridSpec(
        num_scalar_prefetch=1,
        grid=(b,),
        in_specs=[rowspec(A_Q), rowspec(A_KV), rowspec(A_KV)]
                 + [page(i) for i in range(n_pages)] + [page(i) for i in range(n_pages)]
                 + [pl.BlockSpec((n_pages + 1, 1, 1, BLK), lambda bi, pt: (0, bi, 0, 0)),
                    pl.BlockSpec((n_pages + 1, A_HEADS, BLK), lambda bi, pt: (0, 0, 0))],
        out_specs=rowspec(A_Q),
    )
    return pl.pallas_call(
        functools.partial(_dsa_dec_kernel, n_pages=n_pages),
        grid_spec=grid_spec,
        out_shape=jax.ShapeDtypeStruct((b, 1, A_Q), F32),
        compiler_params=_cparams(("arbitrary",)),
        name="dsa_decode",
    )(pt_flat, q, k_new, v_new, *([cache_kt] * n_pages), *([cache_vt] * n_pages), mask, bias_dec)


def _even_weights(w_in):
    sizes = (A_Q, A_KV, A_KV, IDX_HEADS * IDX_DIM, IDX_DIM, IDX_HEADS, B_CONV_DIM, B_V, B_HEADS, B_HEADS)
    offs = np.concatenate([[0], np.cumsum(sizes)])
    seg = lambda i: w_in[:, int(offs[i]):int(offs[i + 1])]
    pad = LANES - (IDX_DIM + IDX_HEADS + 2 * B_HEADS)
    small = jnp.concatenate([seg(4), seg(5), seg(8), seg(9), jnp.zeros((w_in.shape[0], pad), w_in.dtype)], axis=1)
    return jnp.concatenate([seg(0), seg(1), seg(2), seg(3), small, seg(6), seg(7)], axis=1).astype(BF16)


def _seq_outs(widths, dtype=F32):
    outs, off = [], 0
    for wd in widths:
        outs.append((off, wd, dtype))
        off += wd
    return tuple(outs)


_EVEN_OUTS = _seq_outs((A_Q, A_KV, A_KV, IDX_HEADS * IDX_DIM, LANES, B_CONV_DIM, B_V))
_ODD_OUTS = _seq_outs((C_QK, C_QK, C_V))
_ODD_OUTS_PROMPT = _ODD_OUTS[1:] + _seq_outs((C_QK, C_QK, C_V), BF16)


def kernel(x_prompt, x_sample, c_prompt, c_sample, cache_A_k, cache_A_v, cache_A_kidx, cache_C_k, cache_C_v, state_B_ssm, state_B_conv, page_table, rel_bias, ada_w, ada_b, norm_w, final_norm_w, ffn_w1, ffn_w2, ab_w_in, ab_w_out, gdn_conv_w, gdn_a_log, gdn_dt_bias, gdn_norm_w, c_w_in, c_w_out, c_lambda_q1, c_lambda_k1, c_lambda_q2, c_lambda_k2, c_subln_w):
    depth = ada_w.shape[0]
    bp, tp, d = x_prompt.shape
    bs, ts, _ = x_sample.shape
    assert ts == 1 and tp % TQ == 0 and tp % CHUNK == 0
    n_pages = page_table.shape[1]
    page = cache_A_k.shape[2]
    assert page == BLK
    n_phys = cache_A_k.shape[1]
    past_len = n_pages * page
    topk_p = min(TOPK_MAX, tp // 4)
    topk_s = min(TOPK_MAX, (past_len + ts) // 4)
    pt_flat = page_table.reshape(-1).astype(I32)

    n_c = bp + bs
    n_c_pad = -(-n_c // SUBLANES) * SUBLANES
    c_all = jnp.pad(jnp.concatenate([c_prompt, c_sample], axis=0), ((0, n_c_pad - n_c), (0, 0)))
    mod_all = _modulation(c_all, ada_w, ada_b)

    w1b = ffn_w1.astype(BF16)
    w2b = ffn_w2.astype(BF16)
    ab_in_b = [_even_weights(ab_w_in[j]) for j in range(ab_w_in.shape[0])]
    ab_out_b = ab_w_out.astype(BF16)
    c_in_b = c_w_in.astype(BF16)
    c_out_b = c_w_out.astype(BF16)

    bias_a = _bias_tiles(rel_bias[:, :A_HEADS], BLK, False)
    bias_c = jnp.swapaxes(_bias_tiles(rel_bias[:, A_HEADS:], TQ, True), 2, 3)
    bias_a_dec = _bias_decode(rel_bias[:, :A_HEADS], past_len, n_pages + 1)
    bias_c_pages, bias_c_new = _diff_decode_bias(rel_bias[:, A_HEADS:], past_len, n_pages)

    n_ab = cache_A_k.shape[0]
    cache_a_kt = jnp.transpose(cache_A_k, (0, 1, 3, 4, 2)).reshape(n_ab, n_phys, A_KV, page)
    cache_a_vt = jnp.transpose(cache_A_v, (0, 1, 3, 4, 2)).reshape(n_ab, n_phys, A_KV, page)
    cache_a_it = jnp.transpose(cache_A_kidx, (0, 1, 3, 2))
    cache_c_k = cache_C_k.reshape(cache_C_k.shape[0], n_phys, page * C_HEADS, C_HD)
    cache_c_v = cache_C_v.reshape(cache_C_v.shape[0], n_phys, page * C_HEADS, C_HD)
    conv_t = jnp.transpose(state_B_conv, (0, 2, 1, 3))

    xp = x_prompt
    xs = x_sample.reshape(1, bs, d)
    new_p = [[] for _ in range(7)]
    new_s = [[] for _ in range(7)]

    for i in range(depth):
        j = i // 2
        mod_p = mod_all[i, :bp].reshape(bp, 1, N_MOD * d)
        mod_s = mod_all[i, bp:bp + bs].reshape(1, bs, N_MOD * d)
        xp = _ffn(xp, mod_p, 0, norm_w[i, 0], w1b[i, 0], w2b[i, 0])
        xs = _ffn(xs, mod_s, 0, norm_w[i, 0], w1b[i, 0], w2b[i, 0])
        if i % 2 == 0:
            q, k, v, qi, sm, qkv, z = _inproj(xp, mod_p, 3, norm_w[i, 1], ab_in_b[j], _EVEN_OUTS)
            o_a = _dsa_prompt(q, qi, sm, k, v, bias_a, topk_p)
            o_b, s_new = _gdn_prompt(qkv, z, sm, jnp.zeros((bp, B_HEADS, B_KEY_DIM, B_VAL_DIM), F32),
                                     gdn_conv_w[j], gdn_a_log[j], gdn_dt_bias[j], gdn_norm_w[j])
            xp = _outproj(xp, mod_p, 5, ab_out_b[j], [o_a, o_b])
            new_p[0].append(k.reshape(bp, tp, A_KV_HEADS, A_HEAD_DIM))
            new_p[1].append(v.reshape(bp, tp, A_KV_HEADS, A_HEAD_DIM))
            new_p[2].append(sm[:, :, SM_KI:SM_KI + IDX_DIM])
            new_p[3].append(s_new)
            new_p[4].append(qkv[:, tp - (CONV_W - 1):, :])
            q, k, v, qi, sm, qkv, z = _inproj(xs, mod_s, 3, norm_w[i, 1], ab_in_b[j], _EVEN_OUTS)
            as_rows = lambda a: a.reshape(bs, 1, a.shape[-1])
            scores = _idx_decode(as_rows(qi), as_rows(sm), cache_a_it, j, pt_flat, n_pages)
            scores = jnp.moveaxis(scores.reshape(bs, n_pages + 1, BLK), 1, 0)
            mask = _sel_decode(scores, past_len + 1, topk_s).reshape(n_pages + 1, bs, 1, BLK)
            o_a = _dsa_decode(as_rows(q), as_rows(k), as_rows(v), cache_a_kt, cache_a_vt, j, pt_flat, n_pages,
                              mask, bias_a_dec)
            o_b, s_new = _gdn_decode(qkv.reshape(bs, B_CONV_DIM), z.reshape(bs, B_V), sm.reshape(bs, LANES),
                                     conv_t, state_B_ssm, j, gdn_conv_w[j], gdn_a_log[j], gdn_dt_bias[j],
                                     gdn_norm_w[j])
            xs = _outproj(xs, mod_s, 5, ab_out_b[j], [o_a.reshape(1, bs, A_Q), o_b.reshape(1, bs, B_V)])
            new_s[0].append(k.reshape(bs, 1, A_KV_HEADS, A_HEAD_DIM))
            new_s[1].append(v.reshape(bs, 1, A_KV_HEADS, A_HEAD_DIM))
            new_s[2].append(sm.reshape(bs, 1, LANES)[:, :, SM_KI:SM_KI + IDX_DIM])
            new_s[3].append(s_new)
            new_s[4].append(jnp.concatenate([state_B_conv[j], as_rows(qkv)], axis=1)[:, 1:, :])
        else:
            lam_init = 0.8 - 0.6 * math.exp(-0.3 * i)
            lam = (jnp.exp(jnp.sum(c_lambda_q1[j] * c_lambda_k1[j]))
                   - jnp.exp(jnp.sum(c_lambda_q2[j] * c_lambda_k2[j])) + lam_init).astype(F32)
            k, v, qb16, kb16, vb16 = _inproj(xp, mod_p, 3, norm_w[i, 1], c_in_b[j], _ODD_OUTS_PROMPT)
            o = _diff_prompt(qb16, kb16, vb16, bias_c, lam, c_subln_w[j], lam_init)
            xp = _outproj(xp, mod_p, 5, c_out_b[j], [o])
            new_p[5].append(k.reshape(bp, tp, C_HEADS, C_HD))
            new_p[6].append(v.reshape(bp, tp, C_HEADS, C_HD))
            q, k, v = _inproj(xs, mod_s, 3, norm_w[i, 1], c_in_b[j], _ODD_OUTS)
            as_heads = lambda a: a.reshape(bs, C_HEADS, C_HD)
            o = _diff_decode(as_heads(q), as_heads(k), as_heads(v), cache_c_k, cache_c_v, j, pt_flat, n_pages,
                             bias_c_pages, bias_c_new, lam, c_subln_w[j], lam_init)
            xs = _outproj(xs, mod_s, 5, c_out_b[j], [o.reshape(1, bs, C_V)])
            new_s[5].append(k.reshape(bs, 1, C_HEADS, C_HD))
            new_s[6].append(v.reshape(bs, 1, C_HEADS, C_HD))
        xp = _ffn(xp, mod_p, 6, norm_w[i, 2], w1b[i, 1], w2b[i, 1])
        xs = _ffn(xs, mod_s, 6, norm_w[i, 2], w1b[i, 1], w2b[i, 1])

    y_prompt = _final_norm(xp, final_norm_w)
    y_sample = _final_norm(xs, final_norm_w).reshape(bs, 1, d)
    sp = [jnp.stack(lst) for lst in new_p]
    ss = [jnp.stack(lst) for lst in new_s]
    return (y_prompt, y_sample, *sp, *ss)
```

```python
import functools
import math

import numpy as np
import jax
import jax.numpy as jnp
from jax import lax
from jax.experimental import pallas as pl
from jax.experimental.pallas import tpu as pltpu

F32 = jnp.float32
BF16 = jnp.bfloat16
I32 = jnp.int32

A_HEADS = 8
A_KV_HEADS = 2
A_GROUP = A_HEADS // A_KV_HEADS
A_HEAD_DIM = 64
IDX_HEADS = 4
IDX_DIM = 64
TOPK_MAX = 256
B_HEADS = 4
B_KEY_DIM = 128
B_VAL_DIM = 128
CONV_W = 4
CHUNK = 64
C_HEADS = 8
C_HEAD_DIM = 64
NUM_BUCKETS = 32
MAX_DISTANCE = 128
N_MOD = 9
EPS = 1e-6

A_Q = A_HEADS * A_HEAD_DIM
A_KV = A_KV_HEADS * A_HEAD_DIM
B_K = B_HEADS * B_KEY_DIM
B_V = B_HEADS * B_VAL_DIM
B_CONV_DIM = 2 * B_K + B_V
C_QK = C_HEADS * 2 * C_HEAD_DIM
C_V = C_HEADS * 2 * C_HEAD_DIM
C_HD = 2 * C_HEAD_DIM

LANES = 128
SUBLANES = 8
VMEM_LIMIT = 56 * 1024 * 1024

BLK = 128
TQ = 256
NEG = -1e30

SM_KI = 0
SM_WI = IDX_DIM
SM_A = SM_WI + IDX_HEADS
SM_B = SM_A + B_HEADS

_NT = (((1,), (1,)), ((), ()))
_NN = (((1,), (0,)), ((), ()))
_TN = (((0,), (0,)), ((), ()))


def _cparams(sem):
    return pltpu.CompilerParams(dimension_semantics=sem, vmem_limit_bytes=VMEM_LIMIT)


def _dg(a, b, dims=_NN):
    return lax.dot_general(a, b, dims, preferred_element_type=F32)


def _dot1(a, b, dims=_NN):
    return _dg(a.astype(BF16), b.astype(BF16), dims)


def _split2(x):
    hi = x.astype(BF16)
    lo = (x - hi.astype(F32)).astype(BF16)
    return hi, lo


def _split3(x):
    b1 = x.astype(BF16)
    r1 = x - b1.astype(F32)
    b2 = r1.astype(BF16)
    b3 = (r1 - b2.astype(F32)).astype(BF16)
    return b1, b2, b3


def _dot3(a, b, dims=_NN):
    ah, al = _split2(a)
    bh, bl = _split2(b)
    return _dg(ah, bh, dims) + (_dg(ah, bl, dims) + _dg(al, bh, dims))


def _dot_exact_lhs(a_bf, b, dims=_NN):
    b1, b2, b3 = _split3(b)
    return _dg(a_bf, b1, dims) + (_dg(a_bf, b2, dims) + _dg(a_bf, b3, dims))


def _dot_exact_rhs(a, b_bf, dims=_NN):
    a1, a2, a3 = _split3(a)
    return _dg(a1, b_bf, dims) + (_dg(a2, b_bf, dims) + _dg(a3, b_bf, dims))


def _silu(x):
    return x * jax.nn.sigmoid(x)


def _softplus(x):
    return jnp.maximum(x, 0.0) + jnp.log(1.0 + jnp.exp(-jnp.abs(x)))


def _norm_mod(x, nw, sc, sh):
    ms = jnp.mean(x * x, axis=-1, keepdims=True)
    return (x * lax.rsqrt(ms + EPS) * nw) * (1.0 + sc) + sh


def _softmax_step(s, m, l, acc, pv):
    m_new = jnp.maximum(m, jnp.max(s, axis=1, keepdims=True))
    alpha = jnp.exp(m - m_new)
    p = jnp.exp(s - m_new)
    return m_new, alpha * l + jnp.sum(p, axis=1, keepdims=True), alpha * acc + pv(p)


def _mod_kernel(c_ref, w_ref, b_ref, o_ref):
    s = _silu(c_ref[...]).astype(BF16)
    o_ref[0] = _dg(s, w_ref[0].astype(BF16)) + b_ref[0]


def _modulation(c_all, ada_w, ada_b):
    depth, d, n = ada_w.shape
    m = c_all.shape[0]
    tn = 1024
    return pl.pallas_call(
        _mod_kernel,
        grid=(depth, n // tn),
        in_specs=[pl.BlockSpec((m, d), lambda i, j: (0, 0)),
                  pl.BlockSpec((1, d, tn), lambda i, j: (i, 0, j)),
                  pl.BlockSpec((1, 1, tn), lambda i, j: (i, 0, j))],
        out_specs=pl.BlockSpec((1, m, tn), lambda i, j: (i, 0, j)),
        out_shape=jax.ShapeDtypeStruct((depth, m, n), F32),
        compiler_params=_cparams(("arbitrary", "arbitrary")),
        name="adaln_mod",
    )(c_all, ada_w, ada_b.reshape(depth, 1, n))


def _row_tile(t):
    return min(512, t)


def _mod_spec(mod, tm, m):
    r = mod.shape[1]
    d = mod.shape[2] // N_MOD
    if r == 1:
        return pl.BlockSpec((1, 1, d), lambda s, t: (s, 0, m))
    return pl.BlockSpec((1, tm, d), lambda s, t: (s, t, m))


def _resident(shape):
    nd = len(shape)
    return pl.BlockSpec(shape, lambda s, t: (0,) * nd, pipeline_mode=pl.Buffered(1))


def _ffn_kernel(x_ref, sh_ref, sc_ref, g_ref, nw_ref, w1_ref, w2_ref, o_ref, acc_ref, *, fc):
    x = x_ref[0]
    hb = _norm_mod(x, nw_ref[...], sc_ref[0], sh_ref[0]).astype(BF16)
    f = w2_ref.shape[2]
    for c in range(f // fc):
        gt = _dg(hb, w1_ref[0, 0, :, c * fc:(c + 1) * fc])
        up = _dg(hb, w1_ref[0, 0, :, f + c * fc:f + (c + 1) * fc])
        a = (_silu(gt) * up).astype(BF16)
        contrib = _dg(a, w2_ref[0, 0, c * fc:(c + 1) * fc, :])
        if c == 0:
            acc_ref[...] = contrib
        else:
            acc_ref[...] += contrib
    o_ref[0] = x + (0.5 * g_ref[0]) * acc_ref[...]


def _ffn(x, mod, m0, nw, w1b, w2b, layer, which):
    s, t, d = x.shape
    tm = _row_tile(t)
    f = w2b.shape[2]
    wspec = lambda shp: pl.BlockSpec((1, 1) + tuple(shp[2:]), lambda si, ti: (layer, which, 0, 0),
                                     pipeline_mode=pl.Buffered(1))
    fc = 256 if f % 256 == 0 else LANES
    xs = pl.BlockSpec((1, tm, d), lambda si, ti: (si, ti, 0))
    return pl.pallas_call(
        functools.partial(_ffn_kernel, fc=fc),
        grid=(s, t // tm),
        in_specs=[xs, _mod_spec(mod, tm, m0), _mod_spec(mod, tm, m0 + 1), _mod_spec(mod, tm, m0 + 2),
                  _resident((1, d)), wspec(w1b.shape), wspec(w2b.shape)],
        out_specs=xs,
        out_shape=jax.ShapeDtypeStruct(x.shape, F32),
        scratch_shapes=[pltpu.VMEM((tm, d), F32)],
        compiler_params=_cparams(("arbitrary", "arbitrary")),
        name="ffn",
    )(x, mod, mod, mod, nw.reshape(1, d), w1b, w2b)


def _inproj_kernel(x_ref, sh_ref, sc_ref, nw_ref, w_ref, *o_refs, outs):
    hb = _norm_mod(x_ref[0], nw_ref[...], sc_ref[0], sh_ref[0]).astype(BF16)
    done = {}
    for o_ref, (off, wd, dt) in zip(o_refs, outs):
        if (off, wd) not in done:
            done[(off, wd)] = _dg(hb, w_ref[:, off:off + wd])
        o_ref[0] = done[(off, wd)].astype(dt)


def _inproj(x, mod, m0, nw, wb, outs):
    s, t, d = x.shape
    tm = _row_tile(t)
    xs = pl.BlockSpec((1, tm, d), lambda si, ti: (si, ti, 0))
    return pl.pallas_call(
        functools.partial(_inproj_kernel, outs=tuple(outs)),
        grid=(s, t // tm),
        in_specs=[xs, _mod_spec(mod, tm, m0), _mod_spec(mod, tm, m0 + 1), _resident((1, d)), _resident(wb.shape)],
        out_specs=[pl.BlockSpec((1, tm, wd), lambda si, ti: (si, ti, 0)) for _, wd, _ in outs],
        out_shape=[jax.ShapeDtypeStruct((s, t, wd), dt) for _, wd, dt in outs],
        compiler_params=_cparams(("arbitrary", "arbitrary")),
        name="inproj",
    )(x, mod, mod, nw.reshape(1, d), wb)


def _outproj_kernel(*refs, widths):
    n = len(widths)
    x_ref, g_ref, w_ref = refs[0], refs[1], refs[2]
    a_refs = refs[3:3 + n]
    o_ref = refs[3 + n]
    acc = None
    off = 0
    for a_ref, wd in zip(a_refs, widths):
        part = _dg(a_ref[0].astype(BF16), w_ref[off:off + wd, :])
        acc = part if acc is None else acc + part
        off += wd
    o_ref[0] = x_ref[0] + g_ref[0] * acc


def _outproj(x, mod, mg, wb, parts):
    s, t, d = x.shape
    tm = _row_tile(t)
    widths = tuple(p.shape[-1] for p in parts)
    xs = pl.BlockSpec((1, tm, d), lambda si, ti: (si, ti, 0))
    return pl.pallas_call(
        functools.partial(_outproj_kernel, widths=widths),
        grid=(s, t // tm),
        in_specs=[xs, _mod_spec(mod, tm, mg), _resident(wb.shape)]
                 + [pl.BlockSpec((1, tm, wd), lambda si, ti: (si, ti, 0)) for wd in widths],
        out_specs=xs,
        out_shape=jax.ShapeDtypeStruct(x.shape, F32),
        compiler_params=_cparams(("arbitrary", "arbitrary")),
        name="outproj",
    )(x, mod, wb, *parts)


def _final_norm_kernel(x_ref, w_ref, o_ref):
    x = x_ref[0]
    ms = jnp.mean(x * x, axis=-1, keepdims=True)
    o_ref[0] = x * lax.rsqrt(ms + EPS) * w_ref[...]


def _final_norm(x, w):
    s, t, d = x.shape
    tm = _row_tile(t)
    xs = pl.BlockSpec((1, tm, d), lambda si, ti: (si, ti, 0))
    return pl.pallas_call(
        _final_norm_kernel,
        grid=(s, t // tm),
        in_specs=[xs, pl.BlockSpec((1, d), lambda si, ti: (0, 0))],
        out_specs=xs,
        out_shape=jax.ShapeDtypeStruct(x.shape, F32),
        compiler_params=_cparams(("arbitrary", "arbitrary")),
        name="final_norm",
    )(x, w.reshape(1, d))


def _t5_bucket(dist):
    n = jnp.maximum(dist, 0)
    max_exact = NUM_BUCKETS // 2
    nf = jnp.maximum(n, 1).astype(F32)
    large = max_exact + (jnp.log(nf / max_exact) / math.log(MAX_DISTANCE / max_exact)
                         * (NUM_BUCKETS - max_exact)).astype(I32)
    large = jnp.minimum(large, NUM_BUCKETS - 1)
    return jnp.where(n < max_exact, n, large)


def _bias_tiles(bias_cols, blk, causal):
    assert blk >= MAX_DISTANCE
    n = blk
    h = bias_cols.shape[1]
    tiles = []
    for t in range(3):
        off = (2 - t) * n
        if t == 0:
            far = bias_cols[_t5_bucket(jnp.full((), 2 * n, I32))]
            tiles.append(jnp.broadcast_to(far[:, None, None], (h, n, n)))
            continue
        dist = n - 1 + off - jnp.arange(2 * n, dtype=I32)
        g = bias_cols[_t5_bucket(dist)]
        if causal:
            g = jnp.where((dist >= 0)[:, None], g, NEG)
        x = jnp.broadcast_to(g.T[:, None, :], (h, n, 2 * n))
        x = jnp.pad(x, ((0, 0), (0, 0), (0, 1))).reshape(h, n * (2 * n + 1))[:, :2 * n * n].reshape(h, n, 2 * n)
        tiles.append(x[:, :, n - 1:2 * n - 1])
    return jnp.stack(tiles, axis=1)


def _bias_decode(bias_cols, q_pos, n_blocks):
    k_pos = jnp.arange(n_blocks * BLK, dtype=I32)
    b = bias_cols[_t5_bucket(q_pos - k_pos)]
    b = jnp.where((k_pos <= q_pos)[:, None], b, NEG)
    return jnp.moveaxis(b.reshape(n_blocks, BLK, -1), 2, 1)


def _score_keys(s):
    bits = pltpu.bitcast(s, I32)
    return jnp.where(bits < 0, bits ^ jnp.int32(0x7FFFFFFF), bits)


def _select_topk(key_ref, nkb, topk, idx_bits, key_axis=1, pairs=False):
    tile = tuple(key_ref.shape[1:])
    assert tile[key_axis] == BLK
    stat = tuple(1 if a == key_axis else n for a, n in enumerate(tile))
    rows = stat
    col = lax.broadcasted_iota(I32, tile, key_axis)

    def count(pred):
        if pairs:
            def body(i, acc):
                kb = 2 * i
                return (acc + jnp.where(pred(key_ref[kb], kb * BLK + col), 1.0, 0.0)
                        + jnp.where(pred(key_ref[kb + 1], (kb + 1) * BLK + col), 1.0, 0.0))
            acc = lax.fori_loop(0, (nkb + 1) // 2, body, jnp.zeros(tile, F32))
        else:
            def body(kb, acc):
                return acc + jnp.where(pred(key_ref[kb], kb * BLK + col), 1.0, 0.0)
            acc = lax.fori_loop(0, nkb, body, jnp.zeros(tile, F32))
        return jnp.sum(acc, axis=key_axis, keepdims=True)

    kf = float(topk)

    def bit_body(i, carry):
        lo, cnt_lo = carry
        cand = lo + lax.shift_left(jnp.int32(1), jnp.int32(31) - i)
        cnt = count(lambda k, _: k >= cand)
        take = cnt >= kf
        return jnp.where(take, cand, lo), jnp.where(take, cnt, cnt_lo)

    n_visited = 2 * ((nkb + 1) // 2) if pairs else nkb
    total = jnp.zeros(rows, F32) + jnp.asarray(n_visited * BLK, F32)
    thr, cnt_ge = lax.fori_loop(0, 32, bit_body, (jnp.full(rows, -2 ** 31, I32), total))
    need = kf - count(lambda k, _: k > thr)

    has_ties = jnp.max(cnt_ge) > kf

    def idx_body(i, p):
        cand = p + lax.shift_left(jnp.int32(1), jnp.int32(idx_bits - 1) - i)
        cnt = count(lambda k, ix: jnp.logical_and(k == thr, ix < cand))
        return jnp.where(cnt < need, cand, p)

    cut0 = jnp.zeros(rows, I32) + jnp.where(has_ties, 0, 2 ** idx_bits - 1)
    cut = lax.fori_loop(0, jnp.where(has_ties, idx_bits, 0), idx_body, cut0)
    return thr, cut


def _in_topk(key, idx, thr, cut):
    return jnp.logical_or(key > thr, jnp.logical_and(key == thr, idx <= cut))


def _dsa_kernel(q_ref, qi_ref, sm_ref, kf_ref, vf_ref, smf_ref, bias_ref, o_ref, key_ref, msk_ref,
                *, topk, idx_bits):
    qb = pl.program_id(1)
    nkb = qb + 1
    key_i = lax.broadcasted_iota(I32, (BLK, BLK), 0)
    q_pos = qb * BLK + lax.broadcasted_iota(I32, (BLK, BLK), 1)

    qi = qi_ref[0]
    sm_t = sm_ref[0].T
    qi_st = jnp.concatenate([qi[:, h * IDX_DIM:(h + 1) * IDX_DIM] for h in range(IDX_HEADS)], axis=0).astype(BF16)
    wi_rows = [sm_t[SM_WI + h:SM_WI + h + 1, :] * (IDX_HEADS ** -0.5) for h in range(IDX_HEADS)]

    def score_body(i, carry):
        kbs = (2 * i, 2 * i + 1)
        kis = [smf_ref[0, pl.ds(pl.multiple_of(kb * BLK, BLK), BLK), :][:, SM_KI:SM_KI + IDX_DIM].astype(BF16)
               for kb in kbs]
        rs = [jnp.maximum(_dg(ki, qi_st, _NT) * (IDX_DIM ** -0.5), 0.0) for ki in kis]
        for kb, r in zip(kbs, rs):
            s = wi_rows[0] * r[:, 0:BLK]
            for h in range(1, IDX_HEADS):
                s = s + wi_rows[h] * r[:, h * BLK:(h + 1) * BLK]
            s = jnp.where(kb * BLK + key_i <= q_pos, s, -jnp.inf)
            key_ref[kb] = _score_keys(s)
        return carry

    lax.fori_loop(0, (nkb + 1) // 2, score_body, 0)

    @pl.when(nkb * BLK <= topk)
    def _():
        def body(kb, carry):
            msk_ref[kb] = jnp.where(kb * BLK + key_i <= q_pos, 0.0, NEG)
            return carry
        lax.fori_loop(0, nkb, body, 0)

    @pl.when(nkb * BLK > topk)
    def _():
        thr, cut = _select_topk(key_ref, nkb, topk, idx_bits, key_axis=0, pairs=True)

        def body(kb, carry):
            k_pos = kb * BLK + key_i
            sel = _in_topk(key_ref[kb], k_pos, thr, cut)
            msk_ref[kb] = jnp.where(k_pos <= q_pos, jnp.where(sel, 0.0, NEG), NEG)
            return carry
        lax.fori_loop(0, nkb, body, 0)

    q = q_ref[0] * (A_HEAD_DIM ** -0.5)
    gq = A_GROUP * BLK
    ns = range(A_KV_HEADS)
    qs = [jnp.concatenate([q[:, (n * A_GROUP + g) * A_HEAD_DIM:(n * A_GROUP + g + 1) * A_HEAD_DIM]
                           for g in range(A_GROUP)], axis=0).astype(BF16) for n in ns]

    def logits(kb):
        off = pl.multiple_of(kb * BLK, BLK)
        kblk = kf_ref[0, pl.ds(off, BLK), :].astype(BF16)
        t = jnp.clip(kb - qb + 2, 0, 2)
        mk4 = jnp.concatenate([msk_ref[kb]] * A_GROUP, axis=1)
        return [_dg(kblk[:, n * A_HEAD_DIM:(n + 1) * A_HEAD_DIM], qs[n], _NT) + bias_ref[n, t] + mk4 for n in ns]

    def att_body(kb, carry):
        ss = carry[3 * A_KV_HEADS:]
        ss_next = logits(jnp.minimum(kb + 1, nkb - 1))
        off = pl.multiple_of(kb * BLK, BLK)
        vblk = vf_ref[0, pl.ds(off, BLK), :].astype(BF16)
        m_new = [jnp.maximum(carry[3 * n], jnp.max(ss[n], axis=0, keepdims=True)) for n in ns]
        ps = [jnp.exp(ss[n] - m_new[n]) for n in ns]
        pvs = [_dg(vblk[:, n * A_HEAD_DIM:(n + 1) * A_HEAD_DIM], ps[n].astype(BF16), _TN) for n in ns]
        out = []
        for n in ns:
            alpha = jnp.exp(carry[3 * n] - m_new[n])
            out += [m_new[n], alpha * carry[3 * n + 1] + jnp.sum(ps[n], axis=0, keepdims=True),
                    alpha * carry[3 * n + 2] + pvs[n]]
        return tuple(out) + tuple(ss_next)

    init = (jnp.full((1, gq), NEG, F32), jnp.zeros((1, gq), F32), jnp.zeros((A_HEAD_DIM, gq), F32)) * A_KV_HEADS
    res = lax.fori_loop(0, nkb, att_body, init + tuple(logits(0)))
    for n in ns:
        o_t = res[3 * n + 2] / res[3 * n + 1]
        for g in range(0, A_GROUP, 2):
            h = n * A_GROUP + g
            pair = jnp.concatenate([o_t[:, g * BLK:(g + 1) * BLK], o_t[:, (g + 1) * BLK:(g + 2) * BLK]], axis=0)
            o_ref[0, :, h * A_HEAD_DIM:(h + 2) * A_HEAD_DIM] = pair.T


def _dsa_prompt(q, qi, sm, k, v, bias_tiles, topk):
    b, t, _ = q.shape
    nb = t // BLK
    assert nb % 2 == 0
    idx_bits = max(1, int(math.ceil(math.log2(t))))
    bt = bias_tiles.reshape(A_KV_HEADS, A_GROUP, 3, BLK, BLK)
    bt = jnp.transpose(bt, (0, 2, 4, 1, 3)).reshape(A_KV_HEADS, 3, BLK, A_GROUP * BLK)
    blk = lambda w: pl.BlockSpec((1, BLK, w), lambda bi, qb: (bi, qb, 0))
    full = lambda w: pl.BlockSpec((1, t, w), lambda bi, qb: (bi, 0, 0))
    return pl.pallas_call(
        functools.partial(_dsa_kernel, topk=topk, idx_bits=idx_bits),
        grid=(b, nb),
        in_specs=[blk(A_Q), blk(IDX_HEADS * IDX_DIM), blk(LANES), full(A_KV), full(A_KV), full(LANES),
                  pl.BlockSpec(bt.shape, lambda bi, qb: (0, 0, 0, 0))],
        out_specs=blk(A_Q),
        out_shape=jax.ShapeDtypeStruct((b, t, A_Q), F32),
        scratch_shapes=[pltpu.VMEM((nb, BLK, BLK), I32), pltpu.VMEM((nb, BLK, BLK), F32)],
        compiler_params=_cparams(("arbitrary", "arbitrary")),
        name="dsa_prompt",
    )(q, qi, sm, k, v, sm, bt)


def _gdn_kernel(qkv_ref, z_ref, sm_ref, s0_ref, cw_ref, par_ref, nw_ref, o_ref, sout_ref, ext_ref, st_ref):
    t = pl.program_id(1)
    nt = pl.num_programs(1)
    c = CHUNK
    halo = SUBLANES

    nbb = qkv_ref.shape[0]
    bbs = range(nbb)

    @pl.when(t == 0)
    def _():
        for bb in bbs:
            ext_ref[bb, 0:halo, :] = jnp.zeros((halo, B_CONV_DIM), F32)
        st_ref[...] = s0_ref[...]

    convs = []
    for bb in bbs:
        x = qkv_ref[bb]
        ext_ref[bb, halo:halo + c, :] = x
        conv = None
        for j in range(CONV_W):
            start = halo - (CONV_W - 1) + j
            term = ext_ref[bb, start:start + c, :] * cw_ref[j:j + 1, :]
            conv = term if conv is None else conv + term
        ext_ref[bb, 0:halo, :] = x[c - halo:c, :]
        convs.append(_silu(conv))

    ri = lax.broadcasted_iota(I32, (c, c), 0)
    ci = lax.broadcasted_iota(I32, (c, c), 1)
    tri = ri >= ci
    stri = ri > ci
    eye = jnp.where(ri == ci, 1.0, 0.0)
    tri_bf = jnp.where(tri, 1.0, 0.0).astype(BF16)
    ones_bf = jnp.ones((c, c), BF16)

    sms = [sm_ref[bb] for bb in bbs]
    g_alls = [-jnp.exp(par_ref[0:1, :]) * _softplus(sm + par_ref[1:2, :]) for sm in sms]
    gc_alls = [_dot_exact_lhs(tri_bf, g) for g in g_alls]
    beta_alls = [jax.nn.sigmoid(sm) for sm in sms]
    nw = nw_ref[...]
    items = [(bb, h) for bb in bbs for h in range(B_HEADS)]
    hs = range(len(items))
    gcs = [gc_alls[bb][:, SM_A + h:SM_A + h + 1] for bb, h in items]
    gc_rows = [_dot_exact_lhs(ones_bf, jnp.concatenate([eye * gcs[bb * B_HEADS + h] for h in range(B_HEADS)],
                                                       axis=1)) for bb in bbs]
    qs = [convs[bb][:, h * B_KEY_DIM:(h + 1) * B_KEY_DIM] for bb, h in items]
    ks = [convs[bb][:, B_K + h * B_KEY_DIM:B_K + (h + 1) * B_KEY_DIM] for bb, h in items]
    vs = [convs[bb][:, 2 * B_K + h * B_VAL_DIM:2 * B_K + (h + 1) * B_VAL_DIM] for bb, h in items]
    qs = [q * lax.rsqrt(jnp.sum(q * q, axis=-1, keepdims=True) + EPS) * (B_KEY_DIM ** -0.5) for q in qs]
    ks = [k * lax.rsqrt(jnp.sum(k * k, axis=-1, keepdims=True) + EPS) for k in ks]
    betas = [beta_alls[bb][:, SM_B + h:SM_B + h + 1] for bb, h in items]
    decays = [jnp.exp(jnp.where(tri, gcs[i] - gc_rows[bb][:, h * c:(h + 1) * c], -jnp.inf))
              for i, (bb, h) in enumerate(items)]
    kbs = [ks[h] * betas[h] for h in hs]
    vbs = [vs[h] * betas[h] for h in hs]
    k_bf = [k.astype(BF16) for k in ks]
    a_mats = [jnp.where(stri, _dg(kbs[h].astype(BF16), k_bf[h], _NT) * decays[h], 0.0) for h in hs]
    tms = [eye - a for a in a_mats]
    pws = [_dot3(a, a) for a in a_mats]
    steps = int(math.log2(c))
    for j in range(1, steps):
        tms = [tm + _dot3(tm, pw) for tm, pw in zip(tms, pws)]
        if j < steps - 1:
            pws = [_dot3(pw, pw) for pw in pws]
    egs = [jnp.exp(gc) for gc in gcs]
    tm_bf = [tm.astype(BF16) for tm in tms]
    us = [_dg(tm_bf[h], vbs[h].astype(BF16)) for h in hs]
    ws = [_dg(tm_bf[h], (kbs[h] * egs[h]).astype(BF16)) for h in hs]
    a_qks = [jnp.where(tri, _dg(qs[h].astype(BF16), k_bf[h], _NT) * decays[h], 0.0) for h in hs]

    s_olds = [st_ref[bb, h] for bb, h in items]
    s_bf = [s.astype(BF16) for s in s_olds]
    v_news = [us[i] - _dg(ws[i].astype(BF16), s_bf[i]) for i in hs]
    os_ = [_dg((qs[i] * egs[i]).astype(BF16), s_bf[i]) + _dot1(a_qks[i], v_news[i]) for i in hs]
    g_lasts = [gc[c - 1:c, :] for gc in gcs]
    for i, (bb, h) in enumerate(items):
        st_ref[bb, h] = (s_olds[i] * jnp.exp(g_lasts[i])
                         + _dot1(ks[i] * jnp.exp(g_lasts[i] - gcs[i]), v_news[i], _TN))
    for i, (bb, h) in enumerate(items):
        o = os_[i]
        ms = jnp.mean(o * o, axis=-1, keepdims=True)
        zh = z_ref[bb, :, h * B_VAL_DIM:(h + 1) * B_VAL_DIM]
        o_ref[bb, :, h * B_VAL_DIM:(h + 1) * B_VAL_DIM] = (o * lax.rsqrt(ms + EPS) * nw) * _silu(zh)

    @pl.when(t == nt - 1)
    def _():
        sout_ref[...] = st_ref[...]


GDN_SEQS_PER_STEP = 2


def _gdn_params(a_log, dt_bias):
    par = jnp.zeros((SUBLANES, LANES), F32)
    return par.at[0, SM_A:SM_A + B_HEADS].set(a_log).at[1, SM_A:SM_A + B_HEADS].set(dt_bias)


def _gdn_prompt(qkv, z, sm, s0, conv_w, a_log, dt_bias, norm_w):
    b, t, _ = qkv.shape
    nt = t // CHUNK
    nbb = GDN_SEQS_PER_STEP if b % GDN_SEQS_PER_STEP == 0 else 1
    tok = lambda w: pl.BlockSpec((nbb, CHUNK, w), lambda bi, ti: (bi, ti, 0))
    const2 = lambda shp: pl.BlockSpec(shp, lambda bi, ti: (0, 0))
    st_spec = pl.BlockSpec((nbb, B_HEADS, B_KEY_DIM, B_VAL_DIM), lambda bi, ti: (bi, 0, 0, 0))
    o, s_out = pl.pallas_call(
        _gdn_kernel,
        grid=(b // nbb, nt),
        in_specs=[tok(B_CONV_DIM), tok(B_V), tok(LANES), st_spec,
                  const2((CONV_W, B_CONV_DIM)), const2((SUBLANES, LANES)), const2((1, B_VAL_DIM))],
        out_specs=[tok(B_V), st_spec],
        out_shape=[jax.ShapeDtypeStruct((b, t, B_V), F32),
                   jax.ShapeDtypeStruct((b, B_HEADS, B_KEY_DIM, B_VAL_DIM), F32)],
        scratch_shapes=[pltpu.VMEM((nbb, SUBLANES + CHUNK, B_CONV_DIM), F32),
                        pltpu.VMEM((nbb, B_HEADS, B_KEY_DIM, B_VAL_DIM), F32)],
        compiler_params=_cparams(("arbitrary", "arbitrary")),
        name="gdn_prompt",
    )(qkv, z, sm, s0, conv_w, _gdn_params(a_log, dt_bias), norm_w.reshape(1, B_VAL_DIM))
    return o, s_out


GDN_DEC_ROWS = 16


def _gdn_dec_kernel(qkv_ref, z_ref, sm_ref, cb_ref, s_ref, cw_ref, par_ref, nw_ref, o_ref, so_ref, oraw_ref):
    r = GDN_DEC_ROWS
    conv = qkv_ref[...] * cw_ref[CONV_W - 1:CONV_W, :]
    for j in range(CONV_W - 1):
        conv = conv + cb_ref[0, j] * cw_ref[j:j + 1, :]
    conv = _silu(conv)
    sm = sm_ref[...]
    g_all = -jnp.exp(par_ref[0:1, :]) * _softplus(sm + par_ref[1:2, :])
    beta_all = jax.nn.sigmoid(sm)
    ri = lax.broadcasted_iota(I32, (r, LANES), 0)
    ci = lax.broadcasted_iota(I32, (r, LANES), 1)
    eye_bf = jnp.where(ri == ci, 1.0, 0.0).astype(BF16)

    for h in range(B_HEADS):
        qh = conv[:, h * B_KEY_DIM:(h + 1) * B_KEY_DIM]
        kh = conv[:, B_K + h * B_KEY_DIM:B_K + (h + 1) * B_KEY_DIM]
        vh = conv[:, 2 * B_K + h * B_VAL_DIM:2 * B_K + (h + 1) * B_VAL_DIM]
        qh = qh * lax.rsqrt(jnp.sum(qh * qh, axis=-1, keepdims=True) + EPS) * (B_KEY_DIM ** -0.5)
        kh = kh * lax.rsqrt(jnp.sum(kh * kh, axis=-1, keepdims=True) + EPS)
        eg = jnp.exp(g_all[:, SM_A + h:SM_A + h + 1])
        beta = beta_all[:, SM_B + h:SM_B + h + 1]
        qk = jnp.sum(qh * kh, axis=-1, keepdims=True)
        k_t = _dot_exact_rhs(kh, eye_bf, _TN)
        q_t = _dot_exact_rhs(qh, eye_bf, _TN)
        for s in range(r):
            st = s_ref[0, s, h]
            kc = k_t[:, s:s + 1]
            qc = q_t[:, s:s + 1]
            k_s = jnp.sum(st * kc, axis=0, keepdims=True)
            q_s = jnp.sum(st * qc, axis=0, keepdims=True)
            eg_s = eg[s:s + 1, :]
            v_new = beta[s:s + 1, :] * (vh[s:s + 1, :] - eg_s * k_s)
            oraw_ref[s:s + 1, h * B_VAL_DIM:(h + 1) * B_VAL_DIM] = eg_s * q_s + qk[s:s + 1, :] * v_new
            so_ref[s, h] = st * eg_s + kc * v_new

    zz = z_ref[...]
    nw = nw_ref[...]
    for h in range(B_HEADS):
        o = oraw_ref[:, h * B_VAL_DIM:(h + 1) * B_VAL_DIM]
        ms = jnp.mean(o * o, axis=-1, keepdims=True)
        zh = zz[:, h * B_VAL_DIM:(h + 1) * B_VAL_DIM]
        o_ref[:, h * B_VAL_DIM:(h + 1) * B_VAL_DIM] = (o * lax.rsqrt(ms + EPS) * nw) * _silu(zh)


def _gdn_decode(qkv, z, sm, conv_t, state, j, conv_w, a_log, dt_bias, norm_w):
    b = qkv.shape[0]
    r = GDN_DEC_ROWS
    assert b % r == 0
    rows = lambda w: pl.BlockSpec((r, w), lambda i: (i, 0))
    const2 = lambda shp: pl.BlockSpec(shp, lambda i: (0, 0))
    o, s_out = pl.pallas_call(
        _gdn_dec_kernel,
        grid=(b // r,),
        in_specs=[rows(B_CONV_DIM), rows(B_V), rows(LANES),
                  pl.BlockSpec((1, CONV_W - 1, r, B_CONV_DIM), lambda i: (j, 0, i, 0)),
                  pl.BlockSpec((1, r, B_HEADS, B_KEY_DIM, B_VAL_DIM), lambda i: (j, i, 0, 0, 0)),
                  const2((CONV_W, B_CONV_DIM)), const2((SUBLANES, LANES)), const2((1, B_VAL_DIM))],
        out_specs=[rows(B_V), pl.BlockSpec((r, B_HEADS, B_KEY_DIM, B_VAL_DIM), lambda i: (i, 0, 0, 0))],
        out_shape=[jax.ShapeDtypeStruct((b, B_V), F32),
                   jax.ShapeDtypeStruct((b, B_HEADS, B_KEY_DIM, B_VAL_DIM), F32)],
        scratch_shapes=[pltpu.VMEM((r, B_V), F32)],
        compiler_params=_cparams(("arbitrary",)),
        name="gdn_decode",
    )(qkv, z, sm, conv_t, state, conv_w, _gdn_params(a_log, dt_bias), norm_w.reshape(1, B_VAL_DIM))
    return o, s_out


DIFF_HEADS_PER_STEP = 8


def _diff_kernel(lam_ref, q_ref, k_ref, v_ref, bias_ref, sw_ref, o_ref, *, lam_init):
    qb = pl.program_id(2)
    nkb = qb + 1
    nh = DIFF_HEADS_PER_STEP
    lane = lax.broadcasted_iota(I32, (TQ, C_HD), 1)
    q2 = []
    for h in range(nh):
        qh = q_ref[0, :, h * C_HD:(h + 1) * C_HD] * (C_HEAD_DIM ** -0.5)
        zero = jnp.zeros_like(qh)
        q2.append(jnp.concatenate([jnp.where(lane < C_HEAD_DIM, qh, zero),
                                   jnp.where(lane >= C_HEAD_DIM, qh, zero)], axis=0))

    hs = range(nh)

    def body(kb, carry):
        off = pl.multiple_of(kb * TQ, TQ)
        t = jnp.clip(kb - qb + 2, 0, 2)
        ss = [_dg(k_ref[0, pl.ds(off, TQ), h * C_HD:(h + 1) * C_HD], q2[h], _NT) for h in hs]
        ss = [ss[h] + jnp.concatenate([bias_ref[h, t]] * 2, axis=1) for h in hs]
        m_new = [jnp.maximum(carry[3 * h], jnp.max(ss[h], axis=0, keepdims=True)) for h in hs]
        ps = [jnp.exp(ss[h] - m_new[h]) for h in hs]
        pvs = [_dg(v_ref[0, pl.ds(off, TQ), h * C_HD:(h + 1) * C_HD], ps[h].astype(BF16), _TN) for h in hs]
        out = []
        for h in hs:
            alpha = jnp.exp(carry[3 * h] - m_new[h])
            out += [m_new[h], alpha * carry[3 * h + 1] + jnp.sum(ps[h], axis=0, keepdims=True),
                    alpha * carry[3 * h + 2] + pvs[h]]
        return tuple(out)

    init = (jnp.full((1, 2 * TQ), NEG, F32), jnp.zeros((1, 2 * TQ), F32), jnp.zeros((C_HD, 2 * TQ), F32)) * nh
    res = lax.fori_loop(0, nkb, body, init)
    for h in range(nh):
        _, l, acc = res[3 * h:3 * h + 3]
        on = acc / l
        o = (on[:, 0:TQ] - lam_ref[0] * on[:, TQ:2 * TQ]).T
        ms = jnp.mean(o * o, axis=-1, keepdims=True)
        o_ref[0, :, h * C_HD:(h + 1) * C_HD] = (o * lax.rsqrt(ms + EPS) * sw_ref[...]) * (1.0 - lam_init)


def _diff_prompt(q, k, v, bias_tiles, lam, subln_w, lam_init):
    b, t, _ = q.shape
    nh = DIFF_HEADS_PER_STEP
    w = nh * C_HD
    return pl.pallas_call(
        functools.partial(_diff_kernel, lam_init=lam_init),
        grid=(b, C_HEADS // nh, t // TQ),
        in_specs=[pl.BlockSpec(memory_space=pltpu.SMEM),
                  pl.BlockSpec((1, TQ, w), lambda bi, h, qb: (bi, qb, h)),
                  pl.BlockSpec((1, t, w), lambda bi, h, qb: (bi, 0, h)),
                  pl.BlockSpec((1, t, w), lambda bi, h, qb: (bi, 0, h)),
                  pl.BlockSpec((nh, 3, TQ, TQ), lambda bi, h, qb: (h, 0, 0, 0)),
                  pl.BlockSpec((1, C_HD), lambda bi, h, qb: (0, 0))],
        out_specs=pl.BlockSpec((1, TQ, w), lambda bi, h, qb: (bi, qb, h)),
        out_shape=jax.ShapeDtypeStruct((b, t, C_V), F32),
        compiler_params=_cparams(("arbitrary", "arbitrary", "arbitrary")),
        name="diff_prompt",
    )(lam.reshape(1), q, k, v, bias_tiles, subln_w.reshape(1, C_HD))


DIFF_DEC_PAGES = 8


def _diff_dec_kernel(pt_ref, lam_ref, q_ref, kn_ref, vn_ref, *rest, n_steps, lam_init):
    pp = DIFF_DEC_PAGES
    kps, vps = rest[:pp], rest[pp:2 * pp]
    bias_ref, bnew_ref, sw_ref, o_ref, m_ref, l_ref, acc_ref = rest[2 * pp:]
    p = pl.program_id(1)
    nr = 2 * C_HEADS

    @pl.when(p == 0)
    def _():
        m_ref[...] = jnp.full(m_ref.shape, NEG, F32)
        l_ref[...] = jnp.zeros(l_ref.shape, F32)
        acc_ref[...] = jnp.zeros(acc_ref.shape, F32)

    q8 = q_ref[0] * (C_HEAD_DIM ** -0.5)
    lane = lax.broadcasted_iota(I32, (C_HEADS, C_HD), 1)
    q_lo = jnp.where(lane < C_HEAD_DIM, q8, 0.0)
    q_hi = jnp.where(lane >= C_HEAD_DIM, q8, 0.0)
    qm = jnp.concatenate([q_lo, q_hi], axis=0).astype(BF16)

    rows = BLK * C_HEADS
    half = pp // 2
    cs = range(2)
    ss = [jnp.concatenate([_dg(qm, kps[c * half + i][0, 0].astype(BF16), _NT) + bias_ref[c * half + i]
                           for i in range(half)], axis=1) for c in cs]
    m_old = [m_ref[c] for c in cs]
    m_new = [jnp.maximum(m_old[c], jnp.max(ss[c], axis=1, keepdims=True)) for c in cs]
    ps = [jnp.exp(ss[c] - m_new[c]) for c in cs]
    pb = [pr.astype(BF16) for pr in ps]
    pvs = []
    for c in cs:
        out = None
        for i in range(half):
            part = _dg(pb[c][:, i * rows:(i + 1) * rows], vps[c * half + i][0, 0].astype(BF16))
            out = part if out is None else out + part
        pvs.append(out)
    for c in cs:
        alpha = jnp.exp(m_old[c] - m_new[c])
        l_ref[c] = alpha * l_ref[c] + jnp.sum(ps[c], axis=1, keepdims=True)
        acc_ref[c] = alpha * acc_ref[c] + pvs[c]
        m_ref[c] = m_new[c]

    @pl.when(p == n_steps - 1)
    def _():
        kn = kn_ref[0]
        s_new = jnp.concatenate([jnp.sum(q_lo * kn, axis=1, keepdims=True),
                                 jnp.sum(q_hi * kn, axis=1, keepdims=True)], axis=0) + bnew_ref[:, 0:1]
        vn2 = jnp.concatenate([vn_ref[0], vn_ref[0]], axis=0)
        m = jnp.maximum(jnp.maximum(m_ref[0], m_ref[1]), s_new)
        w0 = jnp.exp(m_ref[0] - m)
        w1 = jnp.exp(m_ref[1] - m)
        p_new = jnp.exp(s_new - m)
        l = w0 * l_ref[0] + w1 * l_ref[1] + p_new
        acc = w0 * acc_ref[0] + w1 * acc_ref[1] + p_new * vn2
        on = acc / l
        o = on[0:C_HEADS] - lam_ref[0] * on[C_HEADS:nr]
        ms = jnp.mean(o * o, axis=-1, keepdims=True)
        o_ref[0] = (o * lax.rsqrt(ms + EPS) * sw_ref[...]) * (1.0 - lam_init)


def _diff_decode(q8, k8, v8, cache_k, cache_v, j, pt_flat, n_pages, bias_pages, bias_new, lam, subln_w, lam_init):
    b = q8.shape[0]
    pp = DIFF_DEC_PAGES
    assert n_pages % pp == 0
    n_steps = n_pages // pp
    rows = BLK * C_HEADS
    row = pl.BlockSpec((1, C_HEADS, C_HD), lambda bi, p, pt: (bi, 0, 0))
    page = lambda i: pl.BlockSpec((1, 1, rows, C_HD),
                                  lambda bi, p, pt: (j, pt[bi * n_pages + p * pp + i], 0, 0))
    grid_spec = pltpu.PrefetchScalarGridSpec(
        num_scalar_prefetch=1,
        grid=(b, n_steps),
        in_specs=[pl.BlockSpec(memory_space=pltpu.SMEM), row, row, row]
                 + [page(i) for i in range(pp)] + [page(i) for i in range(pp)]
                 + [pl.BlockSpec((pp, 2 * C_HEADS, rows), lambda bi, p, pt: (p, 0, 0)),
                    pl.BlockSpec((2 * C_HEADS, LANES), lambda bi, p, pt: (0, 0)),
                    pl.BlockSpec((1, C_HD), lambda bi, p, pt: (0, 0))],
        out_specs=row,
        scratch_shapes=[pltpu.VMEM((2, 2 * C_HEADS, 1), F32), pltpu.VMEM((2, 2 * C_HEADS, 1), F32),
                        pltpu.VMEM((2, 2 * C_HEADS, C_HD), F32)],
    )
    return pl.pallas_call(
        functools.partial(_diff_dec_kernel, n_steps=n_steps, lam_init=lam_init),
        grid_spec=grid_spec,
        out_shape=jax.ShapeDtypeStruct((b, C_HEADS, C_HD), F32),
        compiler_params=_cparams(("arbitrary", "arbitrary")),
        name="diff_decode",
    )(pt_flat, lam.reshape(1), q8, k8, v8, *([cache_k] * pp), *([cache_v] * pp),
      bias_pages, bias_new, subln_w.reshape(1, C_HD))


def _diff_decode_bias(bias_cols, q_pos, n_pages):
    bd = _bias_decode(bias_cols, q_pos, n_pages + 1)
    past = jnp.moveaxis(bd[:n_pages], 1, 2)
    same = jnp.eye(C_HEADS, dtype=bool)
    tab = jnp.where(same[None, :, None, :], past[:, None, :, :], NEG)
    tab = tab.reshape(n_pages, C_HEADS, BLK * C_HEADS)
    tab = jnp.concatenate([tab, tab], axis=1)
    new = bd[n_pages, :, 0]
    new = jnp.broadcast_to(jnp.concatenate([new, new])[:, None], (2 * C_HEADS, LANES))
    return tab, new


def _idx_dec_kernel(pt_ref, qi_ref, sm_ref, *rest, n_pages):
    kps = rest[:n_pages]
    o_ref = rest[n_pages]
    qi = qi_ref[0]
    sm = sm_ref[0]
    rowi = lax.broadcasted_iota(I32, (SUBLANES, IDX_HEADS * IDX_DIM), 0)
    lane = lax.broadcasted_iota(I32, (SUBLANES, IDX_HEADS * IDX_DIM), 1)
    qsel = jnp.where(lane // IDX_DIM == rowi, qi, 0.0)
    qt = qsel[:, 0:IDX_DIM]
    for h in range(1, IDX_HEADS):
        qt = qt + qsel[:, h * IDX_DIM:(h + 1) * IDX_DIM]
    r8 = lax.broadcasted_iota(I32, (SUBLANES, LANES), 0)
    l8 = lax.broadcasted_iota(I32, (SUBLANES, LANES), 1)
    wsel = jnp.where(jnp.logical_and(r8 < IDX_HEADS, l8 == r8 + SM_WI), sm, 0.0)
    wcol = jnp.sum(wsel, axis=1, keepdims=True) * (IDX_HEADS ** -0.5)
    qh, ql = _split2(qt)
    kh, kl = _split2(jnp.concatenate([kps[p][0, 0] for p in range(n_pages)], axis=1))
    d = _dg(qh, kh) + (_dg(qh, kl) + _dg(ql, kh))
    sc = jnp.sum(wcol * jnp.maximum(d * (IDX_DIM ** -0.5), 0.0), axis=0, keepdims=True)
    for p in range(n_pages):
        o_ref[0, p] = sc[:, p * BLK:(p + 1) * BLK]
    d_new = jnp.sum(qt * sm[:, SM_KI:SM_KI + IDX_DIM], axis=1, keepdims=True)
    s_new = jnp.sum(wcol * jnp.maximum(d_new * (IDX_DIM ** -0.5), 0.0), axis=0, keepdims=True)
    o_ref[0, n_pages] = jnp.broadcast_to(s_new, (1, BLK))


def _idx_decode(qi, sm, cache_kidx_t, j, pt_flat, n_pages):
    b = qi.shape[0]
    page = lambda p: pl.BlockSpec((1, 1, IDX_DIM, BLK), lambda bi, pt: (j, pt[bi * n_pages + p], 0, 0))
    grid_spec = pltpu.PrefetchScalarGridSpec(
        num_scalar_prefetch=1,
        grid=(b,),
        in_specs=[pl.BlockSpec((1, 1, IDX_HEADS * IDX_DIM), lambda bi, pt: (bi, 0, 0)),
                  pl.BlockSpec((1, 1, LANES), lambda bi, pt: (bi, 0, 0))]
                 + [page(p) for p in range(n_pages)],
        out_specs=pl.BlockSpec((1, n_pages + 1, 1, BLK), lambda bi, pt: (bi, 0, 0, 0)),
    )
    return pl.pallas_call(
        functools.partial(_idx_dec_kernel, n_pages=n_pages),
        grid_spec=grid_spec,
        out_shape=jax.ShapeDtypeStruct((b, n_pages + 1, 1, BLK), F32),
        compiler_params=_cparams(("arbitrary",)),
        name="idx_decode",
    )(pt_flat, qi, sm, *([cache_kidx_t] * n_pages))


def _sel_dec_kernel(s_ref, o_ref, key_ref, *, n_valid, topk, idx_bits):
    nkb, rows, _ = s_ref.shape
    col = lax.broadcasted_iota(I32, (rows, BLK), 1)
    for kb in range(nkb):
        s = jnp.where(kb * BLK + col < n_valid, s_ref[kb], -jnp.inf)
        key_ref[kb] = _score_keys(s)
    thr, cut = _select_topk(key_ref, nkb, topk, idx_bits)
    for kb in range(nkb):
        k_pos = kb * BLK + col
        sel = _in_topk(key_ref[kb], k_pos, thr, cut)
        o_ref[kb] = jnp.where(k_pos < n_valid, jnp.where(sel, 0.0, NEG), NEG)


def _sel_decode(scores, n_valid, topk):
    nkb, rows, _ = scores.shape
    idx_bits = max(1, int(math.ceil(math.log2(nkb * BLK))))
    return pl.pallas_call(
        functools.partial(_sel_dec_kernel, n_valid=n_valid, topk=topk, idx_bits=idx_bits),
        out_shape=jax.ShapeDtypeStruct(scores.shape, F32),
        scratch_shapes=[pltpu.VMEM(scores.shape, I32)],
        compiler_params=pltpu.CompilerParams(vmem_limit_bytes=VMEM_LIMIT),
        name="sel_decode",
    )(scores)


def _dsa_dec_kernel(pt_ref, q_ref, kn_ref, vn_ref, *rest, n_pages):
    kps, vps = rest[:n_pages], rest[n_pages:2 * n_pages]
    msk_ref, bias_ref, o_ref = rest[2 * n_pages:]

    q = q_ref[0] * (A_HEAD_DIM ** -0.5)
    rowi = lax.broadcasted_iota(I32, (A_HEADS, A_Q), 0)
    lane = lax.broadcasted_iota(I32, (A_HEADS, A_Q), 1)
    qsel = jnp.where(lane // A_HEAD_DIM == rowi, q, 0.0)
    halves = []
    for n in range(A_KV_HEADS):
        acc = None
        for g in range(A_GROUP):
            h = n * A_GROUP + g
            part = qsel[:, h * A_HEAD_DIM:(h + 1) * A_HEAD_DIM]
            acc = part if acc is None else acc + part
        halves.append(acc)
    qt = jnp.concatenate(halves, axis=1)
    qt_bf = qt.astype(BF16)

    kt = jnp.concatenate([kps[i][0, 0].astype(BF16) for i in range(n_pages)], axis=1)
    vt = jnp.concatenate([vps[i][0, 0].astype(BF16) for i in range(n_pages)], axis=1)
    extra = jnp.concatenate([bias_ref[i] + msk_ref[i, 0] for i in range(n_pages)], axis=1)
    s = _dg(qt_bf, kt) + extra
    s_new = (jnp.sum(qt * kn_ref[0], axis=1, keepdims=True) + bias_ref[n_pages][:, 0:1]
             + msk_ref[n_pages, 0][:, 0:1])
    m = jnp.maximum(jnp.max(s, axis=1, keepdims=True), s_new)
    p = jnp.exp(s - m)
    p_new = jnp.exp(s_new - m)
    l = jnp.sum(p, axis=1, keepdims=True) + p_new
    acc = p_new * vn_ref[0] + _dg(p.astype(BF16), vt, _NT)
    on = acc / l
    for h in range(A_HEADS):
        n = h // A_GROUP
        o_ref[0, :, h * A_HEAD_DIM:(h + 1) * A_HEAD_DIM] = on[h:h + 1, n * A_HEAD_DIM:(n + 1) * A_HEAD_DIM]


def _dsa_decode(q, k_new, v_new, cache_kt, cache_vt, j, pt_flat, n_pages, mask, bias_dec):
    b = q.shape[0]
    page = lambda i: pl.BlockSpec((1, 1, A_KV, BLK), lambda bi, pt: (j, pt[bi * n_pages + i], 0, 0))
    rowspec = lambda w: pl.BlockSpec((1, 1, w), lambda bi, pt: (bi, 0, 0))
    grid_spec = pltpu.PrefetchScalarGridSpec(
        num_scalar_prefetch=1,
        grid=(b,),
        in_specs=[rowspec(A_Q), rowspec(A_KV), rowspec(A_KV)]
                 + [page(i) for i in range(n_pages)] + [page(i) for i in range(n_pages)]
                 + [pl.BlockSpec((n_pages + 1, 1, 1, BLK), lambda bi, pt: (0, bi, 0, 0)),
                    pl.BlockSpec((n_pages + 1, A_HEADS, BLK), lambda bi, pt: (0, 0, 0))],
        out_specs=rowspec(A_Q),
    )
    return pl.pallas_call(
        functools.partial(_dsa_dec_kernel, n_pages=n_pages),
        grid_spec=grid_spec,
        out_shape=jax.ShapeDtypeStruct((b, 1, A_Q), F32),
        compiler_params=_cparams(("arbitrary",)),
        name="dsa_decode",
    )(pt_flat, q, k_new, v_new, *([cache_kt] * n_pages), *([cache_vt] * n_pages), mask, bias_dec)


def _even_weights(w_in):
    sizes = (A_Q, A_KV, A_KV, IDX_HEADS * IDX_DIM, IDX_DIM, IDX_HEADS, B_CONV_DIM, B_V, B_HEADS, B_HEADS)
    offs = np.concatenate([[0], np.cumsum(sizes)])
    seg = lambda i: w_in[:, int(offs[i]):int(offs[i + 1])]
    pad = LANES - (IDX_DIM + IDX_HEADS + 2 * B_HEADS)
    small = jnp.concatenate([seg(4), seg(5), seg(8), seg(9), jnp.zeros((w_in.shape[0], pad), w_in.dtype)], axis=1)
    return jnp.concatenate([seg(0), seg(1), seg(2), seg(3), small, seg(6), seg(7)], axis=1).astype(BF16)


def _seq_outs(widths, dtype=F32):
    outs, off = [], 0
    for wd in widths:
        outs.append((off, wd, dtype))
        off += wd
    return tuple(outs)


_EVEN_OUTS = _seq_outs((A_Q, A_KV, A_KV, IDX_HEADS * IDX_DIM, LANES, B_CONV_DIM, B_V))
_ODD_OUTS = _seq_outs((C_QK, C_QK, C_V))
_ODD_OUTS_PROMPT = _ODD_OUTS[1:] + _seq_outs((C_QK, C_QK, C_V), BF16)


def kernel(x_prompt, x_sample, c_prompt, c_sample, cache_A_k, cache_A_v, cache_A_kidx, cache_C_k, cache_C_v, state_B_ssm, state_B_conv, page_table, rel_bias, ada_w, ada_b, norm_w, final_norm_w, ffn_w1, ffn_w2, ab_w_in, ab_w_out, gdn_conv_w, gdn_a_log, gdn_dt_bias, gdn_norm_w, c_w_in, c_w_out, c_lambda_q1, c_lambda_k1, c_lambda_q2, c_lambda_k2, c_subln_w):
    depth = ada_w.shape[0]
    bp, tp, d = x_prompt.shape
    bs, ts, _ = x_sample.shape
    assert ts == 1 and tp % TQ == 0 and tp % CHUNK == 0
    n_pages = page_table.shape[1]
    page = cache_A_k.shape[2]
    assert page == BLK
    n_phys = cache_A_k.shape[1]
    past_len = n_pages * page
    topk_p = min(TOPK_MAX, tp // 4)
    topk_s = min(TOPK_MAX, (past_len + ts) // 4)
    pt_flat = page_table.reshape(-1).astype(I32)

    n_c = bp + bs
    n_c_pad = -(-n_c // SUBLANES) * SUBLANES
    c_all = jnp.pad(jnp.concatenate([c_prompt, c_sample], axis=0), ((0, n_c_pad - n_c), (0, 0)))
    mod_all = _modulation(c_all, ada_w, ada_b)

    w1b = ffn_w1.astype(BF16)
    w2b = ffn_w2.astype(BF16)
    ab_in_b = [_even_weights(ab_w_in[j]) for j in range(ab_w_in.shape[0])]
    ab_out_b = ab_w_out.astype(BF16)
    c_in_b = c_w_in.astype(BF16)
    c_out_b = c_w_out.astype(BF16)

    bias_a = _bias_tiles(rel_bias[:, :A_HEADS], BLK, False)
    bias_c = jnp.swapaxes(_bias_tiles(rel_bias[:, A_HEADS:], TQ, True), 2, 3)
    bias_a_dec = _bias_decode(rel_bias[:, :A_HEADS], past_len, n_pages + 1)
    bias_c_pages, bias_c_new = _diff_decode_bias(rel_bias[:, A_HEADS:], past_len, n_pages)

    n_ab = cache_A_k.shape[0]
    cache_a_kt = jnp.transpose(cache_A_k, (0, 1, 3, 4, 2)).reshape(n_ab, n_phys, A_KV, page)
    cache_a_vt = jnp.transpose(cache_A_v, (0, 1, 3, 4, 2)).reshape(n_ab, n_phys, A_KV, page)
    cache_a_it = jnp.transpose(cache_A_kidx, (0, 1, 3, 2))
    cache_c_k = cache_C_k.reshape(cache_C_k.shape[0], n_phys, page * C_HEADS, C_HD)
    cache_c_v = cache_C_v.reshape(cache_C_v.shape[0], n_phys, page * C_HEADS, C_HD)
    conv_t = jnp.transpose(state_B_conv, (0, 2, 1, 3))

    xp = x_prompt
    xs = x_sample.reshape(1, bs, d)
    new_p = [[] for _ in range(7)]
    new_s = [[] for _ in range(7)]

    for i in range(depth):
        j = i // 2
        mod_p = mod_all[i, :bp].reshape(bp, 1, N_MOD * d)
        mod_s = mod_all[i, bp:bp + bs].reshape(1, bs, N_MOD * d)
        xp = _ffn(xp, mod_p, 0, norm_w[i, 0], w1b, w2b, i, 0)
        xs = _ffn(xs, mod_s, 0, norm_w[i, 0], w1b, w2b, i, 0)
        if i % 2 == 0:
            q, k, v, qi, sm, qkv, z = _inproj(xp, mod_p, 3, norm_w[i, 1], ab_in_b[j], _EVEN_OUTS)
            o_a = _dsa_prompt(q, qi, sm, k, v, bias_a, topk_p)
            o_b, s_new = _gdn_prompt(qkv, z, sm, jnp.zeros((bp, B_HEADS, B_KEY_DIM, B_VAL_DIM), F32),
                                     gdn_conv_w[j], gdn_a_log[j], gdn_dt_bias[j], gdn_norm_w[j])
            xp = _outproj(xp, mod_p, 5, ab_out_b[j], [o_a, o_b])
            new_p[0].append(k.reshape(bp, tp, A_KV_HEADS, A_HEAD_DIM))
            new_p[1].append(v.reshape(bp, tp, A_KV_HEADS, A_HEAD_DIM))
            new_p[2].append(sm[:, :, SM_KI:SM_KI + IDX_DIM])
            new_p[3].append(s_new)
            new_p[4].append(qkv[:, tp - (CONV_W - 1):, :])
            q, k, v, qi, sm, qkv, z = _inproj(xs, mod_s, 3, norm_w[i, 1], ab_in_b[j], _EVEN_OUTS)
            as_rows = lambda a: a.reshape(bs, 1, a.shape[-1])
            scores = _idx_decode(as_rows(qi), as_rows(sm), cache_a_it, j, pt_flat, n_pages)
            scores = jnp.moveaxis(scores.reshape(bs, n_pages + 1, BLK), 1, 0)
            mask = _sel_decode(scores, past_len + 1, topk_s).reshape(n_pages + 1, bs, 1, BLK)
            o_a = _dsa_decode(as_rows(q), as_rows(k), as_rows(v), cache_a_kt, cache_a_vt, j, pt_flat, n_pages,
                              mask, bias_a_dec)
            o_b, s_new = _gdn_decode(qkv.reshape(bs, B_CONV_DIM), z.reshape(bs, B_V), sm.reshape(bs, LANES),
                                     conv_t, state_B_ssm, j, gdn_conv_w[j], gdn_a_log[j], gdn_dt_bias[j],
                                     gdn_norm_w[j])
            xs = _outproj(xs, mod_s, 5, ab_out_b[j], [o_a.reshape(1, bs, A_Q), o_b.reshape(1, bs, B_V)])
            new_s[0].append(k.reshape(bs, 1, A_KV_HEADS, A_HEAD_DIM))
            new_s[1].append(v.reshape(bs, 1, A_KV_HEADS, A_HEAD_DIM))
            new_s[2].append(sm.reshape(bs, 1, LANES)[:, :, SM_KI:SM_KI + IDX_DIM])
            new_s[3].append(s_new)
            new_s[4].append(jnp.concatenate([state_B_conv[j], as_rows(qkv)], axis=1)[:, 1:, :])
        else:
            lam_init = 0.8 - 0.6 * math.exp(-0.3 * i)
            lam = (jnp.exp(jnp.sum(c_lambda_q1[j] * c_lambda_k1[j]))
                   - jnp.exp(jnp.sum(c_lambda_q2[j] * c_lambda_k2[j])) + lam_init).astype(F32)
            k, v, qb16, kb16, vb16 = _inproj(xp, mod_p, 3, norm_w[i, 1], c_in_b[j], _ODD_OUTS_PROMPT)
            o = _diff_prompt(qb16, kb16, vb16, bias_c, lam, c_subln_w[j], lam_init)
            xp = _outproj(xp, mod_p, 5, c_out_b[j], [o])
            new_p[5].append(k.reshape(bp, tp, C_HEADS, C_HD))
            new_p[6].append(v.reshape(bp, tp, C_HEADS, C_HD))
            q, k, v = _inproj(xs, mod_s, 3, norm_w[i, 1], c_in_b[j], _ODD_OUTS)
            as_heads = lambda a: a.reshape(bs, C_HEADS, C_HD)
            o = _diff_decode(as_heads(q), as_heads(k), as_heads(v), cache_c_k, cache_c_v, j, pt_flat, n_pages,
                             bias_c_pages, bias_c_new, lam, c_subln_w[j], lam_init)
            xs = _outproj(xs, mod_s, 5, c_out_b[j], [o.reshape(1, bs, C_V)])
            new_s[5].append(k.reshape(bs, 1, C_HEADS, C_HD))
            new_s[6].append(v.reshape(bs, 1, C_HEADS, C_HD))
        xp = _ffn(xp, mod_p, 6, norm_w[i, 2], w1b, w2b, i, 1)
        xs = _ffn(xs, mod_s, 6, norm_w[i, 2], w1b, w2b, i, 1)

    y_prompt = _final_norm(xp, final_norm_w)
    y_sample = _final_norm(xs, final_norm_w).reshape(bs, 1, d)
    sp = [jnp.stack(lst) for lst in new_p]
    ss = [jnp.stack(lst) for lst in new_s]
    return (y_prompt, y_sample, *sp, *ss)
```

```python
import functools
import math

import numpy as np
import jax
import jax.numpy as jnp
from jax import lax
from jax.experimental import pallas as pl
from jax.experimental.pallas import tpu as pltpu

F32 = jnp.float32
BF16 = jnp.bfloat16
I32 = jnp.int32

A_HEADS = 8
A_KV_HEADS = 2
A_GROUP = A_HEADS // A_KV_HEADS
A_HEAD_DIM = 64
IDX_HEADS = 4
IDX_DIM = 64
TOPK_MAX = 256
B_HEADS = 4
B_KEY_DIM = 128
B_VAL_DIM = 128
CONV_W = 4
CHUNK = 64
C_HEADS = 8
C_HEAD_DIM = 64
NUM_BUCKETS = 32
MAX_DISTANCE = 128
N_MOD = 9
EPS = 1e-6

A_Q = A_HEADS * A_HEAD_DIM
A_KV = A_KV_HEADS * A_HEAD_DIM
B_K = B_HEADS * B_KEY_DIM
B_V = B_HEADS * B_VAL_DIM
B_CONV_DIM = 2 * B_K + B_V
C_QK = C_HEADS * 2 * C_HEAD_DIM
C_V = C_HEADS * 2 * C_HEAD_DIM
C_HD = 2 * C_HEAD_DIM

LANES = 128
SUBLANES = 8
VMEM_LIMIT = 56 * 1024 * 1024

BLK = 128
TQ = 256
NEG = -1e30

SM_KI = 0
SM_WI = IDX_DIM
SM_A = SM_WI + IDX_HEADS
SM_B = SM_A + B_HEADS

_NT = (((1,), (1,)), ((), ()))
_NN = (((1,), (0,)), ((), ()))
_TN = (((0,), (0,)), ((), ()))


def _cparams(sem):
    return pltpu.CompilerParams(dimension_semantics=sem, vmem_limit_bytes=VMEM_LIMIT)


def _dg(a, b, dims=_NN):
    return lax.dot_general(a, b, dims, preferred_element_type=F32)


def _dot1(a, b, dims=_NN):
    return _dg(a.astype(BF16), b.astype(BF16), dims)


def _split2(x):
    hi = x.astype(BF16)
    lo = (x - hi.astype(F32)).astype(BF16)
    return hi, lo


def _split3(x):
    b1 = x.astype(BF16)
    r1 = x - b1.astype(F32)
    b2 = r1.astype(BF16)
    b3 = (r1 - b2.astype(F32)).astype(BF16)
    return b1, b2, b3


def _dot3(a, b, dims=_NN):
    ah, al = _split2(a)
    bh, bl = _split2(b)
    return _dg(ah, bh, dims) + (_dg(ah, bl, dims) + _dg(al, bh, dims))


def _dot_exact_lhs(a_bf, b, dims=_NN):
    b1, b2, b3 = _split3(b)
    return _dg(a_bf, b1, dims) + (_dg(a_bf, b2, dims) + _dg(a_bf, b3, dims))


def _dot_exact_rhs(a, b_bf, dims=_NN):
    a1, a2, a3 = _split3(a)
    return _dg(a1, b_bf, dims) + (_dg(a2, b_bf, dims) + _dg(a3, b_bf, dims))


def _silu(x):
    return x * jax.nn.sigmoid(x)


def _softplus(x):
    return jnp.maximum(x, 0.0) + jnp.log(1.0 + jnp.exp(-jnp.abs(x)))


def _norm_mod(x, nw, sc, sh):
    ms = jnp.mean(x * x, axis=-1, keepdims=True)
    return (x * lax.rsqrt(ms + EPS) * nw) * (1.0 + sc) + sh


def _softmax_step(s, m, l, acc, pv):
    m_new = jnp.maximum(m, jnp.max(s, axis=1, keepdims=True))
    alpha = jnp.exp(m - m_new)
    p = jnp.exp(s - m_new)
    return m_new, alpha * l + jnp.sum(p, axis=1, keepdims=True), alpha * acc + pv(p)


def _mod_kernel(c_ref, w_ref, b_ref, o_ref):
    s = _silu(c_ref[...]).astype(BF16)
    o_ref[0] = _dg(s, w_ref[0].astype(BF16)) + b_ref[0]


def _modulation(c_all, ada_w, ada_b):
    depth, d, n = ada_w.shape
    m = c_all.shape[0]
    tn = 1024
    return pl.pallas_call(
        _mod_kernel,
        grid=(depth, n // tn),
        in_specs=[pl.BlockSpec((m, d), lambda i, j: (0, 0)),
                  pl.BlockSpec((1, d, tn), lambda i, j: (i, 0, j)),
                  pl.BlockSpec((1, 1, tn), lambda i, j: (i, 0, j))],
        out_specs=pl.BlockSpec((1, m, tn), lambda i, j: (i, 0, j)),
        out_shape=jax.ShapeDtypeStruct((depth, m, n), F32),
        compiler_params=_cparams(("arbitrary", "arbitrary")),
        name="adaln_mod",
    )(c_all, ada_w, ada_b.reshape(depth, 1, n))


def _row_tile(t):
    return min(512, t)


def _mod_spec(mod, tm, m):
    r = mod.shape[1]
    d = mod.shape[2] // N_MOD
    if r == 1:
        return pl.BlockSpec((1, 1, d), lambda s, t: (s, 0, m))
    return pl.BlockSpec((1, tm, d), lambda s, t: (s, t, m))


def _resident(shape):
    nd = len(shape)
    return pl.BlockSpec(shape, lambda s, t: (0,) * nd, pipeline_mode=pl.Buffered(1))


def _ffn_kernel(x_ref, sh_ref, sc_ref, g_ref, nw_ref, w1_ref, w2_ref, o_ref, acc_ref, *, fc):
    x = x_ref[0]
    hb = _norm_mod(x, nw_ref[...], sc_ref[0], sh_ref[0]).astype(BF16)
    f = w2_ref.shape[2]
    for c in range(f // fc):
        gt = _dg(hb, w1_ref[0, 0, :, c * fc:(c + 1) * fc])
        up = _dg(hb, w1_ref[0, 0, :, f + c * fc:f + (c + 1) * fc])
        a = (_silu(gt) * up).astype(BF16)
        contrib = _dg(a, w2_ref[0, 0, c * fc:(c + 1) * fc, :])
        if c == 0:
            acc_ref[...] = contrib
        else:
            acc_ref[...] += contrib
    o_ref[0] = x + (0.5 * g_ref[0]) * acc_ref[...]


def _ffn(x, mod, m0, nw, w1b, w2b, layer, which):
    s, t, d = x.shape
    tm = _row_tile(t)
    f = w2b.shape[2]
    wspec = lambda shp: pl.BlockSpec((1, 1) + tuple(shp[2:]), lambda si, ti: (layer, which, 0, 0),
                                     pipeline_mode=pl.Buffered(1))
    fc = 256 if f % 256 == 0 else LANES
    xs = pl.BlockSpec((1, tm, d), lambda si, ti: (si, ti, 0))
    return pl.pallas_call(
        functools.partial(_ffn_kernel, fc=fc),
        grid=(s, t // tm),
        in_specs=[xs, _mod_spec(mod, tm, m0), _mod_spec(mod, tm, m0 + 1), _mod_spec(mod, tm, m0 + 2),
                  _resident((1, d)), wspec(w1b.shape), wspec(w2b.shape)],
        out_specs=xs,
        out_shape=jax.ShapeDtypeStruct(x.shape, F32),
        scratch_shapes=[pltpu.VMEM((tm, d), F32)],
        compiler_params=_cparams(("arbitrary", "arbitrary")),
        name="ffn",
    )(x, mod, mod, mod, nw.reshape(1, d), w1b, w2b)


def _inproj_kernel(x_ref, sh_ref, sc_ref, nw_ref, w_ref, *o_refs, outs):
    hb = _norm_mod(x_ref[0], nw_ref[...], sc_ref[0], sh_ref[0]).astype(BF16)
    done = {}
    for o_ref, (off, wd, dt) in zip(o_refs, outs):
        if (off, wd) not in done:
            done[(off, wd)] = _dg(hb, w_ref[:, off:off + wd])
        o_ref[0] = done[(off, wd)].astype(dt)


def _inproj(x, mod, m0, nw, wb, outs):
    s, t, d = x.shape
    tm = _row_tile(t)
    xs = pl.BlockSpec((1, tm, d), lambda si, ti: (si, ti, 0))
    return pl.pallas_call(
        functools.partial(_inproj_kernel, outs=tuple(outs)),
        grid=(s, t // tm),
        in_specs=[xs, _mod_spec(mod, tm, m0), _mod_spec(mod, tm, m0 + 1), _resident((1, d)), _resident(wb.shape)],
        out_specs=[pl.BlockSpec((1, tm, wd), lambda si, ti: (si, ti, 0)) for _, wd, _ in outs],
        out_shape=[jax.ShapeDtypeStruct((s, t, wd), dt) for _, wd, dt in outs],
        compiler_params=_cparams(("arbitrary", "arbitrary")),
        name="inproj",
    )(x, mod, mod, nw.reshape(1, d), wb)


def _outproj_kernel(*refs, widths):
    n = len(widths)
    x_ref, g_ref, w_ref = refs[0], refs[1], refs[2]
    a_refs = refs[3:3 + n]
    o_ref = refs[3 + n]
    acc = None
    off = 0
    for a_ref, wd in zip(a_refs, widths):
        part = _dg(a_ref[0].astype(BF16), w_ref[off:off + wd, :])
        acc = part if acc is None else acc + part
        off += wd
    o_ref[0] = x_ref[0] + g_ref[0] * acc


def _outproj(x, mod, mg, wb, parts):
    s, t, d = x.shape
    tm = _row_tile(t)
    widths = tuple(p.shape[-1] for p in parts)
    xs = pl.BlockSpec((1, tm, d), lambda si, ti: (si, ti, 0))
    return pl.pallas_call(
        functools.partial(_outproj_kernel, widths=widths),
        grid=(s, t // tm),
        in_specs=[xs, _mod_spec(mod, tm, mg), _resident(wb.shape)]
                 + [pl.BlockSpec((1, tm, wd), lambda si, ti: (si, ti, 0)) for wd in widths],
        out_specs=xs,
        out_shape=jax.ShapeDtypeStruct(x.shape, F32),
        compiler_params=_cparams(("arbitrary", "arbitrary")),
        name="outproj",
    )(x, mod, wb, *parts)


def _final_norm_kernel(x_ref, w_ref, o_ref):
    x = x_ref[0]
    ms = jnp.mean(x * x, axis=-1, keepdims=True)
    o_ref[0] = x * lax.rsqrt(ms + EPS) * w_ref[...]


def _final_norm(x, w):
    s, t, d = x.shape
    tm = _row_tile(t)
    xs = pl.BlockSpec((1, tm, d), lambda si, ti: (si, ti, 0))
    return pl.pallas_call(
        _final_norm_kernel,
        grid=(s, t // tm),
        in_specs=[xs, pl.BlockSpec((1, d), lambda si, ti: (0, 0))],
        out_specs=xs,
        out_shape=jax.ShapeDtypeStruct(x.shape, F32),
        compiler_params=_cparams(("arbitrary", "arbitrary")),
        name="final_norm",
    )(x, w.reshape(1, d))


def _t5_bucket(dist):
    n = jnp.maximum(dist, 0)
    max_exact = NUM_BUCKETS // 2
    nf = jnp.maximum(n, 1).astype(F32)
    large = max_exact + (jnp.log(nf / max_exact) / math.log(MAX_DISTANCE / max_exact)
                         * (NUM_BUCKETS - max_exact)).astype(I32)
    large = jnp.minimum(large, NUM_BUCKETS - 1)
    return jnp.where(n < max_exact, n, large)


def _bias_tiles(bias_cols, blk, causal):
    assert blk >= MAX_DISTANCE
    n = blk
    h = bias_cols.shape[1]
    tiles = []
    for t in range(3):
        off = (2 - t) * n
        if t == 0:
            far = bias_cols[_t5_bucket(jnp.full((), 2 * n, I32))]
            tiles.append(jnp.broadcast_to(far[:, None, None], (h, n, n)))
            continue
        dist = n - 1 + off - jnp.arange(2 * n, dtype=I32)
        g = bias_cols[_t5_bucket(dist)]
        if causal:
            g = jnp.where((dist >= 0)[:, None], g, NEG)
        x = jnp.broadcast_to(g.T[:, None, :], (h, n, 2 * n))
        x = jnp.pad(x, ((0, 0), (0, 0), (0, 1))).reshape(h, n * (2 * n + 1))[:, :2 * n * n].reshape(h, n, 2 * n)
        tiles.append(x[:, :, n - 1:2 * n - 1])
    return jnp.stack(tiles, axis=1)


def _bias_decode(bias_cols, q_pos, n_blocks):
    k_pos = jnp.arange(n_blocks * BLK, dtype=I32)
    b = bias_cols[_t5_bucket(q_pos - k_pos)]
    b = jnp.where((k_pos <= q_pos)[:, None], b, NEG)
    return jnp.moveaxis(b.reshape(n_blocks, BLK, -1), 2, 1)


def _score_keys(s):
    bits = pltpu.bitcast(s, I32)
    return jnp.where(bits < 0, bits ^ jnp.int32(0x7FFFFFFF), bits)


def _select_topk(key_ref, nkb, topk, idx_bits, key_axis=1, pairs=False):
    tile = tuple(key_ref.shape[1:])
    assert tile[key_axis] == BLK
    stat = tuple(1 if a == key_axis else n for a, n in enumerate(tile))
    rows = stat
    col = lax.broadcasted_iota(I32, tile, key_axis)

    def count(pred):
        if pairs:
            def body(i, acc):
                kb = 2 * i
                return (acc + jnp.where(pred(key_ref[kb], kb * BLK + col), 1.0, 0.0)
                        + jnp.where(pred(key_ref[kb + 1], (kb + 1) * BLK + col), 1.0, 0.0))
            acc = lax.fori_loop(0, (nkb + 1) // 2, body, jnp.zeros(tile, F32))
        else:
            def body(kb, acc):
                return acc + jnp.where(pred(key_ref[kb], kb * BLK + col), 1.0, 0.0)
            acc = lax.fori_loop(0, nkb, body, jnp.zeros(tile, F32))
        return jnp.sum(acc, axis=key_axis, keepdims=True)

    kf = float(topk)

    def bit_body(i, carry):
        lo, cnt_lo = carry
        cand = lo + lax.shift_left(jnp.int32(1), jnp.int32(31) - i)
        cnt = count(lambda k, _: k >= cand)
        take = cnt >= kf
        return jnp.where(take, cand, lo), jnp.where(take, cnt, cnt_lo)

    n_visited = 2 * ((nkb + 1) // 2) if pairs else nkb
    total = jnp.zeros(rows, F32) + jnp.asarray(n_visited * BLK, F32)
    thr, cnt_ge = lax.fori_loop(0, 32, bit_body, (jnp.full(rows, -2 ** 31, I32), total))
    need = kf - count(lambda k, _: k > thr)

    has_ties = jnp.max(cnt_ge) > kf

    def idx_body(i, p):
        cand = p + lax.shift_left(jnp.int32(1), jnp.int32(idx_bits - 1) - i)
        cnt = count(lambda k, ix: jnp.logical_and(k == thr, ix < cand))
        return jnp.where(cnt < need, cand, p)

    cut0 = jnp.zeros(rows, I32) + jnp.where(has_ties, 0, 2 ** idx_bits - 1)
    cut = lax.fori_loop(0, jnp.where(has_ties, idx_bits, 0), idx_body, cut0)
    return thr, cut


def _in_topk(key, idx, thr, cut):
    return jnp.logical_or(key > thr, jnp.logical_and(key == thr, idx <= cut))


DSA_CHAIN_HEADS = 4


def _dsa_kernel(q_ref, qi_ref, sm_ref, kf_ref, vf_ref, smf_ref, bias_ref, o_ref, key_ref, msk_ref,
                *, topk, idx_bits):
    qb = pl.program_id(1)
    nkb = qb + 1
    key_i = lax.broadcasted_iota(I32, (BLK, BLK), 0)
    q_pos = qb * BLK + lax.broadcasted_iota(I32, (BLK, BLK), 1)

    qi = qi_ref[0]
    sm_t = sm_ref[0].T
    qi_st = jnp.concatenate([qi[:, h * IDX_DIM:(h + 1) * IDX_DIM] for h in range(IDX_HEADS)], axis=0).astype(BF16)
    wi_rows = [sm_t[SM_WI + h:SM_WI + h + 1, :] * (IDX_HEADS ** -0.5) for h in range(IDX_HEADS)]

    def score_body(i, carry):
        kbs = (2 * i, 2 * i + 1)
        kis = [smf_ref[0, pl.ds(pl.multiple_of(kb * BLK, BLK), BLK), :][:, SM_KI:SM_KI + IDX_DIM].astype(BF16)
               for kb in kbs]
        rs = [jnp.maximum(_dg(ki, qi_st, _NT) * (IDX_DIM ** -0.5), 0.0) for ki in kis]
        for kb, r in zip(kbs, rs):
            s = wi_rows[0] * r[:, 0:BLK]
            for h in range(1, IDX_HEADS):
                s = s + wi_rows[h] * r[:, h * BLK:(h + 1) * BLK]
            s = jnp.where(kb * BLK + key_i <= q_pos, s, -jnp.inf)
            key_ref[kb] = _score_keys(s)
        return carry

    lax.fori_loop(0, (nkb + 1) // 2, score_body, 0)

    @pl.when(nkb * BLK <= topk)
    def _():
        def body(kb, carry):
            msk_ref[kb] = jnp.where(kb * BLK + key_i <= q_pos, 0.0, NEG)
            return carry
        lax.fori_loop(0, nkb, body, 0)

    @pl.when(nkb * BLK > topk)
    def _():
        thr, cut = _select_topk(key_ref, nkb, topk, idx_bits, key_axis=0, pairs=True)

        def body(kb, carry):
            k_pos = kb * BLK + key_i
            sel = _in_topk(key_ref[kb], k_pos, thr, cut)
            msk_ref[kb] = jnp.where(k_pos <= q_pos, jnp.where(sel, 0.0, NEG), NEG)
            return carry
        lax.fori_loop(0, nkb, body, 0)

    q = q_ref[0] * (A_HEAD_DIM ** -0.5)
    ch = DSA_CHAIN_HEADS
    cq = ch * BLK
    chains = [(h0 // A_GROUP, h0) for h0 in range(0, A_HEADS, ch)]
    cs = range(len(chains))
    qs = [jnp.concatenate([q[:, (h0 + g) * A_HEAD_DIM:(h0 + g + 1) * A_HEAD_DIM] for g in range(ch)],
                          axis=0).astype(BF16) for _, h0 in chains]

    def logits(kb):
        off = pl.multiple_of(kb * BLK, BLK)
        kblk = kf_ref[0, pl.ds(off, BLK), :].astype(BF16)
        t = jnp.clip(kb - qb + 2, 0, 2)
        mkc = jnp.concatenate([msk_ref[kb]] * ch, axis=1)
        out = []
        for c, (n, h0) in enumerate(chains):
            g0 = h0 - n * A_GROUP
            bias = bias_ref[n, t][:, g0 * BLK:(g0 + ch) * BLK]
            out.append(_dg(kblk[:, n * A_HEAD_DIM:(n + 1) * A_HEAD_DIM], qs[c], _NT) + bias + mkc)
        return out

    def att_body(kb, carry):
        ss = carry[3 * len(chains):]
        ss_next = logits(jnp.minimum(kb + 1, nkb - 1))
        off = pl.multiple_of(kb * BLK, BLK)
        vblk = vf_ref[0, pl.ds(off, BLK), :].astype(BF16)
        m_new = [jnp.maximum(carry[3 * c], jnp.max(ss[c], axis=0, keepdims=True)) for c in cs]
        ps = [jnp.exp(ss[c] - m_new[c]) for c in cs]
        pvs = [_dg(vblk[:, chains[c][0] * A_HEAD_DIM:(chains[c][0] + 1) * A_HEAD_DIM], ps[c].astype(BF16), _TN)
               for c in cs]
        out = []
        for c in cs:
            alpha = jnp.exp(carry[3 * c] - m_new[c])
            out += [m_new[c], alpha * carry[3 * c + 1] + jnp.sum(ps[c], axis=0, keepdims=True),
                    alpha * carry[3 * c + 2] + pvs[c]]
        return tuple(out) + tuple(ss_next)

    init = (jnp.full((1, cq), NEG, F32), jnp.zeros((1, cq), F32), jnp.zeros((A_HEAD_DIM, cq), F32)) * len(chains)
    res = lax.fori_loop(0, nkb, att_body, init + tuple(logits(0)))
    for c, (n, h0) in enumerate(chains):
        o_t = res[3 * c + 2] / res[3 * c + 1]
        for g in range(0, ch, 2):
            pair = jnp.concatenate([o_t[:, g * BLK:(g + 1) * BLK], o_t[:, (g + 1) * BLK:(g + 2) * BLK]], axis=0)
            o_ref[0, :, (h0 + g) * A_HEAD_DIM:(h0 + g + 2) * A_HEAD_DIM] = pair.T


def _dsa_prompt(q, qi, sm, k, v, bias_tiles, topk):
    b, t, _ = q.shape
    nb = t // BLK
    assert nb % 2 == 0
    idx_bits = max(1, int(math.ceil(math.log2(t))))
    bt = bias_tiles.reshape(A_KV_HEADS, A_GROUP, 3, BLK, BLK)
    bt = jnp.transpose(bt, (0, 2, 4, 1, 3)).reshape(A_KV_HEADS, 3, BLK, A_GROUP * BLK)
    blk = lambda w: pl.BlockSpec((1, BLK, w), lambda bi, qb: (bi, qb, 0))
    full = lambda w: pl.BlockSpec((1, t, w), lambda bi, qb: (bi, 0, 0))
    return pl.pallas_call(
        functools.partial(_dsa_kernel, topk=topk, idx_bits=idx_bits),
        grid=(b, nb),
        in_specs=[blk(A_Q), blk(IDX_HEADS * IDX_DIM), blk(LANES), full(A_KV), full(A_KV), full(LANES),
                  pl.BlockSpec(bt.shape, lambda bi, qb: (0, 0, 0, 0))],
        out_specs=blk(A_Q),
        out_shape=jax.ShapeDtypeStruct((b, t, A_Q), F32),
        scratch_shapes=[pltpu.VMEM((nb, BLK, BLK), I32), pltpu.VMEM((nb, BLK, BLK), F32)],
        compiler_params=_cparams(("arbitrary", "arbitrary")),
        name="dsa_prompt",
    )(q, qi, sm, k, v, sm, bt)


def _gdn_kernel(qkv_ref, z_ref, sm_ref, s0_ref, cw_ref, par_ref, nw_ref, o_ref, sout_ref, ext_ref, st_ref):
    t = pl.program_id(1)
    nt = pl.num_programs(1)
    c = CHUNK
    halo = SUBLANES

    nbb = qkv_ref.shape[0]
    bbs = range(nbb)

    @pl.when(t == 0)
    def _():
        for bb in bbs:
            ext_ref[bb, 0:halo, :] = jnp.zeros((halo, B_CONV_DIM), F32)
        st_ref[...] = s0_ref[...]

    convs = []
    for bb in bbs:
        x = qkv_ref[bb]
        ext_ref[bb, halo:halo + c, :] = x
        conv = None
        for j in range(CONV_W):
            start = halo - (CONV_W - 1) + j
            term = ext_ref[bb, start:start + c, :] * cw_ref[j:j + 1, :]
            conv = term if conv is None else conv + term
        ext_ref[bb, 0:halo, :] = x[c - halo:c, :]
        convs.append(_silu(conv))

    ri = lax.broadcasted_iota(I32, (c, c), 0)
    ci = lax.broadcasted_iota(I32, (c, c), 1)
    tri = ri >= ci
    stri = ri > ci
    eye = jnp.where(ri == ci, 1.0, 0.0)
    tri_bf = jnp.where(tri, 1.0, 0.0).astype(BF16)
    ones_bf = jnp.ones((c, c), BF16)

    sms = [sm_ref[bb] for bb in bbs]
    g_alls = [-jnp.exp(par_ref[0:1, :]) * _softplus(sm + par_ref[1:2, :]) for sm in sms]
    gc_alls = [_dot_exact_lhs(tri_bf, g) for g in g_alls]
    beta_alls = [jax.nn.sigmoid(sm) for sm in sms]
    nw = nw_ref[...]
    items = [(bb, h) for bb in bbs for h in range(B_HEADS)]
    hs = range(len(items))
    gcs = [gc_alls[bb][:, SM_A + h:SM_A + h + 1] for bb, h in items]
    gc_rows = [_dot_exact_lhs(ones_bf, jnp.concatenate([eye * gcs[bb * B_HEADS + h] for h in range(B_HEADS)],
                                                       axis=1)) for bb in bbs]
    qs = [convs[bb][:, h * B_KEY_DIM:(h + 1) * B_KEY_DIM] for bb, h in items]
    ks = [convs[bb][:, B_K + h * B_KEY_DIM:B_K + (h + 1) * B_KEY_DIM] for bb, h in items]
    vs = [convs[bb][:, 2 * B_K + h * B_VAL_DIM:2 * B_K + (h + 1) * B_VAL_DIM] for bb, h in items]
    qs = [q * lax.rsqrt(jnp.sum(q * q, axis=-1, keepdims=True) + EPS) * (B_KEY_DIM ** -0.5) for q in qs]
    ks = [k * lax.rsqrt(jnp.sum(k * k, axis=-1, keepdims=True) + EPS) for k in ks]
    betas = [beta_alls[bb][:, SM_B + h:SM_B + h + 1] for bb, h in items]
    decays = [jnp.exp(jnp.where(tri, gcs[i] - gc_rows[bb][:, h * c:(h + 1) * c], -jnp.inf))
              for i, (bb, h) in enumerate(items)]
    kbs = [ks[h] * betas[h] for h in hs]
    vbs = [vs[h] * betas[h] for h in hs]
    k_bf = [k.astype(BF16) for k in ks]
    a_mats = [jnp.where(stri, _dg(kbs[h].astype(BF16), k_bf[h], _NT) * decays[h], 0.0) for h in hs]
    tms = [eye - a for a in a_mats]
    pws = [_dot3(a, a) for a in a_mats]
    steps = int(math.log2(c))
    for j in range(1, steps):
        tms = [tm + _dot3(tm, pw) for tm, pw in zip(tms, pws)]
        if j < steps - 1:
            pws = [_dot3(pw, pw) for pw in pws]
    egs = [jnp.exp(gc) for gc in gcs]
    tm_bf = [tm.astype(BF16) for tm in tms]
    us = [_dg(tm_bf[h], vbs[h].astype(BF16)) for h in hs]
    ws = [_dg(tm_bf[h], (kbs[h] * egs[h]).astype(BF16)) for h in hs]
    a_qks = [jnp.where(tri, _dg(qs[h].astype(BF16), k_bf[h], _NT) * decays[h], 0.0) for h in hs]

    s_olds = [st_ref[bb, h] for bb, h in items]
    s_bf = [s.astype(BF16) for s in s_olds]
    v_news = [us[i] - _dg(ws[i].astype(BF16), s_bf[i]) for i in hs]
    os_ = [_dg((qs[i] * egs[i]).astype(BF16), s_bf[i]) + _dot1(a_qks[i], v_news[i]) for i in hs]
    g_lasts = [gc[c - 1:c, :] for gc in gcs]
    for i, (bb, h) in enumerate(items):
        st_ref[bb, h] = (s_olds[i] * jnp.exp(g_lasts[i])
                         + _dot1(ks[i] * jnp.exp(g_lasts[i] - gcs[i]), v_news[i], _TN))
    for i, (bb, h) in enumerate(items):
        o = os_[i]
        ms = jnp.mean(o * o, axis=-1, keepdims=True)
        zh = z_ref[bb, :, h * B_VAL_DIM:(h + 1) * B_VAL_DIM]
        o_ref[bb, :, h * B_VAL_DIM:(h + 1) * B_VAL_DIM] = (o * lax.rsqrt(ms + EPS) * nw) * _silu(zh)

    @pl.when(t == nt - 1)
    def _():
        sout_ref[...] = st_ref[...]


GDN_SEQS_PER_STEP = 4


def _gdn_params(a_log, dt_bias):
    par = jnp.zeros((SUBLANES, LANES), F32)
    return par.at[0, SM_A:SM_A + B_HEADS].set(a_log).at[1, SM_A:SM_A + B_HEADS].set(dt_bias)


def _gdn_prompt(qkv, z, sm, s0, conv_w, a_log, dt_bias, norm_w):
    b, t, _ = qkv.shape
    nt = t // CHUNK
    nbb = GDN_SEQS_PER_STEP if b % GDN_SEQS_PER_STEP == 0 else 1
    tok = lambda w: pl.BlockSpec((nbb, CHUNK, w), lambda bi, ti: (bi, ti, 0))
    const2 = lambda shp: pl.BlockSpec(shp, lambda bi, ti: (0, 0))
    st_spec = pl.BlockSpec((nbb, B_HEADS, B_KEY_DIM, B_VAL_DIM), lambda bi, ti: (bi, 0, 0, 0))
    o, s_out = pl.pallas_call(
        _gdn_kernel,
        grid=(b // nbb, nt),
        in_specs=[tok(B_CONV_DIM), tok(B_V), tok(LANES), st_spec,
                  const2((CONV_W, B_CONV_DIM)), const2((SUBLANES, LANES)), const2((1, B_VAL_DIM))],
        out_specs=[tok(B_V), st_spec],
        out_shape=[jax.ShapeDtypeStruct((b, t, B_V), F32),
                   jax.ShapeDtypeStruct((b, B_HEADS, B_KEY_DIM, B_VAL_DIM), F32)],
        scratch_shapes=[pltpu.VMEM((nbb, SUBLANES + CHUNK, B_CONV_DIM), F32),
                        pltpu.VMEM((nbb, B_HEADS, B_KEY_DIM, B_VAL_DIM), F32)],
        compiler_params=_cparams(("arbitrary", "arbitrary")),
        name="gdn_prompt",
    )(qkv, z, sm, s0, conv_w, _gdn_params(a_log, dt_bias), norm_w.reshape(1, B_VAL_DIM))
    return o, s_out


GDN_DEC_ROWS = 16


def _gdn_dec_kernel(qkv_ref, z_ref, sm_ref, cb_ref, s_ref, cw_ref, par_ref, nw_ref, o_ref, so_ref, oraw_ref):
    r = GDN_DEC_ROWS
    conv = qkv_ref[...] * cw_ref[CONV_W - 1:CONV_W, :]
    for j in range(CONV_W - 1):
        conv = conv + cb_ref[0, j] * cw_ref[j:j + 1, :]
    conv = _silu(conv)
    sm = sm_ref[...]
    g_all = -jnp.exp(par_ref[0:1, :]) * _softplus(sm + par_ref[1:2, :])
    beta_all = jax.nn.sigmoid(sm)
    ri = lax.broadcasted_iota(I32, (r, LANES), 0)
    ci = lax.broadcasted_iota(I32, (r, LANES), 1)
    eye_bf = jnp.where(ri == ci, 1.0, 0.0).astype(BF16)

    for h in range(B_HEADS):
        qh = conv[:, h * B_KEY_DIM:(h + 1) * B_KEY_DIM]
        kh = conv[:, B_K + h * B_KEY_DIM:B_K + (h + 1) * B_KEY_DIM]
        vh = conv[:, 2 * B_K + h * B_VAL_DIM:2 * B_K + (h + 1) * B_VAL_DIM]
        qh = qh * lax.rsqrt(jnp.sum(qh * qh, axis=-1, keepdims=True) + EPS) * (B_KEY_DIM ** -0.5)
        kh = kh * lax.rsqrt(jnp.sum(kh * kh, axis=-1, keepdims=True) + EPS)
        eg = jnp.exp(g_all[:, SM_A + h:SM_A + h + 1])
        beta = beta_all[:, SM_B + h:SM_B + h + 1]
        qk = jnp.sum(qh * kh, axis=-1, keepdims=True)
        k_t = _dot_exact_rhs(kh, eye_bf, _TN)
        q_t = _dot_exact_rhs(qh, eye_bf, _TN)
        for s in range(r):
            st = s_ref[0, s, h]
            kc = k_t[:, s:s + 1]
            qc = q_t[:, s:s + 1]
            k_s = jnp.sum(st * kc, axis=0, keepdims=True)
            q_s = jnp.sum(st * qc, axis=0, keepdims=True)
            eg_s = eg[s:s + 1, :]
            v_new = beta[s:s + 1, :] * (vh[s:s + 1, :] - eg_s * k_s)
            oraw_ref[s:s + 1, h * B_VAL_DIM:(h + 1) * B_VAL_DIM] = eg_s * q_s + qk[s:s + 1, :] * v_new
            so_ref[s, h] = st * eg_s + kc * v_new

    zz = z_ref[...]
    nw = nw_ref[...]
    for h in range(B_HEADS):
        o = oraw_ref[:, h * B_VAL_DIM:(h + 1) * B_VAL_DIM]
        ms = jnp.mean(o * o, axis=-1, keepdims=True)
        zh = zz[:, h * B_VAL_DIM:(h + 1) * B_VAL_DIM]
        o_ref[:, h * B_VAL_DIM:(h + 1) * B_VAL_DIM] = (o * lax.rsqrt(ms + EPS) * nw) * _silu(zh)


def _gdn_decode(qkv, z, sm, conv_t, state, j, conv_w, a_log, dt_bias, norm_w):
    b = qkv.shape[0]
    r = GDN_DEC_ROWS
    assert b % r == 0
    rows = lambda w: pl.BlockSpec((r, w), lambda i: (i, 0))
    const2 = lambda shp: pl.BlockSpec(shp, lambda i: (0, 0))
    o, s_out = pl.pallas_call(
        _gdn_dec_kernel,
        grid=(b // r,),
        in_specs=[rows(B_CONV_DIM), rows(B_V), rows(LANES),
                  pl.BlockSpec((1, CONV_W - 1, r, B_CONV_DIM), lambda i: (j, 0, i, 0)),
                  pl.BlockSpec((1, r, B_HEADS, B_KEY_DIM, B_VAL_DIM), lambda i: (j, i, 0, 0, 0)),
                  const2((CONV_W, B_CONV_DIM)), const2((SUBLANES, LANES)), const2((1, B_VAL_DIM))],
        out_specs=[rows(B_V), pl.BlockSpec((r, B_HEADS, B_KEY_DIM, B_VAL_DIM), lambda i: (i, 0, 0, 0))],
        out_shape=[jax.ShapeDtypeStruct((b, B_V), F32),
                   jax.ShapeDtypeStruct((b, B_HEADS, B_KEY_DIM, B_VAL_DIM), F32)],
        scratch_shapes=[pltpu.VMEM((r, B_V), F32)],
        compiler_params=_cparams(("arbitrary",)),
        name="gdn_decode",
    )(qkv, z, sm, conv_t, state, conv_w, _gdn_params(a_log, dt_bias), norm_w.reshape(1, B_VAL_DIM))
    return o, s_out


DIFF_HEADS_PER_STEP = 8


def _diff_kernel(lam_ref, q_ref, k_ref, v_ref, bias_ref, sw_ref, o_ref, *, lam_init):
    qb = pl.program_id(2)
    nkb = qb + 1
    nh = DIFF_HEADS_PER_STEP
    lane = lax.broadcasted_iota(I32, (TQ, C_HD), 1)
    q2 = []
    for h in range(nh):
        qh = q_ref[0, :, h * C_HD:(h + 1) * C_HD] * (C_HEAD_DIM ** -0.5)
        zero = jnp.zeros_like(qh)
        q2.append(jnp.concatenate([jnp.where(lane < C_HEAD_DIM, qh, zero),
                                   jnp.where(lane >= C_HEAD_DIM, qh, zero)], axis=0))

    hs = range(nh)

    def body(kb, carry):
        off = pl.multiple_of(kb * TQ, TQ)
        t = jnp.clip(kb - qb + 2, 0, 2)
        ss = [_dg(k_ref[0, pl.ds(off, TQ), h * C_HD:(h + 1) * C_HD], q2[h], _NT) for h in hs]
        ss = [ss[h] + jnp.concatenate([bias_ref[h, t]] * 2, axis=1) for h in hs]
        m_new = [jnp.maximum(carry[3 * h], jnp.max(ss[h], axis=0, keepdims=True)) for h in hs]
        ps = [jnp.exp(ss[h] - m_new[h]) for h in hs]
        pvs = [_dg(v_ref[0, pl.ds(off, TQ), h * C_HD:(h + 1) * C_HD], ps[h].astype(BF16), _TN) for h in hs]
        out = []
        for h in hs:
            alpha = jnp.exp(carry[3 * h] - m_new[h])
            out += [m_new[h], alpha * carry[3 * h + 1] + jnp.sum(ps[h], axis=0, keepdims=True),
                    alpha * carry[3 * h + 2] + pvs[h]]
        return tuple(out)

    init = (jnp.full((1, 2 * TQ), NEG, F32), jnp.zeros((1, 2 * TQ), F32), jnp.zeros((C_HD, 2 * TQ), F32)) * nh
    res = lax.fori_loop(0, nkb, body, init)
    for h in range(nh):
        _, l, acc = res[3 * h:3 * h + 3]
        on = acc / l
        o = (on[:, 0:TQ] - lam_ref[0] * on[:, TQ:2 * TQ]).T
        ms = jnp.mean(o * o, axis=-1, keepdims=True)
        o_ref[0, :, h * C_HD:(h + 1) * C_HD] = (o * lax.rsqrt(ms + EPS) * sw_ref[...]) * (1.0 - lam_init)


def _diff_prompt(q, k, v, bias_tiles, lam, subln_w, lam_init):
    b, t, _ = q.shape
    nh = DIFF_HEADS_PER_STEP
    w = nh * C_HD
    return pl.pallas_call(
        functools.partial(_diff_kernel, lam_init=lam_init),
        grid=(b, C_HEADS // nh, t // TQ),
        in_specs=[pl.BlockSpec(memory_space=pltpu.SMEM),
                  pl.BlockSpec((1, TQ, w), lambda bi, h, qb: (bi, qb, h)),
                  pl.BlockSpec((1, t, w), lambda bi, h, qb: (bi, 0, h)),
                  pl.BlockSpec((1, t, w), lambda bi, h, qb: (bi, 0, h)),
                  pl.BlockSpec((nh, 3, TQ, TQ), lambda bi, h, qb: (h, 0, 0, 0)),
                  pl.BlockSpec((1, C_HD), lambda bi, h, qb: (0, 0))],
        out_specs=pl.BlockSpec((1, TQ, w), lambda bi, h, qb: (bi, qb, h)),
        out_shape=jax.ShapeDtypeStruct((b, t, C_V), F32),
        compiler_params=_cparams(("arbitrary", "arbitrary", "arbitrary")),
        name="diff_prompt",
    )(lam.reshape(1), q, k, v, bias_tiles, subln_w.reshape(1, C_HD))


DIFF_DEC_PAGES = 8


def _diff_dec_kernel(pt_ref, lam_ref, q_ref, kn_ref, vn_ref, *rest, n_steps, lam_init):
    pp = DIFF_DEC_PAGES
    kps, vps = rest[:pp], rest[pp:2 * pp]
    bias_ref, bnew_ref, sw_ref, o_ref, m_ref, l_ref, acc_ref = rest[2 * pp:]
    p = pl.program_id(1)
    nr = 2 * C_HEADS

    @pl.when(p == 0)
    def _():
        m_ref[...] = jnp.full(m_ref.shape, NEG, F32)
        l_ref[...] = jnp.zeros(l_ref.shape, F32)
        acc_ref[...] = jnp.zeros(acc_ref.shape, F32)

    q8 = q_ref[0] * (C_HEAD_DIM ** -0.5)
    lane = lax.broadcasted_iota(I32, (C_HEADS, C_HD), 1)
    q_lo = jnp.where(lane < C_HEAD_DIM, q8, 0.0)
    q_hi = jnp.where(lane >= C_HEAD_DIM, q8, 0.0)
    qm = jnp.concatenate([q_lo, q_hi], axis=0).astype(BF16)

    rows = BLK * C_HEADS
    half = pp // 2
    cs = range(2)
    ss = [jnp.concatenate([_dg(qm, kps[c * half + i][0, 0].astype(BF16), _NT) + bias_ref[c * half + i]
                           for i in range(half)], axis=1) for c in cs]
    m_old = [m_ref[c] for c in cs]
    m_new = [jnp.maximum(m_old[c], jnp.max(ss[c], axis=1, keepdims=True)) for c in cs]
    ps = [jnp.exp(ss[c] - m_new[c]) for c in cs]
    pb = [pr.astype(BF16) for pr in ps]
    pvs = []
    for c in cs:
        out = None
        for i in range(half):
            part = _dg(pb[c][:, i * rows:(i + 1) * rows], vps[c * half + i][0, 0].astype(BF16))
            out = part if out is None else out + part
        pvs.append(out)
    for c in cs:
        alpha = jnp.exp(m_old[c] - m_new[c])
        l_ref[c] = alpha * l_ref[c] + jnp.sum(ps[c], axis=1, keepdims=True)
        acc_ref[c] = alpha * acc_ref[c] + pvs[c]
        m_ref[c] = m_new[c]

    @pl.when(p == n_steps - 1)
    def _():
        kn = kn_ref[0]
        s_new = jnp.concatenate([jnp.sum(q_lo * kn, axis=1, keepdims=True),
                                 jnp.sum(q_hi * kn, axis=1, keepdims=True)], axis=0) + bnew_ref[:, 0:1]
        vn2 = jnp.concatenate([vn_ref[0], vn_ref[0]], axis=0)
        m = jnp.maximum(jnp.maximum(m_ref[0], m_ref[1]), s_new)
        w0 = jnp.exp(m_ref[0] - m)
        w1 = jnp.exp(m_ref[1] - m)
        p_new = jnp.exp(s_new - m)
        l = w0 * l_ref[0] + w1 * l_ref[1] + p_new
        acc = w0 * acc_ref[0] + w1 * acc_ref[1] + p_new * vn2
        on = acc / l
        o = on[0:C_HEADS] - lam_ref[0] * on[C_HEADS:nr]
        ms = jnp.mean(o * o, axis=-1, keepdims=True)
        o_ref[0] = (o * lax.rsqrt(ms + EPS) * sw_ref[...]) * (1.0 - lam_init)


def _diff_decode(q8, k8, v8, cache_k, cache_v, j, pt_flat, n_pages, bias_pages, bias_new, lam, subln_w, lam_init):
    b = q8.shape[0]
    pp = DIFF_DEC_PAGES
    assert n_pages % pp == 0
    n_steps = n_pages // pp
    rows = BLK * C_HEADS
    row = pl.BlockSpec((1, C_HEADS, C_HD), lambda bi, p, pt: (bi, 0, 0))
    page = lambda i: pl.BlockSpec((1, 1, rows, C_HD),
                                  lambda bi, p, pt: (j, pt[bi * n_pages + p * pp + i], 0, 0))
    grid_spec = pltpu.PrefetchScalarGridSpec(
        num_scalar_prefetch=1,
        grid=(b, n_steps),
        in_specs=[pl.BlockSpec(memory_space=pltpu.SMEM), row, row, row]
                 + [page(i) for i in range(pp)] + [page(i) for i in range(pp)]
                 + [pl.BlockSpec((pp, 2 * C_HEADS, rows), lambda bi, p, pt: (p, 0, 0)),
                    pl.BlockSpec((2 * C_HEADS, LANES), lambda bi, p, pt: (0, 0)),
                    pl.BlockSpec((1, C_HD), lambda bi, p, pt: (0, 0))],
        out_specs=row,
        scratch_shapes=[pltpu.VMEM((2, 2 * C_HEADS, 1), F32), pltpu.VMEM((2, 2 * C_HEADS, 1), F32),
                        pltpu.VMEM((2, 2 * C_HEADS, C_HD), F32)],
    )
    return pl.pallas_call(
        functools.partial(_diff_dec_kernel, n_steps=n_steps, lam_init=lam_init),
        grid_spec=grid_spec,
        out_shape=jax.ShapeDtypeStruct((b, C_HEADS, C_HD), F32),
        compiler_params=_cparams(("arbitrary", "arbitrary")),
        name="diff_decode",
    )(pt_flat, lam.reshape(1), q8, k8, v8, *([cache_k] * pp), *([cache_v] * pp),
      bias_pages, bias_new, subln_w.reshape(1, C_HD))


def _diff_decode_bias(bias_cols, q_pos, n_pages):
    bd = _bias_decode(bias_cols, q_pos, n_pages + 1)
    past = jnp.moveaxis(bd[:n_pages], 1, 2)
    same = jnp.eye(C_HEADS, dtype=bool)
    tab = jnp.where(same[None, :, None, :], past[:, None, :, :], NEG)
    tab = tab.reshape(n_pages, C_HEADS, BLK * C_HEADS)
    tab = jnp.concatenate([tab, tab], axis=1)
    new = bd[n_pages, :, 0]
    new = jnp.broadcast_to(jnp.concatenate([new, new])[:, None], (2 * C_HEADS, LANES))
    return tab, new


def _idx_dec_kernel(pt_ref, qi_ref, sm_ref, *rest, n_pages):
    kps = rest[:n_pages]
    o_ref = rest[n_pages]
    qi = qi_ref[0]
    sm = sm_ref[0]
    rowi = lax.broadcasted_iota(I32, (SUBLANES, IDX_HEADS * IDX_DIM), 0)
    lane = lax.broadcasted_iota(I32, (SUBLANES, IDX_HEADS * IDX_DIM), 1)
    qsel = jnp.where(lane // IDX_DIM == rowi, qi, 0.0)
    qt = qsel[:, 0:IDX_DIM]
    for h in range(1, IDX_HEADS):
        qt = qt + qsel[:, h * IDX_DIM:(h + 1) * IDX_DIM]
    r8 = lax.broadcasted_iota(I32, (SUBLANES, LANES), 0)
    l8 = lax.broadcasted_iota(I32, (SUBLANES, LANES), 1)
    wsel = jnp.where(jnp.logical_and(r8 < IDX_HEADS, l8 == r8 + SM_WI), sm, 0.0)
    wcol = jnp.sum(wsel, axis=1, keepdims=True) * (IDX_HEADS ** -0.5)
    qh, ql = _split2(qt)
    kh, kl = _split2(jnp.concatenate([kps[p][0, 0] for p in range(n_pages)], axis=1))
    d = _dg(qh, kh) + (_dg(qh, kl) + _dg(ql, kh))
    sc = jnp.sum(wcol * jnp.maximum(d * (IDX_DIM ** -0.5), 0.0), axis=0, keepdims=True)
    for p in range(n_pages):
        o_ref[0, p] = sc[:, p * BLK:(p + 1) * BLK]
    d_new = jnp.sum(qt * sm[:, SM_KI:SM_KI + IDX_DIM], axis=1, keepdims=True)
    s_new = jnp.sum(wcol * jnp.maximum(d_new * (IDX_DIM ** -0.5), 0.0), axis=0, keepdims=True)
    o_ref[0, n_pages] = jnp.broadcast_to(s_new, (1, BLK))


def _idx_decode(qi, sm, cache_kidx_t, j, pt_flat, n_pages):
    b = qi.shape[0]
    page = lambda p: pl.BlockSpec((1, 1, IDX_DIM, BLK), lambda bi, pt: (j, pt[bi * n_pages + p], 0, 0))
    grid_spec = pltpu.PrefetchScalarGridSpec(
        num_scalar_prefetch=1,
        grid=(b,),
        in_specs=[pl.BlockSpec((1, 1, IDX_HEADS * IDX_DIM), lambda bi, pt: (bi, 0, 0)),
                  pl.BlockSpec((1, 1, LANES), lambda bi, pt: (bi, 0, 0))]
                 + [page(p) for p in range(n_pages)],
        out_specs=pl.BlockSpec((1, n_pages + 1, 1, BLK), lambda bi, pt: (bi, 0, 0, 0)),
    )
    return pl.pallas_call(
        functools.partial(_idx_dec_kernel, n_pages=n_pages),
        grid_spec=grid_spec,
        out_shape=jax.ShapeDtypeStruct((b, n_pages + 1, 1, BLK), F32),
        compiler_params=_cparams(("arbitrary",)),
        name="idx_decode",
    )(pt_flat, qi, sm, *([cache_kidx_t] * n_pages))


def _sel_dec_kernel(s_ref, o_ref, key_ref, *, n_valid, topk, idx_bits):
    nkb, rows, _ = s_ref.shape
    col = lax.broadcasted_iota(I32, (rows, BLK), 1)
    for kb in range(nkb):
        s = jnp.where(kb * BLK + col < n_valid, s_ref[kb], -jnp.inf)
        key_ref[kb] = _score_keys(s)
    thr, cut = _select_topk(key_ref, nkb, topk, idx_bits)
    for kb in range(nkb):
        k_pos = kb * BLK + col
        sel = _in_topk(key_ref[kb], k_pos, thr, cut)
        o_ref[kb] = jnp.where(k_pos < n_valid, jnp.where(sel, 0.0, NEG), NEG)


def _sel_decode(scores, n_valid, topk):
    nkb, rows, _ = scores.shape
    idx_bits = max(1, int(math.ceil(math.log2(nkb * BLK))))
    return pl.pallas_call(
        functools.partial(_sel_dec_kernel, n_valid=n_valid, topk=topk, idx_bits=idx_bits),
        out_shape=jax.ShapeDtypeStruct(scores.shape, F32),
        scratch_shapes=[pltpu.VMEM(scores.shape, I32)],
        compiler_params=pltpu.CompilerParams(vmem_limit_bytes=VMEM_LIMIT),
        name="sel_decode",
    )(scores)


def _dsa_dec_kernel(pt_ref, q_ref, kn_ref, vn_ref, *rest, n_pages):
    kps, vps = rest[:n_pages], rest[n_pages:2 * n_pages]
    msk_ref, bias_ref, o_ref = rest[2 * n_pages:]

    q = q_ref[0] * (A_HEAD_DIM ** -0.5)
    rowi = lax.broadcasted_iota(I32, (A_HEADS, A_Q), 0)
    lane = lax.broadcasted_iota(I32, (A_HEADS, A_Q), 1)
    qsel = jnp.where(lane // A_HEAD_DIM == rowi, q, 0.0)
    halves = []
    for n in range(A_KV_HEADS):
        acc = None
        for g in range(A_GROUP):
            h = n * A_GROUP + g
            part = qsel[:, h * A_HEAD_DIM:(h + 1) * A_HEAD_DIM]
            acc = part if acc is None else acc + part
        halves.append(acc)
    qt = jnp.concatenate(halves, axis=1)
    qt_bf = qt.astype(BF16)

    kt = jnp.concatenate([kps[i][0, 0].astype(BF16) for i in range(n_pages)], axis=1)
    vt = jnp.concatenate([vps[i][0, 0].astype(BF16) for i in range(n_pages)], axis=1)
    extra = jnp.concatenate([bias_ref[i] + msk_ref[i, 0] for i in range(n_pages)], axis=1)
    s = _dg(qt_bf, kt) + extra
    s_new = (jnp.sum(qt * kn_ref[0], axis=1, keepdims=True) + bias_ref[n_pages][:, 0:1]
             + msk_ref[n_pages, 0][:, 0:1])
    m = jnp.maximum(jnp.max(s, axis=1, keepdims=True), s_new)
    p = jnp.exp(s - m)
    p_new = jnp.exp(s_new - m)
    l = jnp.sum(p, axis=1, keepdims=True) + p_new
    acc = p_new * vn_ref[0] + _dg(p.astype(BF16), vt, _NT)
    on = acc / l
    for h in range(A_HEADS):
        n = h // A_GROUP
        o_ref[0, :, h * A_HEAD_DIM:(h + 1) * A_HEAD_DIM] = on[h:h + 1, n * A_HEAD_DIM:(n + 1) * A_HEAD_DIM]


def _dsa_decode(q, k_new, v_new, cache_kt, cache_vt, j, pt_flat, n_pages, mask, bias_dec):
    b = q.shape[0]
    page = lambda i: pl.BlockSpec((1, 1, A_KV, BLK), lambda bi, pt: (j, pt[bi * n_pages + i], 0, 0))
    rowspec = lambda w: pl.BlockSpec((1, 1, w), lambda bi, pt: (bi, 0, 0))
    grid_spec = pltpu.PrefetchScalarGridSpec(
        num_scalar_prefetch=1,
        grid=(b,),
        in_specs=[rowspec(A_Q), rowspec(A_KV), rowspec(A_KV)]
                 + [page(i) for i in range(n_pages)] + [page(i) for i in range(n_pages)]
                 + [pl.BlockSpec((n_pages + 1, 1, 1, BLK), lambda bi, pt: (0, bi, 0, 0)),
                    pl.BlockSpec((n_pages + 1, A_HEADS, BLK), lambda bi, pt: (0, 0, 0))],
        out_specs=rowspec(A_Q),
    )
    return pl.pallas_call(
        functools.partial(_dsa_dec_kernel, n_pages=n_pages),
        grid_spec=grid_spec,
        out_shape=jax.ShapeDtypeStruct((b, 1, A_Q), F32),
        compiler_params=_cparams(("arbitrary",)),
        name="dsa_decode",
    )(pt_flat, q, k_new, v_new, *([cache_kt] * n_pages), *([cache_vt] * n_pages), mask, bias_dec)


def _even_weights(w_in):
    sizes = (A_Q, A_KV, A_KV, IDX_HEADS * IDX_DIM, IDX_DIM, IDX_HEADS, B_CONV_DIM, B_V, B_HEADS, B_HEADS)
    offs = np.concatenate([[0], np.cumsum(sizes)])
    seg = lambda i: w_in[:, int(offs[i]):int(offs[i + 1])]
    pad = LANES - (IDX_DIM + IDX_HEADS + 2 * B_HEADS)
    small = jnp.concatenate([seg(4), seg(5), seg(8), seg(9), jnp.zeros((w_in.shape[0], pad), w_in.dtype)], axis=1)
    return jnp.concatenate([seg(0), seg(1), seg(2), seg(3), small, seg(6), seg(7)], axis=1).astype(BF16)


def _seq_outs(widths, dtype=F32):
    outs, off = [], 0
    for wd in widths:
        outs.append((off, wd, dtype))
        off += wd
    return tuple(outs)


_EVEN_OUTS = _seq_outs((A_Q, A_KV, A_KV, IDX_HEADS * IDX_DIM, LANES, B_CONV_DIM, B_V))
_ODD_OUTS = _seq_outs((C_QK, C_QK, C_V))
_ODD_OUTS_PROMPT = _ODD_OUTS[1:] + _seq_outs((C_QK, C_QK, C_V), BF16)


def kernel(x_prompt, x_sample, c_prompt, c_sample, cache_A_k, cache_A_v, cache_A_kidx, cache_C_k, cache_C_v, state_B_ssm, state_B_conv, page_table, rel_bias, ada_w, ada_b, norm_w, final_norm_w, ffn_w1, ffn_w2, ab_w_in, ab_w_out, gdn_conv_w, gdn_a_log, gdn_dt_bias, gdn_norm_w, c_w_in, c_w_out, c_lambda_q1, c_lambda_k1, c_lambda_q2, c_lambda_k2, c_subln_w):
    depth = ada_w.shape[0]
    bp, tp, d = x_prompt.shape
    bs, ts, _ = x_sample.shape
    assert ts == 1 and tp % TQ == 0 and tp % CHUNK == 0
    n_pages = page_table.shape[1]
    page = cache_A_k.shape[2]
    assert page == BLK
    n_phys = cache_A_k.shape[1]
    past_len = n_pages * page
    topk_p = min(TOPK_MAX, tp // 4)
    topk_s = min(TOPK_MAX, (past_len + ts) // 4)
    pt_flat = page_table.reshape(-1).astype(I32)

    n_c = bp + bs
    n_c_pad = -(-n_c // SUBLANES) * SUBLANES
    c_all = jnp.pad(jnp.concatenate([c_prompt, c_sample], axis=0), ((0, n_c_pad - n_c), (0, 0)))
    mod_all = _modulation(c_all, ada_w, ada_b)

    w1b = ffn_w1.astype(BF16)
    w2b = ffn_w2.astype(BF16)
    ab_in_b = [_even_weights(ab_w_in[j]) for j in range(ab_w_in.shape[0])]
    ab_out_b = ab_w_out.astype(BF16)
    c_in_b = c_w_in.astype(BF16)
    c_out_b = c_w_out.astype(BF16)

    bias_a = _bias_tiles(rel_bias[:, :A_HEADS], BLK, False)
    bias_c = jnp.swapaxes(_bias_tiles(rel_bias[:, A_HEADS:], TQ, True), 2, 3)
    bias_a_dec = _bias_decode(rel_bias[:, :A_HEADS], past_len, n_pages + 1)
    bias_c_pages, bias_c_new = _diff_decode_bias(rel_bias[:, A_HEADS:], past_len, n_pages)

    n_ab = cache_A_k.shape[0]
    cache_a_kt = jnp.transpose(cache_A_k, (0, 1, 3, 4, 2)).reshape(n_ab, n_phys, A_KV, page)
    cache_a_vt = jnp.transpose(cache_A_v, (0, 1, 3, 4, 2)).reshape(n_ab, n_phys, A_KV, page)
    cache_a_it = jnp.transpose(cache_A_kidx, (0, 1, 3, 2))
    cache_c_k = cache_C_k.reshape(cache_C_k.shape[0], n_phys, page * C_HEADS, C_HD)
    cache_c_v = cache_C_v.reshape(cache_C_v.shape[0], n_phys, page * C_HEADS, C_HD)
    conv_t = jnp.transpose(state_B_conv, (0, 2, 1, 3))

    xp = x_prompt
    xs = x_sample.reshape(1, bs, d)
    new_p = [[] for _ in range(7)]
    new_s = [[] for _ in range(7)]

    for i in range(depth):
        j = i // 2
        mod_p = mod_all[i, :bp].reshape(bp, 1, N_MOD * d)
        mod_s = mod_all[i, bp:bp + bs].reshape(1, bs, N_MOD * d)
        xp = _ffn(xp, mod_p, 0, norm_w[i, 0], w1b, w2b, i, 0)
        xs = _ffn(xs, mod_s, 0, norm_w[i, 0], w1b, w2b, i, 0)
        if i % 2 == 0:
            q, k, v, qi, sm, qkv, z = _inproj(xp, mod_p, 3, norm_w[i, 1], ab_in_b[j], _EVEN_OUTS)
            o_a = _dsa_prompt(q, qi, sm, k, v, bias_a, topk_p)
            o_b, s_new = _gdn_prompt(qkv, z, sm, jnp.zeros((bp, B_HEADS, B_KEY_DIM, B_VAL_DIM), F32),
                                     gdn_conv_w[j], gdn_a_log[j], gdn_dt_bias[j], gdn_norm_w[j])
            xp = _outproj(xp, mod_p, 5, ab_out_b[j], [o_a, o_b])
            new_p[0].append(k.reshape(bp, tp, A_KV_HEADS, A_HEAD_DIM))
            new_p[1].append(v.reshape(bp, tp, A_KV_HEADS, A_HEAD_DIM))
            new_p[2].append(sm[:, :, SM_KI:SM_KI + IDX_DIM])
            new_p[3].append(s_new)
            new_p[4].append(qkv[:, tp - (CONV_W - 1):, :])
            q, k, v, qi, sm, qkv, z = _inproj(xs, mod_s, 3, norm_w[i, 1], ab_in_b[j], _EVEN_OUTS)
            as_rows = lambda a: a.reshape(bs, 1, a.shape[-1])
            scores = _idx_decode(as_rows(qi), as_rows(sm), cache_a_it, j, pt_flat, n_pages)
            scores = jnp.moveaxis(scores.reshape(bs, n_pages + 1, BLK), 1, 0)
            mask = _sel_decode(scores, past_len + 1, topk_s).reshape(n_pages + 1, bs, 1, BLK)
            o_a = _dsa_decode(as_rows(q), as_rows(k), as_rows(v), cache_a_kt, cache_a_vt, j, pt_flat, n_pages,
                              mask, bias_a_dec)
            o_b, s_new = _gdn_decode(qkv.reshape(bs, B_CONV_DIM), z.reshape(bs, B_V), sm.reshape(bs, LANES),
                                     conv_t, state_B_ssm, j, gdn_conv_w[j], gdn_a_log[j], gdn_dt_bias[j],
                                     gdn_norm_w[j])
            xs = _outproj(xs, mod_s, 5, ab_out_b[j], [o_a.reshape(1, bs, A_Q), o_b.reshape(1, bs, B_V)])
            new_s[0].append(k.reshape(bs, 1, A_KV_HEADS, A_HEAD_DIM))
            new_s[1].append(v.reshape(bs, 1, A_KV_HEADS, A_HEAD_DIM))
            new_s[2].append(sm.reshape(bs, 1, LANES)[:, :, SM_KI:SM_KI + IDX_DIM])
            new_s[3].append(s_new)
            new_s[4].append(jnp.concatenate([state_B_conv[j], as_rows(qkv)], axis=1)[:, 1:, :])
        else:
            lam_init = 0.8 - 0.6 * math.exp(-0.3 * i)
            lam = (jnp.exp(jnp.sum(c_lambda_q1[j] * c_lambda_k1[j]))
                   - jnp.exp(jnp.sum(c_lambda_q2[j] * c_lambda_k2[j])) + lam_init).astype(F32)
            k, v, qb16, kb16, vb16 = _inproj(xp, mod_p, 3, norm_w[i, 1], c_in_b[j], _ODD_OUTS_PROMPT)
            o = _diff_prompt(qb16, kb16, vb16, bias_c, lam, c_subln_w[j], lam_init)
            xp = _outproj(xp, mod_p, 5, c_out_b[j], [o])
            new_p[5].append(k.reshape(bp, tp, C_HEADS, C_HD))
            new_p[6].append(v.reshape(bp, tp, C_HEADS, C_HD))
            q, k, v = _inproj(xs, mod_s, 3, norm_w[i, 1], c_in_b[j], _ODD_OUTS)
            as_heads = lambda a: a.reshape(bs, C_HEADS, C_HD)
            o = _diff_decode(as_heads(q), as_heads(k), as_heads(v), cache_c_k, cache_c_v, j, pt_flat, n_pages,
                             bias_c_pages, bias_c_new, lam, c_subln_w[j], lam_init)
            xs = _outproj(xs, mod_s, 5, c_out_b[j], [o.reshape(1, bs, C_V)])
            new_s[5].append(k.reshape(bs, 1, C_HEADS, C_HD))
            new_s[6].append(v.reshape(bs, 1, C_HEADS, C_HD))
        xp = _ffn(xp, mod_p, 6, norm_w[i, 2], w1b, w2b, i, 1)
        xs = _ffn(xs, mod_s, 6, norm_w[i, 2], w1b, w2b, i, 1)

    y_prompt = _final_norm(xp, final_norm_w)
    y_sample = _final_norm(xs, final_norm_w).reshape(bs, 1, d)
    sp = [jnp.stack(lst) for lst in new_p]
    ss = [jnp.stack(lst) for lst in new_s]
    return (y_prompt, y_sample, *sp, *ss)
```

```python
import functools
import math

import numpy as np
import jax
import jax.numpy as jnp
from jax import lax
from jax.experimental import pallas as pl
from jax.experimental.pallas import tpu as pltpu

F32 = jnp.float32
BF16 = jnp.bfloat16
I32 = jnp.int32

A_HEADS = 8
A_KV_HEADS = 2
A_GROUP = A_HEADS // A_KV_HEADS
A_HEAD_DIM = 64
IDX_HEADS = 4
IDX_DIM = 64
TOPK_MAX = 256
B_HEADS = 4
B_KEY_DIM = 128
B_VAL_DIM = 128
CONV_W = 4
CHUNK = 64
C_HEADS = 8
C_HEAD_DIM = 64
NUM_BUCKETS = 32
MAX_DISTANCE = 128
N_MOD = 9
EPS = 1e-6

A_Q = A_HEADS * A_HEAD_DIM
A_KV = A_KV_HEADS * A_HEAD_DIM
B_K = B_HEADS * B_KEY_DIM
B_V = B_HEADS * B_VAL_DIM
B_CONV_DIM = 2 * B_K + B_V
C_QK = C_HEADS * 2 * C_HEAD_DIM
C_V = C_HEADS * 2 * C_HEAD_DIM
C_HD = 2 * C_HEAD_DIM

LANES = 128
SUBLANES = 8
VMEM_LIMIT = 56 * 1024 * 1024

BLK = 128
TQ = 256
NEG = -1e30

SM_KI = 0
SM_WI = IDX_DIM
SM_A = SM_WI + IDX_HEADS
SM_B = SM_A + B_HEADS

_NT = (((1,), (1,)), ((), ()))
_NN = (((1,), (0,)), ((), ()))
_TN = (((0,), (0,)), ((), ()))


def _cparams(sem):
    return pltpu.CompilerParams(dimension_semantics=sem, vmem_limit_bytes=VMEM_LIMIT)


def _dg(a, b, dims=_NN):
    return lax.dot_general(a, b, dims, preferred_element_type=F32)


def _dot1(a, b, dims=_NN):
    return _dg(a.astype(BF16), b.astype(BF16), dims)


def _split2(x):
    hi = x.astype(BF16)
    lo = (x - hi.astype(F32)).astype(BF16)
    return hi, lo


def _split3(x):
    b1 = x.astype(BF16)
    r1 = x - b1.astype(F32)
    b2 = r1.astype(BF16)
    b3 = (r1 - b2.astype(F32)).astype(BF16)
    return b1, b2, b3


def _dot3(a, b, dims=_NN):
    ah, al = _split2(a)
    bh, bl = _split2(b)
    return _dg(ah, bh, dims) + (_dg(ah, bl, dims) + _dg(al, bh, dims))


def _dot_exact_lhs(a_bf, b, dims=_NN):
    b1, b2, b3 = _split3(b)
    return _dg(a_bf, b1, dims) + (_dg(a_bf, b2, dims) + _dg(a_bf, b3, dims))


def _dot_exact_rhs(a, b_bf, dims=_NN):
    a1, a2, a3 = _split3(a)
    return _dg(a1, b_bf, dims) + (_dg(a2, b_bf, dims) + _dg(a3, b_bf, dims))


def _silu(x):
    return x * jax.nn.sigmoid(x)


def _softplus(x):
    return jnp.maximum(x, 0.0) + jnp.log(1.0 + jnp.exp(-jnp.abs(x)))


def _norm_mod(x, nw, sc, sh):
    ms = jnp.mean(x * x, axis=-1, keepdims=True)
    return (x * lax.rsqrt(ms + EPS) * nw) * (1.0 + sc) + sh


def _softmax_step(s, m, l, acc, pv):
    m_new = jnp.maximum(m, jnp.max(s, axis=1, keepdims=True))
    alpha = jnp.exp(m - m_new)
    p = jnp.exp(s - m_new)
    return m_new, alpha * l + jnp.sum(p, axis=1, keepdims=True), alpha * acc + pv(p)


def _mod_kernel(c_ref, w_ref, b_ref, o_ref):
    s = _silu(c_ref[...]).astype(BF16)
    o_ref[0] = _dg(s, w_ref[0].astype(BF16)) + b_ref[0]


def _modulation(c_all, ada_w, ada_b):
    depth, d, n = ada_w.shape
    m = c_all.shape[0]
    tn = 1024
    return pl.pallas_call(
        _mod_kernel,
        grid=(depth, n // tn),
        in_specs=[pl.BlockSpec((m, d), lambda i, j: (0, 0)),
                  pl.BlockSpec((1, d, tn), lambda i, j: (i, 0, j)),
                  pl.BlockSpec((1, 1, tn), lambda i, j: (i, 0, j))],
        out_specs=pl.BlockSpec((1, m, tn), lambda i, j: (i, 0, j)),
        out_shape=jax.ShapeDtypeStruct((depth, m, n), F32),
        compiler_params=_cparams(("arbitrary", "arbitrary")),
        name="adaln_mod",
    )(c_all, ada_w, ada_b.reshape(depth, 1, n))


def _row_tile(t):
    return min(512, t)


def _mod_spec(mod, tm, m):
    r = mod.shape[1]
    d = mod.shape[2] // N_MOD
    if r == 1:
        return pl.BlockSpec((1, 1, d), lambda s, t: (s, 0, m))
    return pl.BlockSpec((1, tm, d), lambda s, t: (s, t, m))


def _resident(shape):
    nd = len(shape)
    return pl.BlockSpec(shape, lambda s, t: (0,) * nd, pipeline_mode=pl.Buffered(1))


def _ffn_kernel(x_ref, sh_ref, sc_ref, g_ref, nw_ref, w1_ref, w2_ref, o_ref, acc_ref, *, fc):
    x = x_ref[0]
    hb = _norm_mod(x, nw_ref[...], sc_ref[0], sh_ref[0]).astype(BF16)
    f = w2_ref.shape[2]
    for c in range(f // fc):
        gt = _dg(hb, w1_ref[0, 0, :, c * fc:(c + 1) * fc])
        up = _dg(hb, w1_ref[0, 0, :, f + c * fc:f + (c + 1) * fc])
        a = (_silu(gt) * up).astype(BF16)
        contrib = _dg(a, w2_ref[0, 0, c * fc:(c + 1) * fc, :])
        if c == 0:
            acc_ref[...] = contrib
        else:
            acc_ref[...] += contrib
    o_ref[0] = x + (0.5 * g_ref[0]) * acc_ref[...]


def _ffn(x, mod, m0, nw, w1b, w2b, layer, which):
    s, t, d = x.shape
    tm = _row_tile(t)
    f = w2b.shape[2]
    wspec = lambda shp: pl.BlockSpec((1, 1) + tuple(shp[2:]), lambda si, ti: (layer, which, 0, 0),
                                     pipeline_mode=pl.Buffered(1))
    fc = 256 if f % 256 == 0 else LANES
    xs = pl.BlockSpec((1, tm, d), lambda si, ti: (si, ti, 0))
    return pl.pallas_call(
        functools.partial(_ffn_kernel, fc=fc),
        grid=(s, t // tm),
        in_specs=[xs, _mod_spec(mod, tm, m0), _mod_spec(mod, tm, m0 + 1), _mod_spec(mod, tm, m0 + 2),
                  _resident((1, d)), wspec(w1b.shape), wspec(w2b.shape)],
        out_specs=xs,
        out_shape=jax.ShapeDtypeStruct(x.shape, F32),
        scratch_shapes=[pltpu.VMEM((tm, d), F32)],
        compiler_params=_cparams(("arbitrary", "arbitrary")),
        name="ffn",
    )(x, mod, mod, mod, nw.reshape(1, d), w1b, w2b)


def _inproj_kernel(x_ref, sh_ref, sc_ref, nw_ref, w_ref, *o_refs, outs):
    hb = _norm_mod(x_ref[0], nw_ref[...], sc_ref[0], sh_ref[0]).astype(BF16)
    done = {}
    for o_ref, (off, wd, dt) in zip(o_refs, outs):
        if (off, wd) not in done:
            done[(off, wd)] = _dg(hb, w_ref[:, off:off + wd])
        o_ref[0] = done[(off, wd)].astype(dt)


def _inproj(x, mod, m0, nw, wb, outs):
    s, t, d = x.shape
    tm = _row_tile(t)
    xs = pl.BlockSpec((1, tm, d), lambda si, ti: (si, ti, 0))
    return pl.pallas_call(
        functools.partial(_inproj_kernel, outs=tuple(outs)),
        grid=(s, t // tm),
        in_specs=[xs, _mod_spec(mod, tm, m0), _mod_spec(mod, tm, m0 + 1), _resident((1, d)), _resident(wb.shape)],
        out_specs=[pl.BlockSpec((1, tm, wd), lambda si, ti: (si, ti, 0)) for _, wd, _ in outs],
        out_shape=[jax.ShapeDtypeStruct((s, t, wd), dt) for _, wd, dt in outs],
        compiler_params=_cparams(("arbitrary", "arbitrary")),
        name="inproj",
    )(x, mod, mod, nw.reshape(1, d), wb)


def _outproj_kernel(*refs, widths):
    n = len(widths)
    x_ref, g_ref, w_ref = refs[0], refs[1], refs[2]
    a_refs = refs[3:3 + n]
    o_ref = refs[3 + n]
    acc = None
    off = 0
    for a_ref, wd in zip(a_refs, widths):
        part = _dg(a_ref[0].astype(BF16), w_ref[off:off + wd, :])
        acc = part if acc is None else acc + part
        off += wd
    o_ref[0] = x_ref[0] + g_ref[0] * acc


def _outproj(x, mod, mg, wb, parts):
    s, t, d = x.shape
    tm = _row_tile(t)
    widths = tuple(p.shape[-1] for p in parts)
    xs = pl.BlockSpec((1, tm, d), lambda si, ti: (si, ti, 0))
    return pl.pallas_call(
        functools.partial(_outproj_kernel, widths=widths),
        grid=(s, t // tm),
        in_specs=[xs, _mod_spec(mod, tm, mg), _resident(wb.shape)]
                 + [pl.BlockSpec((1, tm, wd), lambda si, ti: (si, ti, 0)) for wd in widths],
        out_specs=xs,
        out_shape=jax.ShapeDtypeStruct(x.shape, F32),
        compiler_params=_cparams(("arbitrary", "arbitrary")),
        name="outproj",
    )(x, mod, wb, *parts)


def _final_norm_kernel(x_ref, w_ref, o_ref):
    x = x_ref[0]
    ms = jnp.mean(x * x, axis=-1, keepdims=True)
    o_ref[0] = x * lax.rsqrt(ms + EPS) * w_ref[...]


def _final_norm(x, w):
    s, t, d = x.shape
    tm = _row_tile(t)
    xs = pl.BlockSpec((1, tm, d), lambda si, ti: (si, ti, 0))
    return pl.pallas_call(
        _final_norm_kernel,
        grid=(s, t // tm),
        in_specs=[xs, pl.BlockSpec((1, d), lambda si, ti: (0, 0))],
        out_specs=xs,
        out_shape=jax.ShapeDtypeStruct(x.shape, F32),
        compiler_params=_cparams(("arbitrary", "arbitrary")),
        name="final_norm",
    )(x, w.reshape(1, d))


def _t5_bucket(dist):
    n = jnp.maximum(dist, 0)
    max_exact = NUM_BUCKETS // 2
    nf = jnp.maximum(n, 1).astype(F32)
    large = max_exact + (jnp.log(nf / max_exact) / math.log(MAX_DISTANCE / max_exact)
                         * (NUM_BUCKETS - max_exact)).astype(I32)
    large = jnp.minimum(large, NUM_BUCKETS - 1)
    return jnp.where(n < max_exact, n, large)


def _bias_tiles(bias_cols, blk, causal):
    assert blk >= MAX_DISTANCE
    n = blk
    h = bias_cols.shape[1]
    tiles = []
    for t in range(3):
        off = (2 - t) * n
        if t == 0:
            far = bias_cols[_t5_bucket(jnp.full((), 2 * n, I32))]
            tiles.append(jnp.broadcast_to(far[:, None, None], (h, n, n)))
            continue
        dist = n - 1 + off - jnp.arange(2 * n, dtype=I32)
        g = bias_cols[_t5_bucket(dist)]
        if causal:
            g = jnp.where((dist >= 0)[:, None], g, NEG)
        x = jnp.broadcast_to(g.T[:, None, :], (h, n, 2 * n))
        x = jnp.pad(x, ((0, 0), (0, 0), (0, 1))).reshape(h, n * (2 * n + 1))[:, :2 * n * n].reshape(h, n, 2 * n)
        tiles.append(x[:, :, n - 1:2 * n - 1])
    return jnp.stack(tiles, axis=1)


def _bias_decode(bias_cols, q_pos, n_blocks):
    k_pos = jnp.arange(n_blocks * BLK, dtype=I32)
    b = bias_cols[_t5_bucket(q_pos - k_pos)]
    b = jnp.where((k_pos <= q_pos)[:, None], b, NEG)
    return jnp.moveaxis(b.reshape(n_blocks, BLK, -1), 2, 1)


def _score_keys(s):
    bits = pltpu.bitcast(s, I32)
    return jnp.where(bits < 0, bits ^ jnp.int32(0x7FFFFFFF), bits)


def _select_topk(key_ref, nkb, topk, idx_bits, key_axis=1, pairs=False):
    tile = tuple(key_ref.shape[1:])
    assert tile[key_axis] == BLK
    stat = tuple(1 if a == key_axis else n for a, n in enumerate(tile))
    rows = stat
    col = lax.broadcasted_iota(I32, tile, key_axis)

    def count(pred):
        if pairs:
            def body(i, acc):
                kb = 2 * i
                return (acc + jnp.where(pred(key_ref[kb], kb * BLK + col), 1.0, 0.0)
                        + jnp.where(pred(key_ref[kb + 1], (kb + 1) * BLK + col), 1.0, 0.0))
            acc = lax.fori_loop(0, (nkb + 1) // 2, body, jnp.zeros(tile, F32))
        else:
            def body(kb, acc):
                return acc + jnp.where(pred(key_ref[kb], kb * BLK + col), 1.0, 0.0)
            acc = lax.fori_loop(0, nkb, body, jnp.zeros(tile, F32))
        return jnp.sum(acc, axis=key_axis, keepdims=True)

    kf = float(topk)

    def bit_body(i, carry):
        lo, cnt_lo = carry
        cand = lo + lax.shift_left(jnp.int32(1), jnp.int32(31) - i)
        cnt = count(lambda k, _: k >= cand)
        take = cnt >= kf
        return jnp.where(take, cand, lo), jnp.where(take, cnt, cnt_lo)

    n_visited = 2 * ((nkb + 1) // 2) if pairs else nkb
    total = jnp.zeros(rows, F32) + jnp.asarray(n_visited * BLK, F32)
    thr, cnt_ge = lax.fori_loop(0, 32, bit_body, (jnp.full(rows, -2 ** 31, I32), total))
    need = kf - count(lambda k, _: k > thr)

    has_ties = jnp.max(cnt_ge) > kf

    def idx_body(i, p):
        cand = p + lax.shift_left(jnp.int32(1), jnp.int32(idx_bits - 1) - i)
        cnt = count(lambda k, ix: jnp.logical_and(k == thr, ix < cand))
        return jnp.where(cnt < need, cand, p)

    cut0 = jnp.zeros(rows, I32) + jnp.where(has_ties, 0, 2 ** idx_bits - 1)
    cut = lax.fori_loop(0, jnp.where(has_ties, idx_bits, 0), idx_body, cut0)
    return thr, cut


def _in_topk(key, idx, thr, cut):
    return jnp.logical_or(key > thr, jnp.logical_and(key == thr, idx <= cut))


DSA_CHAIN_HEADS = 4


def _dsa_kernel(q_ref, qi_ref, sm_ref, kf_ref, vf_ref, smf_ref, bias_ref, o_ref, key_ref, msk_ref,
                *, topk, idx_bits):
    qb = pl.program_id(1)
    nkb = qb + 1
    key_i = lax.broadcasted_iota(I32, (BLK, BLK), 0)
    q_pos = qb * BLK + lax.broadcasted_iota(I32, (BLK, BLK), 1)

    qi = qi_ref[0]
    sm_t = sm_ref[0].T
    qi_st = jnp.concatenate([qi[:, h * IDX_DIM:(h + 1) * IDX_DIM] for h in range(IDX_HEADS)], axis=0).astype(BF16)
    wi_rows = [sm_t[SM_WI + h:SM_WI + h + 1, :] * (IDX_HEADS ** -0.5) for h in range(IDX_HEADS)]

    def score_body(i, carry):
        kbs = (2 * i, 2 * i + 1)
        kis = [smf_ref[0, pl.ds(pl.multiple_of(kb * BLK, BLK), BLK), :][:, SM_KI:SM_KI + IDX_DIM].astype(BF16)
               for kb in kbs]
        rs = [jnp.maximum(_dg(ki, qi_st, _NT) * (IDX_DIM ** -0.5), 0.0) for ki in kis]
        for kb, r in zip(kbs, rs):
            s = wi_rows[0] * r[:, 0:BLK]
            for h in range(1, IDX_HEADS):
                s = s + wi_rows[h] * r[:, h * BLK:(h + 1) * BLK]
            s = jnp.where(kb * BLK + key_i <= q_pos, s, -jnp.inf)
            key_ref[kb] = _score_keys(s)
        return carry

    lax.fori_loop(0, (nkb + 1) // 2, score_body, 0)

    @pl.when(nkb * BLK <= topk)
    def _():
        def body(kb, carry):
            msk_ref[kb] = jnp.where(kb * BLK + key_i <= q_pos, 0.0, NEG)
            return carry
        lax.fori_loop(0, nkb, body, 0)

    @pl.when(nkb * BLK > topk)
    def _():
        thr, cut = _select_topk(key_ref, nkb, topk, idx_bits, key_axis=0, pairs=True)

        def body(kb, carry):
            k_pos = kb * BLK + key_i
            sel = _in_topk(key_ref[kb], k_pos, thr, cut)
            msk_ref[kb] = jnp.where(k_pos <= q_pos, jnp.where(sel, 0.0, NEG), NEG)
            return carry
        lax.fori_loop(0, nkb, body, 0)

    q = q_ref[0] * (A_HEAD_DIM ** -0.5)
    ch = DSA_CHAIN_HEADS
    cq = ch * BLK
    chains = [(h0 // A_GROUP, h0) for h0 in range(0, A_HEADS, ch)]
    cs = range(len(chains))
    qs = [jnp.concatenate([q[:, (h0 + g) * A_HEAD_DIM:(h0 + g + 1) * A_HEAD_DIM] for g in range(ch)],
                          axis=0).astype(BF16) for _, h0 in chains]

    def logits(kb):
        off = pl.multiple_of(kb * BLK, BLK)
        kblk = kf_ref[0, pl.ds(off, BLK), :].astype(BF16)
        t = jnp.clip(kb - qb + 2, 0, 2)
        mkc = jnp.concatenate([msk_ref[kb]] * ch, axis=1)
        out = []
        for c, (n, h0) in enumerate(chains):
            g0 = h0 - n * A_GROUP
            bias = bias_ref[n, t][:, g0 * BLK:(g0 + ch) * BLK]
            out.append(_dg(kblk[:, n * A_HEAD_DIM:(n + 1) * A_HEAD_DIM], qs[c], _NT) + bias + mkc)
        return out

    def att_body(kb, carry):
        ss = carry[3 * len(chains):]
        ss_next = logits(jnp.minimum(kb + 1, nkb - 1))
        off = pl.multiple_of(kb * BLK, BLK)
        vblk = vf_ref[0, pl.ds(off, BLK), :].astype(BF16)
        m_new = [jnp.maximum(carry[3 * c], jnp.max(ss[c], axis=0, keepdims=True)) for c in cs]
        ps = [jnp.exp(ss[c] - m_new[c]) for c in cs]
        pvs = [_dg(vblk[:, chains[c][0] * A_HEAD_DIM:(chains[c][0] + 1) * A_HEAD_DIM], ps[c].astype(BF16), _TN)
               for c in cs]
        out = []
        for c in cs:
            alpha = jnp.exp(carry[3 * c] - m_new[c])
            out += [m_new[c], alpha * carry[3 * c + 1] + jnp.sum(ps[c], axis=0, keepdims=True),
                    alpha * carry[3 * c + 2] + pvs[c]]
        return tuple(out) + tuple(ss_next)

    init = (jnp.full((1, cq), NEG, F32), jnp.zeros((1, cq), F32), jnp.zeros((A_HEAD_DIM, cq), F32)) * len(chains)
    res = lax.fori_loop(0, nkb, att_body, init + tuple(logits(0)))
    for c, (n, h0) in enumerate(chains):
        o_t = res[3 * c + 2] / res[3 * c + 1]
        for g in range(0, ch, 2):
            pair = jnp.concatenate([o_t[:, g * BLK:(g + 1) * BLK], o_t[:, (g + 1) * BLK:(g + 2) * BLK]], axis=0)
            o_ref[0, :, (h0 + g) * A_HEAD_DIM:(h0 + g + 2) * A_HEAD_DIM] = pair.T


def _dsa_prompt(q, qi, sm, k, v, bias_tiles, topk):
    b, t, _ = q.shape
    nb = t // BLK
    assert nb % 2 == 0
    idx_bits = max(1, int(math.ceil(math.log2(t))))
    bt = bias_tiles.reshape(A_KV_HEADS, A_GROUP, 3, BLK, BLK)
    bt = jnp.transpose(bt, (0, 2, 4, 1, 3)).reshape(A_KV_HEADS, 3, BLK, A_GROUP * BLK)
    blk = lambda w: pl.BlockSpec((1, BLK, w), lambda bi, qb: (bi, qb, 0))
    full = lambda w: pl.BlockSpec((1, t, w), lambda bi, qb: (bi, 0, 0))
    return pl.pallas_call(
        functools.partial(_dsa_kernel, topk=topk, idx_bits=idx_bits),
        grid=(b, nb),
        in_specs=[blk(A_Q), blk(IDX_HEADS * IDX_DIM), blk(LANES), full(A_KV), full(A_KV), full(LANES),
                  pl.BlockSpec(bt.shape, lambda bi, qb: (0, 0, 0, 0))],
        out_specs=blk(A_Q),
        out_shape=jax.ShapeDtypeStruct((b, t, A_Q), F32),
        scratch_shapes=[pltpu.VMEM((nb, BLK, BLK), I32), pltpu.VMEM((nb, BLK, BLK), F32)],
        compiler_params=_cparams(("arbitrary", "arbitrary")),
        name="dsa_prompt",
    )(q, qi, sm, k, v, sm, bt)


def _gdn_kernel(qkv_ref, z_ref, sm_ref, s0_ref, cw_ref, par_ref, nw_ref, o_ref, sout_ref, ext_ref, st_ref):
    t = pl.program_id(1)
    nt = pl.num_programs(1)
    c = CHUNK
    halo = SUBLANES

    nbb = qkv_ref.shape[0]
    bbs = range(nbb)

    @pl.when(t == 0)
    def _():
        for bb in bbs:
            ext_ref[bb, 0:halo, :] = jnp.zeros((halo, B_CONV_DIM), F32)
        st_ref[...] = s0_ref[...]

    convs = []
    for bb in bbs:
        x = qkv_ref[bb]
        ext_ref[bb, halo:halo + c, :] = x
        conv = None
        for j in range(CONV_W):
            start = halo - (CONV_W - 1) + j
            term = ext_ref[bb, start:start + c, :] * cw_ref[j:j + 1, :]
            conv = term if conv is None else conv + term
        ext_ref[bb, 0:halo, :] = x[c - halo:c, :]
        convs.append(_silu(conv))

    ri = lax.broadcasted_iota(I32, (c, c), 0)
    ci = lax.broadcasted_iota(I32, (c, c), 1)
    tri = ri >= ci
    stri = ri > ci
    eye = jnp.where(ri == ci, 1.0, 0.0)
    tri_bf = jnp.where(tri, 1.0, 0.0).astype(BF16)
    ones_bf = jnp.ones((c, c), BF16)

    sms = [sm_ref[bb] for bb in bbs]
    g_alls = [-jnp.exp(par_ref[0:1, :]) * _softplus(sm + par_ref[1:2, :]) for sm in sms]
    gc_alls = [_dot_exact_lhs(tri_bf, g) for g in g_alls]
    beta_alls = [jax.nn.sigmoid(sm) for sm in sms]
    nw = nw_ref[...]
    items = [(bb, h) for bb in bbs for h in range(B_HEADS)]
    hs = range(len(items))
    gcs = [gc_alls[bb][:, SM_A + h:SM_A + h + 1] for bb, h in items]
    gc_rows = [_dot_exact_lhs(ones_bf, jnp.concatenate([eye * gcs[bb * B_HEADS + h] for h in range(B_HEADS)],
                                                       axis=1)) for bb in bbs]
    qs = [convs[bb][:, h * B_KEY_DIM:(h + 1) * B_KEY_DIM] for bb, h in items]
    ks = [convs[bb][:, B_K + h * B_KEY_DIM:B_K + (h + 1) * B_KEY_DIM] for bb, h in items]
    vs = [convs[bb][:, 2 * B_K + h * B_VAL_DIM:2 * B_K + (h + 1) * B_VAL_DIM] for bb, h in items]
    qs = [q * lax.rsqrt(jnp.sum(q * q, axis=-1, keepdims=True) + EPS) * (B_KEY_DIM ** -0.5) for q in qs]
    ks = [k * lax.rsqrt(jnp.sum(k * k, axis=-1, keepdims=True) + EPS) for k in ks]
    betas = [beta_alls[bb][:, SM_B + h:SM_B + h + 1] for bb, h in items]
    decays = [jnp.exp(jnp.where(tri, gcs[i] - gc_rows[bb][:, h * c:(h + 1) * c], -jnp.inf))
              for i, (bb, h) in enumerate(items)]
    kbs = [ks[h] * betas[h] for h in hs]
    vbs = [vs[h] * betas[h] for h in hs]
    k_bf = [k.astype(BF16) for k in ks]
    a_mats = [jnp.where(stri, _dg(kbs[h].astype(BF16), k_bf[h], _NT) * decays[h], 0.0) for h in hs]
    tms = [eye - a for a in a_mats]
    pws = [_dot3(a, a) for a in a_mats]
    steps = int(math.log2(c))
    for j in range(1, steps):
        tms = [tm + _dot3(tm, pw) for tm, pw in zip(tms, pws)]
        if j < steps - 1:
            pws = [_dot3(pw, pw) for pw in pws]
    egs = [jnp.exp(gc) for gc in gcs]
    tm_bf = [tm.astype(BF16) for tm in tms]
    us = [_dg(tm_bf[h], vbs[h].astype(BF16)) for h in hs]
    ws = [_dg(tm_bf[h], (kbs[h] * egs[h]).astype(BF16)) for h in hs]
    a_qks = [jnp.where(tri, _dg(qs[h].astype(BF16), k_bf[h], _NT) * decays[h], 0.0) for h in hs]

    s_olds = [st_ref[bb, h] for bb, h in items]
    s_bf = [s.astype(BF16) for s in s_olds]
    v_news = [us[i] - _dg(ws[i].astype(BF16), s_bf[i]) for i in hs]
    os_ = [_dg((qs[i] * egs[i]).astype(BF16), s_bf[i]) + _dot1(a_qks[i], v_news[i]) for i in hs]
    g_lasts = [gc[c - 1:c, :] for gc in gcs]
    for i, (bb, h) in enumerate(items):
        st_ref[bb, h] = (s_olds[i] * jnp.exp(g_lasts[i])
                         + _dot1(ks[i] * jnp.exp(g_lasts[i] - gcs[i]), v_news[i], _TN))
    for i, (bb, h) in enumerate(items):
        o = os_[i]
        ms = jnp.mean(o * o, axis=-1, keepdims=True)
        zh = z_ref[bb, :, h * B_VAL_DIM:(h + 1) * B_VAL_DIM]
        o_ref[bb, :, h * B_VAL_DIM:(h + 1) * B_VAL_DIM] = (o * lax.rsqrt(ms + EPS) * nw) * _silu(zh)

    @pl.when(t == nt - 1)
    def _():
        sout_ref[...] = st_ref[...]


GDN_SEQS_PER_STEP = 4


def _gdn_params(a_log, dt_bias):
    par = jnp.zeros((SUBLANES, LANES), F32)
    return par.at[0, SM_A:SM_A + B_HEADS].set(a_log).at[1, SM_A:SM_A + B_HEADS].set(dt_bias)


def _gdn_prompt(qkv, z, sm, s0, conv_w, a_log, dt_bias, norm_w):
    b, t, _ = qkv.shape
    nt = t // CHUNK
    nbb = GDN_SEQS_PER_STEP if b % GDN_SEQS_PER_STEP == 0 else 1
    tok = lambda w: pl.BlockSpec((nbb, CHUNK, w), lambda bi, ti: (bi, ti, 0))
    const2 = lambda shp: pl.BlockSpec(shp, lambda bi, ti: (0, 0))
    st_spec = pl.BlockSpec((nbb, B_HEADS, B_KEY_DIM, B_VAL_DIM), lambda bi, ti: (bi, 0, 0, 0))
    o, s_out = pl.pallas_call(
        _gdn_kernel,
        grid=(b // nbb, nt),
        in_specs=[tok(B_CONV_DIM), tok(B_V), tok(LANES), st_spec,
                  const2((CONV_W, B_CONV_DIM)), const2((SUBLANES, LANES)), const2((1, B_VAL_DIM))],
        out_specs=[tok(B_V), st_spec],
        out_shape=[jax.ShapeDtypeStruct((b, t, B_V), F32),
                   jax.ShapeDtypeStruct((b, B_HEADS, B_KEY_DIM, B_VAL_DIM), F32)],
        scratch_shapes=[pltpu.VMEM((nbb, SUBLANES + CHUNK, B_CONV_DIM), F32),
                        pltpu.VMEM((nbb, B_HEADS, B_KEY_DIM, B_VAL_DIM), F32)],
        compiler_params=_cparams(("arbitrary", "arbitrary")),
        name="gdn_prompt",
    )(qkv, z, sm, s0, conv_w, _gdn_params(a_log, dt_bias), norm_w.reshape(1, B_VAL_DIM))
    return o, s_out


GDN_DEC_ROWS = 16


def _gdn_dec_kernel(qkv_ref, z_ref, sm_ref, cb_ref, s_ref, cw_ref, par_ref, nw_ref, o_ref, so_ref, oraw_ref):
    r = GDN_DEC_ROWS
    conv = qkv_ref[...] * cw_ref[CONV_W - 1:CONV_W, :]
    for j in range(CONV_W - 1):
        conv = conv + cb_ref[0, j] * cw_ref[j:j + 1, :]
    conv = _silu(conv)
    sm = sm_ref[...]
    g_all = -jnp.exp(par_ref[0:1, :]) * _softplus(sm + par_ref[1:2, :])
    beta_all = jax.nn.sigmoid(sm)
    ri = lax.broadcasted_iota(I32, (r, LANES), 0)
    ci = lax.broadcasted_iota(I32, (r, LANES), 1)
    eye_bf = jnp.where(ri == ci, 1.0, 0.0).astype(BF16)

    for h in range(B_HEADS):
        qh = conv[:, h * B_KEY_DIM:(h + 1) * B_KEY_DIM]
        kh = conv[:, B_K + h * B_KEY_DIM:B_K + (h + 1) * B_KEY_DIM]
        vh = conv[:, 2 * B_K + h * B_VAL_DIM:2 * B_K + (h + 1) * B_VAL_DIM]
        qh = qh * lax.rsqrt(jnp.sum(qh * qh, axis=-1, keepdims=True) + EPS) * (B_KEY_DIM ** -0.5)
        kh = kh * lax.rsqrt(jnp.sum(kh * kh, axis=-1, keepdims=True) + EPS)
        eg = jnp.exp(g_all[:, SM_A + h:SM_A + h + 1])
        beta = beta_all[:, SM_B + h:SM_B + h + 1]
        qk = jnp.sum(qh * kh, axis=-1, keepdims=True)
        k_t = _dot_exact_rhs(kh, eye_bf, _TN)
        q_t = _dot_exact_rhs(qh, eye_bf, _TN)
        for s in range(r):
            st = s_ref[0, s, h]
            kc = k_t[:, s:s + 1]
            qc = q_t[:, s:s + 1]
            k_s = jnp.sum(st * kc, axis=0, keepdims=True)
            q_s = jnp.sum(st * qc, axis=0, keepdims=True)
            eg_s = eg[s:s + 1, :]
            v_new = beta[s:s + 1, :] * (vh[s:s + 1, :] - eg_s * k_s)
            oraw_ref[s:s + 1, h * B_VAL_DIM:(h + 1) * B_VAL_DIM] = eg_s * q_s + qk[s:s + 1, :] * v_new
            so_ref[s, h] = st * eg_s + kc * v_new

    zz = z_ref[...]
    nw = nw_ref[...]
    for h in range(B_HEADS):
        o = oraw_ref[:, h * B_VAL_DIM:(h + 1) * B_VAL_DIM]
        ms = jnp.mean(o * o, axis=-1, keepdims=True)
        zh = zz[:, h * B_VAL_DIM:(h + 1) * B_VAL_DIM]
        o_ref[:, h * B_VAL_DIM:(h + 1) * B_VAL_DIM] = (o * lax.rsqrt(ms + EPS) * nw) * _silu(zh)


def _gdn_decode(qkv, z, sm, conv_t, state, j, conv_w, a_log, dt_bias, norm_w):
    b = qkv.shape[0]
    r = GDN_DEC_ROWS
    assert b % r == 0
    rows = lambda w: pl.BlockSpec((r, w), lambda i: (i, 0))
    const2 = lambda shp: pl.BlockSpec(shp, lambda i: (0, 0))
    o, s_out = pl.pallas_call(
        _gdn_dec_kernel,
        grid=(b // r,),
        in_specs=[rows(B_CONV_DIM), rows(B_V), rows(LANES),
                  pl.BlockSpec((1, CONV_W - 1, r, B_CONV_DIM), lambda i: (j, 0, i, 0)),
                  pl.BlockSpec((1, r, B_HEADS, B_KEY_DIM, B_VAL_DIM), lambda i: (j, i, 0, 0, 0)),
                  const2((CONV_W, B_CONV_DIM)), const2((SUBLANES, LANES)), const2((1, B_VAL_DIM))],
        out_specs=[rows(B_V), pl.BlockSpec((r, B_HEADS, B_KEY_DIM, B_VAL_DIM), lambda i: (i, 0, 0, 0))],
        out_shape=[jax.ShapeDtypeStruct((b, B_V), F32),
                   jax.ShapeDtypeStruct((b, B_HEADS, B_KEY_DIM, B_VAL_DIM), F32)],
        scratch_shapes=[pltpu.VMEM((r, B_V), F32)],
        compiler_params=_cparams(("arbitrary",)),
        name="gdn_decode",
    )(qkv, z, sm, conv_t, state, conv_w, _gdn_params(a_log, dt_bias), norm_w.reshape(1, B_VAL_DIM))
    return o, s_out


DIFF_HEADS_PER_STEP = 8


def _diff_kernel(lam_ref, q_ref, k_ref, v_ref, bias_ref, sw_ref, o_ref, *, lam_init):
    qb = pl.program_id(2)
    nkb = qb + 1
    nh = DIFF_HEADS_PER_STEP
    lane = lax.broadcasted_iota(I32, (TQ, C_HD), 1)
    q2 = []
    for h in range(nh):
        qh = q_ref[0, :, h * C_HD:(h + 1) * C_HD] * (C_HEAD_DIM ** -0.5)
        zero = jnp.zeros_like(qh)
        q2.append(jnp.concatenate([jnp.where(lane < C_HEAD_DIM, qh, zero),
                                   jnp.where(lane >= C_HEAD_DIM, qh, zero)], axis=0))

    hs = range(nh)

    def body(kb, carry):
        off = pl.multiple_of(kb * TQ, TQ)
        t = jnp.clip(kb - qb + 2, 0, 2)
        ss = [_dg(k_ref[0, pl.ds(off, TQ), h * C_HD:(h + 1) * C_HD], q2[h], _NT) for h in hs]
        ss = [ss[h] + jnp.concatenate([bias_ref[h, t]] * 2, axis=1) for h in hs]
        m_new = [jnp.maximum(carry[3 * h], jnp.max(ss[h], axis=0, keepdims=True)) for h in hs]
        ps = [jnp.exp(ss[h] - m_new[h]) for h in hs]
        pvs = [_dg(v_ref[0, pl.ds(off, TQ), h * C_HD:(h + 1) * C_HD], ps[h].astype(BF16), _TN) for h in hs]
        out = []
        for h in hs:
            alpha = jnp.exp(carry[3 * h] - m_new[h])
            out += [m_new[h], alpha * carry[3 * h + 1] + jnp.sum(ps[h], axis=0, keepdims=True),
                    alpha * carry[3 * h + 2] + pvs[h]]
        return tuple(out)

    init = (jnp.full((1, 2 * TQ), NEG, F32), jnp.zeros((1, 2 * TQ), F32), jnp.zeros((C_HD, 2 * TQ), F32)) * nh
    res = lax.fori_loop(0, nkb, body, init)
    for h in range(nh):
        _, l, acc = res[3 * h:3 * h + 3]
        on = acc / l
        o = (on[:, 0:TQ] - lam_ref[0] * on[:, TQ:2 * TQ]).T
        ms = jnp.mean(o * o, axis=-1, keepdims=True)
        o_ref[0, :, h * C_HD:(h + 1) * C_HD] = (o * lax.rsqrt(ms + EPS) * sw_ref[...]) * (1.0 - lam_init)


def _diff_prompt(q, k, v, bias_tiles, lam, subln_w, lam_init):
    b, t, _ = q.shape
    nh = DIFF_HEADS_PER_STEP
    w = nh * C_HD
    return pl.pallas_call(
        functools.partial(_diff_kernel, lam_init=lam_init),
        grid=(b, C_HEADS // nh, t // TQ),
        in_specs=[pl.BlockSpec(memory_space=pltpu.SMEM),
                  pl.BlockSpec((1, TQ, w), lambda bi, h, qb: (bi, qb, h)),
                  pl.BlockSpec((1, t, w), lambda bi, h, qb: (bi, 0, h)),
                  pl.BlockSpec((1, t, w), lambda bi, h, qb: (bi, 0, h)),
                  pl.BlockSpec((nh, 3, TQ, TQ), lambda bi, h, qb: (h, 0, 0, 0)),
                  pl.BlockSpec((1, C_HD), lambda bi, h, qb: (0, 0))],
        out_specs=pl.BlockSpec((1, TQ, w), lambda bi, h, qb: (bi, qb, h)),
        out_shape=jax.ShapeDtypeStruct((b, t, C_V), F32),
        compiler_params=_cparams(("arbitrary", "arbitrary", "arbitrary")),
        name="diff_prompt",
    )(lam.reshape(1), q, k, v, bias_tiles, subln_w.reshape(1, C_HD))


DIFF_DEC_PAGES = 16


def _diff_dec_kernel(pt_ref, lam_ref, q_ref, kn_ref, vn_ref, *rest, n_steps, lam_init):
    pp = DIFF_DEC_PAGES
    kps, vps = rest[:pp], rest[pp:2 * pp]
    bias_ref, bnew_ref, sw_ref, o_ref, m_ref, l_ref, acc_ref = rest[2 * pp:]
    p = pl.program_id(1)
    nr = 2 * C_HEADS

    @pl.when(p == 0)
    def _():
        m_ref[...] = jnp.full(m_ref.shape, NEG, F32)
        l_ref[...] = jnp.zeros(l_ref.shape, F32)
        acc_ref[...] = jnp.zeros(acc_ref.shape, F32)

    q8 = q_ref[0] * (C_HEAD_DIM ** -0.5)
    lane = lax.broadcasted_iota(I32, (C_HEADS, C_HD), 1)
    q_lo = jnp.where(lane < C_HEAD_DIM, q8, 0.0)
    q_hi = jnp.where(lane >= C_HEAD_DIM, q8, 0.0)
    qm = jnp.concatenate([q_lo, q_hi], axis=0).astype(BF16)

    rows = BLK * C_HEADS
    half = pp // 2
    cs = range(2)
    ss = [jnp.concatenate([_dg(qm, kps[c * half + i][0, 0].astype(BF16), _NT) + bias_ref[c * half + i]
                           for i in range(half)], axis=1) for c in cs]
    m_old = [m_ref[c] for c in cs]
    m_new = [jnp.maximum(m_old[c], jnp.max(ss[c], axis=1, keepdims=True)) for c in cs]
    ps = [jnp.exp(ss[c] - m_new[c]) for c in cs]
    pb = [pr.astype(BF16) for pr in ps]
    pvs = []
    for c in cs:
        out = None
        for i in range(half):
            part = _dg(pb[c][:, i * rows:(i + 1) * rows], vps[c * half + i][0, 0].astype(BF16))
            out = part if out is None else out + part
        pvs.append(out)
    for c in cs:
        alpha = jnp.exp(m_old[c] - m_new[c])
        l_ref[c] = alpha * l_ref[c] + jnp.sum(ps[c], axis=1, keepdims=True)
        acc_ref[c] = alpha * acc_ref[c] + pvs[c]
        m_ref[c] = m_new[c]

    @pl.when(p == n_steps - 1)
    def _():
        kn = kn_ref[0]
        s_new = jnp.concatenate([jnp.sum(q_lo * kn, axis=1, keepdims=True),
                                 jnp.sum(q_hi * kn, axis=1, keepdims=True)], axis=0) + bnew_ref[:, 0:1]
        vn2 = jnp.concatenate([vn_ref[0], vn_ref[0]], axis=0)
        m = jnp.maximum(jnp.maximum(m_ref[0], m_ref[1]), s_new)
        w0 = jnp.exp(m_ref[0] - m)
        w1 = jnp.exp(m_ref[1] - m)
        p_new = jnp.exp(s_new - m)
        l = w0 * l_ref[0] + w1 * l_ref[1] + p_new
        acc = w0 * acc_ref[0] + w1 * acc_ref[1] + p_new * vn2
        on = acc / l
        o = on[0:C_HEADS] - lam_ref[0] * on[C_HEADS:nr]
        ms = jnp.mean(o * o, axis=-1, keepdims=True)
        o_ref[0] = (o * lax.rsqrt(ms + EPS) * sw_ref[...]) * (1.0 - lam_init)


def _diff_decode(q8, k8, v8, cache_k, cache_v, j, pt_flat, n_pages, bias_pages, bias_new, lam, subln_w, lam_init):
    b = q8.shape[0]
    pp = DIFF_DEC_PAGES
    assert n_pages % pp == 0
    n_steps = n_pages // pp
    rows = BLK * C_HEADS
    row = pl.BlockSpec((1, C_HEADS, C_HD), lambda bi, p, pt: (bi, 0, 0))
    page = lambda i: pl.BlockSpec((1, 1, rows, C_HD),
                                  lambda bi, p, pt: (j, pt[bi * n_pages + p * pp + i], 0, 0))
    grid_spec = pltpu.PrefetchScalarGridSpec(
        num_scalar_prefetch=1,
        grid=(b, n_steps),
        in_specs=[pl.BlockSpec(memory_space=pltpu.SMEM), row, row, row]
                 + [page(i) for i in range(pp)] + [page(i) for i in range(pp)]
                 + [pl.BlockSpec((pp, 2 * C_HEADS, rows), lambda bi, p, pt: (p, 0, 0)),
                    pl.BlockSpec((2 * C_HEADS, LANES), lambda bi, p, pt: (0, 0)),
                    pl.BlockSpec((1, C_HD), lambda bi, p, pt: (0, 0))],
        out_specs=row,
        scratch_shapes=[pltpu.VMEM((2, 2 * C_HEADS, 1), F32), pltpu.VMEM((2, 2 * C_HEADS, 1), F32),
                        pltpu.VMEM((2, 2 * C_HEADS, C_HD), F32)],
    )
    return pl.pallas_call(
        functools.partial(_diff_dec_kernel, n_steps=n_steps, lam_init=lam_init),
        grid_spec=grid_spec,
        out_shape=jax.ShapeDtypeStruct((b, C_HEADS, C_HD), F32),
        compiler_params=_cparams(("arbitrary", "arbitrary")),
        name="diff_decode",
    )(pt_flat, lam.reshape(1), q8, k8, v8, *([cache_k] * pp), *([cache_v] * pp),
      bias_pages, bias_new, subln_w.reshape(1, C_HD))


def _diff_decode_bias(bias_cols, q_pos, n_pages):
    bd = _bias_decode(bias_cols, q_pos, n_pages + 1)
    past = jnp.moveaxis(bd[:n_pages], 1, 2)
    same = jnp.eye(C_HEADS, dtype=bool)
    tab = jnp.where(same[None, :, None, :], past[:, None, :, :], NEG)
    tab = tab.reshape(n_pages, C_HEADS, BLK * C_HEADS)
    tab = jnp.concatenate([tab, tab], axis=1)
    new = bd[n_pages, :, 0]
    new = jnp.broadcast_to(jnp.concatenate([new, new])[:, None], (2 * C_HEADS, LANES))
    return tab, new


def _idx_dec_kernel(pt_ref, qi_ref, sm_ref, *rest, n_pages):
    kps = rest[:n_pages]
    o_ref = rest[n_pages]
    qi = qi_ref[0]
    sm = sm_ref[0]
    rowi = lax.broadcasted_iota(I32, (SUBLANES, IDX_HEADS * IDX_DIM), 0)
    lane = lax.broadcasted_iota(I32, (SUBLANES, IDX_HEADS * IDX_DIM), 1)
    qsel = jnp.where(lane // IDX_DIM == rowi, qi, 0.0)
    qt = qsel[:, 0:IDX_DIM]
    for h in range(1, IDX_HEADS):
        qt = qt + qsel[:, h * IDX_DIM:(h + 1) * IDX_DIM]
    r8 = lax.broadcasted_iota(I32, (SUBLANES, LANES), 0)
    l8 = lax.broadcasted_iota(I32, (SUBLANES, LANES), 1)
    wsel = jnp.where(jnp.logical_and(r8 < IDX_HEADS, l8 == r8 + SM_WI), sm, 0.0)
    wcol = jnp.sum(wsel, axis=1, keepdims=True) * (IDX_HEADS ** -0.5)
    qh, ql = _split2(qt)
    kh, kl = _split2(jnp.concatenate([kps[p][0, 0] for p in range(n_pages)], axis=1))
    d = _dg(qh, kh) + (_dg(qh, kl) + _dg(ql, kh))
    sc = jnp.sum(wcol * jnp.maximum(d * (IDX_DIM ** -0.5), 0.0), axis=0, keepdims=True)
    for p in range(n_pages):
        o_ref[0, p] = sc[:, p * BLK:(p + 1) * BLK]
    d_new = jnp.sum(qt * sm[:, SM_KI:SM_KI + IDX_DIM], axis=1, keepdims=True)
    s_new = jnp.sum(wcol * jnp.maximum(d_new * (IDX_DIM ** -0.5), 0.0), axis=0, keepdims=True)
    o_ref[0, n_pages] = jnp.broadcast_to(s_new, (1, BLK))


def _idx_decode(qi, sm, cache_kidx_t, j, pt_flat, n_pages):
    b = qi.shape[0]
    page = lambda p: pl.BlockSpec((1, 1, IDX_DIM, BLK), lambda bi, pt: (j, pt[bi * n_pages + p], 0, 0))
    grid_spec = pltpu.PrefetchScalarGridSpec(
        num_scalar_prefetch=1,
        grid=(b,),
        in_specs=[pl.BlockSpec((1, 1, IDX_HEADS * IDX_DIM), lambda bi, pt: (bi, 0, 0)),
                  pl.BlockSpec((1, 1, LANES), lambda bi, pt: (bi, 0, 0))]
                 + [page(p) for p in range(n_pages)],
        out_specs=pl.BlockSpec((1, n_pages + 1, 1, BLK), lambda bi, pt: (bi, 0, 0, 0)),
    )
    return pl.pallas_call(
        functools.partial(_idx_dec_kernel, n_pages=n_pages),
        grid_spec=grid_spec,
        out_shape=jax.ShapeDtypeStruct((b, n_pages + 1, 1, BLK), F32),
        compiler_params=_cparams(("arbitrary",)),
        name="idx_decode",
    )(pt_flat, qi, sm, *([cache_kidx_t] * n_pages))


def _sel_dec_kernel(s_ref, o_ref, key_ref, *, n_valid, topk, idx_bits):
    nkb, rows, _ = s_ref.shape
    col = lax.broadcasted_iota(I32, (rows, BLK), 1)
    for kb in range(nkb):
        s = jnp.where(kb * BLK + col < n_valid, s_ref[kb], -jnp.inf)
        key_ref[kb] = _score_keys(s)
    thr, cut = _select_topk(key_ref, nkb, topk, idx_bits)
    for kb in range(nkb):
        k_pos = kb * BLK + col
        sel = _in_topk(key_ref[kb], k_pos, thr, cut)
        o_ref[kb] = jnp.where(k_pos < n_valid, jnp.where(sel, 0.0, NEG), NEG)


def _sel_decode(scores, n_valid, topk):
    nkb, rows, _ = scores.shape
    idx_bits = max(1, int(math.ceil(math.log2(nkb * BLK))))
    return pl.pallas_call(
        functools.partial(_sel_dec_kernel, n_valid=n_valid, topk=topk, idx_bits=idx_bits),
        out_shape=jax.ShapeDtypeStruct(scores.shape, F32),
        scratch_shapes=[pltpu.VMEM(scores.shape, I32)],
        compiler_params=pltpu.CompilerParams(vmem_limit_bytes=VMEM_LIMIT),
        name="sel_decode",
    )(scores)


def _dsa_dec_kernel(pt_ref, q_ref, kn_ref, vn_ref, *rest, n_pages):
    kps, vps = rest[:n_pages], rest[n_pages:2 * n_pages]
    msk_ref, bias_ref, o_ref = rest[2 * n_pages:]

    q = q_ref[0] * (A_HEAD_DIM ** -0.5)
    rowi = lax.broadcasted_iota(I32, (A_HEADS, A_Q), 0)
    lane = lax.broadcasted_iota(I32, (A_HEADS, A_Q), 1)
    qsel = jnp.where(lane // A_HEAD_DIM == rowi, q, 0.0)
    halves = []
    for n in range(A_KV_HEADS):
        acc = None
        for g in range(A_GROUP):
            h = n * A_GROUP + g
            part = qsel[:, h * A_HEAD_DIM:(h + 1) * A_HEAD_DIM]
            acc = part if acc is None else acc + part
        halves.append(acc)
    qt = jnp.concatenate(halves, axis=1)
    qt_bf = qt.astype(BF16)

    kt = jnp.concatenate([kps[i][0, 0].astype(BF16) for i in range(n_pages)], axis=1)
    vt = jnp.concatenate([vps[i][0, 0].astype(BF16) for i in range(n_pages)], axis=1)
    extra = jnp.concatenate([bias_ref[i] + msk_ref[i, 0] for i in range(n_pages)], axis=1)
    s = _dg(qt_bf, kt) + extra
    s_new = (jnp.sum(qt * kn_ref[0], axis=1, keepdims=True) + bias_ref[n_pages][:, 0:1]
             + msk_ref[n_pages, 0][:, 0:1])
    m = jnp.maximum(jnp.max(s, axis=1, keepdims=True), s_new)
    p = jnp.exp(s - m)
    p_new = jnp.exp(s_new - m)
    l = jnp.sum(p, axis=1, keepdims=True) + p_new
    acc = p_new * vn_ref[0] + _dg(p.astype(BF16), vt, _NT)
    on = acc / l
    for h in range(A_HEADS):
        n = h // A_GROUP
        o_ref[0, :, h * A_HEAD_DIM:(h + 1) * A_HEAD_DIM] = on[h:h + 1, n * A_HEAD_DIM:(n + 1) * A_HEAD_DIM]


def _dsa_decode(q, k_new, v_new, cache_kt, cache_vt, j, pt_flat, n_pages, mask, bias_dec):
    b = q.shape[0]
    page = lambda i: pl.BlockSpec((1, 1, A_KV, BLK), lambda bi, pt: (j, pt[bi * n_pages + i], 0, 0))
    rowspec = lambda w: pl.BlockSpec((1, 1, w), lambda bi, pt: (bi, 0, 0))
    grid_spec = pltpu.PrefetchScalarGridSpec(
        num_scalar_prefetch=1,
        grid=(b,),
        in_specs=[rowspec(A_Q), rowspec(A_KV), rowspec(A_KV)]
                 + [page(i) for i in range(n_pages)] + [page(i) for i in range(n_pages)]
                 + [pl.BlockSpec((n_pages + 1, 1, 1, BLK), lambda bi, pt: (0, bi, 0, 0)),
                    pl.BlockSpec((n_pages + 1, A_HEADS, BLK), lambda bi, pt: (0, 0, 0))],
        out_specs=rowspec(A_Q),
    )
    return pl.pallas_call(
        functools.partial(_dsa_dec_kernel, n_pages=n_pages),
        grid_spec=grid_spec,
        out_shape=jax.ShapeDtypeStruct((b, 1, A_Q), F32),
        compiler_params=_cparams(("arbitrary",)),
        name="dsa_decode",
    )(pt_flat, q, k_new, v_new, *([cache_kt] * n_pages), *([cache_vt] * n_pages), mask, bias_dec)


def _even_weights(w_in):
    sizes = (A_Q, A_KV, A_KV, IDX_HEADS * IDX_DIM, IDX_DIM, IDX_HEADS, B_CONV_DIM, B_V, B_HEADS, B_HEADS)
    offs = np.concatenate([[0], np.cumsum(sizes)])
    seg = lambda i: w_in[:, int(offs[i]):int(offs[i + 1])]
    pad = LANES - (IDX_DIM + IDX_HEADS + 2 * B_HEADS)
    small = jnp.concatenate([seg(4), seg(5), seg(8), seg(9), jnp.zeros((w_in.shape[0], pad), w_in.dtype)], axis=1)
    return jnp.concatenate([seg(0), seg(1), seg(2), seg(3), small, seg(6), seg(7)], axis=1).astype(BF16)


def _seq_outs(widths, dtype=F32):
    outs, off = [], 0
    for wd in widths:
        outs.append((off, wd, dtype))
        off += wd
    return tuple(outs)


_EVEN_OUTS = _seq_outs((A_Q, A_KV, A_KV, IDX_HEADS * IDX_DIM, LANES, B_CONV_DIM, B_V))
_ODD_OUTS = _seq_outs((C_QK, C_QK, C_V))
_ODD_OUTS_PROMPT = _ODD_OUTS[1:] + _seq_outs((C_QK, C_QK, C_V), BF16)


def kernel(x_prompt, x_sample, c_prompt, c_sample, cache_A_k, cache_A_v, cache_A_kidx, cache_C_k, cache_C_v, state_B_ssm, state_B_conv, page_table, rel_bias, ada_w, ada_b, norm_w, final_norm_w, ffn_w1, ffn_w2, ab_w_in, ab_w_out, gdn_conv_w, gdn_a_log, gdn_dt_bias, gdn_norm_w, c_w_in, c_w_out, c_lambda_q1, c_lambda_k1, c_lambda_q2, c_lambda_k2, c_subln_w):
    depth = ada_w.shape[0]
    bp, tp, d = x_prompt.shape
    bs, ts, _ = x_sample.shape
    assert ts == 1 and tp % TQ == 0 and tp % CHUNK == 0
    n_pages = page_table.shape[1]
    page = cache_A_k.shape[2]
    assert page == BLK
    n_phys = cache_A_k.shape[1]
    past_len = n_pages * page
    topk_p = min(TOPK_MAX, tp // 4)
    topk_s = min(TOPK_MAX, (past_len + ts) // 4)
    pt_flat = page_table.reshape(-1).astype(I32)

    n_c = bp + bs
    n_c_pad = -(-n_c // SUBLANES) * SUBLANES
    c_all = jnp.pad(jnp.concatenate([c_prompt, c_sample], axis=0), ((0, n_c_pad - n_c), (0, 0)))
    mod_all = _modulation(c_all, ada_w, ada_b)

    w1b = ffn_w1.astype(BF16)
    w2b = ffn_w2.astype(BF16)
    ab_in_b = [_even_weights(ab_w_in[j]) for j in range(ab_w_in.shape[0])]
    ab_out_b = ab_w_out.astype(BF16)
    c_in_b = c_w_in.astype(BF16)
    c_out_b = c_w_out.astype(BF16)

    bias_a = _bias_tiles(rel_bias[:, :A_HEADS], BLK, False)
    bias_c = jnp.swapaxes(_bias_tiles(rel_bias[:, A_HEADS:], TQ, True), 2, 3)
    bias_a_dec = _bias_decode(rel_bias[:, :A_HEADS], past_len, n_pages + 1)
    bias_c_pages, bias_c_new = _diff_decode_bias(rel_bias[:, A_HEADS:], past_len, n_pages)

    n_ab = cache_A_k.shape[0]
    cache_a_kt = jnp.transpose(cache_A_k, (0, 1, 3, 4, 2)).reshape(n_ab, n_phys, A_KV, page)
    cache_a_vt = jnp.transpose(cache_A_v, (0, 1, 3, 4, 2)).reshape(n_ab, n_phys, A_KV, page)
    cache_a_it = jnp.transpose(cache_A_kidx, (0, 1, 3, 2))
    cache_c_k = cache_C_k.reshape(cache_C_k.shape[0], n_phys, page * C_HEADS, C_HD)
    cache_c_v = cache_C_v.reshape(cache_C_v.shape[0], n_phys, page * C_HEADS, C_HD)
    conv_t = jnp.transpose(state_B_conv, (0, 2, 1, 3))

    xp = x_prompt
    xs = x_sample.reshape(1, bs, d)
    new_p = [[] for _ in range(7)]
    new_s = [[] for _ in range(7)]

    for i in range(depth):
        j = i // 2
        mod_p = mod_all[i, :bp].reshape(bp, 1, N_MOD * d)
        mod_s = mod_all[i, bp:bp + bs].reshape(1, bs, N_MOD * d)
        xp = _ffn(xp, mod_p, 0, norm_w[i, 0], w1b, w2b, i, 0)
        xs = _ffn(xs, mod_s, 0, norm_w[i, 0], w1b, w2b, i, 0)
        if i % 2 == 0:
            q, k, v, qi, sm, qkv, z = _inproj(xp, mod_p, 3, norm_w[i, 1], ab_in_b[j], _EVEN_OUTS)
            o_a = _dsa_prompt(q, qi, sm, k, v, bias_a, topk_p)
            o_b, s_new = _gdn_prompt(qkv, z, sm, jnp.zeros((bp, B_HEADS, B_KEY_DIM, B_VAL_DIM), F32),
                                     gdn_conv_w[j], gdn_a_log[j], gdn_dt_bias[j], gdn_norm_w[j])
            xp = _outproj(xp, mod_p, 5, ab_out_b[j], [o_a, o_b])
            new_p[0].append(k.reshape(bp, tp, A_KV_HEADS, A_HEAD_DIM))
            new_p[1].append(v.reshape(bp, tp, A_KV_HEADS, A_HEAD_DIM))
            new_p[2].append(sm[:, :, SM_KI:SM_KI + IDX_DIM])
            new_p[3].append(s_new)
            new_p[4].append(qkv[:, tp - (CONV_W - 1):, :])
            q, k, v, qi, sm, qkv, z = _inproj(xs, mod_s, 3, norm_w[i, 1], ab_in_b[j], _EVEN_OUTS)
            as_rows = lambda a: a.reshape(bs, 1, a.shape[-1])
            scores = _idx_decode(as_rows(qi), as_rows(sm), cache_a_it, j, pt_flat, n_pages)
            scores = jnp.moveaxis(scores.reshape(bs, n_pages + 1, BLK), 1, 0)
            mask = _sel_decode(scores, past_len + 1, topk_s).reshape(n_pages + 1, bs, 1, BLK)
            o_a = _dsa_decode(as_rows(q), as_rows(k), as_rows(v), cache_a_kt, cache_a_vt, j, pt_flat, n_pages,
                              mask, bias_a_dec)
            o_b, s_new = _gdn_decode(qkv.reshape(bs, B_CONV_DIM), z.reshape(bs, B_V), sm.reshape(bs, LANES),
                                     conv_t, state_B_ssm, j, gdn_conv_w[j], gdn_a_log[j], gdn_dt_bias[j],
                                     gdn_norm_w[j])
            xs = _outproj(xs, mod_s, 5, ab_out_b[j], [o_a.reshape(1, bs, A_Q), o_b.reshape(1, bs, B_V)])
            new_s[0].append(k.reshape(bs, 1, A_KV_HEADS, A_HEAD_DIM))
            new_s[1].append(v.reshape(bs, 1, A_KV_HEADS, A_HEAD_DIM))
            new_s[2].append(sm.reshape(bs, 1, LANES)[:, :, SM_KI:SM_KI + IDX_DIM])
            new_s[3].append(s_new)
            new_s[4].append(jnp.concatenate([state_B_conv[j], as_rows(qkv)], axis=1)[:, 1:, :])
        else:
            lam_init = 0.8 - 0.6 * math.exp(-0.3 * i)
            lam = (jnp.exp(jnp.sum(c_lambda_q1[j] * c_lambda_k1[j]))
                   - jnp.exp(jnp.sum(c_lambda_q2[j] * c_lambda_k2[j])) + lam_init).astype(F32)
            k, v, qb16, kb16, vb16 = _inproj(xp, mod_p, 3, norm_w[i, 1], c_in_b[j], _ODD_OUTS_PROMPT)
            o = _diff_prompt(qb16, kb16, vb16, bias_c, lam, c_subln_w[j], lam_init)
            xp = _outproj(xp, mod_p, 5, c_out_b[j], [o])
            new_p[5].append(k.reshape(bp, tp, C_HEADS, C_HD))
            new_p[6].append(v.reshape(bp, tp, C_HEADS, C_HD))
            q, k, v = _inproj(xs, mod_s, 3, norm_w[i, 1], c_in_b[j], _ODD_OUTS)
            as_heads = lambda a: a.reshape(bs, C_HEADS, C_HD)
            o = _diff_decode(as_heads(q), as_heads(k), as_heads(v), cache_c_k, cache_c_v, j, pt_flat, n_pages,
                             bias_c_pages, bias_c_new, lam, c_subln_w[j], lam_init)
            xs = _outproj(xs, mod_s, 5, c_out_b[j], [o.reshape(1, bs, C_V)])
            new_s[5].append(k.reshape(bs, 1, C_HEADS, C_HD))
            new_s[6].append(v.reshape(bs, 1, C_HEADS, C_HD))
        xp = _ffn(xp, mod_p, 6, norm_w[i, 2], w1b, w2b, i, 1)
        xs = _ffn(xs, mod_s, 6, norm_w[i, 2], w1b, w2b, i, 1)

    y_prompt = _final_norm(xp, final_norm_w)
    y_sample = _final_norm(xs, final_norm_w).reshape(bs, 1, d)
    sp = [jnp.stack(lst) for lst in new_p]
    ss = [jnp.stack(lst) for lst in new_s]
    return (y_prompt, y_sample, *sp, *ss)
```

```python
import functools
import math

import numpy as np
import jax
import jax.numpy as jnp
from jax import lax
from jax.experimental import pallas as pl
from jax.experimental.pallas import tpu as pltpu

F32 = jnp.float32
BF16 = jnp.bfloat16
I32 = jnp.int32

A_HEADS = 8
A_KV_HEADS = 2
A_GROUP = A_HEADS // A_KV_HEADS
A_HEAD_DIM = 64
IDX_HEADS = 4
IDX_DIM = 64
TOPK_MAX = 256
B_HEADS = 4
B_KEY_DIM = 128
B_VAL_DIM = 128
CONV_W = 4
CHUNK = 64
C_HEADS = 8
C_HEAD_DIM = 64
NUM_BUCKETS = 32
MAX_DISTANCE = 128
N_MOD = 9
EPS = 1e-6

A_Q = A_HEADS * A_HEAD_DIM
A_KV = A_KV_HEADS * A_HEAD_DIM
B_K = B_HEADS * B_KEY_DIM
B_V = B_HEADS * B_VAL_DIM
B_CONV_DIM = 2 * B_K + B_V
C_QK = C_HEADS * 2 * C_HEAD_DIM
C_V = C_HEADS * 2 * C_HEAD_DIM
C_HD = 2 * C_HEAD_DIM

LANES = 128
SUBLANES = 8
VMEM_LIMIT = 56 * 1024 * 1024

BLK = 128
TQ = 256
NEG = -1e30

SM_KI = 0
SM_WI = IDX_DIM
SM_A = SM_WI + IDX_HEADS
SM_B = SM_A + B_HEADS

_NT = (((1,), (1,)), ((), ()))
_NN = (((1,), (0,)), ((), ()))
_TN = (((0,), (0,)), ((), ()))


def _cparams(sem):
    return pltpu.CompilerParams(dimension_semantics=sem, vmem_limit_bytes=VMEM_LIMIT)


def _dg(a, b, dims=_NN):
    return lax.dot_general(a, b, dims, preferred_element_type=F32)


def _dot1(a, b, dims=_NN):
    return _dg(a.astype(BF16), b.astype(BF16), dims)


def _split2(x):
    hi = x.astype(BF16)
    lo = (x - hi.astype(F32)).astype(BF16)
    return hi, lo


def _split3(x):
    b1 = x.astype(BF16)
    r1 = x - b1.astype(F32)
    b2 = r1.astype(BF16)
    b3 = (r1 - b2.astype(F32)).astype(BF16)
    return b1, b2, b3


def _dot3(a, b, dims=_NN):
    ah, al = _split2(a)
    bh, bl = _split2(b)
    return _dg(ah, bh, dims) + (_dg(ah, bl, dims) + _dg(al, bh, dims))


def _dot_exact_lhs(a_bf, b, dims=_NN):
    b1, b2, b3 = _split3(b)
    return _dg(a_bf, b1, dims) + (_dg(a_bf, b2, dims) + _dg(a_bf, b3, dims))


def _dot_exact_rhs(a, b_bf, dims=_NN):
    a1, a2, a3 = _split3(a)
    return _dg(a1, b_bf, dims) + (_dg(a2, b_bf, dims) + _dg(a3, b_bf, dims))


def _silu(x):
    return x * jax.nn.sigmoid(x)


def _softplus(x):
    return jnp.maximum(x, 0.0) + jnp.log(1.0 + jnp.exp(-jnp.abs(x)))


def _norm_mod(x, nw, sc, sh):
    ms = jnp.mean(x * x, axis=-1, keepdims=True)
    return (x * lax.rsqrt(ms + EPS) * nw) * (1.0 + sc) + sh


def _softmax_step(s, m, l, acc, pv):
    m_new = jnp.maximum(m, jnp.max(s, axis=1, keepdims=True))
    alpha = jnp.exp(m - m_new)
    p = jnp.exp(s - m_new)
    return m_new, alpha * l + jnp.sum(p, axis=1, keepdims=True), alpha * acc + pv(p)


def _mod_kernel(c_ref, w_ref, b_ref, o_ref):
    s = _silu(c_ref[...]).astype(BF16)
    o_ref[0] = _dg(s, w_ref[0].astype(BF16)) + b_ref[0]


def _modulation(c_all, ada_w, ada_b):
    depth, d, n = ada_w.shape
    m = c_all.shape[0]
    tn = 1024
    return pl.pallas_call(
        _mod_kernel,
        grid=(depth, n // tn),
        in_specs=[pl.BlockSpec((m, d), lambda i, j: (0, 0)),
                  pl.BlockSpec((1, d, tn), lambda i, j: (i, 0, j)),
                  pl.BlockSpec((1, 1, tn), lambda i, j: (i, 0, j))],
        out_specs=pl.BlockSpec((1, m, tn), lambda i, j: (i, 0, j)),
        out_shape=jax.ShapeDtypeStruct((depth, m, n), F32),
        compiler_params=_cparams(("arbitrary", "arbitrary")),
        name="adaln_mod",
    )(c_all, ada_w, ada_b.reshape(depth, 1, n))


def _row_tile(t):
    return min(512, t)


def _mod_spec(mod, tm, m):
    r = mod.shape[1]
    d = mod.shape[2] // N_MOD
    if r == 1:
        return pl.BlockSpec((1, 1, d), lambda s, t: (s, 0, m))
    return pl.BlockSpec((1, tm, d), lambda s, t: (s, t, m))


def _resident(shape):
    nd = len(shape)
    return pl.BlockSpec(shape, lambda s, t: (0,) * nd, pipeline_mode=pl.Buffered(1))


def _ffn_kernel(x_ref, sh_ref, sc_ref, g_ref, nw_ref, fw_ref, w1_ref, w2_ref, o_ref, acc_ref, *, fc, final):
    x = x_ref[0]
    hb = _norm_mod(x, nw_ref[...], sc_ref[0], sh_ref[0]).astype(BF16)
    f = w2_ref.shape[2]
    for c in range(f // fc):
        gt = _dg(hb, w1_ref[0, 0, :, c * fc:(c + 1) * fc])
        up = _dg(hb, w1_ref[0, 0, :, f + c * fc:f + (c + 1) * fc])
        a = (_silu(gt) * up).astype(BF16)
        contrib = _dg(a, w2_ref[0, 0, c * fc:(c + 1) * fc, :])
        if c == 0:
            acc_ref[...] = contrib
        else:
            acc_ref[...] += contrib
    y = x + (0.5 * g_ref[0]) * acc_ref[...]
    if final:
        ms = jnp.mean(y * y, axis=-1, keepdims=True)
        y = y * lax.rsqrt(ms + EPS) * fw_ref[...]
    o_ref[0] = y


def _ffn(x, mod, m0, nw, w1b, w2b, layer, which, final_w=None):
    s, t, d = x.shape
    tm = _row_tile(t)
    f = w2b.shape[2]
    wspec = lambda shp: pl.BlockSpec((1, 1) + tuple(shp[2:]), lambda si, ti: (layer, which, 0, 0),
                                     pipeline_mode=pl.Buffered(1))
    fc = 256 if f % 256 == 0 else LANES
    xs = pl.BlockSpec((1, tm, d), lambda si, ti: (si, ti, 0))
    return pl.pallas_call(
        functools.partial(_ffn_kernel, fc=fc, final=final_w is not None),
        grid=(s, t // tm),
        in_specs=[xs, _mod_spec(mod, tm, m0), _mod_spec(mod, tm, m0 + 1), _mod_spec(mod, tm, m0 + 2),
                  _resident((1, d)), _resident((1, d)), wspec(w1b.shape), wspec(w2b.shape)],
        out_specs=xs,
        out_shape=jax.ShapeDtypeStruct(x.shape, F32),
        scratch_shapes=[pltpu.VMEM((tm, d), F32)],
        compiler_params=_cparams(("arbitrary", "arbitrary")),
        name="ffn",
    )(x, mod, mod, mod, nw.reshape(1, d), (nw if final_w is None else final_w).reshape(1, d), w1b, w2b)


def _inproj_kernel(x_ref, sh_ref, sc_ref, nw_ref, w_ref, *o_refs, outs):
    hb = _norm_mod(x_ref[0], nw_ref[...], sc_ref[0], sh_ref[0]).astype(BF16)
    done = {}
    for o_ref, (off, wd, dt) in zip(o_refs, outs):
        if (off, wd) not in done:
            done[(off, wd)] = _dg(hb, w_ref[:, off:off + wd])
        o_ref[0] = done[(off, wd)].astype(dt)


def _inproj(x, mod, m0, nw, wb, outs):
    s, t, d = x.shape
    tm = _row_tile(t)
    xs = pl.BlockSpec((1, tm, d), lambda si, ti: (si, ti, 0))
    return pl.pallas_call(
        functools.partial(_inproj_kernel, outs=tuple(outs)),
        grid=(s, t // tm),
        in_specs=[xs, _mod_spec(mod, tm, m0), _mod_spec(mod, tm, m0 + 1), _resident((1, d)), _resident(wb.shape)],
        out_specs=[pl.BlockSpec((1, tm, wd), lambda si, ti: (si, ti, 0)) for _, wd, _ in outs],
        out_shape=[jax.ShapeDtypeStruct((s, t, wd), dt) for _, wd, dt in outs],
        compiler_params=_cparams(("arbitrary", "arbitrary")),
        name="inproj",
    )(x, mod, mod, nw.reshape(1, d), wb)


def _outproj_kernel(*refs, widths):
    n = len(widths)
    x_ref, g_ref, w_ref = refs[0], refs[1], refs[2]
    a_refs = refs[3:3 + n]
    o_ref = refs[3 + n]
    acc = None
    off = 0
    for a_ref, wd in zip(a_refs, widths):
        part = _dg(a_ref[0].astype(BF16), w_ref[off:off + wd, :])
        acc = part if acc is None else acc + part
        off += wd
    o_ref[0] = x_ref[0] + g_ref[0] * acc


def _outproj(x, mod, mg, wb, parts):
    s, t, d = x.shape
    tm = _row_tile(t)
    widths = tuple(p.shape[-1] for p in parts)
    xs = pl.BlockSpec((1, tm, d), lambda si, ti: (si, ti, 0))
    return pl.pallas_call(
        functools.partial(_outproj_kernel, widths=widths),
        grid=(s, t // tm),
        in_specs=[xs, _mod_spec(mod, tm, mg), _resident(wb.shape)]
                 + [pl.BlockSpec((1, tm, wd), lambda si, ti: (si, ti, 0)) for wd in widths],
        out_specs=xs,
        out_shape=jax.ShapeDtypeStruct(x.shape, F32),
        compiler_params=_cparams(("arbitrary", "arbitrary")),
        name="outproj",
    )(x, mod, wb, *parts)


def _final_norm_kernel(x_ref, w_ref, o_ref):
    x = x_ref[0]
    ms = jnp.mean(x * x, axis=-1, keepdims=True)
    o_ref[0] = x * lax.rsqrt(ms + EPS) * w_ref[...]


def _final_norm(x, w):
    s, t, d = x.shape
    tm = _row_tile(t)
    xs = pl.BlockSpec((1, tm, d), lambda si, ti: (si, ti, 0))
    return pl.pallas_call(
        _final_norm_kernel,
        grid=(s, t // tm),
        in_specs=[xs, pl.BlockSpec((1, d), lambda si, ti: (0, 0))],
        out_specs=xs,
        out_shape=jax.ShapeDtypeStruct(x.shape, F32),
        compiler_params=_cparams(("arbitrary", "arbitrary")),
        name="final_norm",
    )(x, w.reshape(1, d))


def _t5_bucket(dist):
    n = jnp.maximum(dist, 0)
    max_exact = NUM_BUCKETS // 2
    nf = jnp.maximum(n, 1).astype(F32)
    large = max_exact + (jnp.log(nf / max_exact) / math.log(MAX_DISTANCE / max_exact)
                         * (NUM_BUCKETS - max_exact)).astype(I32)
    large = jnp.minimum(large, NUM_BUCKETS - 1)
    return jnp.where(n < max_exact, n, large)


def _bias_tiles(bias_cols, blk, causal):
    assert blk >= MAX_DISTANCE
    n = blk
    h = bias_cols.shape[1]
    tiles = []
    for t in range(3):
        off = (2 - t) * n
        if t == 0:
            far = bias_cols[_t5_bucket(jnp.full((), 2 * n, I32))]
            tiles.append(jnp.broadcast_to(far[:, None, None], (h, n, n)))
            continue
        dist = n - 1 + off - jnp.arange(2 * n, dtype=I32)
        g = bias_cols[_t5_bucket(dist)]
        if causal:
            g = jnp.where((dist >= 0)[:, None], g, NEG)
        x = jnp.broadcast_to(g.T[:, None, :], (h, n, 2 * n))
        x = jnp.pad(x, ((0, 0), (0, 0), (0, 1))).reshape(h, n * (2 * n + 1))[:, :2 * n * n].reshape(h, n, 2 * n)
        tiles.append(x[:, :, n - 1:2 * n - 1])
    return jnp.stack(tiles, axis=1)


def _bias_decode(bias_cols, q_pos, n_blocks):
    k_pos = jnp.arange(n_blocks * BLK, dtype=I32)
    b = bias_cols[_t5_bucket(q_pos - k_pos)]
    b = jnp.where((k_pos <= q_pos)[:, None], b, NEG)
    return jnp.moveaxis(b.reshape(n_blocks, BLK, -1), 2, 1)


def _score_keys(s):
    bits = pltpu.bitcast(s, I32)
    return jnp.where(bits < 0, bits ^ jnp.int32(0x7FFFFFFF), bits)


def _select_topk(key_ref, nkb, topk, idx_bits, key_axis=1, pairs=False):
    tile = tuple(key_ref.shape[1:])
    assert tile[key_axis] == BLK
    stat = tuple(1 if a == key_axis else n for a, n in enumerate(tile))
    rows = stat
    col = lax.broadcasted_iota(I32, tile, key_axis)

    def count(pred):
        if pairs:
            def body(i, acc):
                kb = 2 * i
                return (acc + jnp.where(pred(key_ref[kb], kb * BLK + col), 1.0, 0.0)
                        + jnp.where(pred(key_ref[kb + 1], (kb + 1) * BLK + col), 1.0, 0.0))
            acc = lax.fori_loop(0, (nkb + 1) // 2, body, jnp.zeros(tile, F32))
        else:
            def body(kb, acc):
                return acc + jnp.where(pred(key_ref[kb], kb * BLK + col), 1.0, 0.0)
            acc = lax.fori_loop(0, nkb, body, jnp.zeros(tile, F32))
        return jnp.sum(acc, axis=key_axis, keepdims=True)

    kf = float(topk)

    def bit_body(i, carry):
        lo, cnt_lo = carry
        cand = lo + lax.shift_left(jnp.int32(1), jnp.int32(31) - i)
        cnt = count(lambda k, _: k >= cand)
        take = cnt >= kf
        return jnp.where(take, cand, lo), jnp.where(take, cnt, cnt_lo)

    n_visited = 2 * ((nkb + 1) // 2) if pairs else nkb
    total = jnp.zeros(rows, F32) + jnp.asarray(n_visited * BLK, F32)
    thr, cnt_ge = lax.fori_loop(0, 32, bit_body, (jnp.full(rows, -2 ** 31, I32), total))
    need = kf - count(lambda k, _: k > thr)

    has_ties = jnp.max(cnt_ge) > kf

    def idx_body(i, p):
        cand = p + lax.shift_left(jnp.int32(1), jnp.int32(idx_bits - 1) - i)
        cnt = count(lambda k, ix: jnp.logical_and(k == thr, ix < cand))
        return jnp.where(cnt < need, cand, p)

    cut0 = jnp.zeros(rows, I32) + jnp.where(has_ties, 0, 2 ** idx_bits - 1)
    cut = lax.fori_loop(0, jnp.where(has_ties, idx_bits, 0), idx_body, cut0)
    return thr, cut


def _in_topk(key, idx, thr, cut):
    return jnp.logical_or(key > thr, jnp.logical_and(key == thr, idx <= cut))


DSA_CHAIN_HEADS = 4


def _dsa_kernel(q_ref, qi_ref, sm_ref, kf_ref, vf_ref, smf_ref, bias_ref, o_ref, key_ref, msk_ref,
                *, topk, idx_bits):
    qb = pl.program_id(1)
    nkb = qb + 1
    key_i = lax.broadcasted_iota(I32, (BLK, BLK), 0)
    q_pos = qb * BLK + lax.broadcasted_iota(I32, (BLK, BLK), 1)

    qi = qi_ref[0]
    sm_t = sm_ref[0].T
    qi_st = jnp.concatenate([qi[:, h * IDX_DIM:(h + 1) * IDX_DIM] for h in range(IDX_HEADS)], axis=0).astype(BF16)
    wi_rows = [sm_t[SM_WI + h:SM_WI + h + 1, :] * (IDX_HEADS ** -0.5) for h in range(IDX_HEADS)]

    def score_body(i, carry):
        kbs = (2 * i, 2 * i + 1)
        kis = [smf_ref[0, pl.ds(pl.multiple_of(kb * BLK, BLK), BLK), :][:, SM_KI:SM_KI + IDX_DIM].astype(BF16)
               for kb in kbs]
        rs = [jnp.maximum(_dg(ki, qi_st, _NT) * (IDX_DIM ** -0.5), 0.0) for ki in kis]
        for kb, r in zip(kbs, rs):
            s = wi_rows[0] * r[:, 0:BLK]
            for h in range(1, IDX_HEADS):
                s = s + wi_rows[h] * r[:, h * BLK:(h + 1) * BLK]
            s = jnp.where(kb * BLK + key_i <= q_pos, s, -jnp.inf)
            key_ref[kb] = _score_keys(s)
        return carry

    lax.fori_loop(0, (nkb + 1) // 2, score_body, 0)

    @pl.when(nkb * BLK <= topk)
    def _():
        def body(kb, carry):
            msk_ref[kb] = jnp.where(kb * BLK + key_i <= q_pos, 0.0, NEG)
            return carry
        lax.fori_loop(0, nkb, body, 0)

    @pl.when(nkb * BLK > topk)
    def _():
        thr, cut = _select_topk(key_ref, nkb, topk, idx_bits, key_axis=0, pairs=True)

        def body(kb, carry):
            k_pos = kb * BLK + key_i
            sel = _in_topk(key_ref[kb], k_pos, thr, cut)
            msk_ref[kb] = jnp.where(k_pos <= q_pos, jnp.where(sel, 0.0, NEG), NEG)
            return carry
        lax.fori_loop(0, nkb, body, 0)

    q = q_ref[0] * (A_HEAD_DIM ** -0.5)
    ch = DSA_CHAIN_HEADS
    cq = ch * BLK
    chains = [(h0 // A_GROUP, h0) for h0 in range(0, A_HEADS, ch)]
    cs = range(len(chains))
    qs = [jnp.concatenate([q[:, (h0 + g) * A_HEAD_DIM:(h0 + g + 1) * A_HEAD_DIM] for g in range(ch)],
                          axis=0).astype(BF16) for _, h0 in chains]

    def logits(kb):
        off = pl.multiple_of(kb * BLK, BLK)
        kblk = kf_ref[0, pl.ds(off, BLK), :].astype(BF16)
        t = jnp.clip(kb - qb + 2, 0, 2)
        mkc = jnp.concatenate([msk_ref[kb]] * ch, axis=1)
        out = []
        for c, (n, h0) in enumerate(chains):
            g0 = h0 - n * A_GROUP
            bias = bias_ref[n, t][:, g0 * BLK:(g0 + ch) * BLK]
            out.append(_dg(kblk[:, n * A_HEAD_DIM:(n + 1) * A_HEAD_DIM], qs[c], _NT) + bias + mkc)
        return out

    def att_body(kb, carry):
        ss = carry[3 * len(chains):]
        ss_next = logits(jnp.minimum(kb + 1, nkb - 1))
        off = pl.multiple_of(kb * BLK, BLK)
        vblk = vf_ref[0, pl.ds(off, BLK), :].astype(BF16)
        m_new = [jnp.maximum(carry[3 * c], jnp.max(ss[c], axis=0, keepdims=True)) for c in cs]
        ps = [jnp.exp(ss[c] - m_new[c]) for c in cs]
        pvs = [_dg(vblk[:, chains[c][0] * A_HEAD_DIM:(chains[c][0] + 1) * A_HEAD_DIM], ps[c].astype(BF16), _TN)
               for c in cs]
        out = []
        for c in cs:
            alpha = jnp.exp(carry[3 * c] - m_new[c])
            out += [m_new[c], alpha * carry[3 * c + 1] + jnp.sum(ps[c], axis=0, keepdims=True),
                    alpha * carry[3 * c + 2] + pvs[c]]
        return tuple(out) + tuple(ss_next)

    init = (jnp.full((1, cq), NEG, F32), jnp.zeros((1, cq), F32), jnp.zeros((A_HEAD_DIM, cq), F32)) * len(chains)
    res = lax.fori_loop(0, nkb, att_body, init + tuple(logits(0)))
    for c, (n, h0) in enumerate(chains):
        o_t = res[3 * c + 2] / res[3 * c + 1]
        for g in range(0, ch, 2):
            pair = jnp.concatenate([o_t[:, g * BLK:(g + 1) * BLK], o_t[:, (g + 1) * BLK:(g + 2) * BLK]], axis=0)
            o_ref[0, :, (h0 + g) * A_HEAD_DIM:(h0 + g + 2) * A_HEAD_DIM] = pair.T


def _dsa_prompt(q, qi, sm, k, v, bias_tiles, topk):
    b, t, _ = q.shape
    nb = t // BLK
    assert nb % 2 == 0
    idx_bits = max(1, int(math.ceil(math.log2(t))))
    bt = bias_tiles.reshape(A_KV_HEADS, A_GROUP, 3, BLK, BLK)
    bt = jnp.transpose(bt, (0, 2, 4, 1, 3)).reshape(A_KV_HEADS, 3, BLK, A_GROUP * BLK)
    blk = lambda w: pl.BlockSpec((1, BLK, w), lambda bi, qb: (bi, qb, 0))
    full = lambda w: pl.BlockSpec((1, t, w), lambda bi, qb: (bi, 0, 0))
    return pl.pallas_call(
        functools.partial(_dsa_kernel, topk=topk, idx_bits=idx_bits),
        grid=(b, nb),
        in_specs=[blk(A_Q), blk(IDX_HEADS * IDX_DIM), blk(LANES), full(A_KV), full(A_KV), full(LANES),
                  pl.BlockSpec(bt.shape, lambda bi, qb: (0, 0, 0, 0))],
        out_specs=blk(A_Q),
        out_shape=jax.ShapeDtypeStruct((b, t, A_Q), F32),
        scratch_shapes=[pltpu.VMEM((nb, BLK, BLK), I32), pltpu.VMEM((nb, BLK, BLK), F32)],
        compiler_params=_cparams(("arbitrary", "arbitrary")),
        name="dsa_prompt",
    )(q, qi, sm, k, v, sm, bt)


def _gdn_kernel(qkv_ref, z_ref, sm_ref, s0_ref, cw_ref, par_ref, nw_ref, o_ref, sout_ref, ext_ref, st_ref):
    t = pl.program_id(1)
    nt = pl.num_programs(1)
    c = CHUNK
    halo = SUBLANES

    nbb = qkv_ref.shape[0]
    bbs = range(nbb)

    @pl.when(t == 0)
    def _():
        for bb in bbs:
            ext_ref[bb, 0:halo, :] = jnp.zeros((halo, B_CONV_DIM), F32)
        st_ref[...] = s0_ref[...]

    convs = []
    for bb in bbs:
        x = qkv_ref[bb]
        ext_ref[bb, halo:halo + c, :] = x
        conv = None
        for j in range(CONV_W):
            start = halo - (CONV_W - 1) + j
            term = ext_ref[bb, start:start + c, :] * cw_ref[j:j + 1, :]
            conv = term if conv is None else conv + term
        ext_ref[bb, 0:halo, :] = x[c - halo:c, :]
        convs.append(_silu(conv))

    ri = lax.broadcasted_iota(I32, (c, c), 0)
    ci = lax.broadcasted_iota(I32, (c, c), 1)
    tri = ri >= ci
    stri = ri > ci
    eye = jnp.where(ri == ci, 1.0, 0.0)
    tri_bf = jnp.where(tri, 1.0, 0.0).astype(BF16)
    ones_bf = jnp.ones((c, c), BF16)

    sms = [sm_ref[bb] for bb in bbs]
    g_alls = [-jnp.exp(par_ref[0:1, :]) * _softplus(sm + par_ref[1:2, :]) for sm in sms]
    gc_alls = [_dot_exact_lhs(tri_bf, g) for g in g_alls]
    beta_alls = [jax.nn.sigmoid(sm) for sm in sms]
    nw = nw_ref[...]
    items = [(bb, h) for bb in bbs for h in range(B_HEADS)]
    hs = range(len(items))
    gcs = [gc_alls[bb][:, SM_A + h:SM_A + h + 1] for bb, h in items]
    gc_rows = [_dot_exact_lhs(ones_bf, jnp.concatenate([eye * gcs[bb * B_HEADS + h] for h in range(B_HEADS)],
                                                       axis=1)) for bb in bbs]
    qs = [convs[bb][:, h * B_KEY_DIM:(h + 1) * B_KEY_DIM] for bb, h in items]
    ks = [convs[bb][:, B_K + h * B_KEY_DIM:B_K + (h + 1) * B_KEY_DIM] for bb, h in items]
    vs = [convs[bb][:, 2 * B_K + h * B_VAL_DIM:2 * B_K + (h + 1) * B_VAL_DIM] for bb, h in items]
    qs = [q * lax.rsqrt(jnp.sum(q * q, axis=-1, keepdims=True) + EPS) * (B_KEY_DIM ** -0.5) for q in qs]
    ks = [k * lax.rsqrt(jnp.sum(k * k, axis=-1, keepdims=True) + EPS) for k in ks]
    betas = [beta_alls[bb][:, SM_B + h:SM_B + h + 1] for bb, h in items]
    decays = [jnp.exp(jnp.where(tri, gcs[i] - gc_rows[bb][:, h * c:(h + 1) * c], -jnp.inf))
              for i, (bb, h) in enumerate(items)]
    kbs = [ks[h] * betas[h] for h in hs]
    vbs = [vs[h] * betas[h] for h in hs]
    k_bf = [k.astype(BF16) for k in ks]
    a_mats = [jnp.where(stri, _dg(kbs[h].astype(BF16), k_bf[h], _NT) * decays[h], 0.0) for h in hs]
    tms = [eye - a for a in a_mats]
    pws = [_dot3(a, a) for a in a_mats]
    steps = int(math.log2(c))
    for j in range(1, steps):
        tms = [tm + _dot3(tm, pw) for tm, pw in zip(tms, pws)]
        if j < steps - 1:
            pws = [_dot3(pw, pw) for pw in pws]
    egs = [jnp.exp(gc) for gc in gcs]
    tm_bf = [tm.astype(BF16) for tm in tms]
    us = [_dg(tm_bf[h], vbs[h].astype(BF16)) for h in hs]
    ws = [_dg(tm_bf[h], (kbs[h] * egs[h]).astype(BF16)) for h in hs]
    a_qks = [jnp.where(tri, _dg(qs[h].astype(BF16), k_bf[h], _NT) * decays[h], 0.0) for h in hs]

    s_olds = [st_ref[bb, h] for bb, h in items]
    s_bf = [s.astype(BF16) for s in s_olds]
    v_news = [us[i] - _dg(ws[i].astype(BF16), s_bf[i]) for i in hs]
    os_ = [_dg((qs[i] * egs[i]).astype(BF16), s_bf[i]) + _dot1(a_qks[i], v_news[i]) for i in hs]
    g_lasts = [gc[c - 1:c, :] for gc in gcs]
    for i, (bb, h) in enumerate(items):
        st_ref[bb, h] = (s_olds[i] * jnp.exp(g_lasts[i])
                         + _dot1(ks[i] * jnp.exp(g_lasts[i] - gcs[i]), v_news[i], _TN))
    for i, (bb, h) in enumerate(items):
        o = os_[i]
        ms = jnp.mean(o * o, axis=-1, keepdims=True)
        zh = z_ref[bb, :, h * B_VAL_DIM:(h + 1) * B_VAL_DIM]
        o_ref[bb, :, h * B_VAL_DIM:(h + 1) * B_VAL_DIM] = (o * lax.rsqrt(ms + EPS) * nw) * _silu(zh)

    @pl.when(t == nt - 1)
    def _():
        sout_ref[...] = st_ref[...]


GDN_SEQS_PER_STEP = 4


def _gdn_params(a_log, dt_bias):
    par = jnp.zeros((SUBLANES, LANES), F32)
    return par.at[0, SM_A:SM_A + B_HEADS].set(a_log).at[1, SM_A:SM_A + B_HEADS].set(dt_bias)


def _gdn_prompt(qkv, z, sm, s0, conv_w, a_log, dt_bias, norm_w):
    b, t, _ = qkv.shape
    nt = t // CHUNK
    nbb = GDN_SEQS_PER_STEP if b % GDN_SEQS_PER_STEP == 0 else 1
    tok = lambda w: pl.BlockSpec((nbb, CHUNK, w), lambda bi, ti: (bi, ti, 0))
    const2 = lambda shp: pl.BlockSpec(shp, lambda bi, ti: (0, 0))
    st_spec = pl.BlockSpec((nbb, B_HEADS, B_KEY_DIM, B_VAL_DIM), lambda bi, ti: (bi, 0, 0, 0))
    o, s_out = pl.pallas_call(
        _gdn_kernel,
        grid=(b // nbb, nt),
        in_specs=[tok(B_CONV_DIM), tok(B_V), tok(LANES), st_spec,
                  const2((CONV_W, B_CONV_DIM)), const2((SUBLANES, LANES)), const2((1, B_VAL_DIM))],
        out_specs=[tok(B_V), st_spec],
        out_shape=[jax.ShapeDtypeStruct((b, t, B_V), F32),
                   jax.ShapeDtypeStruct((b, B_HEADS, B_KEY_DIM, B_VAL_DIM), F32)],
        scratch_shapes=[pltpu.VMEM((nbb, SUBLANES + CHUNK, B_CONV_DIM), F32),
                        pltpu.VMEM((nbb, B_HEADS, B_KEY_DIM, B_VAL_DIM), F32)],
        compiler_params=_cparams(("arbitrary", "arbitrary")),
        name="gdn_prompt",
    )(qkv, z, sm, s0, conv_w, _gdn_params(a_log, dt_bias), norm_w.reshape(1, B_VAL_DIM))
    return o, s_out


GDN_DEC_ROWS = 16


def _gdn_dec_kernel(qkv_ref, z_ref, sm_ref, cb_ref, s_ref, cw_ref, par_ref, nw_ref, o_ref, so_ref, oraw_ref):
    r = GDN_DEC_ROWS
    conv = qkv_ref[...] * cw_ref[CONV_W - 1:CONV_W, :]
    for j in range(CONV_W - 1):
        conv = conv + cb_ref[0, j] * cw_ref[j:j + 1, :]
    conv = _silu(conv)
    sm = sm_ref[...]
    g_all = -jnp.exp(par_ref[0:1, :]) * _softplus(sm + par_ref[1:2, :])
    beta_all = jax.nn.sigmoid(sm)
    ri = lax.broadcasted_iota(I32, (r, LANES), 0)
    ci = lax.broadcasted_iota(I32, (r, LANES), 1)
    eye_bf = jnp.where(ri == ci, 1.0, 0.0).astype(BF16)

    for h in range(B_HEADS):
        qh = conv[:, h * B_KEY_DIM:(h + 1) * B_KEY_DIM]
        kh = conv[:, B_K + h * B_KEY_DIM:B_K + (h + 1) * B_KEY_DIM]
        vh = conv[:, 2 * B_K + h * B_VAL_DIM:2 * B_K + (h + 1) * B_VAL_DIM]
        qh = qh * lax.rsqrt(jnp.sum(qh * qh, axis=-1, keepdims=True) + EPS) * (B_KEY_DIM ** -0.5)
        kh = kh * lax.rsqrt(jnp.sum(kh * kh, axis=-1, keepdims=True) + EPS)
        eg = jnp.exp(g_all[:, SM_A + h:SM_A + h + 1])
        beta = beta_all[:, SM_B + h:SM_B + h + 1]
        qk = jnp.sum(qh * kh, axis=-1, keepdims=True)
        k_t = _dot_exact_rhs(kh, eye_bf, _TN)
        q_t = _dot_exact_rhs(qh, eye_bf, _TN)
        for s in range(r):
            st = s_ref[0, s, h]
            kc = k_t[:, s:s + 1]
            qc = q_t[:, s:s + 1]
            k_s = jnp.sum(st * kc, axis=0, keepdims=True)
            q_s = jnp.sum(st * qc, axis=0, keepdims=True)
            eg_s = eg[s:s + 1, :]
            v_new = beta[s:s + 1, :] * (vh[s:s + 1, :] - eg_s * k_s)
            oraw_ref[s:s + 1, h * B_VAL_DIM:(h + 1) * B_VAL_DIM] = eg_s * q_s + qk[s:s + 1, :] * v_new
            so_ref[s, h] = st * eg_s + kc * v_new

    zz = z_ref[...]
    nw = nw_ref[...]
    for h in range(B_HEADS):
        o = oraw_ref[:, h * B_VAL_DIM:(h + 1) * B_VAL_DIM]
        ms = jnp.mean(o * o, axis=-1, keepdims=True)
        zh = zz[:, h * B_VAL_DIM:(h + 1) * B_VAL_DIM]
        o_ref[:, h * B_VAL_DIM:(h + 1) * B_VAL_DIM] = (o * lax.rsqrt(ms + EPS) * nw) * _silu(zh)


def _gdn_decode(qkv, z, sm, conv_t, state, j, conv_w, a_log, dt_bias, norm_w):
    b = qkv.shape[0]
    r = GDN_DEC_ROWS
    assert b % r == 0
    rows = lambda w: pl.BlockSpec((r, w), lambda i: (i, 0))
    const2 = lambda shp: pl.BlockSpec(shp, lambda i: (0, 0))
    o, s_out = pl.pallas_call(
        _gdn_dec_kernel,
        grid=(b // r,),
        in_specs=[rows(B_CONV_DIM), rows(B_V), rows(LANES),
                  pl.BlockSpec((1, CONV_W - 1, r, B_CONV_DIM), lambda i: (j, 0, i, 0)),
                  pl.BlockSpec((1, r, B_HEADS, B_KEY_DIM, B_VAL_DIM), lambda i: (j, i, 0, 0, 0)),
                  const2((CONV_W, B_CONV_DIM)), const2((SUBLANES, LANES)), const2((1, B_VAL_DIM))],
        out_specs=[rows(B_V), pl.BlockSpec((r, B_HEADS, B_KEY_DIM, B_VAL_DIM), lambda i: (i, 0, 0, 0))],
        out_shape=[jax.ShapeDtypeStruct((b, B_V), F32),
                   jax.ShapeDtypeStruct((b, B_HEADS, B_KEY_DIM, B_VAL_DIM), F32)],
        scratch_shapes=[pltpu.VMEM((r, B_V), F32)],
        compiler_params=_cparams(("arbitrary",)),
        name="gdn_decode",
    )(qkv, z, sm, conv_t, state, conv_w, _gdn_params(a_log, dt_bias), norm_w.reshape(1, B_VAL_DIM))
    return o, s_out


DIFF_HEADS_PER_STEP = 8


def _diff_kernel(lam_ref, q_ref, k_ref, v_ref, bias_ref, sw_ref, o_ref, *, lam_init):
    qb = pl.program_id(2)
    nkb = qb + 1
    nh = DIFF_HEADS_PER_STEP
    lane = lax.broadcasted_iota(I32, (TQ, C_HD), 1)
    q2 = []
    for h in range(nh):
        qh = q_ref[0, :, h * C_HD:(h + 1) * C_HD] * (C_HEAD_DIM ** -0.5)
        zero = jnp.zeros_like(qh)
        q2.append(jnp.concatenate([jnp.where(lane < C_HEAD_DIM, qh, zero),
                                   jnp.where(lane >= C_HEAD_DIM, qh, zero)], axis=0))

    hs = range(nh)

    def body(kb, carry):
        off = pl.multiple_of(kb * TQ, TQ)
        t = jnp.clip(kb - qb + 2, 0, 2)
        ss = [_dg(k_ref[0, pl.ds(off, TQ), h * C_HD:(h + 1) * C_HD], q2[h], _NT) for h in hs]
        ss = [ss[h] + jnp.concatenate([bias_ref[h, t]] * 2, axis=1) for h in hs]
        m_new = [jnp.maximum(carry[3 * h], jnp.max(ss[h], axis=0, keepdims=True)) for h in hs]
        ps = [jnp.exp(ss[h] - m_new[h]) for h in hs]
        pvs = [_dg(v_ref[0, pl.ds(off, TQ), h * C_HD:(h + 1) * C_HD], ps[h].astype(BF16), _TN) for h in hs]
        out = []
        for h in hs:
            alpha = jnp.exp(carry[3 * h] - m_new[h])
            out += [m_new[h], alpha * carry[3 * h + 1] + jnp.sum(ps[h], axis=0, keepdims=True),
                    alpha * carry[3 * h + 2] + pvs[h]]
        return tuple(out)

    init = (jnp.full((1, 2 * TQ), NEG, F32), jnp.zeros((1, 2 * TQ), F32), jnp.zeros((C_HD, 2 * TQ), F32)) * nh
    res = lax.fori_loop(0, nkb, body, init)
    for h in range(nh):
        _, l, acc = res[3 * h:3 * h + 3]
        on = acc / l
        o = (on[:, 0:TQ] - lam_ref[0] * on[:, TQ:2 * TQ]).T
        ms = jnp.mean(o * o, axis=-1, keepdims=True)
        o_ref[0, :, h * C_HD:(h + 1) * C_HD] = (o * lax.rsqrt(ms + EPS) * sw_ref[...]) * (1.0 - lam_init)


def _diff_prompt(q, k, v, bias_tiles, lam, subln_w, lam_init):
    b, t, _ = q.shape
    nh = DIFF_HEADS_PER_STEP
    w = nh * C_HD
    return pl.pallas_call(
        functools.partial(_diff_kernel, lam_init=lam_init),
        grid=(b, C_HEADS // nh, t // TQ),
        in_specs=[pl.BlockSpec(memory_space=pltpu.SMEM),
                  pl.BlockSpec((1, TQ, w), lambda bi, h, qb: (bi, qb, h)),
                  pl.BlockSpec((1, t, w), lambda bi, h, qb: (bi, 0, h)),
                  pl.BlockSpec((1, t, w), lambda bi, h, qb: (bi, 0, h)),
                  pl.BlockSpec((nh, 3, TQ, TQ), lambda bi, h, qb: (h, 0, 0, 0)),
                  pl.BlockSpec((1, C_HD), lambda bi, h, qb: (0, 0))],
        out_specs=pl.BlockSpec((1, TQ, w), lambda bi, h, qb: (bi, qb, h)),
        out_shape=jax.ShapeDtypeStruct((b, t, C_V), F32),
        compiler_params=_cparams(("arbitrary", "arbitrary", "arbitrary")),
        name="diff_prompt",
    )(lam.reshape(1), q, k, v, bias_tiles, subln_w.reshape(1, C_HD))


DIFF_DEC_PAGES = 16


def _diff_dec_kernel(pt_ref, lam_ref, q_ref, kn_ref, vn_ref, *rest, n_steps, lam_init):
    pp = DIFF_DEC_PAGES
    kps, vps = rest[:pp], rest[pp:2 * pp]
    bias_ref, bnew_ref, sw_ref, o_ref, m_ref, l_ref, acc_ref = rest[2 * pp:]
    p = pl.program_id(1)
    nr = 2 * C_HEADS

    @pl.when(p == 0)
    def _():
        m_ref[...] = jnp.full(m_ref.shape, NEG, F32)
        l_ref[...] = jnp.zeros(l_ref.shape, F32)
        acc_ref[...] = jnp.zeros(acc_ref.shape, F32)

    q8 = q_ref[0] * (C_HEAD_DIM ** -0.5)
    lane = lax.broadcasted_iota(I32, (C_HEADS, C_HD), 1)
    q_lo = jnp.where(lane < C_HEAD_DIM, q8, 0.0)
    q_hi = jnp.where(lane >= C_HEAD_DIM, q8, 0.0)
    qm = jnp.concatenate([q_lo, q_hi], axis=0).astype(BF16)

    rows = BLK * C_HEADS
    half = pp // 2
    cs = range(2)
    ss = [jnp.concatenate([_dg(qm, kps[c * half + i][0, 0].astype(BF16), _NT) + bias_ref[c * half + i]
                           for i in range(half)], axis=1) for c in cs]
    m_old = [m_ref[c] for c in cs]
    m_new = [jnp.maximum(m_old[c], jnp.max(ss[c], axis=1, keepdims=True)) for c in cs]
    ps = [jnp.exp(ss[c] - m_new[c]) for c in cs]
    pb = [pr.astype(BF16) for pr in ps]
    pvs = []
    for c in cs:
        out = None
        for i in range(half):
            part = _dg(pb[c][:, i * rows:(i + 1) * rows], vps[c * half + i][0, 0].astype(BF16))
            out = part if out is None else out + part
        pvs.append(out)
    for c in cs:
        alpha = jnp.exp(m_old[c] - m_new[c])
        l_ref[c] = alpha * l_ref[c] + jnp.sum(ps[c], axis=1, keepdims=True)
        acc_ref[c] = alpha * acc_ref[c] + pvs[c]
        m_ref[c] = m_new[c]

    @pl.when(p == n_steps - 1)
    def _():
        kn = kn_ref[0]
        s_new = jnp.concatenate([jnp.sum(q_lo * kn, axis=1, keepdims=True),
                                 jnp.sum(q_hi * kn, axis=1, keepdims=True)], axis=0) + bnew_ref[:, 0:1]
        vn2 = jnp.concatenate([vn_ref[0], vn_ref[0]], axis=0)
        m = jnp.maximum(jnp.maximum(m_ref[0], m_ref[1]), s_new)
        w0 = jnp.exp(m_ref[0] - m)
        w1 = jnp.exp(m_ref[1] - m)
        p_new = jnp.exp(s_new - m)
        l = w0 * l_ref[0] + w1 * l_ref[1] + p_new
        acc = w0 * acc_ref[0] + w1 * acc_ref[1] + p_new * vn2
        on = acc / l
        o = on[0:C_HEADS] - lam_ref[0] * on[C_HEADS:nr]
        ms = jnp.mean(o * o, axis=-1, keepdims=True)
        o_ref[0] = (o * lax.rsqrt(ms + EPS) * sw_ref[...]) * (1.0 - lam_init)


def _diff_decode(q8, k8, v8, cache_k, cache_v, j, pt_flat, n_pages, bias_pages, bias_new, lam, subln_w, lam_init):
    b = q8.shape[0]
    pp = DIFF_DEC_PAGES
    assert n_pages % pp == 0
    n_steps = n_pages // pp
    rows = BLK * C_HEADS
    row = pl.BlockSpec((1, C_HEADS, C_HD), lambda bi, p, pt: (bi, 0, 0))
    page = lambda i: pl.BlockSpec((1, 1, rows, C_HD),
                                  lambda bi, p, pt: (j, pt[bi * n_pages + p * pp + i], 0, 0))
    grid_spec = pltpu.PrefetchScalarGridSpec(
        num_scalar_prefetch=1,
        grid=(b, n_steps),
        in_specs=[pl.BlockSpec(memory_space=pltpu.SMEM), row, row, row]
                 + [page(i) for i in range(pp)] + [page(i) for i in range(pp)]
                 + [pl.BlockSpec((pp, 2 * C_HEADS, rows), lambda bi, p, pt: (p, 0, 0)),
                    pl.BlockSpec((2 * C_HEADS, LANES), lambda bi, p, pt: (0, 0)),
                    pl.BlockSpec((1, C_HD), lambda bi, p, pt: (0, 0))],
        out_specs=row,
        scratch_shapes=[pltpu.VMEM((2, 2 * C_HEADS, 1), F32), pltpu.VMEM((2, 2 * C_HEADS, 1), F32),
                        pltpu.VMEM((2, 2 * C_HEADS, C_HD), F32)],
    )
    return pl.pallas_call(
        functools.partial(_diff_dec_kernel, n_steps=n_steps, lam_init=lam_init),
        grid_spec=grid_spec,
        out_shape=jax.ShapeDtypeStruct((b, C_HEADS, C_HD), F32),
        compiler_params=_cparams(("arbitrary", "arbitrary")),
        name="diff_decode",
    )(pt_flat, lam.reshape(1), q8, k8, v8, *([cache_k] * pp), *([cache_v] * pp),
      bias_pages, bias_new, subln_w.reshape(1, C_HD))


def _diff_decode_bias(bias_cols, q_pos, n_pages):
    bd = _bias_decode(bias_cols, q_pos, n_pages + 1)
    past = jnp.moveaxis(bd[:n_pages], 1, 2)
    same = jnp.eye(C_HEADS, dtype=bool)
    tab = jnp.where(same[None, :, None, :], past[:, None, :, :], NEG)
    tab = tab.reshape(n_pages, C_HEADS, BLK * C_HEADS)
    tab = jnp.concatenate([tab, tab], axis=1)
    new = bd[n_pages, :, 0]
    new = jnp.broadcast_to(jnp.concatenate([new, new])[:, None], (2 * C_HEADS, LANES))
    return tab, new


def _idx_dec_kernel(pt_ref, qi_ref, sm_ref, *rest, n_pages):
    kps = rest[:n_pages]
    o_ref = rest[n_pages]
    qi = qi_ref[0]
    sm = sm_ref[0]
    rowi = lax.broadcasted_iota(I32, (SUBLANES, IDX_HEADS * IDX_DIM), 0)
    lane = lax.broadcasted_iota(I32, (SUBLANES, IDX_HEADS * IDX_DIM), 1)
    qsel = jnp.where(lane // IDX_DIM == rowi, qi, 0.0)
    qt = qsel[:, 0:IDX_DIM]
    for h in range(1, IDX_HEADS):
        qt = qt + qsel[:, h * IDX_DIM:(h + 1) * IDX_DIM]
    r8 = lax.broadcasted_iota(I32, (SUBLANES, LANES), 0)
    l8 = lax.broadcasted_iota(I32, (SUBLANES, LANES), 1)
    wsel = jnp.where(jnp.logical_and(r8 < IDX_HEADS, l8 == r8 + SM_WI), sm, 0.0)
    wcol = jnp.sum(wsel, axis=1, keepdims=True) * (IDX_HEADS ** -0.5)
    qh, ql = _split2(qt)
    kh, kl = _split2(jnp.concatenate([kps[p][0, 0] for p in range(n_pages)], axis=1))
    d = _dg(qh, kh) + (_dg(qh, kl) + _dg(ql, kh))
    sc = jnp.sum(wcol * jnp.maximum(d * (IDX_DIM ** -0.5), 0.0), axis=0, keepdims=True)
    for p in range(n_pages):
        o_ref[0, p] = sc[:, p * BLK:(p + 1) * BLK]
    d_new = jnp.sum(qt * sm[:, SM_KI:SM_KI + IDX_DIM], axis=1, keepdims=True)
    s_new = jnp.sum(wcol * jnp.maximum(d_new * (IDX_DIM ** -0.5), 0.0), axis=0, keepdims=True)
    o_ref[0, n_pages] = jnp.broadcast_to(s_new, (1, BLK))


def _idx_decode(qi, sm, cache_kidx_t, j, pt_flat, n_pages):
    b = qi.shape[0]
    page = lambda p: pl.BlockSpec((1, 1, IDX_DIM, BLK), lambda bi, pt: (j, pt[bi * n_pages + p], 0, 0))
    grid_spec = pltpu.PrefetchScalarGridSpec(
        num_scalar_prefetch=1,
        grid=(b,),
        in_specs=[pl.BlockSpec((1, 1, IDX_HEADS * IDX_DIM), lambda bi, pt: (bi, 0, 0)),
                  pl.BlockSpec((1, 1, LANES), lambda bi, pt: (bi, 0, 0))]
                 + [page(p) for p in range(n_pages)],
        out_specs=pl.BlockSpec((1, n_pages + 1, 1, BLK), lambda bi, pt: (bi, 0, 0, 0)),
    )
    return pl.pallas_call(
        functools.partial(_idx_dec_kernel, n_pages=n_pages),
        grid_spec=grid_spec,
        out_shape=jax.ShapeDtypeStruct((b, n_pages + 1, 1, BLK), F32),
        compiler_params=_cparams(("arbitrary",)),
        name="idx_decode",
    )(pt_flat, qi, sm, *([cache_kidx_t] * n_pages))


def _sel_dec_kernel(s_ref, o_ref, key_ref, *, n_valid, topk, idx_bits):
    nkb, rows, _ = s_ref.shape
    col = lax.broadcasted_iota(I32, (rows, BLK), 1)
    for kb in range(nkb):
        s = jnp.where(kb * BLK + col < n_valid, s_ref[kb], -jnp.inf)
        key_ref[kb] = _score_keys(s)
    thr, cut = _select_topk(key_ref, nkb, topk, idx_bits)
    for kb in range(nkb):
        k_pos = kb * BLK + col
        sel = _in_topk(key_ref[kb], k_pos, thr, cut)
        o_ref[kb] = jnp.where(k_pos < n_valid, jnp.where(sel, 0.0, NEG), NEG)


def _sel_decode(scores, n_valid, topk):
    nkb, rows, _ = scores.shape
    idx_bits = max(1, int(math.ceil(math.log2(nkb * BLK))))
    return pl.pallas_call(
        functools.partial(_sel_dec_kernel, n_valid=n_valid, topk=topk, idx_bits=idx_bits),
        out_shape=jax.ShapeDtypeStruct(scores.shape, F32),
        scratch_shapes=[pltpu.VMEM(scores.shape, I32)],
        compiler_params=pltpu.CompilerParams(vmem_limit_bytes=VMEM_LIMIT),
        name="sel_decode",
    )(scores)


def _dsa_dec_kernel(pt_ref, q_ref, kn_ref, vn_ref, *rest, n_pages):
    kps, vps = rest[:n_pages], rest[n_pages:2 * n_pages]
    msk_ref, bias_ref, o_ref = rest[2 * n_pages:]

    q = q_ref[0] * (A_HEAD_DIM ** -0.5)
    rowi = lax.broadcasted_iota(I32, (A_HEADS, A_Q), 0)
    lane = lax.broadcasted_iota(I32, (A_HEADS, A_Q), 1)
    qsel = jnp.where(lane // A_HEAD_DIM == rowi, q, 0.0)
    halves = []
    for n in range(A_KV_HEADS):
        acc = None
        for g in range(A_GROUP):
            h = n * A_GROUP + g
            part = qsel[:, h * A_HEAD_DIM:(h + 1) * A_HEAD_DIM]
            acc = part if acc is None else acc + part
        halves.append(acc)
    qt = jnp.concatenate(halves, axis=1)
    qt_bf = qt.astype(BF16)

    kt = jnp.concatenate([kps[i][0, 0].astype(BF16) for i in range(n_pages)], axis=1)
    vt = jnp.concatenate([vps[i][0, 0].astype(BF16) for i in range(n_pages)], axis=1)
    extra = jnp.concatenate([bias_ref[i] + msk_ref[i, 0] for i in range(n_pages)], axis=1)
    s = _dg(qt_bf, kt) + extra
    s_new = (jnp.sum(qt * kn_ref[0], axis=1, keepdims=True) + bias_ref[n_pages][:, 0:1]
             + msk_ref[n_pages, 0][:, 0:1])
    m = jnp.maximum(jnp.max(s, axis=1, keepdims=True), s_new)
    p = jnp.exp(s - m)
    p_new = jnp.exp(s_new - m)
    l = jnp.sum(p, axis=1, keepdims=True) + p_new
    acc = p_new * vn_ref[0] + _dg(p.astype(BF16), vt, _NT)
    on = acc / l
    for h in range(A_HEADS):
        n = h // A_GROUP
        o_ref[0, :, h * A_HEAD_DIM:(h + 1) * A_HEAD_DIM] = on[h:h + 1, n * A_HEAD_DIM:(n + 1) * A_HEAD_DIM]


def _dsa_decode(q, k_new, v_new, cache_kt, cache_vt, j, pt_flat, n_pages, mask, bias_dec):
    b = q.shape[0]
    page = lambda i: pl.BlockSpec((1, 1, A_KV, BLK), lambda bi, pt: (j, pt[bi * n_pages + i], 0, 0))
    rowspec = lambda w: pl.BlockSpec((1, 1, w), lambda bi, pt: (bi, 0, 0))
    grid_spec = pltpu.PrefetchScalarGridSpec(
        num_scalar_prefetch=1,
        grid=(b,),
        in_specs=[rowspec(A_Q), rowspec(A_KV), rowspec(A_KV)]
                 + [page(i) for i in range(n_pages)] + [page(i) for i in range(n_pages)]
                 + [pl.BlockSpec((n_pages + 1, 1, 1, BLK), lambda bi, pt: (0, bi, 0, 0)),
                    pl.BlockSpec((n_pages + 1, A_HEADS, BLK), lambda bi, pt: (0, 0, 0))],
        out_specs=rowspec(A_Q),
    )
    return pl.pallas_call(
        functools.partial(_dsa_dec_kernel, n_pages=n_pages),
        grid_spec=grid_spec,
        out_shape=jax.ShapeDtypeStruct((b, 1, A_Q), F32),
        compiler_params=_cparams(("arbitrary",)),
        name="dsa_decode",
    )(pt_flat, q, k_new, v_new, *([cache_kt] * n_pages), *([cache_vt] * n_pages), mask, bias_dec)


def _even_weights(w_in):
    sizes = (A_Q, A_KV, A_KV, IDX_HEADS * IDX_DIM, IDX_DIM, IDX_HEADS, B_CONV_DIM, B_V, B_HEADS, B_HEADS)
    offs = np.concatenate([[0], np.cumsum(sizes)])
    seg = lambda i: w_in[:, int(offs[i]):int(offs[i + 1])]
    pad = LANES - (IDX_DIM + IDX_HEADS + 2 * B_HEADS)
    small = jnp.concatenate([seg(4), seg(5), seg(8), seg(9), jnp.zeros((w_in.shape[0], pad), w_in.dtype)], axis=1)
    return jnp.concatenate([seg(0), seg(1), seg(2), seg(3), small, seg(6), seg(7)], axis=1).astype(BF16)


def _seq_outs(widths, dtype=F32):
    outs, off = [], 0
    for wd in widths:
        outs.append((off, wd, dtype))
        off += wd
    return tuple(outs)


_EVEN_OUTS = _seq_outs((A_Q, A_KV, A_KV, IDX_HEADS * IDX_DIM, LANES, B_CONV_DIM, B_V))
_ODD_OUTS = _seq_outs((C_QK, C_QK, C_V))
_ODD_OUTS_PROMPT = _ODD_OUTS[1:] + _seq_outs((C_QK, C_QK, C_V), BF16)


def kernel(x_prompt, x_sample, c_prompt, c_sample, cache_A_k, cache_A_v, cache_A_kidx, cache_C_k, cache_C_v, state_B_ssm, state_B_conv, page_table, rel_bias, ada_w, ada_b, norm_w, final_norm_w, ffn_w1, ffn_w2, ab_w_in, ab_w_out, gdn_conv_w, gdn_a_log, gdn_dt_bias, gdn_norm_w, c_w_in, c_w_out, c_lambda_q1, c_lambda_k1, c_lambda_q2, c_lambda_k2, c_subln_w):
    depth = ada_w.shape[0]
    bp, tp, d = x_prompt.shape
    bs, ts, _ = x_sample.shape
    assert ts == 1 and tp % TQ == 0 and tp % CHUNK == 0
    n_pages = page_table.shape[1]
    page = cache_A_k.shape[2]
    assert page == BLK
    n_phys = cache_A_k.shape[1]
    past_len = n_pages * page
    topk_p = min(TOPK_MAX, tp // 4)
    topk_s = min(TOPK_MAX, (past_len + ts) // 4)
    pt_flat = page_table.reshape(-1).astype(I32)

    n_c = bp + bs
    n_c_pad = -(-n_c // SUBLANES) * SUBLANES
    c_all = jnp.pad(jnp.concatenate([c_prompt, c_sample], axis=0), ((0, n_c_pad - n_c), (0, 0)))
    mod_all = _modulation(c_all, ada_w, ada_b)

    w1b = ffn_w1.astype(BF16)
    w2b = ffn_w2.astype(BF16)
    ab_in_b = [_even_weights(ab_w_in[j]) for j in range(ab_w_in.shape[0])]
    ab_out_b = ab_w_out.astype(BF16)
    c_in_b = c_w_in.astype(BF16)
    c_out_b = c_w_out.astype(BF16)

    bias_a = _bias_tiles(rel_bias[:, :A_HEADS], BLK, False)
    bias_c = jnp.swapaxes(_bias_tiles(rel_bias[:, A_HEADS:], TQ, True), 2, 3)
    bias_a_dec = _bias_decode(rel_bias[:, :A_HEADS], past_len, n_pages + 1)
    bias_c_pages, bias_c_new = _diff_decode_bias(rel_bias[:, A_HEADS:], past_len, n_pages)

    n_ab = cache_A_k.shape[0]
    cache_a_kt = jnp.transpose(cache_A_k, (0, 1, 3, 4, 2)).reshape(n_ab, n_phys, A_KV, page)
    cache_a_vt = jnp.transpose(cache_A_v, (0, 1, 3, 4, 2)).reshape(n_ab, n_phys, A_KV, page)
    cache_a_it = jnp.transpose(cache_A_kidx, (0, 1, 3, 2))
    cache_c_k = cache_C_k.reshape(cache_C_k.shape[0], n_phys, page * C_HEADS, C_HD)
    cache_c_v = cache_C_v.reshape(cache_C_v.shape[0], n_phys, page * C_HEADS, C_HD)
    conv_t = jnp.transpose(state_B_conv, (0, 2, 1, 3))

    xp = x_prompt
    xs = x_sample.reshape(1, bs, d)
    new_p = [[] for _ in range(7)]
    new_s = [[] for _ in range(7)]

    for i in range(depth):
        j = i // 2
        mod_p = mod_all[i, :bp].reshape(bp, 1, N_MOD * d)
        mod_s = mod_all[i, bp:bp + bs].reshape(1, bs, N_MOD * d)
        xp = _ffn(xp, mod_p, 0, norm_w[i, 0], w1b, w2b, i, 0)
        xs = _ffn(xs, mod_s, 0, norm_w[i, 0], w1b, w2b, i, 0)
        if i % 2 == 0:
            q, k, v, qi, sm, qkv, z = _inproj(xp, mod_p, 3, norm_w[i, 1], ab_in_b[j], _EVEN_OUTS)
            o_a = _dsa_prompt(q, qi, sm, k, v, bias_a, topk_p)
            o_b, s_new = _gdn_prompt(qkv, z, sm, jnp.zeros((bp, B_HEADS, B_KEY_DIM, B_VAL_DIM), F32),
                                     gdn_conv_w[j], gdn_a_log[j], gdn_dt_bias[j], gdn_norm_w[j])
            xp = _outproj(xp, mod_p, 5, ab_out_b[j], [o_a, o_b])
            new_p[0].append(k.reshape(bp, tp, A_KV_HEADS, A_HEAD_DIM))
            new_p[1].append(v.reshape(bp, tp, A_KV_HEADS, A_HEAD_DIM))
            new_p[2].append(sm[:, :, SM_KI:SM_KI + IDX_DIM])
            new_p[3].append(s_new)
            new_p[4].append(qkv[:, tp - (CONV_W - 1):, :])
            q, k, v, qi, sm, qkv, z = _inproj(xs, mod_s, 3, norm_w[i, 1], ab_in_b[j], _EVEN_OUTS)
            as_rows = lambda a: a.reshape(bs, 1, a.shape[-1])
            scores = _idx_decode(as_rows(qi), as_rows(sm), cache_a_it, j, pt_flat, n_pages)
            scores = jnp.moveaxis(scores.reshape(bs, n_pages + 1, BLK), 1, 0)
            mask = _sel_decode(scores, past_len + 1, topk_s).reshape(n_pages + 1, bs, 1, BLK)
            o_a = _dsa_decode(as_rows(q), as_rows(k), as_rows(v), cache_a_kt, cache_a_vt, j, pt_flat, n_pages,
                              mask, bias_a_dec)
            o_b, s_new = _gdn_decode(qkv.reshape(bs, B_CONV_DIM), z.reshape(bs, B_V), sm.reshape(bs, LANES),
                                     conv_t, state_B_ssm, j, gdn_conv_w[j], gdn_a_log[j], gdn_dt_bias[j],
                                     gdn_norm_w[j])
            xs = _outproj(xs, mod_s, 5, ab_out_b[j], [o_a.reshape(1, bs, A_Q), o_b.reshape(1, bs, B_V)])
            new_s[0].append(k.reshape(bs, 1, A_KV_HEADS, A_HEAD_DIM))
            new_s[1].append(v.reshape(bs, 1, A_KV_HEADS, A_HEAD_DIM))
            new_s[2].append(sm.reshape(bs, 1, LANES)[:, :, SM_KI:SM_KI + IDX_DIM])
            new_s[3].append(s_new)
            new_s[4].append(jnp.concatenate([state_B_conv[j], as_rows(qkv)], axis=1)[:, 1:, :])
        else:
            lam_init = 0.8 - 0.6 * math.exp(-0.3 * i)
            lam = (jnp.exp(jnp.sum(c_lambda_q1[j] * c_lambda_k1[j]))
                   - jnp.exp(jnp.sum(c_lambda_q2[j] * c_lambda_k2[j])) + lam_init).astype(F32)
            k, v, qb16, kb16, vb16 = _inproj(xp, mod_p, 3, norm_w[i, 1], c_in_b[j], _ODD_OUTS_PROMPT)
            o = _diff_prompt(qb16, kb16, vb16, bias_c, lam, c_subln_w[j], lam_init)
            xp = _outproj(xp, mod_p, 5, c_out_b[j], [o])
            new_p[5].append(k.reshape(bp, tp, C_HEADS, C_HD))
            new_p[6].append(v.reshape(bp, tp, C_HEADS, C_HD))
            q, k, v = _inproj(xs, mod_s, 3, norm_w[i, 1], c_in_b[j], _ODD_OUTS)
            as_heads = lambda a: a.reshape(bs, C_HEADS, C_HD)
            o = _diff_decode(as_heads(q), as_heads(k), as_heads(v), cache_c_k, cache_c_v, j, pt_flat, n_pages,
                             bias_c_pages, bias_c_new, lam, c_subln_w[j], lam_init)
            xs = _outproj(xs, mod_s, 5, c_out_b[j], [o.reshape(1, bs, C_V)])
            new_s[5].append(k.reshape(bs, 1, C_HEADS, C_HD))
            new_s[6].append(v.reshape(bs, 1, C_HEADS, C_HD))
        fw = final_norm_w if i == depth - 1 else None
        xp = _ffn(xp, mod_p, 6, norm_w[i, 2], w1b, w2b, i, 1, fw)
        xs = _ffn(xs, mod_s, 6, norm_w[i, 2], w1b, w2b, i, 1, fw)

    y_prompt = xp
    y_sample = xs.reshape(bs, 1, d)
    sp = [jnp.stack(lst) for lst in new_p]
    ss = [jnp.stack(lst) for lst in new_s]
    return (y_prompt, y_sample, *sp, *ss)
```
